```python
import math
import jax, jax.numpy as jnp
from jax import lax
import numpy as np

D_MODEL = 1024
BATCH = 8
SEQ = 2048
DEPTH = 1
DEC_BATCH = 32
DEC_SEQ = 8
PAST_LEN = 16384
PAGE_SIZE = 128

HEAD_DIM = 64
D_MIX = D_MODEL
D_A = D_MIX // 2
A_GROUPS = D_A // HEAD_DIM
CHUNK = 128
N_HEADS = (D_MIX - D_A) // HEAD_DIM
D_B = N_HEADS * HEAD_DIM
N_KV = 2
GQA = N_HEADS // N_KV
KV_COLS = N_KV * HEAD_DIM
CMP_BLOCK = 32
CMP_STRIDE = 16
CMP_RATIO = CMP_BLOCK // CMP_STRIDE
CMP_HIDDEN = 128
SLC_BLOCK = 64
TOP_N = 16
WINDOW = 512
ROPE_THETA = 10000.0
EPS = 1e-6
Q_BLOCK = 64
WIN_QBLOCK = 128
SCALE = HEAD_DIM ** -0.5
NEG = -1e30
FORCE = 1e9
D_IN = 3 * D_A + D_B + 6 * KV_COLS + 3 * N_HEADS + D_B

kernel_name = 'hymba_gmlp_nsa_decode_step'


def rmsnorm(x, g):
    xf = x.astype(jnp.float32)
    y = xf * lax.rsqrt(jnp.mean(xf * xf, axis=-1, keepdims=True) + EPS)
    return y.astype(x.dtype) * g


def layernorm(x, g, b):
    xf = x.astype(jnp.float32)
    mu = jnp.mean(xf, axis=-1, keepdims=True)
    var = jnp.mean(jnp.square(xf - mu), axis=-1, keepdims=True)
    return ((xf - mu) * lax.rsqrt(var + EPS)).astype(x.dtype) * g + b


def rope(x, pos):
    half = HEAD_DIM // 2
    inv = ROPE_THETA ** (-jnp.arange(half, dtype=jnp.float32) / half)
    ang = pos.astype(jnp.float32)[:, None] * inv
    shape = (pos.shape[0],) + (1,) * (x.ndim - 3) + (half,)
    cos = jnp.cos(ang).reshape(shape)
    sin = jnp.sin(ang).reshape(shape)
    xf = x.astype(jnp.float32)
    x1, x2 = xf[..., :half], xf[..., half:]
    return jnp.concatenate([x1 * cos - x2 * sin, x2 * cos + x1 * sin], axis=-1).astype(x.dtype)


def masked_softmax(s, mask):
    s = jnp.where(mask, s.astype(jnp.float32), NEG)
    e = jnp.where(mask, jnp.exp(s - jnp.max(s, axis=-1, keepdims=True)), 0.0)
    return e / jnp.maximum(jnp.sum(e, axis=-1, keepdims=True), 1.0)


def project(x, norm_g, w_in):
    B, T, _ = x.shape
    p = rmsnorm(x, norm_g) @ w_in
    cuts = np.cumsum([D_A, D_A, D_A, D_B, 6 * KV_COLS, 3 * N_HEADS]).tolist()
    u, v, z_a, q, kv, g, z_b = jnp.split(p, cuts, axis=-1)
    return (jax.nn.gelu(u), jax.nn.gelu(v), z_a,
            q.reshape(B, T, N_KV, GQA, HEAD_DIM),
            kv.reshape(B, T, 6, N_KV, HEAD_DIM),
            jax.nn.sigmoid(g.reshape(B, T, N_KV, GQA, 3)), z_b)


def chunk_gmlp(u, v, ln_g, ln_b, w_s, b_s):
    B, T, _ = v.shape
    v = layernorm(v, ln_g, ln_b)
    n_c = -(-T // CHUNK)
    vp = jnp.pad(v, ((0, 0), (0, n_c * CHUNK - T), (0, 0))).reshape(B, n_c, CHUNK, A_GROUPS, HEAD_DIM)
    w = w_s * jnp.tril(jnp.ones((CHUNK, CHUNK), w_s.dtype))
    mixed = jnp.einsum('gts,bcsgd->bctgd', w, vp) + b_s.T[None, None, :, :, None]
    mixed = mixed.reshape(B, n_c * CHUNK, D_A)[:, :T]
    return u * mixed, v


def compress(k_raw, pe, w1, b1, w2, n_cmp):
    B = k_raw.shape[0]
    n_seg = n_cmp + CMP_RATIO - 1
    seg = k_raw[:, :n_seg * CMP_STRIDE].reshape(B, n_seg, CMP_STRIDE, N_KV, HEAD_DIM)
    seg = seg.transpose(0, 1, 3, 2, 4).reshape(B, n_seg, N_KV, CMP_STRIDE * HEAD_DIM)
    w1r = w1.reshape(CMP_RATIO, CMP_STRIDE * HEAD_DIM, CMP_HIDDEN)
    pe_r = pe.reshape(CMP_RATIO, CMP_STRIDE * HEAD_DIM)
    part = jnp.einsum('bnhx,rxk->bnhrk', seg, w1r) + jnp.einsum('rx,rxk->rk', pe_r, w1r)
    acc = b1
    for r in range(CMP_RATIO):
        acc = acc + part[:, r:r + n_cmp, :, r]
    return jax.nn.gelu(acc) @ w2


def cmp_attend(q, kc, vc, tq):
    n_cmp = kc.shape[1]
    s = jnp.einsum('bthgd,bnhd->bhgtn', q, kc) * SCALE
    end = jnp.arange(n_cmp) * CMP_STRIDE + CMP_BLOCK - 1
    p = masked_softmax(s, end[None, :] <= tq[:, None])
    return jnp.einsum('bhgtn,bnhd->bthgd', p.astype(vc.dtype), vc), p


def select_blocks(p, tq, n_cmp, n_slc):
    i = jnp.arange(n_cmp)[:, None]
    j = jnp.arange(n_slc)[None, :]
    start = i * CMP_STRIDE
    overlap = ((start <= j * SLC_BLOCK + SLC_BLOCK - 1) & (start + CMP_BLOCK - 1 >= j * SLC_BLOCK)).astype(jnp.float32)
    imp = jnp.einsum('bhgtn,nj->bhtj', p, overlap)
    tb = (tq // SLC_BLOCK)[:, None]
    valid = j * SLC_BLOCK <= tq[:, None]
    forced = (j == 0) | (j == tb) | (j == tb - 1)
    score = jnp.where(forced, FORCE, jnp.where(valid, imp, NEG))
    _, sel = lax.top_k(score, min(TOP_N, n_slc))
    return sel


def sel_attend(q, kb, vb, kpos, tq):
    B, T = q.shape[:2]
    kb = kb.reshape(B, N_KV, T, -1, HEAD_DIM)
    vb = vb.reshape(B, N_KV, T, -1, HEAD_DIM)
    kpos = kpos.reshape(B, N_KV, T, -1)
    s = jnp.einsum('bthgd,bhtkd->bhgtk', q, kb) * SCALE
    p = masked_softmax(s, (kpos <= tq[None, None, :, None])[:, :, None])
    return jnp.einsum('bhgtk,bhtkd->bthgd', p.astype(vb.dtype), vb)


def win_attend(q, k, v, tq, kpos):
    s = jnp.einsum('bthgd,bkhd->bhgtk', q, k) * SCALE
    dist = tq[:, None] - kpos[None, :]
    mask = (dist >= 0) & (dist < WINDOW) & (kpos[None, :] >= 0)
    p = masked_softmax(s, mask)
    return jnp.einsum('bhgtk,bkhd->bthgd', p.astype(v.dtype), v)


def combine(g, o_cmp, o_slc, o_win):
    B, T = g.shape[:2]
    o = g[..., 0:1] * o_cmp + g[..., 1:2] * o_slc + g[..., 2:3] * o_win
    return o.reshape(B, T, D_B)


def merge(a, z_a, b, z_b, w_out):
    return jnp.concatenate([a * jax.nn.silu(z_a), b * jax.nn.silu(z_b)], axis=-1) @ w_out


def nsa_prompt(q, kv, g, pe, w1, b1, w2, win_buf):
    B, T = q.shape[:2]
    tq = jnp.arange(T)
    kc, vc, ks, vs, kw, vw = (kv[:, :, i] for i in range(6))
    qr, ks, kw = rope(q, tq), rope(ks, tq), rope(kw, tq)
    n_cmp = (T - CMP_BLOCK) // CMP_STRIDE + 1
    kcmp = compress(kc, pe[0], w1[0], b1[0], w2[0], n_cmp)
    vcmp = compress(vc, pe[1], w1[1], b1[1], w2[1], n_cmp)
    o_cmp, p = cmp_attend(q, kcmp, vcmp, tq)
    n_slc = -(-T // SLC_BLOCK)
    sel = select_blocks(p, tq, n_cmp, n_slc)
    ks_blk = ks.reshape(B, n_slc, SLC_BLOCK, N_KV, HEAD_DIM)
    vs_blk = vs.reshape(B, n_slc, SLC_BLOCK, N_KV, HEAD_DIM)
    bi = jnp.arange(B)[:, None, None, None]
    hi = jnp.arange(N_KV)[None, :, None, None]
    nq = T // Q_BLOCK

    def sel_block(args):
        qb, selb, tb = args
        kb = ks_blk[bi, selb, :, hi]
        vb = vs_blk[bi, selb, :, hi]
        kpos = selb[..., None] * SLC_BLOCK + jnp.arange(SLC_BLOCK)
        return sel_attend(qb, kb, vb, kpos, tb)

    o_slc = lax.map(sel_block, (qr.reshape(B, nq, Q_BLOCK, N_KV, GQA, HEAD_DIM).swapaxes(0, 1),
                                sel.reshape(B, N_KV, nq, Q_BLOCK, -1).transpose(2, 0, 1, 3, 4),
                                tq.reshape(nq, Q_BLOCK)))
    o_slc = o_slc.swapaxes(0, 1).reshape(B, T, N_KV, GQA, HEAD_DIM)
    kvw = jnp.stack([kw, vw], axis=2)
    kvw_pad = jnp.pad(kvw, ((0, 0), (WINDOW, 0), (0, 0), (0, 0), (0, 0)))
    nb = T // WIN_QBLOCK

    def win_block(args):
        qb, b0 = args
        kvb = lax.dynamic_slice_in_dim(kvw_pad, b0, WIN_QBLOCK + WINDOW, axis=1)
        tb = b0 + jnp.arange(WIN_QBLOCK)
        kpos = b0 - WINDOW + jnp.arange(WIN_QBLOCK + WINDOW)
        return win_attend(qb, kvb[:, :, 0], kvb[:, :, 1], tb, kpos)

    o_win = lax.map(win_block, (qr.reshape(B, nb, WIN_QBLOCK, N_KV, GQA, HEAD_DIM).swapaxes(0, 1),
                                jnp.arange(nb) * WIN_QBLOCK))
    o_win = o_win.swapaxes(0, 1).reshape(B, T, N_KV, GQA, HEAD_DIM)
    rows = jnp.stack([kc, vc, ks, vs], axis=2)
    win_state = kvw_pad[:, -win_buf:]
    return combine(g, o_cmp, o_slc, o_win), rows, win_state


def nsa_sample(q, kv, g, cache_kv, layer, page_table, win_prev, pe, w1, b1, w2):
    Bd, T = q.shape[:2]
    tq = PAST_LEN + jnp.arange(T)
    L = PAST_LEN + T
    n_pages = PAST_LEN // PAGE_SIZE
    kc, vc, ks, vs, kw, vw = (kv[:, :, i] for i in range(6))
    qr, ks, kw = rope(q, tq), rope(ks, tq), rope(kw, tq)
    past = cache_kv[layer, page_table, :, 0:2].reshape(Bd, PAST_LEN, 2, N_KV, HEAD_DIM)
    full = jnp.concatenate([past, jnp.stack([kc, vc], axis=2)], axis=1)
    n_cmp = (L - CMP_BLOCK) // CMP_STRIDE + 1
    kcmp = compress(full[:, :, 0], pe[0], w1[0], b1[0], w2[0], n_cmp)
    vcmp = compress(full[:, :, 1], pe[1], w1[1], b1[1], w2[1], n_cmp)
    o_cmp, p = cmp_attend(q, kcmp, vcmp, tq)
    n_slc = -(-L // SLC_BLOCK)
    sel = select_blocks(p, tq, n_cmp, n_slc)
    pos = sel[..., None] * SLC_BLOCK + jnp.arange(SLC_BLOCK)
    bi = jnp.arange(Bd)[:, None, None, None, None]
    hi = jnp.arange(N_KV)[None, :, None, None, None]
    phys = page_table[bi, jnp.minimum(pos // PAGE_SIZE, n_pages - 1)]
    off = pos % PAGE_SIZE
    from_past = (pos < PAST_LEN)[..., None]
    new_i = jnp.clip(pos - PAST_LEN, 0, T - 1)
    kb = jnp.where(from_past, cache_kv[layer, phys, off, 2, hi], ks[bi, new_i, hi])
    vb = jnp.where(from_past, cache_kv[layer, phys, off, 3, hi], vs[bi, new_i, hi])
    o_slc = sel_attend(qr, kb, vb, pos, tq)
    win_buf = win_prev.shape[1]
    buf = jnp.concatenate([win_prev, jnp.stack([kw, vw], axis=2)], axis=1)
    kpos = PAST_LEN - win_buf + jnp.arange(win_buf + T)
    o_win = win_attend(qr, buf[:, :, 0], buf[:, :, 1], tq, kpos)
    rows = jnp.stack([kc, vc, ks, vs], axis=2)
    return combine(g, o_cmp, o_slc, o_win), rows, buf[:, -win_buf:]


def setup_inputs(seed: int = 0) -> dict:
    key = jax.random.key(seed)
    ks = jax.random.split(key, 17)
    n_pages = PAST_LEN // PAGE_SIZE
    n_used = DEC_BATCH * n_pages
    n_phys = n_used + max(1, n_used // 4)
    win_buf = min(WINDOW, PAST_LEN)

    def nrm(k, shape, scale):
        return jax.random.normal(k, shape, jnp.float32) * scale

    perm = jax.random.permutation(ks[4], n_phys)
    return {
        'x_prompt': nrm(ks[0], (BATCH, SEQ, D_MODEL), 1.0),
        'x_sample': nrm(ks[1], (DEC_BATCH, DEC_SEQ, D_MODEL), 1.0),
        'cache_kv': nrm(ks[2], (DEPTH, n_phys, PAGE_SIZE, 4, N_KV, HEAD_DIM), 1.0),
        'state_win': nrm(ks[3], (DEPTH, DEC_BATCH, win_buf, 2, N_KV, HEAD_DIM), 1.0),
        'page_table': perm[:n_used].reshape(DEC_BATCH, n_pages).astype(jnp.int32),
        'norm_g': 1.0 + nrm(ks[5], (DEPTH, D_MODEL), 0.02),
        'w_in': nrm(ks[6], (DEPTH, D_MODEL, D_IN), D_MODEL ** -0.5),
        'ln_g': 1.0 + nrm(ks[7], (DEPTH, D_A), 0.02),
        'ln_b': nrm(ks[8], (DEPTH, D_A), 0.02),
        'w_s': nrm(ks[9], (DEPTH, A_GROUPS, CHUNK, CHUNK), CHUNK ** -0.5),
        'b_s': 1.0 + nrm(ks[10], (DEPTH, A_GROUPS, CHUNK), 0.1),
        'cmp_pos': nrm(ks[11], (DEPTH, 2, CMP_BLOCK, HEAD_DIM), 0.1),
        'w_cmp1': nrm(ks[12], (DEPTH, 2, CMP_BLOCK * HEAD_DIM, CMP_HIDDEN), (CMP_BLOCK * HEAD_DIM) ** -0.5),
        'b_cmp1': nrm(ks[13], (DEPTH, 2, CMP_HIDDEN), 0.02),
        'w_cmp2': nrm(ks[14], (DEPTH, 2, CMP_HIDDEN, HEAD_DIM), CMP_HIDDEN ** -0.5),
        'w_out': nrm(ks[15], (DEPTH, D_MIX, D_MODEL), D_MIX ** -0.5),
        'final_g': 1.0 + nrm(ks[16], (D_MODEL,), 0.02),
    }


def reference(x_prompt, x_sample, cache_kv, state_win, page_table, norm_g, w_in, ln_g, ln_b,
              w_s, b_s, cmp_pos, w_cmp1, b_cmp1, w_cmp2, w_out, final_g):
    win_buf = min(WINDOW, PAST_LEN)
    x_p, x_s = x_prompt, x_sample
    kv_p, win_p, kv_s, win_s, v_s = [], [], [], [], []
    for l in range(DEPTH):
        u, v, z_a, q, kv, g, z_b = project(x_p, norm_g[l], w_in[l])
        a_out, _ = chunk_gmlp(u, v, ln_g[l], ln_b[l], w_s[l], b_s[l])
        b_out, rows, win = nsa_prompt(q, kv, g, cmp_pos[l], w_cmp1[l], b_cmp1[l], w_cmp2[l], win_buf)
        x_p = x_p + merge(a_out, z_a, b_out, z_b, w_out[l])
        kv_p.append(rows)
        win_p.append(win)
        u, v, z_a, q, kv, g, z_b = project(x_s, norm_g[l], w_in[l])
        a_out, v_rows = chunk_gmlp(u, v, ln_g[l], ln_b[l], w_s[l], b_s[l])
        b_out, rows, win = nsa_sample(q, kv, g, cache_kv, l, page_table, state_win[l],
                                      cmp_pos[l], w_cmp1[l], b_cmp1[l], w_cmp2[l])
        x_s = x_s + merge(a_out, z_a, b_out, z_b, w_out[l])
        kv_s.append(rows)
        win_s.append(win)
        v_s.append(v_rows)
    y_prompt = rmsnorm(x_p, final_g)
    y_sample = rmsnorm(x_s, final_g)
    return (y_prompt, y_sample, jnp.stack(kv_p), jnp.stack(win_p), jnp.stack(kv_s), jnp.stack(win_s), jnp.stack(v_s))
```

```python
import functools
import math

import jax
import jax.numpy as jnp
import numpy as np
from jax import lax
from jax.experimental import pallas as pl
from jax.experimental.pallas import tpu as pltpu

F32 = jnp.float32
BF16 = jnp.bfloat16

D_MODEL = 1024
HEAD_DIM = 64
D_A = 512
A_GROUPS = 8
CHUNK = 128
N_HEADS = 8
D_B = 512
N_KV = 2
GQA = 4
CMP_BLOCK = 32
CMP_STRIDE = 16
CMP_HIDDEN = 128
SLC_BLOCK = 64
TOP_N = 16
WINDOW = 512
ROPE_THETA = 10000.0
EPS = 1e-6
SCALE = HEAD_DIM ** -0.5
NEG = -1e30
FORCE = 1e9
PAGE_SIZE = 128

LANES = 128
VMEM_LIMIT = 56 * 1024 * 1024

C_U, C_V, C_ZA, C_Q, C_KV, C_ZB, C_G, C_END = 0, 512, 1024, 1536, 2048, 2816, 3328, 3456

TM = 256
TQ = 256
SEGS_PER_PAGE = PAGE_SIZE // CMP_STRIDE


def _lane_iota(shape):
    return lax.broadcasted_iota(jnp.int32, shape, len(shape) - 1)


def _row_iota(shape):
    return lax.broadcasted_iota(jnp.int32, shape, len(shape) - 2)


def _div_pow2(x, n):
    assert n & (n - 1) == 0
    return lax.shift_right_logical(x, int(math.log2(n))) if n > 1 else x


def _mod_pow2(x, n):
    assert n & (n - 1) == 0
    return x & (n - 1)


def _dot(a, b):
    return jnp.dot(a, b, preferred_element_type=F32)


def _dot_nt(a, b):
    return lax.dot_general(a, b, (((1,), (1,)), ((), ())), preferred_element_type=F32)


def _rope(x, cos, sin_signed):
    lo = _mod_pow2(_lane_iota(x.shape), HEAD_DIM) < (HEAD_DIM // 2)
    swapped = jnp.where(lo, pltpu.roll(x, LANES - HEAD_DIM // 2, 1), pltpu.roll(x, HEAD_DIM // 2, 1))
    return x * cos + swapped * sin_signed


def _proj_kernel(x_ref, ng_ref, w_ref, lng_ref, lnb_ref, wmix_ref, bmix_ref, cos_ref, sin_ref,
                 ag_ref, qc_ref, qr_ref, rows_ref, kvw_ref, ksel_ref, kwin_ref, gate_ref, szb_ref,
                 *maybe_vln_ref):
    x = x_ref[...]
    ms = jnp.mean(x * x, axis=-1, keepdims=True)
    hb = ((x * lax.rsqrt(ms + EPS)) * ng_ref[...]).astype(BF16)
    cos = cos_ref[...]
    sin = sin_ref[...]

    def proj(c0, n):
        return _dot(hb, w_ref[:, c0:c0 + n])

    v = jax.nn.gelu(proj(C_V, D_A))
    mu = jnp.mean(v, axis=-1, keepdims=True)
    vc = v - mu
    var = jnp.mean(vc * vc, axis=-1, keepdims=True)
    vln = vc * lax.rsqrt(var + EPS) * lng_ref[...] + lnb_ref[...]
    if maybe_vln_ref:
        maybe_vln_ref[0][...] = vln
    vb = vln.astype(BF16)
    first_head = _lane_iota((x.shape[0], LANES)) < HEAD_DIM
    for gp in range(A_GROUPS // 2):
        sl = slice(gp * LANES, (gp + 1) * LANES)
        blk = vb[:, sl]
        mixed = jnp.where(first_head, _dot(wmix_ref[2 * gp], blk), _dot(wmix_ref[2 * gp + 1], blk))
        mixed = mixed + bmix_ref[:, sl]
        u = jax.nn.gelu(proj(C_U + gp * LANES, LANES))
        za = proj(C_ZA + gp * LANES, LANES)
        ag_ref[:, sl] = (u * mixed * jax.nn.silu(za)).astype(ag_ref.dtype)

    for g in range(GQA):
        sl = slice(g * LANES, (g + 1) * LANES)
        q = proj(C_Q + g * LANES, LANES)
        qc_ref[:, sl] = (q * SCALE).astype(qc_ref.dtype)
        qr_ref[:, sl] = (_rope(q, cos, sin) * SCALE).astype(qr_ref.dtype)

    kc = proj(C_KV, LANES)
    vcm = proj(C_KV + LANES, LANES)
    ks = _rope(proj(C_KV + 2 * LANES, LANES), cos, sin)
    vs = proj(C_KV + 3 * LANES, LANES)
    kw = _rope(proj(C_KV + 4 * LANES, LANES), cos, sin)
    vw = proj(C_KV + 5 * LANES, LANES)
    rows_ref[:, 0 * LANES:1 * LANES] = kc
    rows_ref[:, 1 * LANES:2 * LANES] = vcm
    rows_ref[:, 2 * LANES:3 * LANES] = ks
    rows_ref[:, 3 * LANES:4 * LANES] = vs
    kvw_ref[:, 0:LANES] = kw
    kvw_ref[:, LANES:2 * LANES] = vw
    ksel_ref[:, 0:LANES] = ks.astype(BF16)
    ksel_ref[:, LANES:2 * LANES] = vs.astype(BF16)
    kwin_ref[:, 0:LANES] = kw.astype(BF16)
    kwin_ref[:, LANES:2 * LANES] = vw.astype(BF16)

    gate_ref[...] = jax.nn.sigmoid(proj(C_G, LANES))
    for g in range(GQA):
        sl = slice(g * LANES, (g + 1) * LANES)
        szb_ref[:, sl] = jax.nn.silu(proj(C_ZB + g * LANES, LANES)).astype(szb_ref.dtype)


def _proj_call(x2d, ng, w, lng, lnb, wmix, bmix, cos_t, sin_t, act_dtype, emit_vln):
    n_rows = x2d.shape[0]
    n_tiles = n_rows // TM
    pos_tiles = cos_t.shape[0] // TM
    row = lambda i: (i, 0)
    const2 = lambda i: (0, 0)
    out_shapes = [
        jax.ShapeDtypeStruct((n_rows, D_A), act_dtype),
        jax.ShapeDtypeStruct((n_rows, D_B), act_dtype),
        jax.ShapeDtypeStruct((n_rows, D_B), act_dtype),
        jax.ShapeDtypeStruct((n_rows, 4 * LANES), F32),
        jax.ShapeDtypeStruct((n_rows, 2 * LANES), F32),
        jax.ShapeDtypeStruct((n_rows, 2 * LANES), BF16),
        jax.ShapeDtypeStruct((n_rows, 2 * LANES), BF16),
        jax.ShapeDtypeStruct((n_rows, LANES), F32),
        jax.ShapeDtypeStruct((n_rows, D_B), act_dtype),
    ]
    out_specs = [pl.BlockSpec((TM, s.shape[1]), row) for s in out_shapes]
    if emit_vln:
        out_shapes.append(jax.ShapeDtypeStruct((n_rows, D_A), F32))
        out_specs.append(pl.BlockSpec((TM, D_A), row))
    return pl.pallas_call(
        _proj_kernel,
        grid=(n_tiles,),
        in_specs=[
            pl.BlockSpec((TM, D_MODEL), row),
            pl.BlockSpec((1, D_MODEL), const2),
            pl.BlockSpec((D_MODEL, C_END), const2),
            pl.BlockSpec((1, D_A), const2),
            pl.BlockSpec((1, D_A), const2),
            pl.BlockSpec((A_GROUPS, TM, TM), lambda i: (0, 0, 0)),
            pl.BlockSpec((TM, D_A), const2),
            pl.BlockSpec((TM, LANES), lambda i: (i % pos_tiles, 0)),
            pl.BlockSpec((TM, LANES), lambda i: (i % pos_tiles, 0)),
        ],
        out_specs=out_specs,
        out_shape=out_shapes,
        compiler_params=pltpu.CompilerParams(dimension_semantics=("arbitrary",),
                                             vmem_limit_bytes=VMEM_LIMIT),
        name="proj",
    )(x2d, ng, w, lng, lnb, wmix, bmix, cos_t, sin_t)


def _compress_kernel(pt_ref, src_ref, wc_ref, pe_ref, b1_ref, w2_ref, out_ref, buf, sem, carry,
                     *, pages_per_step):
    n_steps = pl.num_programs(0) * pl.num_programs(1)
    step = pl.program_id(0) * pl.num_programs(1) + pl.program_id(1)
    m = pages_per_step * SEGS_PER_PAGE

    def page_copies(step_idx, slot, i):
        phys = pt_ref[step_idx * pages_per_step + i]
        return [pltpu.make_async_copy(src_ref.at[phys, :, pl.ds(c * LANES, LANES)],
                                      buf.at[slot, c, pl.ds(i * PAGE_SIZE, PAGE_SIZE), :],
                                      sem.at[slot]) for c in range(2)]

    def start_step(step_idx, slot):
        def body(i, carry_):
            for cp in page_copies(step_idx, slot, i):
                cp.start()
            return carry_
        lax.fori_loop(0, pages_per_step, body, 0)

    def wait_step(step_idx, slot):
        def body(i, carry_):
            for cp in page_copies(step_idx, slot, i):
                cp.wait()
            return carry_
        lax.fori_loop(0, pages_per_step, body, 0)

    @pl.when(step == 0)
    def _():
        start_step(0, 0)

    @pl.when(step + 1 < n_steps)
    def _():
        start_step(step + 1, (step + 1) % 2)

    slot = step % 2
    wait_step(step, slot)

    first_of_seq = pl.program_id(1) == 0
    first_row = _row_iota((m, LANES)) == 0
    for c in range(2):
        pieces = [buf[slot, c, pl.ds(j, m, stride=CMP_STRIDE), :].astype(BF16)
                  for j in range(CMP_STRIDE)]
        xs = jnp.concatenate(pieces, axis=1)
        part = _dot(xs, wc_ref[c])
        pe_part = _dot(pe_ref[c], wc_ref[c])
        acc_out = None
        for h in range(N_KV):
            lo = slice(h * 2 * CMP_HIDDEN, h * 2 * CMP_HIDDEN + CMP_HIDDEN)
            hi = slice(h * 2 * CMP_HIDDEN + CMP_HIDDEN, (h + 1) * 2 * CMP_HIDDEN)
            bias = pe_part[0:1, lo] + pe_part[1:2, hi] + b1_ref[c]
            part0 = part[:, lo]
            prev_last = jnp.where(first_of_seq, 0.0, carry[c, h, 0:1, :])
            prev0 = jnp.where(first_row, prev_last, pltpu.roll(part0, 1, 0))
            carry[c, h, 0:1, :] = part0[m - 1:m, :]
            hid = jax.nn.gelu(prev0 + part[:, hi] + bias).astype(BF16)
            contrib = _dot(hid, w2_ref[c, h])
            acc_out = contrib if acc_out is None else acc_out + contrib
        out_ref[0, :, c * LANES:(c + 1) * LANES] = acc_out.astype(out_ref.dtype)


def _compress_call(src3, page_table, wc, pe, b1, w2p, pages_per_step):
    n_seq, n_pages = page_table.shape
    n_chunks = n_pages // pages_per_step
    m = pages_per_step * SEGS_PER_PAGE
    grid_spec = pltpu.PrefetchScalarGridSpec(
        num_scalar_prefetch=1,
        grid=(n_seq, n_chunks),
        in_specs=[
            pl.BlockSpec(memory_space=pl.ANY),
            pl.BlockSpec((2, CMP_STRIDE * LANES, 4 * CMP_HIDDEN), lambda b, k, pt: (0, 0, 0)),
            pl.BlockSpec((2, 16, CMP_STRIDE * LANES), lambda b, k, pt: (0, 0, 0)),
            pl.BlockSpec((2, 1, CMP_HIDDEN), lambda b, k, pt: (0, 0, 0)),
            pl.BlockSpec((2, N_KV, CMP_HIDDEN, LANES), lambda b, k, pt: (0, 0, 0, 0)),
        ],
        out_specs=pl.BlockSpec((1, m, 2 * LANES), lambda b, k, pt: (b, k, 0)),
        scratch_shapes=[
            pltpu.VMEM((2, 2, pages_per_step * PAGE_SIZE, LANES), F32),
            pltpu.SemaphoreType.DMA((2,)),
            pltpu.VMEM((2, N_KV, 8, CMP_HIDDEN), F32),
        ],
    )
    return pl.pallas_call(
        functools.partial(_compress_kernel, pages_per_step=pages_per_step),
        grid_spec=grid_spec,
        out_shape=jax.ShapeDtypeStruct((n_seq, n_pages * SEGS_PER_PAGE, 2 * LANES), BF16),
        compiler_params=pltpu.CompilerParams(dimension_semantics=("arbitrary", "arbitrary"),
                                             vmem_limit_bytes=VMEM_LIMIT),
        name="compress",
    )(page_table.reshape(-1), src3, wc, pe, b1, w2p)


def _softmax_rows(s, mask):
    s = jnp.where(mask, s, NEG)
    e = jnp.where(mask, jnp.exp(s - jnp.max(s, axis=-1, keepdims=True)), 0.0)
    return e / jnp.maximum(jnp.sum(e, axis=-1, keepdims=True), 1.0)


def _dot_split(p, w):
    hi = p.astype(BF16)
    lo = (p - hi.astype(F32)).astype(BF16)
    return _dot(hi, w) + _dot(lo, w)


def _flash_init(m_sc, l_sc, acc_sc):
    m_sc[...] = jnp.full(m_sc.shape, NEG, F32)
    l_sc[...] = jnp.zeros(l_sc.shape, F32)
    acc_sc[...] = jnp.zeros(acc_sc.shape, F32)


def _flash_update(rows, s, v_tile, m_sc, l_sc, acc_sc):
    m_prev = m_sc[rows, :]
    m_new = jnp.maximum(m_prev, jnp.max(s, axis=-1, keepdims=True))
    alpha = jnp.exp(m_prev - m_new)
    p = jnp.exp(s - m_new)
    l_sc[rows, :] = alpha * l_sc[rows, :] + jnp.sum(p, axis=-1, keepdims=True)
    acc_sc[rows, :] = alpha * acc_sc[rows, :] + _dot(p.astype(BF16), v_tile)
    m_sc[rows, :] = m_new


def _gate_col(gates, idx):
    return gates[:, idx:idx + 1]


def _attn_prompt_kernel(qc_ref, qr_ref, cmp_ref, ov_ref, ksel_ref, kwin_ref, gate_ref, szb_ref,
                        out_ref, m_sc, l_sc, acc_sc, comb_sc):
    qi = pl.program_id(1)
    gates = gate_ref[...]
    lane = _lane_iota((TQ, LANES))
    tq = qi * TQ + _row_iota((TQ, LANES))
    r_minus_c = _row_iota((TQ, TQ)) - _lane_iota((TQ, TQ))
    kcmp = cmp_ref[0, :, 0:LANES]
    vcmp = cmp_ref[0, :, LANES:2 * LANES]
    cmp_valid = (lane >= 1) & (lane * CMP_STRIDE + (CMP_STRIDE - 1) <= tq)

    for h in range(N_KV):
        keep = _div_pow2(lane, HEAD_DIM) == h

        def q_rows(ref):
            return jnp.concatenate(
                [jnp.where(keep, ref[:, g * LANES:(g + 1) * LANES], 0).astype(BF16) for g in range(GQA)],
                axis=0)

        qc = q_rows(qc_ref)
        s_cmp = _dot_nt(qc, kcmp)
        p_sum = jnp.zeros((TQ, LANES), F32)
        o_cmp = []
        for g in range(GQA):
            p = _softmax_rows(s_cmp[g * TQ:(g + 1) * TQ], cmp_valid)
            p_sum = p_sum + p
            o_cmp.append(_dot(p.astype(BF16), vcmp))
        imp = _dot_split(p_sum, ov_ref[...])
        tb = _div_pow2(tq, SLC_BLOCK)
        forced = (lane == 0) | (lane == tb) | (lane == tb - 1)
        valid = lane * SLC_BLOCK <= tq
        n_slc = ksel_ref.shape[0] // SLC_BLOCK
        score = jnp.where(forced, FORCE, jnp.where(valid, imp, NEG))
        score = jnp.where(lane < n_slc, score, -3e38)
        rank = jnp.zeros((TQ, LANES), jnp.int32)
        for j in range(n_slc):
            col = score[:, j:j + 1]
            ahead = (col > score) | ((col == score) & (lane > j))
            rank = rank + ahead.astype(jnp.int32)
        selected = jnp.where(rank < TOP_N, 1.0, 0.0).astype(BF16)

        qr = q_rows(qr_ref)

        _flash_init(m_sc, l_sc, acc_sc)

        def sel_tile(kt, carry_):
            k_tile = ksel_ref[pl.ds(kt * TQ, TQ), 0:LANES]
            v_tile = ksel_ref[pl.ds(kt * TQ, TQ), LANES:2 * LANES]
            blocks_per_tile = TQ // SLC_BLOCK
            expand = (_row_iota((LANES, TQ)) == kt * blocks_per_tile + _div_pow2(_lane_iota((LANES, TQ)), SLC_BLOCK))
            chosen = _dot(selected, jnp.where(expand, 1.0, 0.0).astype(BF16))
            allowed = (chosen > 0.5) & (r_minus_c >= (kt - qi) * TQ)
            s_all = _dot_nt(qr, k_tile)
            for g in range(GQA):
                rows = slice(g * TQ, (g + 1) * TQ)
                _flash_update(rows, jnp.where(allowed, s_all[rows], NEG), v_tile, m_sc, l_sc, acc_sc)
            return carry_

        lax.fori_loop(0, qi + 1, sel_tile, 0)
        o_slc = [acc_sc[g * TQ:(g + 1) * TQ, :] / l_sc[g * TQ:(g + 1) * TQ, :] for g in range(GQA)]

        _flash_init(m_sc, l_sc, acc_sc)

        def win_tile(kt, carry_):
            k_tile = kwin_ref[pl.ds(kt * TQ, TQ), 0:LANES]
            v_tile = kwin_ref[pl.ds(kt * TQ, TQ), LANES:2 * LANES]
            dist = (qi - kt) * TQ + r_minus_c
            allowed = (dist >= 0) & (dist < WINDOW)
            s_all = _dot_nt(qr, k_tile)
            for g in range(GQA):
                rows = slice(g * TQ, (g + 1) * TQ)
                _flash_update(rows, jnp.where(allowed, s_all[rows], NEG), v_tile, m_sc, l_sc, acc_sc)
            return carry_

        lax.fori_loop(jnp.maximum(qi - WINDOW // TQ, 0), qi + 1, win_tile, 0)

        for g in range(GQA):
            rows = slice(g * TQ, (g + 1) * TQ)
            base = h * GQA * 3 + g * 3
            o_win = acc_sc[rows, :] / l_sc[rows, :]
            o = (_gate_col(gates, base) * o_cmp[g] + _gate_col(gates, base + 1) * o_slc[g]
                 + _gate_col(gates, base + 2) * o_win)
            sl = slice(g * LANES, (g + 1) * LANES)
            if h == 0:
                comb_sc[:, sl] = o
            else:
                merged = jnp.where(keep, o, comb_sc[:, sl])
                out_ref[:, sl] = (merged * szb_ref[:, sl].astype(F32)).astype(out_ref.dtype)


def _attn_prompt_call(qc, qr, cmp_p, ov, ksel, kwin, gates, szb, n_batch, seq):
    nq = seq // TQ
    tile = lambda b, q: (b * nq + q, 0)
    whole = lambda b, q: (b, 0)
    return pl.pallas_call(
        _attn_prompt_kernel,
        grid=(n_batch, nq),
        in_specs=[
            pl.BlockSpec((TQ, D_B), tile),
            pl.BlockSpec((TQ, D_B), tile),
            pl.BlockSpec((1, LANES, 2 * LANES), lambda b, q: (b, 0, 0)),
            pl.BlockSpec((LANES, LANES), lambda b, q: (0, 0)),
            pl.BlockSpec((seq, 2 * LANES), whole),
            pl.BlockSpec((seq, 2 * LANES), whole),
            pl.BlockSpec((TQ, LANES), tile),
            pl.BlockSpec((TQ, D_B), tile),
        ],
        out_specs=pl.BlockSpec((TQ, D_B), tile),
        out_shape=jax.ShapeDtypeStruct((n_batch * seq, D_B), BF16),
        scratch_shapes=[
            pltpu.VMEM((GQA * TQ, 1), F32),
            pltpu.VMEM((GQA * TQ, 1), F32),
            pltpu.VMEM((GQA * TQ, LANES), F32),
            pltpu.VMEM((TQ, D_B), F32),
        ],
        compiler_params=pltpu.CompilerParams(dimension_semantics=("arbitrary", "arbitrary"),
                                             vmem_limit_bytes=VMEM_LIMIT),
        name="attn_prompt",
    )(qc, qr, cmp_p, ov, ksel, kwin, gates, szb)


def _cmpwin_sample_kernel(qc_ref, qr_ref, cmp_ref, ov_ref, win_ref, kvw_ref, gate_ref,
                          ocw_ref, sel_ref, *, past_len, n_pick):
    t_new = qc_ref.shape[0]
    n_rows = N_KV * GQA * t_new
    gates = gate_ref[...]
    lane = _lane_iota((t_new, LANES))

    def q_rows(ref):
        return jnp.concatenate(
            [jnp.where(_div_pow2(lane, HEAD_DIM) == h, ref[:, g * LANES:(g + 1) * LANES], 0).astype(BF16)
             for h in range(N_KV) for g in range(GQA)], axis=0)

    n_cmp = cmp_ref.shape[1]
    row_t = _mod_pow2(_row_iota((n_rows, n_cmp)), t_new)
    slot = _lane_iota((n_rows, n_cmp))
    cmp_valid = (slot >= 1) & (slot * CMP_STRIDE + (CMP_STRIDE - 1) <= past_len + row_t)
    p = _softmax_rows(_dot_nt(q_rows(qc_ref), cmp_ref[0, :, 0:LANES]), cmp_valid)
    o_cmp = _dot(p.astype(BF16), cmp_ref[0, :, LANES:2 * LANES])

    p_sum = jnp.concatenate(
        [sum(p[(h * GQA + g) * t_new:(h * GQA + g + 1) * t_new] for g in range(GQA)) for h in range(N_KV)],
        axis=0)
    imp = _dot_split(p_sum, ov_ref[...])
    blk = _lane_iota(imp.shape)
    tb = (past_len + t_new - 1) // SLC_BLOCK
    candidate = (blk >= 1) & (blk < tb - 1)
    score = jnp.where(candidate, imp, -1.0)
    out_lane = _lane_iota((N_KV * t_new, LANES))
    picks = jnp.zeros((N_KV * t_new, LANES), jnp.int32)
    for k in range(n_pick):
        best = jnp.max(score, axis=-1, keepdims=True)
        idx = jnp.min(jnp.where(score == best, blk, 1 << 20), axis=-1, keepdims=True)
        picks = jnp.where(out_lane == k, idx, picks)
        score = jnp.where(blk == idx, -2.0, score)
    sel_ref[0] = picks

    qr = q_rows(qr_ref)
    n_buf = win_ref.shape[1]
    k_buf = win_ref[0, :, 0:LANES].astype(BF16)
    v_buf = win_ref[0, :, LANES:2 * LANES].astype(BF16)
    row_tb = _mod_pow2(_row_iota((n_rows, n_buf)), t_new)
    i_buf = _lane_iota((n_rows, n_buf))
    dist = n_buf + row_tb - i_buf
    s_buf = jnp.where((dist >= 0) & (dist < WINDOW), _dot_nt(qr, k_buf), NEG)
    k_new = kvw_ref[:, 0:LANES]
    v_new = kvw_ref[:, LANES:2 * LANES]
    qr32 = qr.astype(F32)
    row_t1 = _mod_pow2(_row_iota((n_rows, 1)), t_new)
    s_new = [jnp.where(row_t1 >= i, jnp.sum(qr32 * k_new[i:i + 1, :].astype(BF16).astype(F32), axis=-1, keepdims=True), NEG)
             for i in range(t_new)]
    m = jnp.max(s_buf, axis=-1, keepdims=True)
    for s in s_new:
        m = jnp.maximum(m, s)
    e_buf = jnp.exp(s_buf - m)
    e_new = [jnp.exp(s - m) for s in s_new]
    denom = jnp.sum(e_buf, axis=-1, keepdims=True) + sum(e_new)
    o_win = _dot(e_buf.astype(BF16), v_buf)
    for i in range(t_new):
        o_win = o_win + e_new[i].astype(BF16).astype(F32) * v_new[i:i + 1, :].astype(BF16).astype(F32)
    o_win = o_win / denom

    for g in range(GQA):
        parts = []
        for h in range(N_KV):
            rows = slice((h * GQA + g) * t_new, (h * GQA + g + 1) * t_new)
            base = h * GQA * 3 + g * 3
            parts.append(_gate_col(gates, base) * o_cmp[rows] + _gate_col(gates, base + 2) * o_win[rows])
        ocw_ref[:, g * LANES:(g + 1) * LANES] = jnp.where(lane < HEAD_DIM, parts[0], parts[1])


def _cmpwin_sample_call(qc, qr, cmp_s, ov, win, kvw, gates, t_new, past_len, n_pick):
    n_seq = cmp_s.shape[0]
    tile = lambda b: (b, 0)
    return pl.pallas_call(
        functools.partial(_cmpwin_sample_kernel, past_len=past_len, n_pick=n_pick),
        grid=(n_seq,),
        in_specs=[
            pl.BlockSpec((t_new, D_B), tile),
            pl.BlockSpec((t_new, D_B), tile),
            pl.BlockSpec((1, cmp_s.shape[1], 2 * LANES), lambda b: (b, 0, 0)),
            pl.BlockSpec(ov.shape, lambda b: (0, 0)),
            pl.BlockSpec((1, win.shape[1], 2 * LANES), lambda b: (b, 0, 0)),
            pl.BlockSpec((t_new, 2 * LANES), tile),
            pl.BlockSpec((t_new, LANES), tile),
        ],
        out_specs=[
            pl.BlockSpec((t_new, D_B), tile),
            pl.BlockSpec((1, N_KV * t_new, LANES), lambda b: (b, 0, 0)),
        ],
        out_shape=[
            jax.ShapeDtypeStruct((n_seq * t_new, D_B), F32),
            jax.ShapeDtypeStruct((n_seq, N_KV * t_new, LANES), jnp.int32),
        ],
        compiler_params=pltpu.CompilerParams(dimension_semantics=("arbitrary",),
                                             vmem_limit_bytes=VMEM_LIMIT),
        name="cmpwin_sample",
    )(qc, qr, cmp_s, ov, win, kvw, gates)


def _sel_sample_kernel(sel_ref, pt_ref, cache_ref, qr_ref, rows_ref, gate_ref, ocw_ref, szb_ref,
                       out_ref, buf, sem, *, n_pick, n_pages, tb):
    t_new = qr_ref.shape[0]
    n_own = t_new * n_pick
    n_blocks = 2 + n_own
    n_steps = pl.num_programs(0) * pl.num_programs(1)
    b = pl.program_id(0)
    h = pl.program_id(1)
    step = b * pl.num_programs(1) + h
    pages_per_block = PAGE_SIZE // SLC_BLOCK

    def block_copy(step_idx, slot, i, blk):
        seq = _div_pow2(step_idx, N_KV)
        phys = pt_ref[seq * n_pages + _div_pow2(blk, pages_per_block)]
        off = _mod_pow2(blk, pages_per_block) * SLC_BLOCK
        return pltpu.make_async_copy(cache_ref.at[phys, pl.ds(off, SLC_BLOCK), pl.ds(2 * LANES, 2 * LANES)],
                                     buf.at[slot, i], sem.at[slot])

    def for_all_blocks(step_idx, slot, fn):
        fn(block_copy(step_idx, slot, 0, 0))
        fn(block_copy(step_idx, slot, 1, tb - 1))

        def body(i, carry_):
            fn(block_copy(step_idx, slot, 2 + i, sel_ref[step_idx * n_own + i]))
            return carry_
        lax.fori_loop(0, n_own, body, 0)

    @pl.when(step == 0)
    def _():
        for_all_blocks(0, 0, lambda cp: cp.start())

    @pl.when(step + 1 < n_steps)
    def _():
        for_all_blocks(step + 1, (step + 1) % 2, lambda cp: cp.start())

    slot = step % 2
    for_all_blocks(step, slot, lambda cp: cp.wait())

    n_rows = GQA * t_new
    lane = _lane_iota((t_new, LANES))
    keep = _div_pow2(lane, HEAD_DIM) == h
    qr = jnp.concatenate([jnp.where(keep, qr_ref[:, g * LANES:(g + 1) * LANES], 0.0) for g in range(GQA)],
                         axis=0)
    qrb = qr.astype(BF16)
    row_t = _mod_pow2(_row_iota((n_rows, 1)), t_new)

    k_sh = buf[slot, 0:2, :, 0:LANES].reshape(2 * SLC_BLOCK, LANES).astype(BF16)
    v_sh = buf[slot, 0:2, :, LANES:2 * LANES].reshape(2 * SLC_BLOCK, LANES).astype(BF16)
    s_sh = _dot_nt(qrb, k_sh)
    k_new = rows_ref[:, 2 * LANES:3 * LANES]
    v_new = rows_ref[:, 3 * LANES:4 * LANES]
    qr32 = qrb.astype(F32)
    s_new = [jnp.where(row_t >= i, jnp.sum(qr32 * k_new[i:i + 1, :].astype(BF16).astype(F32), axis=-1, keepdims=True), NEG)
             for i in range(t_new)]
    own_keys = n_pick * SLC_BLOCK
    s_own = jnp.zeros((n_rows, own_keys), F32)
    v_own = []
    for t in range(t_new):
        kv_t = buf[slot, 2 + t * n_pick:2 + (t + 1) * n_pick].reshape(own_keys, 2 * LANES)
        v_own.append(kv_t[:, LANES:2 * LANES].astype(BF16))
        s_own = jnp.where(row_t == t, _dot_nt(qrb, kv_t[:, 0:LANES].astype(BF16)), s_own)

    m = jnp.maximum(jnp.max(s_sh, axis=-1, keepdims=True), jnp.max(s_own, axis=-1, keepdims=True))
    for s in s_new:
        m = jnp.maximum(m, s)
    e_sh = jnp.exp(s_sh - m)
    e_own = jnp.exp(s_own - m)
    e_new = [jnp.exp(s - m) for s in s_new]
    denom = jnp.sum(e_sh, axis=-1, keepdims=True) + jnp.sum(e_own, axis=-1, keepdims=True) + sum(e_new)
    o = _dot(e_sh.astype(BF16), v_sh)
    for t in range(t_new):
        o = o + _dot(jnp.where(row_t == t, e_own, 0.0).astype(BF16), v_own[t])
    for i in range(t_new):
        o = o + e_new[i].astype(BF16).astype(F32) * v_new[i:i + 1, :].astype(BF16).astype(F32)
    o = o / denom

    gates = gate_ref[...]
    for g in range(GQA):
        sl = slice(g * LANES, (g + 1) * LANES)
        gate = jnp.sum(jnp.where(lane == h * GQA * 3 + g * 3 + 1, gates, 0.0), axis=-1, keepdims=True)
        contrib = jnp.where(keep, gate * o[g * t_new:(g + 1) * t_new], 0.0)

        @pl.when(h == 0)
        def _():
            out_ref[:, sl] = ocw_ref[:, sl] + contrib

        @pl.when(h == N_KV - 1)
        def _():
            out_ref[:, sl] = (out_ref[:, sl] + contrib) * szb_ref[:, sl]


def _sel_sample_call(sel_flat, pt_flat, cache3, qr, rows, gates, ocw, szb, n_seq, t_new, n_pick, n_pages, tb):
    tile = lambda b, h, sel, pt: (b, 0)
    n_blocks = 2 + t_new * n_pick
    grid_spec = pltpu.PrefetchScalarGridSpec(
        num_scalar_prefetch=2,
        grid=(n_seq, N_KV),
        in_specs=[
            pl.BlockSpec(memory_space=pl.ANY),
            pl.BlockSpec((t_new, D_B), tile),
            pl.BlockSpec((t_new, 4 * LANES), tile),
            pl.BlockSpec((t_new, LANES), tile),
            pl.BlockSpec((t_new, D_B), tile),
            pl.BlockSpec((t_new, D_B), tile),
        ],
        out_specs=pl.BlockSpec((t_new, D_B), tile),
        scratch_shapes=[
            pltpu.VMEM((2, n_blocks, SLC_BLOCK, 2 * LANES), F32),
            pltpu.SemaphoreType.DMA((2,)),
        ],
    )
    return pl.pallas_call(
        functools.partial(_sel_sample_kernel, n_pick=n_pick, n_pages=n_pages, tb=tb),
        grid_spec=grid_spec,
        out_shape=jax.ShapeDtypeStruct((n_seq * t_new, D_B), F32),
        compiler_params=pltpu.CompilerParams(dimension_semantics=("arbitrary", "arbitrary"),
                                             vmem_limit_bytes=VMEM_LIMIT),
        name="sel_sample",
    )(sel_flat, pt_flat, cache3, qr, rows, gates, ocw, szb)


def _merge_kernel(x_ref, a_ref, b_ref, wa_ref, wb_ref, fg_ref, y_ref):
    delta = _dot(a_ref[...].astype(BF16), wa_ref[...]) + _dot(b_ref[...].astype(BF16), wb_ref[...])
    x = x_ref[...] + delta
    ms = jnp.mean(x * x, axis=-1, keepdims=True)
    y_ref[...] = (x * lax.rsqrt(ms + EPS)) * fg_ref[...]


def _merge_call(x2d, a, b, wa, wb, fg):
    n_rows = x2d.shape[0]
    row = lambda i: (i, 0)
    const2 = lambda i: (0, 0)
    return pl.pallas_call(
        _merge_kernel,
        grid=(n_rows // TM,),
        in_specs=[
            pl.BlockSpec((TM, D_MODEL), row),
            pl.BlockSpec((TM, D_A), row),
            pl.BlockSpec((TM, D_B), row),
            pl.BlockSpec((D_A, D_MODEL), const2),
            pl.BlockSpec((D_B, D_MODEL), const2),
            pl.BlockSpec((1, D_MODEL), const2),
        ],
        out_specs=pl.BlockSpec((TM, D_MODEL), row),
        out_shape=jax.ShapeDtypeStruct((n_rows, D_MODEL), F32),
        compiler_params=pltpu.CompilerParams(dimension_semantics=("arbitrary",),
                                             vmem_limit_bytes=VMEM_LIMIT),
        name="merge",
    )(x2d, a, b, wa, wb, fg)


def _head_pair_perm(w_cols):
    lead = w_cols.shape[:-1]
    return w_cols.reshape(*lead, N_KV, GQA, HEAD_DIM).swapaxes(-3, -2).reshape(*lead, D_B)


def _prep_w_in(w_in):
    cuts = np.cumsum([D_A, D_A, D_A, D_B, 6 * N_KV * HEAD_DIM, 3 * N_HEADS]).tolist()
    u, v, za, q, kv, g, zb = jnp.split(w_in, cuts, axis=-1)
    g_pad = jnp.pad(g, ((0, 0), (0, LANES - g.shape[1])))
    return jnp.concatenate([u, v, za, _head_pair_perm(q), kv, _head_pair_perm(zb), g_pad], axis=-1).astype(BF16)


def _prep_mix(w_s, b_s, chunk_len):
    tril = w_s[:, :chunk_len, :chunk_len] * jnp.tril(jnp.ones((chunk_len, chunk_len), w_s.dtype))
    reps = TM // chunk_len
    eye = jnp.eye(reps, dtype=w_s.dtype)
    wmix = jnp.einsum("ab,gts->gatbs", eye, tril).reshape(A_GROUPS, TM, TM).astype(BF16)
    bias = jnp.repeat(b_s[:, :chunk_len].T, HEAD_DIM, axis=1)
    return wmix, jnp.tile(bias, (reps, 1))


def _rope_tables(pos):
    half = HEAD_DIM // 2
    inv = ROPE_THETA ** (-jnp.arange(half, dtype=F32) / half)
    ang = pos.astype(F32)[:, None] * inv
    cos = jnp.tile(jnp.cos(ang), (1, LANES // half))
    sin = jnp.tile(jnp.concatenate([-jnp.sin(ang), jnp.sin(ang)], axis=1), (1, LANES // HEAD_DIM))
    return cos, sin


def _prep_compress(cmp_pos, w_cmp1, b_cmp1, w_cmp2):
    ratio = CMP_BLOCK // CMP_STRIDE
    w1 = w_cmp1.reshape(2, ratio, CMP_STRIDE, HEAD_DIM, CMP_HIDDEN).transpose(0, 2, 3, 1, 4)
    w1 = w1.reshape(2, CMP_STRIDE, HEAD_DIM, ratio * CMP_HIDDEN)
    zeros = jnp.zeros_like(w1)
    wc = jnp.stack([jnp.concatenate([w1, zeros], axis=-1), jnp.concatenate([zeros, w1], axis=-1)], axis=2)
    wc = wc.reshape(2, CMP_STRIDE * LANES, N_KV * ratio * CMP_HIDDEN).astype(BF16)
    pe = cmp_pos.reshape(2, ratio, CMP_STRIDE, 1, HEAD_DIM)
    pe = jnp.broadcast_to(pe, (2, ratio, CMP_STRIDE, N_KV, HEAD_DIM)).reshape(2, ratio, CMP_STRIDE * LANES)
    pe = jnp.pad(pe, ((0, 0), (0, 16 - ratio), (0, 0))).astype(BF16)
    w2 = w_cmp2[:, None]
    zeros2 = jnp.zeros_like(w2)
    w2p = jnp.concatenate([jnp.concatenate([w2, zeros2], axis=-1), jnp.concatenate([zeros2, w2], axis=-1)],
                          axis=1).astype(BF16)
    return wc, pe, b_cmp1.reshape(2, 1, CMP_HIDDEN), w2p


def _overlap_matrix(n_cmp_slots, n_slc, n_lanes):
    start = (np.arange(n_cmp_slots)[:, None] - 1) * CMP_STRIDE
    j = np.arange(n_lanes)[None, :]
    ov = (start <= j * SLC_BLOCK + SLC_BLOCK - 1) & (start + CMP_BLOCK - 1 >= j * SLC_BLOCK)
    ov &= (np.arange(n_cmp_slots)[:, None] >= 1) & (j < n_slc)
    return jnp.asarray(ov, dtype=BF16)


def kernel(x_prompt, x_sample, cache_kv, state_win, page_table, norm_g, w_in, ln_g, ln_b, w_s, b_s,
           cmp_pos, w_cmp1, b_cmp1, w_cmp2, w_out, final_g):
    n_batch, seq, _ = x_prompt.shape
    n_seq, t_new, _ = x_sample.shape
    depth, n_phys = cache_kv.shape[:2]
    n_pages = page_table.shape[1]
    past_len = n_pages * PAGE_SIZE
    win_buf = state_win.shape[2]
    assert depth == 1 and seq % TQ == 0 and n_seq * t_new == TM and win_buf == WINDOW
    assert (past_len + t_new - 1) // SLC_BLOCK == past_len // SLC_BLOCK and past_len % CMP_STRIDE == 0

    w_all = _prep_w_in(w_in[0])
    ng = norm_g[0].reshape(1, D_MODEL)
    lng = ln_g[0].reshape(1, D_A)
    lnb = ln_b[0].reshape(1, D_A)
    fg = final_g.reshape(1, D_MODEL)
    wo = w_out[0]
    wo_a = wo[:D_A].astype(BF16)
    wo_b = _head_pair_perm(wo[D_A:].T).T.astype(BF16)
    wc, pe, b1, w2p = _prep_compress(cmp_pos[0], w_cmp1[0], b_cmp1[0], w_cmp2[0])

    xp = x_prompt.reshape(n_batch * seq, D_MODEL)
    wmix_p, bmix_p = _prep_mix(w_s[0], b_s[0], CHUNK)
    cos_p, sin_p = _rope_tables(jnp.arange(seq))
    ag_p, qc_p, qr_p, rows_p, kvw_p, ksel_p, kwin_p, gate_p, szb_p = _proj_call(
        xp, ng, w_all, lng, lnb, wmix_p, bmix_p, cos_p, sin_p, BF16, False)
    pages_p = seq // PAGE_SIZE
    ident = jnp.arange(n_batch * pages_p, dtype=jnp.int32).reshape(n_batch, pages_p)
    cmp_p = _compress_call(rows_p.reshape(n_batch * pages_p, PAGE_SIZE, 4 * LANES), ident, wc, pe, b1, w2p, pages_p)
    ov_p = _overlap_matrix(seq // CMP_STRIDE, seq // SLC_BLOCK, LANES)
    bg_p = _attn_prompt_call(qc_p, qr_p, cmp_p, ov_p, ksel_p, kwin_p, gate_p, szb_p, n_batch, seq)
    y_p = _merge_call(xp, ag_p, bg_p, wo_a, wo_b, fg)

    xs = x_sample.reshape(n_seq * t_new, D_MODEL)
    wmix_s, bmix_s = _prep_mix(w_s[0], b_s[0], t_new)
    cos_s, sin_s = _rope_tables(jnp.tile(past_len + jnp.arange(t_new), n_seq))
    ag_s, qc_s, qr_s, rows_s, kvw_s, _, _, gate_s, szb_s, vln_s = _proj_call(
        xs, ng, w_all, lng, lnb, wmix_s, bmix_s, cos_s, sin_s, F32, True)
    cache3 = cache_kv[0].reshape(n_phys, PAGE_SIZE, 4 * LANES)
    cmp_s = _compress_call(cache3, page_table, wc, pe, b1, w2p, 32)
    n_slc_s = -(-(past_len + t_new) // SLC_BLOCK)
    tb = past_len // SLC_BLOCK
    n_pick = TOP_N - 3
    ov_s = _overlap_matrix(past_len // CMP_STRIDE, n_slc_s, -(-n_slc_s // LANES) * LANES)
    win_prev = state_win[0].reshape(n_seq, win_buf, 2 * LANES)
    ocw_s, sel_s = _cmpwin_sample_call(qc_s, qr_s, cmp_s, ov_s, win_prev, kvw_s, gate_s, t_new, past_len, n_pick)
    sel_flat = sel_s[:, :, :n_pick].reshape(-1)
    bg_s = _sel_sample_call(sel_flat, page_table.reshape(-1), cache3, qr_s, rows_s, gate_s, ocw_s, szb_s,
                            n_seq, t_new, n_pick, n_pages, tb)
    y_s = _merge_call(xs, ag_s, bg_s, wo_a, wo_b, fg)

    new_win_p = kvw_p.reshape(n_batch, seq, 2, N_KV, HEAD_DIM)[:, seq - win_buf:]
    new_win_s = jnp.concatenate([state_win[0][:, t_new:], kvw_s.reshape(n_seq, t_new, 2, N_KV, HEAD_DIM)], axis=1)
    return (y_p.reshape(n_batch, seq, D_MODEL),
            y_s.reshape(n_seq, t_new, D_MODEL),
            rows_p.reshape(1, n_batch, seq, 4, N_KV, HEAD_DIM),
            new_win_p[None],
            rows_s.reshape(1, n_seq, t_new, 4, N_KV, HEAD_DIM),
            new_win_s[None],
            vln_s.reshape(1, n_seq, t_new, D_A))
```

```python
import functools
import math

import jax
import jax.numpy as jnp
import numpy as np
from jax import lax
from jax.experimental import pallas as pl
from jax.experimental.pallas import tpu as pltpu

F32 = jnp.float32
BF16 = jnp.bfloat16

D_MODEL = 1024
HEAD_DIM = 64
D_A = 512
A_GROUPS = 8
CHUNK = 128
N_HEADS = 8
D_B = 512
N_KV = 2
GQA = 4
CMP_BLOCK = 32
CMP_STRIDE = 16
CMP_HIDDEN = 128
SLC_BLOCK = 64
TOP_N = 16
WINDOW = 512
ROPE_THETA = 10000.0
EPS = 1e-6
SCALE = HEAD_DIM ** -0.5
NEG = -1e30
FORCE = 1e9
PAGE_SIZE = 128

LANES = 128
VMEM_LIMIT = 56 * 1024 * 1024

C_U, C_V, C_ZA, C_Q, C_KV, C_ZB, C_G, C_END = 0, 512, 1024, 1536, 2048, 2816, 3328, 3456

TM = 256
TQ = 256
SEGS_PER_PAGE = PAGE_SIZE // CMP_STRIDE


def _lane_iota(shape):
    return lax.broadcasted_iota(jnp.int32, shape, len(shape) - 1)


def _row_iota(shape):
    return lax.broadcasted_iota(jnp.int32, shape, len(shape) - 2)


def _div_pow2(x, n):
    assert n & (n - 1) == 0
    return lax.shift_right_logical(x, int(math.log2(n))) if n > 1 else x


def _mod_pow2(x, n):
    assert n & (n - 1) == 0
    return x & (n - 1)


def _dot(a, b):
    return jnp.dot(a, b, preferred_element_type=F32)


def _dot_nt(a, b):
    return lax.dot_general(a, b, (((1,), (1,)), ((), ())), preferred_element_type=F32)


def _rope(x, cos, sin_signed):
    lo = _mod_pow2(_lane_iota(x.shape), HEAD_DIM) < (HEAD_DIM // 2)
    swapped = jnp.where(lo, pltpu.roll(x, LANES - HEAD_DIM // 2, 1), pltpu.roll(x, HEAD_DIM // 2, 1))
    return x * cos + swapped * sin_signed


def _proj_kernel(x_ref, ng_ref, w_ref, lng_ref, lnb_ref, wmix_ref, bmix_ref, cos_ref, sin_ref,
                 ag_ref, qc_ref, qr_ref, rows_ref, kvw_ref, ksel_ref, kwin_ref, gate_ref, szb_ref,
                 *maybe_vln_ref):
    x = x_ref[...]
    ms = jnp.mean(x * x, axis=-1, keepdims=True)
    hb = ((x * lax.rsqrt(ms + EPS)) * ng_ref[...]).astype(BF16)
    cos = cos_ref[...]
    sin = sin_ref[...]

    def proj(c0, n):
        return _dot(hb, w_ref[:, c0:c0 + n])

    v = jax.nn.gelu(proj(C_V, D_A))
    mu = jnp.mean(v, axis=-1, keepdims=True)
    vc = v - mu
    var = jnp.mean(vc * vc, axis=-1, keepdims=True)
    vln = vc * lax.rsqrt(var + EPS) * lng_ref[...] + lnb_ref[...]
    if maybe_vln_ref:
        maybe_vln_ref[0][...] = vln
    vb = vln.astype(BF16)
    first_head = _lane_iota((x.shape[0], LANES)) < HEAD_DIM
    for gp in range(A_GROUPS // 2):
        sl = slice(gp * LANES, (gp + 1) * LANES)
        blk = vb[:, sl]
        mixed = jnp.where(first_head, _dot(wmix_ref[2 * gp], blk), _dot(wmix_ref[2 * gp + 1], blk))
        mixed = mixed + bmix_ref[:, sl]
        u = jax.nn.gelu(proj(C_U + gp * LANES, LANES))
        za = proj(C_ZA + gp * LANES, LANES)
        ag_ref[:, sl] = (u * mixed * jax.nn.silu(za)).astype(ag_ref.dtype)

    for g in range(GQA):
        sl = slice(g * LANES, (g + 1) * LANES)
        q = proj(C_Q + g * LANES, LANES)
        qc_ref[:, sl] = (q * SCALE).astype(qc_ref.dtype)
        qr_ref[:, sl] = (_rope(q, cos, sin) * SCALE).astype(qr_ref.dtype)

    kc = proj(C_KV, LANES)
    vcm = proj(C_KV + LANES, LANES)
    ks = _rope(proj(C_KV + 2 * LANES, LANES), cos, sin)
    vs = proj(C_KV + 3 * LANES, LANES)
    kw = _rope(proj(C_KV + 4 * LANES, LANES), cos, sin)
    vw = proj(C_KV + 5 * LANES, LANES)
    rows_ref[:, 0 * LANES:1 * LANES] = kc
    rows_ref[:, 1 * LANES:2 * LANES] = vcm
    rows_ref[:, 2 * LANES:3 * LANES] = ks
    rows_ref[:, 3 * LANES:4 * LANES] = vs
    kvw_ref[:, 0:LANES] = kw
    kvw_ref[:, LANES:2 * LANES] = vw
    ksel_ref[:, 0:LANES] = ks.astype(BF16)
    ksel_ref[:, LANES:2 * LANES] = vs.astype(BF16)
    kwin_ref[:, 0:LANES] = kw.astype(BF16)
    kwin_ref[:, LANES:2 * LANES] = vw.astype(BF16)

    gate_ref[...] = jax.nn.sigmoid(proj(C_G, LANES))
    for g in range(GQA):
        sl = slice(g * LANES, (g + 1) * LANES)
        szb_ref[:, sl] = jax.nn.silu(proj(C_ZB + g * LANES, LANES)).astype(szb_ref.dtype)


def _proj_call(x2d, ng, w, lng, lnb, wmix, bmix, cos_t, sin_t, act_dtype, emit_vln):
    n_rows = x2d.shape[0]
    n_tiles = n_rows // TM
    pos_tiles = cos_t.shape[0] // TM
    row = lambda i: (i, 0)
    const2 = lambda i: (0, 0)
    out_shapes = [
        jax.ShapeDtypeStruct((n_rows, D_A), act_dtype),
        jax.ShapeDtypeStruct((n_rows, D_B), act_dtype),
        jax.ShapeDtypeStruct((n_rows, D_B), act_dtype),
        jax.ShapeDtypeStruct((n_rows, 4 * LANES), F32),
        jax.ShapeDtypeStruct((n_rows, 2 * LANES), F32),
        jax.ShapeDtypeStruct((n_rows, 2 * LANES), BF16),
        jax.ShapeDtypeStruct((n_rows, 2 * LANES), BF16),
        jax.ShapeDtypeStruct((n_rows, LANES), F32),
        jax.ShapeDtypeStruct((n_rows, D_B), act_dtype),
    ]
    out_specs = [pl.BlockSpec((TM, s.shape[1]), row) for s in out_shapes]
    if emit_vln:
        out_shapes.append(jax.ShapeDtypeStruct((n_rows, D_A), F32))
        out_specs.append(pl.BlockSpec((TM, D_A), row))
    return pl.pallas_call(
        _proj_kernel,
        grid=(n_tiles,),
        in_specs=[
            pl.BlockSpec((TM, D_MODEL), row),
            pl.BlockSpec((1, D_MODEL), const2),
            pl.BlockSpec((D_MODEL, C_END), const2),
            pl.BlockSpec((1, D_A), const2),
            pl.BlockSpec((1, D_A), const2),
            pl.BlockSpec((A_GROUPS, TM, TM), lambda i: (0, 0, 0)),
            pl.BlockSpec((TM, D_A), const2),
            pl.BlockSpec((TM, LANES), lambda i: (i % pos_tiles, 0)),
            pl.BlockSpec((TM, LANES), lambda i: (i % pos_tiles, 0)),
        ],
        out_specs=out_specs,
        out_shape=out_shapes,
        compiler_params=pltpu.CompilerParams(dimension_semantics=("arbitrary",),
                                             vmem_limit_bytes=VMEM_LIMIT),
        name="proj",
    )(x2d, ng, w, lng, lnb, wmix, bmix, cos_t, sin_t)


def _compress_kernel(pt_ref, src_ref, wc_ref, pe_ref, b1_ref, w2_ref, out_ref, buf, sem, carry,
                     *maybe_stage, pages_per_step):
    transposed_src = bool(maybe_stage)
    n_steps = pl.num_programs(0) * pl.num_programs(1)
    step = pl.program_id(0) * pl.num_programs(1) + pl.program_id(1)
    m = pages_per_step * SEGS_PER_PAGE

    def page_copies(step_idx, slot, i):
        phys = pt_ref[step_idx * pages_per_step + i]
        if transposed_src:
            return [pltpu.make_async_copy(src_ref.at[phys, pl.ds(0, 2)], maybe_stage[0].at[slot, i], sem.at[slot])]
        return [pltpu.make_async_copy(src_ref.at[phys, :, pl.ds(c * LANES, LANES)],
                                      buf.at[slot, c, pl.ds(i * PAGE_SIZE, PAGE_SIZE), :],
                                      sem.at[slot]) for c in range(2)]

    def start_step(step_idx, slot):
        def body(i, carry_):
            for cp in page_copies(step_idx, slot, i):
                cp.start()
            return carry_
        lax.fori_loop(0, pages_per_step, body, 0)

    def wait_step(step_idx, slot):
        def body(i, carry_):
            for cp in page_copies(step_idx, slot, i):
                cp.wait()
            return carry_
        lax.fori_loop(0, pages_per_step, body, 0)

    @pl.when(step == 0)
    def _():
        start_step(0, 0)

    @pl.when(step + 1 < n_steps)
    def _():
        start_step(step + 1, (step + 1) % 2)

    slot = step % 2
    wait_step(step, slot)

    if transposed_src:
        stage = maybe_stage[0]

        def transpose_page(i, carry_):
            for c in range(2):
                slab = stage[slot, i, c].reshape(N_KV * HEAD_DIM, PAGE_SIZE)
                buf[0, c, pl.ds(i * PAGE_SIZE, PAGE_SIZE), :] = slab.T
            return carry_
        lax.fori_loop(0, pages_per_step, transpose_page, 0)
        rows_slot = 0
    else:
        rows_slot = slot

    first_of_seq = pl.program_id(1) == 0
    first_row = _row_iota((m, LANES)) == 0
    for c in range(2):
        pieces = [buf[rows_slot, c, pl.ds(j, m, stride=CMP_STRIDE), :].astype(BF16)
                  for j in range(CMP_STRIDE)]
        xs = jnp.concatenate(pieces, axis=1)
        part = _dot(xs, wc_ref[c])
        pe_part = _dot(pe_ref[c], wc_ref[c])
        acc_out = None
        for h in range(N_KV):
            lo = slice(h * 2 * CMP_HIDDEN, h * 2 * CMP_HIDDEN + CMP_HIDDEN)
            hi = slice(h * 2 * CMP_HIDDEN + CMP_HIDDEN, (h + 1) * 2 * CMP_HIDDEN)
            bias = pe_part[0:1, lo] + pe_part[1:2, hi] + b1_ref[c]
            part0 = part[:, lo]
            prev_last = jnp.where(first_of_seq, 0.0, carry[c, h, 0:1, :])
            prev0 = jnp.where(first_row, prev_last, pltpu.roll(part0, 1, 0))
            carry[c, h, 0:1, :] = part0[m - 1:m, :]
            hid = jax.nn.gelu(prev0 + part[:, hi] + bias).astype(BF16)
            contrib = _dot(hid, w2_ref[c, h])
            acc_out = contrib if acc_out is None else acc_out + contrib
        out_ref[0, :, c * LANES:(c + 1) * LANES] = acc_out.astype(out_ref.dtype)


def _compress_call(src, page_table, wc, pe, b1, w2p, pages_per_step, transposed_src):
    n_seq, n_pages = page_table.shape
    n_chunks = n_pages // pages_per_step
    m = pages_per_step * SEGS_PER_PAGE
    rows_slots = 1 if transposed_src else 2
    scratch = [
        pltpu.VMEM((rows_slots, 2, pages_per_step * PAGE_SIZE, LANES), F32),
        pltpu.SemaphoreType.DMA((2,)),
        pltpu.VMEM((2, N_KV, 8, CMP_HIDDEN), F32),
    ]
    if transposed_src:
        scratch.append(pltpu.VMEM((2, pages_per_step, 2, N_KV, HEAD_DIM, PAGE_SIZE), F32))
    grid_spec = pltpu.PrefetchScalarGridSpec(
        num_scalar_prefetch=1,
        grid=(n_seq, n_chunks),
        in_specs=[
            pl.BlockSpec(memory_space=pl.ANY),
            pl.BlockSpec((2, CMP_STRIDE * LANES, 4 * CMP_HIDDEN), lambda b, k, pt: (0, 0, 0)),
            pl.BlockSpec((2, 16, CMP_STRIDE * LANES), lambda b, k, pt: (0, 0, 0)),
            pl.BlockSpec((2, 1, CMP_HIDDEN), lambda b, k, pt: (0, 0, 0)),
            pl.BlockSpec((2, N_KV, CMP_HIDDEN, LANES), lambda b, k, pt: (0, 0, 0, 0)),
        ],
        out_specs=pl.BlockSpec((1, m, 2 * LANES), lambda b, k, pt: (b, k, 0)),
        scratch_shapes=scratch,
    )
    return pl.pallas_call(
        functools.partial(_compress_kernel, pages_per_step=pages_per_step),
        grid_spec=grid_spec,
        out_shape=jax.ShapeDtypeStruct((n_seq, n_pages * SEGS_PER_PAGE, 2 * LANES), BF16),
        compiler_params=pltpu.CompilerParams(dimension_semantics=("arbitrary", "arbitrary"),
                                             vmem_limit_bytes=VMEM_LIMIT),
        name="compress",
    )(page_table.reshape(-1), src, wc, pe, b1, w2p)


def _softmax_rows(s, mask):
    s = jnp.where(mask, s, NEG)
    e = jnp.where(mask, jnp.exp(s - jnp.max(s, axis=-1, keepdims=True)), 0.0)
    return e / jnp.maximum(jnp.sum(e, axis=-1, keepdims=True), 1.0)


def _dot_split(p, w):
    hi = p.astype(BF16)
    lo = (p - hi.astype(F32)).astype(BF16)
    return _dot(hi, w) + _dot(lo, w)


def _gate_col(gates, idx):
    return gates[:, idx:idx + 1]


def _attn_prompt_kernel(qc_ref, qr_ref, cmp_ref, ov_ref, ksel_ref, kwin_ref, gate_ref, szb_ref,
                        out_ref, q_sc, m_sc, acc_sc, comb_sc):
    qi = pl.program_id(1)
    gates = gate_ref[...]
    lane = _lane_iota((TQ, LANES))
    tq = qi * TQ + _row_iota((TQ, LANES))
    r_minus_c = _row_iota((TQ, TQ)) - _lane_iota((TQ, TQ))
    kcmp = cmp_ref[0, :, 0:LANES]
    vcmp = cmp_ref[0, :, LANES:2 * LANES]
    cmp_valid = (lane >= 1) & (lane * CMP_STRIDE + (CMP_STRIDE - 1) <= tq)
    n_slc = ksel_ref.shape[0] // SLC_BLOCK
    blocks_per_tile = TQ // SLC_BLOCK

    for h in range(N_KV):
        keep = _div_pow2(lane, HEAD_DIM) == h

        p_sum = jnp.zeros((TQ, LANES), F32)
        o_cmp = []
        for g in range(GQA):
            sl = slice(g * LANES, (g + 1) * LANES)
            qc = jnp.where(keep, qc_ref[:, sl], 0)
            p = _softmax_rows(_dot_nt(qc, kcmp), cmp_valid)
            p_sum = p_sum + p
            o_cmp.append(_dot(p.astype(BF16), vcmp))
            q_sc[g * TQ:(g + 1) * TQ, :] = jnp.where(keep, qr_ref[:, sl], 0)
        imp = _dot_split(p_sum, ov_ref[...])
        tb = _div_pow2(tq, SLC_BLOCK)
        forced = (lane == 0) | (lane == tb) | (lane == tb - 1)
        valid = lane * SLC_BLOCK <= tq
        score = jnp.where(forced, FORCE, jnp.where(valid, imp, NEG))
        score_t = score.T[0:n_slc, :]
        blk = _row_iota((n_slc, TQ))
        rank = jnp.zeros((n_slc, TQ), jnp.int32)
        for j in range(n_slc):
            other = score_t[j:j + 1, :]
            ahead = (other > score_t) | ((other == score_t) & (blk > j))
            rank = rank + ahead.astype(jnp.int32)
        sel_t = jnp.where(rank < TOP_N, 1.0, 0.0)
        selected = jnp.concatenate([sel_t, jnp.zeros((LANES - n_slc, TQ), F32)], axis=0).T.astype(BF16)

        def sel_bias(kt):
            expand = _row_iota((LANES, TQ)) == kt * blocks_per_tile + _div_pow2(_lane_iota((LANES, TQ)), SLC_BLOCK)
            chosen = _dot(selected, jnp.where(expand, 1.0, 0.0).astype(BF16))
            allowed = (chosen > 0.5) & (r_minus_c >= (kt - qi) * TQ)
            return jnp.where(allowed, 0.0, NEG)

        def win_bias(kt):
            dist = (qi - kt) * TQ + r_minus_c
            return jnp.where((dist >= 0) & (dist < WINDOW), 0.0, NEG)

        def run_branch(kv_ref, first_tile, bias_fn):
            m_sc[...] = jnp.full(m_sc.shape, NEG, F32)
            acc_sc[...] = jnp.zeros(acc_sc.shape, F32)

            def tile(kt, carry_):
                k_tile = kv_ref[pl.ds(kt * TQ, TQ), 0:LANES]
                v_aug = jnp.where(keep, kv_ref[pl.ds(kt * TQ, TQ), LANES:2 * LANES], 1.0)
                bias = bias_fn(kt)
                for g in range(GQA):
                    rows = slice(g * TQ, (g + 1) * TQ)
                    s = _dot_nt(q_sc[rows, :], k_tile) + bias
                    m_prev = m_sc[rows, :]
                    m_new = jnp.maximum(m_prev, jnp.max(s, axis=-1, keepdims=True))
                    alpha = jnp.exp(m_prev - m_new)
                    p = jnp.exp(s - jnp.concatenate([m_new] * (TQ // LANES), axis=1))
                    acc_sc[rows, :] = alpha * acc_sc[rows, :] + _dot(p.astype(BF16), v_aug)
                    m_sc[rows, :] = m_new
                return carry_

            lax.fori_loop(first_tile, qi + 1, tile, 0)
            outs = []
            for g in range(GQA):
                acc = acc_sc[g * TQ:(g + 1) * TQ, :]
                denom = jnp.where(keep, pltpu.roll(acc, HEAD_DIM, 1), 1.0)
                outs.append(acc / denom)
            return outs

        o_slc = run_branch(ksel_ref, 0, sel_bias)
        o_win = run_branch(kwin_ref, jnp.maximum(qi - WINDOW // TQ, 0), win_bias)

        for g in range(GQA):
            base = h * GQA * 3 + g * 3
            o = (_gate_col(gates, base) * o_cmp[g] + _gate_col(gates, base + 1) * o_slc[g]
                 + _gate_col(gates, base + 2) * o_win[g])
            sl = slice(g * LANES, (g + 1) * LANES)
            if h == 0:
                comb_sc[:, sl] = o
            else:
                merged = jnp.where(keep, o, comb_sc[:, sl])
                out_ref[:, sl] = (merged * szb_ref[:, sl].astype(F32)).astype(out_ref.dtype)


def _attn_prompt_call(qc, qr, cmp_p, ov, ksel, kwin, gates, szb, n_batch, seq):
    nq = seq // TQ
    tile = lambda b, q: (b * nq + q, 0)
    whole = lambda b, q: (b, 0)
    return pl.pallas_call(
        _attn_prompt_kernel,
        grid=(n_batch, nq),
        in_specs=[
            pl.BlockSpec((TQ, D_B), tile),
            pl.BlockSpec((TQ, D_B), tile),
            pl.BlockSpec((1, LANES, 2 * LANES), lambda b, q: (b, 0, 0)),
            pl.BlockSpec((LANES, LANES), lambda b, q: (0, 0)),
            pl.BlockSpec((seq, 2 * LANES), whole),
            pl.BlockSpec((seq, 2 * LANES), whole),
            pl.BlockSpec((TQ, LANES), tile),
            pl.BlockSpec((TQ, D_B), tile),
        ],
        out_specs=pl.BlockSpec((TQ, D_B), tile),
        out_shape=jax.ShapeDtypeStruct((n_batch * seq, D_B), BF16),
        scratch_shapes=[
            pltpu.VMEM((GQA * TQ, LANES), BF16),
            pltpu.VMEM((GQA * TQ, LANES), F32),
            pltpu.VMEM((GQA * TQ, LANES), F32),
            pltpu.VMEM((TQ, D_B), F32),
        ],
        compiler_params=pltpu.CompilerParams(dimension_semantics=("arbitrary", "arbitrary"),
                                             vmem_limit_bytes=VMEM_LIMIT),
        name="attn_prompt",
    )(qc, qr, cmp_p, ov, ksel, kwin, gates, szb)


def _cmpwin_sample_kernel(qc_ref, qr_ref, cmp_ref, ov_ref, win_ref, kvw_ref, gate_ref,
                          ocw_ref, sel_ref, *, past_len, n_pick):
    t_new = qc_ref.shape[0]
    n_rows = N_KV * GQA * t_new
    gates = gate_ref[...]
    lane = _lane_iota((t_new, LANES))

    def q_rows(ref):
        return jnp.concatenate(
            [jnp.where(_div_pow2(lane, HEAD_DIM) == h, ref[:, g * LANES:(g + 1) * LANES], 0).astype(BF16)
             for h in range(N_KV) for g in range(GQA)], axis=0)

    n_cmp = cmp_ref.shape[1]
    row_t = _mod_pow2(_row_iota((n_rows, n_cmp)), t_new)
    slot = _lane_iota((n_rows, n_cmp))
    cmp_valid = (slot >= 1) & (slot * CMP_STRIDE + (CMP_STRIDE - 1) <= past_len + row_t)
    p = _softmax_rows(_dot_nt(q_rows(qc_ref), cmp_ref[0, :, 0:LANES]), cmp_valid)
    o_cmp = _dot(p.astype(BF16), cmp_ref[0, :, LANES:2 * LANES])

    p_sum = jnp.concatenate(
        [sum(p[(h * GQA + g) * t_new:(h * GQA + g + 1) * t_new] for g in range(GQA)) for h in range(N_KV)],
        axis=0)
    imp = _dot_split(p_sum, ov_ref[...])
    blk = _lane_iota(imp.shape)
    tb = (past_len + t_new - 1) // SLC_BLOCK
    candidate = (blk >= 1) & (blk < tb - 1)
    score = jnp.where(candidate, imp, -1.0)
    out_lane = _lane_iota((N_KV * t_new, LANES))
    picks = jnp.zeros((N_KV * t_new, LANES), jnp.int32)
    for k in range(n_pick):
        best = jnp.max(score, axis=-1, keepdims=True)
        idx = jnp.min(jnp.where(score == best, blk, 1 << 20), axis=-1, keepdims=True)
        picks = jnp.where(out_lane == k, idx, picks)
        score = jnp.where(blk == idx, -2.0, score)
    sel_ref[0] = picks

    qr = q_rows(qr_ref)
    n_buf = win_ref.shape[1]
    k_buf = win_ref[0, :, 0:LANES].astype(BF16)
    v_buf = win_ref[0, :, LANES:2 * LANES].astype(BF16)
    row_tb = _mod_pow2(_row_iota((n_rows, n_buf)), t_new)
    i_buf = _lane_iota((n_rows, n_buf))
    dist = n_buf + row_tb - i_buf
    s_buf = jnp.where((dist >= 0) & (dist < WINDOW), _dot_nt(qr, k_buf), NEG)
    k_new = kvw_ref[:, 0:LANES]
    v_new = kvw_ref[:, LANES:2 * LANES]
    qr32 = qr.astype(F32)
    row_t1 = _mod_pow2(_row_iota((n_rows, 1)), t_new)
    s_new = [jnp.where(row_t1 >= i, jnp.sum(qr32 * k_new[i:i + 1, :].astype(BF16).astype(F32), axis=-1, keepdims=True), NEG)
             for i in range(t_new)]
    m = jnp.max(s_buf, axis=-1, keepdims=True)
    for s in s_new:
        m = jnp.maximum(m, s)
    e_buf = jnp.exp(s_buf - m)
    e_new = [jnp.exp(s - m) for s in s_new]
    denom = jnp.sum(e_buf, axis=-1, keepdims=True) + sum(e_new)
    o_win = _dot(e_buf.astype(BF16), v_buf)
    for i in range(t_new):
        o_win = o_win + e_new[i].astype(BF16).astype(F32) * v_new[i:i + 1, :].astype(BF16).astype(F32)
    o_win = o_win / denom

    for g in range(GQA):
        parts = []
        for h in range(N_KV):
            rows = slice((h * GQA + g) * t_new, (h * GQA + g + 1) * t_new)
            base = h * GQA * 3 + g * 3
            parts.append(_gate_col(gates, base) * o_cmp[rows] + _gate_col(gates, base + 2) * o_win[rows])
        ocw_ref[:, g * LANES:(g + 1) * LANES] = jnp.where(lane < HEAD_DIM, parts[0], parts[1])


def _cmpwin_sample_call(qc, qr, cmp_s, ov, win, kvw, gates, t_new, past_len, n_pick):
    n_seq = cmp_s.shape[0]
    tile = lambda b: (b, 0)
    return pl.pallas_call(
        functools.partial(_cmpwin_sample_kernel, past_len=past_len, n_pick=n_pick),
        grid=(n_seq,),
        in_specs=[
            pl.BlockSpec((t_new, D_B), tile),
            pl.BlockSpec((t_new, D_B), tile),
            pl.BlockSpec((1, cmp_s.shape[1], 2 * LANES), lambda b: (b, 0, 0)),
            pl.BlockSpec(ov.shape, lambda b: (0, 0)),
            pl.BlockSpec((1, win.shape[1], 2 * LANES), lambda b: (b, 0, 0)),
            pl.BlockSpec((t_new, 2 * LANES), tile),
            pl.BlockSpec((t_new, LANES), tile),
        ],
        out_specs=[
            pl.BlockSpec((t_new, D_B), tile),
            pl.BlockSpec((1, N_KV * t_new, LANES), lambda b: (b, 0, 0)),
        ],
        out_shape=[
            jax.ShapeDtypeStruct((n_seq * t_new, D_B), F32),
            jax.ShapeDtypeStruct((n_seq, N_KV * t_new, LANES), jnp.int32),
        ],
        compiler_params=pltpu.CompilerParams(dimension_semantics=("arbitrary",),
                                             vmem_limit_bytes=VMEM_LIMIT),
        name="cmpwin_sample",
    )(qc, qr, cmp_s, ov, win, kvw, gates)


def _sel_sample_kernel(sel_ref, pt_ref, cache_ref, qr_ref, rows_ref, gate_ref, ocw_ref, szb_ref,
                       out_ref, buf, sem, *, n_pick, n_pages, tb):
    t_new = qr_ref.shape[0]
    n_own = t_new * n_pick
    n_steps = pl.num_programs(0) * pl.num_programs(1)
    h = pl.program_id(1)
    step = pl.program_id(0) * pl.num_programs(1) + h
    blocks_per_page = PAGE_SIZE // SLC_BLOCK

    def block_copy(step_idx, slot, i, blk):
        seq = _div_pow2(step_idx, N_KV)
        head = _mod_pow2(step_idx, N_KV)
        phys = pt_ref[seq * n_pages + _div_pow2(blk, blocks_per_page)]
        return pltpu.make_async_copy(cache_ref.at[phys, pl.ds(2, 2), head], buf.at[slot, i], sem.at[slot])

    def for_all_blocks(step_idx, slot, fn):
        fn(block_copy(step_idx, slot, 0, 0))
        fn(block_copy(step_idx, slot, 1, tb - 1))

        def body(i, carry_):
            fn(block_copy(step_idx, slot, 2 + i, sel_ref[step_idx * n_own + i]))
            return carry_
        lax.fori_loop(0, n_own, body, 0)

    @pl.when(step == 0)
    def _():
        for_all_blocks(0, 0, lambda cp: cp.start())

    @pl.when(step + 1 < n_steps)
    def _():
        for_all_blocks(step + 1, (step + 1) % 2, lambda cp: cp.start())

    slot = step % 2
    for_all_blocks(step, slot, lambda cp: cp.wait())

    n_rows = GQA * t_new
    half = _div_pow2(_lane_iota((1, LANES)), SLC_BLOCK)

    def head_half(x):
        return jnp.where(h == 0, x[:, 0:HEAD_DIM], x[:, HEAD_DIM:LANES])

    qr = jnp.concatenate([head_half(qr_ref[:, g * LANES:(g + 1) * LANES]) for g in range(GQA)], axis=0)
    qrb = qr.astype(BF16)
    qr32 = qrb.astype(F32)
    row_t = _mod_pow2(_row_iota((n_rows, 1)), t_new)

    def slabs(first, count, kind):
        return jnp.concatenate([buf[slot, first + k, kind].astype(BF16) for k in range(count)], axis=1)

    bias_sh = jnp.concatenate(
        [jnp.where(half == blk % blocks_per_page, 0.0, NEG) for blk in (0, tb - 1)], axis=1)
    s_sh = _dot(qrb, slabs(0, 2, 0)) + bias_sh
    k_new = head_half(rows_ref[:, 2 * LANES:3 * LANES]).astype(BF16).astype(F32)
    v_new = head_half(rows_ref[:, 3 * LANES:4 * LANES]).astype(BF16).astype(F32)
    s_new = [jnp.where(row_t >= i, jnp.sum(qr32 * k_new[i:i + 1, :], axis=-1, keepdims=True), NEG)
             for i in range(t_new)]
    own_keys = n_pick * PAGE_SIZE
    s_own = jnp.zeros((n_rows, own_keys), F32)
    for t in range(t_new):
        bias_t = jnp.concatenate(
            [jnp.where(half == _mod_pow2(sel_ref[step * n_own + t * n_pick + k], blocks_per_page), 0.0, NEG)
             for k in range(n_pick)], axis=1)
        s_t = _dot(qrb, slabs(2 + t * n_pick, n_pick, 0)) + bias_t
        s_own = jnp.where(row_t == t, s_t, s_own)

    m = jnp.maximum(jnp.max(s_sh, axis=-1, keepdims=True), jnp.max(s_own, axis=-1, keepdims=True))
    for s in s_new:
        m = jnp.maximum(m, s)
    e_sh = jnp.exp(s_sh - m)
    e_own = jnp.exp(s_own - m)
    e_new = [jnp.exp(s - m) for s in s_new]
    denom = jnp.sum(e_sh, axis=-1, keepdims=True) + jnp.sum(e_own, axis=-1, keepdims=True) + sum(e_new)
    o = _dot_nt(e_sh.astype(BF16), slabs(0, 2, 1))
    for t in range(t_new):
        o = o + _dot_nt(jnp.where(row_t == t, e_own, 0.0).astype(BF16), slabs(2 + t * n_pick, n_pick, 1))
    for i in range(t_new):
        o = o + e_new[i].astype(BF16).astype(F32) * v_new[i:i + 1, :]
    o = o / denom

    gates = gate_ref[...]
    lane = _lane_iota((t_new, LANES))
    keep = _div_pow2(lane, HEAD_DIM) == h
    for g in range(GQA):
        sl = slice(g * LANES, (g + 1) * LANES)
        gate = jnp.sum(jnp.where(lane == h * GQA * 3 + g * 3 + 1, gates, 0.0), axis=-1, keepdims=True)
        o_g = o[g * t_new:(g + 1) * t_new]
        contrib = jnp.where(keep, gate * jnp.concatenate([o_g, o_g], axis=1), 0.0)

        @pl.when(h == 0)
        def _():
            out_ref[:, sl] = ocw_ref[:, sl] + contrib

        @pl.when(h == N_KV - 1)
        def _():
            out_ref[:, sl] = (out_ref[:, sl] + contrib) * szb_ref[:, sl]


def _sel_sample_call(sel_flat, pt_flat, cache_t, qr, rows, gates, ocw, szb, n_seq, t_new, n_pick, n_pages, tb):
    tile = lambda b, h, sel, pt: (b, 0)
    n_blocks = 2 + t_new * n_pick
    grid_spec = pltpu.PrefetchScalarGridSpec(
        num_scalar_prefetch=2,
        grid=(n_seq, N_KV),
        in_specs=[
            pl.BlockSpec(memory_space=pl.ANY),
            pl.BlockSpec((t_new, D_B), tile),
            pl.BlockSpec((t_new, 4 * LANES), tile),
            pl.BlockSpec((t_new, LANES), tile),
            pl.BlockSpec((t_new, D_B), tile),
            pl.BlockSpec((t_new, D_B), tile),
        ],
        out_specs=pl.BlockSpec((t_new, D_B), tile),
        scratch_shapes=[
            pltpu.VMEM((2, n_blocks, 2, HEAD_DIM, PAGE_SIZE), F32),
            pltpu.SemaphoreType.DMA((2,)),
        ],
    )
    return pl.pallas_call(
        functools.partial(_sel_sample_kernel, n_pick=n_pick, n_pages=n_pages, tb=tb),
        grid_spec=grid_spec,
        out_shape=jax.ShapeDtypeStruct((n_seq * t_new, D_B), F32),
        compiler_params=pltpu.CompilerParams(dimension_semantics=("arbitrary", "arbitrary"),
                                             vmem_limit_bytes=VMEM_LIMIT),
        name="sel_sample",
    )(sel_flat, pt_flat, cache_t, qr, rows, gates, ocw, szb)


def _merge_kernel(x_ref, a_ref, b_ref, wa_ref, wb_ref, fg_ref, y_ref):
    delta = _dot(a_ref[...].astype(BF16), wa_ref[...]) + _dot(b_ref[...].astype(BF16), wb_ref[...])
    x = x_ref[...] + delta
    ms = jnp.mean(x * x, axis=-1, keepdims=True)
    y_ref[...] = (x * lax.rsqrt(ms + EPS)) * fg_ref[...]


def _merge_call(x2d, a, b, wa, wb, fg):
    n_rows = x2d.shape[0]
    row = lambda i: (i, 0)
    const2 = lambda i: (0, 0)
    return pl.pallas_call(
        _merge_kernel,
        grid=(n_rows // TM,),
        in_specs=[
            pl.BlockSpec((TM, D_MODEL), row),
            pl.BlockSpec((TM, D_A), row),
            pl.BlockSpec((TM, D_B), row),
            pl.BlockSpec((D_A, D_MODEL), const2),
            pl.BlockSpec((D_B, D_MODEL), const2),
            pl.BlockSpec((1, D_MODEL), const2),
        ],
        out_specs=pl.BlockSpec((TM, D_MODEL), row),
        out_shape=jax.ShapeDtypeStruct((n_rows, D_MODEL), F32),
        compiler_params=pltpu.CompilerParams(dimension_semantics=("arbitrary",),
                                             vmem_limit_bytes=VMEM_LIMIT),
        name="merge",
    )(x2d, a, b, wa, wb, fg)


def _head_pair_perm(w_cols):
    lead = w_cols.shape[:-1]
    return w_cols.reshape(*lead, N_KV, GQA, HEAD_DIM).swapaxes(-3, -2).reshape(*lead, D_B)


def _prep_w_in(w_in):
    cuts = np.cumsum([D_A, D_A, D_A, D_B, 6 * N_KV * HEAD_DIM, 3 * N_HEADS]).tolist()
    u, v, za, q, kv, g, zb = jnp.split(w_in, cuts, axis=-1)
    g_pad = jnp.pad(g, ((0, 0), (0, LANES - g.shape[1])))
    return jnp.concatenate([u, v, za, _head_pair_perm(q), kv, _head_pair_perm(zb), g_pad], axis=-1).astype(BF16)


def _prep_mix(w_s, b_s, chunk_len):
    tril = w_s[:, :chunk_len, :chunk_len] * jnp.tril(jnp.ones((chunk_len, chunk_len), w_s.dtype))
    reps = TM // chunk_len
    eye = jnp.eye(reps, dtype=w_s.dtype)
    wmix = jnp.einsum("ab,gts->gatbs", eye, tril).reshape(A_GROUPS, TM, TM).astype(BF16)
    bias = jnp.repeat(b_s[:, :chunk_len].T, HEAD_DIM, axis=1)
    return wmix, jnp.tile(bias, (reps, 1))


def _rope_tables(pos):
    half = HEAD_DIM // 2
    inv = ROPE_THETA ** (-jnp.arange(half, dtype=F32) / half)
    ang = pos.astype(F32)[:, None] * inv
    cos = jnp.tile(jnp.cos(ang), (1, LANES // half))
    sin = jnp.tile(jnp.concatenate([-jnp.sin(ang), jnp.sin(ang)], axis=1), (1, LANES // HEAD_DIM))
    return cos, sin


def _prep_compress(cmp_pos, w_cmp1, b_cmp1, w_cmp2):
    ratio = CMP_BLOCK // CMP_STRIDE
    w1 = w_cmp1.reshape(2, ratio, CMP_STRIDE, HEAD_DIM, CMP_HIDDEN).transpose(0, 2, 3, 1, 4)
    w1 = w1.reshape(2, CMP_STRIDE, HEAD_DIM, ratio * CMP_HIDDEN)
    zeros = jnp.zeros_like(w1)
    wc = jnp.stack([jnp.concatenate([w1, zeros], axis=-1), jnp.concatenate([zeros, w1], axis=-1)], axis=2)
    wc = wc.reshape(2, CMP_STRIDE * LANES, N_KV * ratio * CMP_HIDDEN).astype(BF16)
    pe = cmp_pos.reshape(2, ratio, CMP_STRIDE, 1, HEAD_DIM)
    pe = jnp.broadcast_to(pe, (2, ratio, CMP_STRIDE, N_KV, HEAD_DIM)).reshape(2, ratio, CMP_STRIDE * LANES)
    pe = jnp.pad(pe, ((0, 0), (0, 16 - ratio), (0, 0))).astype(BF16)
    w2 = w_cmp2[:, None]
    zeros2 = jnp.zeros_like(w2)
    w2p = jnp.concatenate([jnp.concatenate([w2, zeros2], axis=-1), jnp.concatenate([zeros2, w2], axis=-1)],
                          axis=1).astype(BF16)
    return wc, pe, b_cmp1.reshape(2, 1, CMP_HIDDEN), w2p


def _overlap_matrix(n_cmp_slots, n_slc, n_lanes):
    start = (np.arange(n_cmp_slots)[:, None] - 1) * CMP_STRIDE
    j = np.arange(n_lanes)[None, :]
    ov = (start <= j * SLC_BLOCK + SLC_BLOCK - 1) & (start + CMP_BLOCK - 1 >= j * SLC_BLOCK)
    ov &= (np.arange(n_cmp_slots)[:, None] >= 1) & (j < n_slc)
    return jnp.asarray(ov, dtype=BF16)


def kernel(x_prompt, x_sample, cache_kv, state_win, page_table, norm_g, w_in, ln_g, ln_b, w_s, b_s,
           cmp_pos, w_cmp1, b_cmp1, w_cmp2, w_out, final_g):
    n_batch, seq, _ = x_prompt.shape
    n_seq, t_new, _ = x_sample.shape
    depth, n_phys = cache_kv.shape[:2]
    n_pages = page_table.shape[1]
    past_len = n_pages * PAGE_SIZE
    win_buf = state_win.shape[2]
    assert depth == 1 and seq % TQ == 0 and n_seq * t_new == TM and win_buf == WINDOW
    assert (past_len + t_new - 1) // SLC_BLOCK == past_len // SLC_BLOCK and past_len % CMP_STRIDE == 0

    w_all = _prep_w_in(w_in[0])
    ng = norm_g[0].reshape(1, D_MODEL)
    lng = ln_g[0].reshape(1, D_A)
    lnb = ln_b[0].reshape(1, D_A)
    fg = final_g.reshape(1, D_MODEL)
    wo = w_out[0]
    wo_a = wo[:D_A].astype(BF16)
    wo_b = _head_pair_perm(wo[D_A:].T).T.astype(BF16)
    wc, pe, b1, w2p = _prep_compress(cmp_pos[0], w_cmp1[0], b_cmp1[0], w_cmp2[0])

    xp = x_prompt.reshape(n_batch * seq, D_MODEL)
    wmix_p, bmix_p = _prep_mix(w_s[0], b_s[0], CHUNK)
    cos_p, sin_p = _rope_tables(jnp.arange(seq))
    ag_p, qc_p, qr_p, rows_p, kvw_p, ksel_p, kwin_p, gate_p, szb_p = _proj_call(
        xp, ng, w_all, lng, lnb, wmix_p, bmix_p, cos_p, sin_p, BF16, False)
    pages_p = seq // PAGE_SIZE
    ident = jnp.arange(n_batch * pages_p, dtype=jnp.int32).reshape(n_batch, pages_p)
    cmp_p = _compress_call(rows_p.reshape(n_batch * pages_p, PAGE_SIZE, 4 * LANES), ident, wc, pe, b1, w2p,
                           pages_p, False)
    ov_p = _overlap_matrix(seq // CMP_STRIDE, seq // SLC_BLOCK, LANES)
    bg_p = _attn_prompt_call(qc_p, qr_p, cmp_p, ov_p, ksel_p, kwin_p, gate_p, szb_p, n_batch, seq)
    y_p = _merge_call(xp, ag_p, bg_p, wo_a, wo_b, fg)

    xs = x_sample.reshape(n_seq * t_new, D_MODEL)
    wmix_s, bmix_s = _prep_mix(w_s[0], b_s[0], t_new)
    cos_s, sin_s = _rope_tables(jnp.tile(past_len + jnp.arange(t_new), n_seq))
    ag_s, qc_s, qr_s, rows_s, kvw_s, _, _, gate_s, szb_s, vln_s = _proj_call(
        xs, ng, w_all, lng, lnb, wmix_s, bmix_s, cos_s, sin_s, F32, True)
    cache_t = cache_kv[0].transpose(0, 2, 3, 4, 1)
    cmp_s = _compress_call(cache_t, page_table, wc, pe, b1, w2p, 32, True)
    n_slc_s = -(-(past_len + t_new) // SLC_BLOCK)
    tb = past_len // SLC_BLOCK
    n_pick = TOP_N - 3
    ov_s = _overlap_matrix(past_len // CMP_STRIDE, n_slc_s, -(-n_slc_s // LANES) * LANES)
    win_prev = state_win[0].reshape(n_seq, win_buf, 2 * LANES)
    ocw_s, sel_s = _cmpwin_sample_call(qc_s, qr_s, cmp_s, ov_s, win_prev, kvw_s, gate_s, t_new, past_len, n_pick)
    sel_flat = sel_s[:, :, :n_pick].reshape(-1)
    bg_s = _sel_sample_call(sel_flat, page_table.reshape(-1), cache_t, qr_s, rows_s, gate_s, ocw_s, szb_s,
                            n_seq, t_new, n_pick, n_pages, tb)
    y_s = _merge_call(xs, ag_s, bg_s, wo_a, wo_b, fg)

    new_win_p = kvw_p.reshape(n_batch, seq, 2, N_KV, HEAD_DIM)[:, seq - win_buf:]
    new_win_s = jnp.concatenate([state_win[0][:, t_new:], kvw_s.reshape(n_seq, t_new, 2, N_KV, HEAD_DIM)], axis=1)
    return (y_p.reshape(n_batch, seq, D_MODEL),
            y_s.reshape(n_seq, t_new, D_MODEL),
            rows_p.reshape(1, n_batch, seq, 4, N_KV, HEAD_DIM),
            new_win_p[None],
            rows_s.reshape(1, n_seq, t_new, 4, N_KV, HEAD_DIM),
            new_win_s[None],
            vln_s.reshape(1, n_seq, t_new, D_A))
```

```python
import functools
import math

import jax
import jax.numpy as jnp
import numpy as np
from jax import lax
from jax.experimental import pallas as pl
from jax.experimental.pallas import tpu as pltpu

F32 = jnp.float32
BF16 = jnp.bfloat16

D_MODEL = 1024
HEAD_DIM = 64
D_A = 512
A_GROUPS = 8
CHUNK = 128
N_HEADS = 8
D_B = 512
N_KV = 2
GQA = 4
CMP_BLOCK = 32
CMP_STRIDE = 16
CMP_HIDDEN = 128
SLC_BLOCK = 64
TOP_N = 16
WINDOW = 512
ROPE_THETA = 10000.0
EPS = 1e-6
SCALE = HEAD_DIM ** -0.5
NEG = -1e30
FORCE = 1e9
PAGE_SIZE = 128

LANES = 128
VMEM_LIMIT = 56 * 1024 * 1024

C_U, C_V, C_ZA, C_Q, C_KV, C_ZB, C_G, C_END = 0, 512, 1024, 1536, 2048, 2816, 3328, 3456

TM = 256
TQ = 256
SEGS_PER_PAGE = PAGE_SIZE // CMP_STRIDE


def _lane_iota(shape):
    return lax.broadcasted_iota(jnp.int32, shape, len(shape) - 1)


def _row_iota(shape):
    return lax.broadcasted_iota(jnp.int32, shape, len(shape) - 2)


def _div_pow2(x, n):
    assert n & (n - 1) == 0
    return lax.shift_right_logical(x, int(math.log2(n))) if n > 1 else x


def _mod_pow2(x, n):
    assert n & (n - 1) == 0
    return x & (n - 1)


def _dot(a, b):
    return jnp.dot(a, b, preferred_element_type=F32)


def _dot_nt(a, b):
    return lax.dot_general(a, b, (((1,), (1,)), ((), ())), preferred_element_type=F32)


def _rope(x, cos, sin_signed):
    lo = _mod_pow2(_lane_iota(x.shape), HEAD_DIM) < (HEAD_DIM // 2)
    swapped = jnp.where(lo, pltpu.roll(x, LANES - HEAD_DIM // 2, 1), pltpu.roll(x, HEAD_DIM // 2, 1))
    return x * cos + swapped * sin_signed


def _proj_kernel(x_ref, ng_ref, w_ref, lng_ref, lnb_ref, wmix_ref, bmix_ref, cos_ref, sin_ref,
                 ag_ref, qc_ref, qr_ref, rows_ref, kvw_ref, ksel_ref, kwin_ref, gate_ref, szb_ref,
                 extra_ref, *, prompt):
    x = x_ref[...]
    ms = jnp.mean(x * x, axis=-1, keepdims=True)
    hb = ((x * lax.rsqrt(ms + EPS)) * ng_ref[...]).astype(BF16)
    cos = cos_ref[...]
    sin = sin_ref[...]
    pair = 2 * LANES

    def proj(c0, n):
        return _dot(hb, w_ref[:, c0:c0 + n])

    def halves(x2):
        return x2[:, 0:LANES], x2[:, LANES:pair]

    v = jax.nn.gelu(proj(C_V, D_A))
    mu = jnp.mean(v, axis=-1, keepdims=True)
    vc = v - mu
    var = jnp.mean(vc * vc, axis=-1, keepdims=True)
    vln = vc * lax.rsqrt(var + EPS) * lng_ref[...] + lnb_ref[...]
    if not prompt:
        extra_ref[...] = vln
    vb = vln.astype(BF16)
    first_head = _lane_iota((x.shape[0], LANES)) < HEAD_DIM
    for gpp in range(A_GROUPS // 4):
        u2 = halves(jax.nn.gelu(proj(C_U + gpp * pair, pair)))
        za2 = halves(proj(C_ZA + gpp * pair, pair))
        for k in range(2):
            gp = 2 * gpp + k
            sl = slice(gp * LANES, (gp + 1) * LANES)
            blk = vb[:, sl]
            mixed = jnp.where(first_head, _dot(wmix_ref[2 * gp], blk), _dot(wmix_ref[2 * gp + 1], blk))
            mixed = mixed + bmix_ref[:, sl]
            ag_ref[:, sl] = (u2[k] * mixed * jax.nn.silu(za2[k])).astype(ag_ref.dtype)

    for gpp in range(GQA // 2):
        q2 = halves(proj(C_Q + gpp * pair, pair))
        for k in range(2):
            sl = slice((2 * gpp + k) * LANES, (2 * gpp + k + 1) * LANES)
            qc_ref[:, sl] = (q2[k] * SCALE).astype(qc_ref.dtype)
            qr_ref[:, sl] = (_rope(q2[k], cos, sin) * SCALE).astype(qr_ref.dtype)

    kc, vcm = halves(proj(C_KV, pair))
    ks, vs = halves(proj(C_KV + pair, pair))
    kw, vw = halves(proj(C_KV + 2 * pair, pair))
    ks = _rope(ks, cos, sin)
    kw = _rope(kw, cos, sin)
    for i, blk in enumerate((kc, vcm, ks, vs)):
        rows_ref[:, i * LANES:(i + 1) * LANES] = blk
        if prompt:
            extra_ref[0, i * LANES:(i + 1) * LANES, :] = blk.T
    for i, blk in enumerate((kw, vw)):
        if prompt:
            kvw_ref[0, i * LANES:(i + 1) * LANES, :] = blk.T
        else:
            kvw_ref[:, i * LANES:(i + 1) * LANES] = blk
    ksel_ref[:, 0:LANES] = ks.astype(BF16)
    ksel_ref[:, LANES:2 * LANES] = vs.astype(BF16)
    kwin_ref[:, 0:LANES] = kw.astype(BF16)
    kwin_ref[:, LANES:2 * LANES] = vw.astype(BF16)

    gate_ref[...] = jax.nn.sigmoid(proj(C_G, LANES))
    for gpp in range(GQA // 2):
        zb2 = halves(proj(C_ZB + gpp * pair, pair))
        for k in range(2):
            sl = slice((2 * gpp + k) * LANES, (2 * gpp + k + 1) * LANES)
            szb_ref[:, sl] = jax.nn.silu(zb2[k]).astype(szb_ref.dtype)


def _proj_call(x2d, ng, w, lng, lnb, wmix, bmix, cos_t, sin_t, act_dtype, seq_len):
    n_rows = x2d.shape[0]
    n_tiles = n_rows // TM
    pos_tiles = cos_t.shape[0] // TM
    row = lambda i: (i, 0)
    const2 = lambda i: (0, 0)
    out_shapes = [
        jax.ShapeDtypeStruct((n_rows, D_A), act_dtype),
        jax.ShapeDtypeStruct((n_rows, D_B), act_dtype),
        jax.ShapeDtypeStruct((n_rows, D_B), act_dtype),
        jax.ShapeDtypeStruct((n_rows, 4 * LANES), F32),
        jax.ShapeDtypeStruct((n_rows, 2 * LANES), F32),
        jax.ShapeDtypeStruct((n_rows, 2 * LANES), BF16),
        jax.ShapeDtypeStruct((n_rows, 2 * LANES), BF16),
        jax.ShapeDtypeStruct((n_rows, LANES), F32),
        jax.ShapeDtypeStruct((n_rows, D_B), act_dtype),
        jax.ShapeDtypeStruct((n_rows, D_A), F32),
    ]
    out_specs = [pl.BlockSpec((TM, s.shape[1]), row) for s in out_shapes]
    if seq_len is not None:
        tiles_per_seq = seq_len // TM
        dim_major = lambda i: (i // tiles_per_seq, 0, i % tiles_per_seq)
        for idx, width in ((4, 2 * LANES), (9, 4 * LANES)):
            out_shapes[idx] = jax.ShapeDtypeStruct((n_rows // seq_len, width, seq_len), F32)
            out_specs[idx] = pl.BlockSpec((1, width, TM), dim_major)
    return pl.pallas_call(
        functools.partial(_proj_kernel, prompt=seq_len is not None),
        grid=(n_tiles,),
        in_specs=[
            pl.BlockSpec((TM, D_MODEL), row),
            pl.BlockSpec((1, D_MODEL), const2),
            pl.BlockSpec((D_MODEL, C_END), const2),
            pl.BlockSpec((1, D_A), const2),
            pl.BlockSpec((1, D_A), const2),
            pl.BlockSpec((A_GROUPS, TM, TM), lambda i: (0, 0, 0)),
            pl.BlockSpec((TM, D_A), const2),
            pl.BlockSpec((TM, LANES), lambda i: (i % pos_tiles, 0)),
            pl.BlockSpec((TM, LANES), lambda i: (i % pos_tiles, 0)),
        ],
        out_specs=out_specs,
        out_shape=out_shapes,
        compiler_params=pltpu.CompilerParams(dimension_semantics=("arbitrary",),
                                             vmem_limit_bytes=VMEM_LIMIT),
        name="proj",
    )(x2d, ng, w, lng, lnb, wmix, bmix, cos_t, sin_t)


def _compress_kernel(pt_ref, src_ref, wc_ref, pe_ref, b1_ref, w2_ref, out_ref, buf, sem, carry,
                     *maybe_stage, pages_per_step):
    transposed_src = bool(maybe_stage)
    n_steps = pl.num_programs(0) * pl.num_programs(1)
    step = pl.program_id(0) * pl.num_programs(1) + pl.program_id(1)
    m = pages_per_step * SEGS_PER_PAGE

    def page_copies(step_idx, slot, i):
        phys = pt_ref[step_idx * pages_per_step + i]
        if transposed_src:
            return [pltpu.make_async_copy(src_ref.at[phys, pl.ds(0, 2)], maybe_stage[0].at[slot, i], sem.at[slot])]
        return [pltpu.make_async_copy(src_ref.at[phys, :, pl.ds(c * LANES, LANES)],
                                      buf.at[slot, c, pl.ds(i * PAGE_SIZE, PAGE_SIZE), :],
                                      sem.at[slot]) for c in range(2)]

    def start_step(step_idx, slot):
        def body(i, carry_):
            for cp in page_copies(step_idx, slot, i):
                cp.start()
            return carry_
        lax.fori_loop(0, pages_per_step, body, 0)

    def wait_step(step_idx, slot):
        def body(i, carry_):
            for cp in page_copies(step_idx, slot, i):
                cp.wait()
            return carry_
        lax.fori_loop(0, pages_per_step, body, 0)

    @pl.when(step == 0)
    def _():
        start_step(0, 0)

    @pl.when(step + 1 < n_steps)
    def _():
        start_step(step + 1, (step + 1) % 2)

    slot = step % 2
    wait_step(step, slot)

    if transposed_src:
        stage = maybe_stage[0]

        for i in range(pages_per_step):
            for c in range(2):
                slab = stage[slot, i, c].reshape(N_KV * HEAD_DIM, PAGE_SIZE)
                buf[0, c, i * PAGE_SIZE:(i + 1) * PAGE_SIZE, :] = slab.T
        rows_slot = 0
    else:
        rows_slot = slot

    first_of_seq = pl.program_id(1) == 0
    first_row = _row_iota((m, LANES)) == 0
    for c in range(2):
        pieces = [buf[rows_slot, c, pl.ds(j, m, stride=CMP_STRIDE), :].astype(BF16)
                  for j in range(CMP_STRIDE)]
        xs = jnp.concatenate(pieces, axis=1)
        part = _dot(xs, wc_ref[c])
        pe_part = _dot(pe_ref[c], wc_ref[c])
        acc_out = None
        for h in range(N_KV):
            lo = slice(h * 2 * CMP_HIDDEN, h * 2 * CMP_HIDDEN + CMP_HIDDEN)
            hi = slice(h * 2 * CMP_HIDDEN + CMP_HIDDEN, (h + 1) * 2 * CMP_HIDDEN)
            bias = pe_part[0:1, lo] + pe_part[1:2, hi] + b1_ref[c]
            part0 = part[:, lo]
            prev_last = jnp.where(first_of_seq, 0.0, carry[c, h, 0:1, :])
            prev0 = jnp.where(first_row, prev_last, pltpu.roll(part0, 1, 0))
            carry[c, h, 0:1, :] = part0[m - 1:m, :]
            hid = jax.nn.gelu(prev0 + part[:, hi] + bias).astype(BF16)
            contrib = _dot(hid, w2_ref[c, h])
            acc_out = contrib if acc_out is None else acc_out + contrib
        out_ref[0, :, c * LANES:(c + 1) * LANES] = acc_out.astype(out_ref.dtype)


def _compress_call(src, page_table, wc, pe, b1, w2p, pages_per_step, transposed_src):
    n_seq, n_pages = page_table.shape
    n_chunks = n_pages // pages_per_step
    m = pages_per_step * SEGS_PER_PAGE
    rows_slots = 1 if transposed_src else 2
    scratch = [
        pltpu.VMEM((rows_slots, 2, pages_per_step * PAGE_SIZE, LANES), F32),
        pltpu.SemaphoreType.DMA((2,)),
        pltpu.VMEM((2, N_KV, 8, CMP_HIDDEN), F32),
    ]
    if transposed_src:
        scratch.append(pltpu.VMEM((2, pages_per_step, 2, N_KV, HEAD_DIM, PAGE_SIZE), F32))
    grid_spec = pltpu.PrefetchScalarGridSpec(
        num_scalar_prefetch=1,
        grid=(n_seq, n_chunks),
        in_specs=[
            pl.BlockSpec(memory_space=pl.ANY),
            pl.BlockSpec((2, CMP_STRIDE * LANES, 4 * CMP_HIDDEN), lambda b, k, pt: (0, 0, 0)),
            pl.BlockSpec((2, 16, CMP_STRIDE * LANES), lambda b, k, pt: (0, 0, 0)),
            pl.BlockSpec((2, 1, CMP_HIDDEN), lambda b, k, pt: (0, 0, 0)),
            pl.BlockSpec((2, N_KV, CMP_HIDDEN, LANES), lambda b, k, pt: (0, 0, 0, 0)),
        ],
        out_specs=pl.BlockSpec((1, m, 2 * LANES), lambda b, k, pt: (b, k, 0)),
        scratch_shapes=scratch,
    )
    return pl.pallas_call(
        functools.partial(_compress_kernel, pages_per_step=pages_per_step),
        grid_spec=grid_spec,
        out_shape=jax.ShapeDtypeStruct((n_seq, n_pages * SEGS_PER_PAGE, 2 * LANES), BF16),
        compiler_params=pltpu.CompilerParams(dimension_semantics=("arbitrary", "arbitrary"),
                                             vmem_limit_bytes=VMEM_LIMIT),
        name="compress",
    )(page_table.reshape(-1), src, wc, pe, b1, w2p)


def _softmax_rows(s, mask):
    s = jnp.where(mask, s, NEG)
    e = jnp.where(mask, jnp.exp(s - jnp.max(s, axis=-1, keepdims=True)), 0.0)
    return e / jnp.maximum(jnp.sum(e, axis=-1, keepdims=True), 1.0)


def _dot_split(p, w):
    hi = p.astype(BF16)
    lo = (p - hi.astype(F32)).astype(BF16)
    return _dot(hi, w) + _dot(lo, w)


def _gate_col(gates, idx):
    return gates[:, idx:idx + 1]


def _attn_prompt_kernel(qc_ref, qr_ref, cmp_ref, ov_ref, eg_ref, ksel_ref, kwin_ref, gate_ref, szb_ref,
                        out_ref, q_sc, m_sc, acc_sc, cmp_sc):
    qi = pl.program_id(1)
    lane = _lane_iota((TQ, LANES))
    lo_half = lane < HEAD_DIM
    tq = qi * TQ + _row_iota((TQ, LANES))
    r_minus_c = _row_iota((TQ, TQ)) - _lane_iota((TQ, TQ))
    causal_bias = jnp.where(r_minus_c >= 0, 0.0, NEG)
    far_bias = jnp.where(r_minus_c < 0, 0.0, NEG)
    kcmp = cmp_ref[0, :, 0:LANES]
    vcmp = cmp_ref[0, :, LANES:2 * LANES]
    cmp_valid = (lane >= 1) & (lane * CMP_STRIDE + (CMP_STRIDE - 1) <= tq)
    n_slc = ksel_ref.shape[0] // SLC_BLOCK
    blocks_per_tile = TQ // SLC_BLOCK

    def flash_tile(q_ref, acc_ref, k_tile, v_aug, bias):
        width = k_tile.shape[0]
        s = _dot_nt(q_ref[...], k_tile)
        if bias is not None:
            s = (s.reshape(GQA, TQ, width) + bias[None]).reshape(GQA * TQ, width)
        m_prev = m_sc[...]
        m_new = jnp.maximum(m_prev, jnp.max(s, axis=-1, keepdims=True))
        alpha = jnp.exp(m_prev - m_new)
        p = jnp.exp(s - jnp.concatenate([m_new] * (width // LANES), axis=1))
        acc_ref[...] = alpha * acc_ref[...] + _dot(p.astype(BF16), v_aug)
        m_sc[...] = m_new

    def flash_init(acc_ref):
        m_sc[...] = jnp.full(m_sc.shape, NEG, F32)
        acc_ref[...] = jnp.zeros(acc_ref.shape, F32)

    for h in range(N_KV):
        keep = _div_pow2(lane, HEAD_DIM) == h
        other_off = HEAD_DIM * (1 - h)

        p_sum = jnp.zeros((TQ, LANES), F32)
        for g in range(GQA):
            sl = slice(g * LANES, (g + 1) * LANES)
            qc = jnp.where(keep, qc_ref[:, sl], 0)
            p = _softmax_rows(_dot_nt(qc, kcmp), cmp_valid)
            p_sum = p_sum + p
            o_cmp = _dot(p.astype(BF16), vcmp)
            cmp_sc[:, sl] = o_cmp if h == 0 else jnp.where(keep, o_cmp, cmp_sc[:, sl])
        imp = _dot_split(p_sum, ov_ref[...])
        tb = _div_pow2(tq, SLC_BLOCK)
        forced = (lane == 0) | (lane == tb) | (lane == tb - 1)
        valid = lane * SLC_BLOCK <= tq
        score = jnp.where(forced, FORCE, jnp.where(valid, imp, NEG))
        score_t = score.T[0:n_slc, :]
        blk = _row_iota((n_slc, TQ))
        rank = jnp.zeros((n_slc, TQ), jnp.int32)
        for j in range(n_slc):
            other = score_t[j:j + 1, :]
            ahead = (other > score_t) | ((other == score_t) & (blk > j))
            rank = rank + ahead.astype(jnp.int32)
        sel_bias_t = jnp.where(rank < TOP_N, 0.0, NEG)
        pieces = [jnp.zeros((other_off, TQ), F32)] if other_off else []
        pieces += [sel_bias_t, jnp.zeros((LANES - other_off - n_slc, TQ), F32)]
        sel_bias = jnp.concatenate(pieces, axis=0).T.astype(BF16)
        for g in range(GQA):
            sl = slice(g * LANES, (g + 1) * LANES)
            q_sc[h, g * TQ:(g + 1) * TQ, :] = jnp.where(keep, qr_ref[:, sl], sel_bias)

        def v_aug(kv_ref, kt, n_tiles):
            v = kv_ref[pl.ds(kt * TQ, n_tiles * TQ), LANES:2 * LANES]
            return jnp.where(_div_pow2(_lane_iota(v.shape), HEAD_DIM) == h, v, 1.0)

        def sel_span(kt, n_tiles, bias):
            k = ksel_ref[pl.ds(kt * TQ, n_tiles * TQ), 0:LANES]
            key_lane = _lane_iota(k.shape)
            indicator = key_lane == other_off + kt * blocks_per_tile + _div_pow2(_row_iota(k.shape), SLC_BLOCK)
            k_aug = jnp.where(_div_pow2(key_lane, HEAD_DIM) == h, k, jnp.where(indicator, 1, 0).astype(BF16))
            flash_tile(q_sc.at[h], acc_sc.at[0, h], k_aug, v_aug(ksel_ref, kt, n_tiles), bias)

        flash_init(acc_sc.at[0, h])

        def sel_pair(i, carry_):
            sel_span(2 * i, 2, None)
            return carry_
        lax.fori_loop(0, _div_pow2(qi, 2), sel_pair, 0)

        @pl.when(_mod_pow2(qi, 2) == 1)
        def _():
            sel_span(qi - 1, 1, None)
        sel_span(qi, 1, causal_bias)

        def win_span(kt, n_tiles, bias):
            k = kwin_ref[pl.ds(kt * TQ, n_tiles * TQ), 0:LANES]
            k_own = jnp.where(_div_pow2(_lane_iota(k.shape), HEAD_DIM) == h, k, 0)
            flash_tile(q_sc.at[h], acc_sc.at[1, h], k_own, v_aug(kwin_ref, kt, n_tiles), bias)

        flash_init(acc_sc.at[1, h])
        far = WINDOW // TQ
        assert far == 2

        @pl.when(qi >= far)
        def _():
            win_span(qi - far, 1, far_bias)

        @pl.when(qi >= 1)
        def _():
            win_span(qi - 1, 2, jnp.concatenate([jnp.zeros((TQ, TQ), F32), causal_bias], axis=1))

        @pl.when(qi == 0)
        def _():
            win_span(qi, 1, causal_bias)

    gates_x = _dot_split(gate_ref[...], eg_ref[...])
    for g in range(GQA):
        rows = slice(g * TQ, (g + 1) * TQ)
        sl = slice(g * LANES, (g + 1) * LANES)
        o = gates_x[:, (g * 3) * LANES:(g * 3 + 1) * LANES] * cmp_sc[:, sl]
        for br in range(2):
            a0 = acc_sc[br, 0, rows, :]
            a1 = acc_sc[br, 1, rows, :]
            num = jnp.where(lo_half, a0, a1)
            den = pltpu.roll(jnp.where(lo_half, a1, a0), HEAD_DIM, 1)
            o = o + gates_x[:, (g * 3 + 1 + br) * LANES:(g * 3 + 2 + br) * LANES] * (num / den)
        out_ref[:, sl] = (o * szb_ref[:, sl].astype(F32)).astype(out_ref.dtype)


def _gate_expansion():
    eg = np.zeros((LANES, GQA * 3 * LANES), np.float32)
    for h in range(N_KV):
        for g in range(GQA):
            for br in range(3):
                c0 = (g * 3 + br) * LANES + h * HEAD_DIM
                eg[h * GQA * 3 + g * 3 + br, c0:c0 + HEAD_DIM] = 1.0
    return jnp.asarray(eg, dtype=BF16)


def _attn_prompt_call(qc, qr, cmp_p, ov, ksel, kwin, gates, szb, n_batch, seq):
    nq = seq // TQ
    tile = lambda b, q: (b * nq + q, 0)
    whole = lambda b, q: (b, 0)
    return pl.pallas_call(
        _attn_prompt_kernel,
        grid=(n_batch, nq),
        in_specs=[
            pl.BlockSpec((TQ, D_B), tile),
            pl.BlockSpec((TQ, D_B), tile),
            pl.BlockSpec((1, LANES, 2 * LANES), lambda b, q: (b, 0, 0)),
            pl.BlockSpec((LANES, LANES), lambda b, q: (0, 0)),
            pl.BlockSpec((LANES, GQA * 3 * LANES), lambda b, q: (0, 0)),
            pl.BlockSpec((seq, 2 * LANES), whole),
            pl.BlockSpec((seq, 2 * LANES), whole),
            pl.BlockSpec((TQ, LANES), tile),
            pl.BlockSpec((TQ, D_B), tile),
        ],
        out_specs=pl.BlockSpec((TQ, D_B), tile),
        out_shape=jax.ShapeDtypeStruct((n_batch * seq, D_B), BF16),
        scratch_shapes=[
            pltpu.VMEM((N_KV, GQA * TQ, LANES), BF16),
            pltpu.VMEM((GQA * TQ, LANES), F32),
            pltpu.VMEM((2, N_KV, GQA * TQ, LANES), F32),
            pltpu.VMEM((TQ, D_B), F32),
        ],
        compiler_params=pltpu.CompilerParams(dimension_semantics=("arbitrary", "arbitrary"),
                                             vmem_limit_bytes=VMEM_LIMIT),
        name="attn_prompt",
    )(qc, qr, cmp_p, ov, _gate_expansion(), ksel, kwin, gates, szb)


def _cmpwin_sample_kernel(qc_ref, qr_ref, cmp_ref, ov_ref, win_ref, kvw_ref, gate_ref,
                          ocw_ref, sel_ref, *, past_len, n_pick):
    t_new = qc_ref.shape[0]
    n_rows = N_KV * GQA * t_new
    gates = gate_ref[...]
    lane = _lane_iota((t_new, LANES))

    def q_rows(ref):
        return jnp.concatenate(
            [jnp.where(_div_pow2(lane, HEAD_DIM) == h, ref[:, g * LANES:(g + 1) * LANES], 0).astype(BF16)
             for h in range(N_KV) for g in range(GQA)], axis=0)

    n_cmp = cmp_ref.shape[1]
    row_t = _mod_pow2(_row_iota((n_rows, n_cmp)), t_new)
    slot = _lane_iota((n_rows, n_cmp))
    cmp_valid = (slot >= 1) & (slot * CMP_STRIDE + (CMP_STRIDE - 1) <= past_len + row_t)
    p = _softmax_rows(_dot_nt(q_rows(qc_ref), cmp_ref[0, :, 0:LANES]), cmp_valid)
    o_cmp = _dot(p.astype(BF16), cmp_ref[0, :, LANES:2 * LANES])

    p_sum = jnp.concatenate(
        [sum(p[(h * GQA + g) * t_new:(h * GQA + g + 1) * t_new] for g in range(GQA)) for h in range(N_KV)],
        axis=0)
    imp = _dot_split(p_sum, ov_ref[...])
    blk = _lane_iota(imp.shape)
    tb = (past_len + t_new - 1) // SLC_BLOCK
    candidate = (blk >= 1) & (blk < tb - 1)
    score = jnp.where(candidate, imp, -1.0)
    out_lane = _lane_iota((N_KV * t_new, LANES))
    picks = jnp.zeros((N_KV * t_new, LANES), jnp.int32)
    for k in range(n_pick):
        best = jnp.max(score, axis=-1, keepdims=True)
        idx = jnp.min(jnp.where(score == best, blk, 1 << 20), axis=-1, keepdims=True)
        picks = jnp.where(out_lane == k, idx, picks)
        score = jnp.where(blk == idx, -2.0, score)
    sel_ref[0] = picks

    qr = q_rows(qr_ref)
    n_buf = win_ref.shape[1]
    k_buf = win_ref[0, :, 0:LANES].astype(BF16)
    v_buf = win_ref[0, :, LANES:2 * LANES].astype(BF16)
    row_tb = _mod_pow2(_row_iota((n_rows, n_buf)), t_new)
    i_buf = _lane_iota((n_rows, n_buf))
    dist = n_buf + row_tb - i_buf
    s_buf = jnp.where((dist >= 0) & (dist < WINDOW), _dot_nt(qr, k_buf), NEG)
    k_new = kvw_ref[:, 0:LANES]
    v_new = kvw_ref[:, LANES:2 * LANES]
    qr32 = qr.astype(F32)
    row_t1 = _mod_pow2(_row_iota((n_rows, 1)), t_new)
    s_new = [jnp.where(row_t1 >= i, jnp.sum(qr32 * k_new[i:i + 1, :].astype(BF16).astype(F32), axis=-1, keepdims=True), NEG)
             for i in range(t_new)]
    m = jnp.max(s_buf, axis=-1, keepdims=True)
    for s in s_new:
        m = jnp.maximum(m, s)
    e_buf = jnp.exp(s_buf - m)
    e_new = [jnp.exp(s - m) for s in s_new]
    denom = jnp.sum(e_buf, axis=-1, keepdims=True) + sum(e_new)
    o_win = _dot(e_buf.astype(BF16), v_buf)
    for i in range(t_new):
        o_win = o_win + e_new[i].astype(BF16).astype(F32) * v_new[i:i + 1, :].astype(BF16).astype(F32)
    o_win = o_win / denom

    for g in range(GQA):
        parts = []
        for h in range(N_KV):
            rows = slice((h * GQA + g) * t_new, (h * GQA + g + 1) * t_new)
            base = h * GQA * 3 + g * 3
            parts.append(_gate_col(gates, base) * o_cmp[rows] + _gate_col(gates, base + 2) * o_win[rows])
        ocw_ref[:, g * LANES:(g + 1) * LANES] = jnp.where(lane < HEAD_DIM, parts[0], parts[1])


def _cmpwin_sample_call(qc, qr, cmp_s, ov, win, kvw, gates, t_new, past_len, n_pick):
    n_seq = cmp_s.shape[0]
    tile = lambda b: (b, 0)
    return pl.pallas_call(
        functools.partial(_cmpwin_sample_kernel, past_len=past_len, n_pick=n_pick),
        grid=(n_seq,),
        in_specs=[
            pl.BlockSpec((t_new, D_B), tile),
            pl.BlockSpec((t_new, D_B), tile),
            pl.BlockSpec((1, cmp_s.shape[1], 2 * LANES), lambda b: (b, 0, 0)),
            pl.BlockSpec(ov.shape, lambda b: (0, 0)),
            pl.BlockSpec((1, win.shape[1], 2 * LANES), lambda b: (b, 0, 0)),
            pl.BlockSpec((t_new, 2 * LANES), tile),
            pl.BlockSpec((t_new, LANES), tile),
        ],
        out_specs=[
            pl.BlockSpec((t_new, D_B), tile),
            pl.BlockSpec((1, N_KV * t_new, LANES), lambda b: (b, 0, 0)),
        ],
        out_shape=[
            jax.ShapeDtypeStruct((n_seq * t_new, D_B), F32),
            jax.ShapeDtypeStruct((n_seq, N_KV * t_new, LANES), jnp.int32),
        ],
        compiler_params=pltpu.CompilerParams(dimension_semantics=("arbitrary",),
                                             vmem_limit_bytes=VMEM_LIMIT),
        name="cmpwin_sample",
    )(qc, qr, cmp_s, ov, win, kvw, gates)


def _sel_sample_kernel(sel_ref, pt_ref, cache_ref, qr_ref, rows_ref, gate_ref, ocw_ref, szb_ref,
                       out_ref, buf, sem, *, n_pick, n_pages, tb):
    t_new = qr_ref.shape[0]
    n_own = t_new * n_pick
    n_steps = pl.num_programs(0) * pl.num_programs(1)
    h = pl.program_id(1)
    step = pl.program_id(0) * pl.num_programs(1) + h
    blocks_per_page = PAGE_SIZE // SLC_BLOCK

    def block_copy(step_idx, slot, i, blk):
        seq = _div_pow2(step_idx, N_KV)
        head = _mod_pow2(step_idx, N_KV)
        phys = pt_ref[seq * n_pages + _div_pow2(blk, blocks_per_page)]
        return pltpu.make_async_copy(cache_ref.at[phys, pl.ds(2, 2), head], buf.at[slot, i], sem.at[slot])

    def for_all_blocks(step_idx, slot, fn):
        fn(block_copy(step_idx, slot, 0, 0))
        fn(block_copy(step_idx, slot, 1, tb - 1))

        def body(i, carry_):
            fn(block_copy(step_idx, slot, 2 + i, sel_ref[step_idx * n_own + i]))
            return carry_
        lax.fori_loop(0, n_own, body, 0)

    @pl.when(step == 0)
    def _():
        for_all_blocks(0, 0, lambda cp: cp.start())

    @pl.when(step + 1 < n_steps)
    def _():
        for_all_blocks(step + 1, (step + 1) % 2, lambda cp: cp.start())

    slot = step % 2
    for_all_blocks(step, slot, lambda cp: cp.wait())

    n_rows = GQA * t_new
    half = _div_pow2(_lane_iota((1, LANES)), SLC_BLOCK)

    def head_half(x):
        return jnp.where(h == 0, x[:, 0:HEAD_DIM], x[:, HEAD_DIM:LANES])

    qr = jnp.concatenate([head_half(qr_ref[:, g * LANES:(g + 1) * LANES]) for g in range(GQA)], axis=0)
    qrb = qr.astype(BF16)
    qr32 = qrb.astype(F32)
    row_t = _mod_pow2(_row_iota((n_rows, 1)), t_new)

    def slabs(first, count, kind):
        return jnp.concatenate([buf[slot, first + k, kind].astype(BF16) for k in range(count)], axis=1)

    bias_sh = jnp.concatenate(
        [jnp.where(half == blk % blocks_per_page, 0.0, NEG) for blk in (0, tb - 1)], axis=1)
    s_sh = _dot(qrb, slabs(0, 2, 0)) + bias_sh
    k_new = head_half(rows_ref[:, 2 * LANES:3 * LANES]).astype(BF16).astype(F32)
    v_new = head_half(rows_ref[:, 3 * LANES:4 * LANES]).astype(BF16).astype(F32)
    s_new = [jnp.where(row_t >= i, jnp.sum(qr32 * k_new[i:i + 1, :], axis=-1, keepdims=True), NEG)
             for i in range(t_new)]
    own_keys = n_pick * PAGE_SIZE
    s_own = jnp.zeros((n_rows, own_keys), F32)
    for t in range(t_new):
        bias_t = jnp.concatenate(
            [jnp.where(half == _mod_pow2(sel_ref[step * n_own + t * n_pick + k], blocks_per_page), 0.0, NEG)
             for k in range(n_pick)], axis=1)
        s_t = _dot(qrb, slabs(2 + t * n_pick, n_pick, 0)) + bias_t
        s_own = jnp.where(row_t == t, s_t, s_own)

    m = jnp.maximum(jnp.max(s_sh, axis=-1, keepdims=True), jnp.max(s_own, axis=-1, keepdims=True))
    for s in s_new:
        m = jnp.maximum(m, s)
    e_sh = jnp.exp(s_sh - m)
    e_own = jnp.exp(s_own - m)
    e_new = [jnp.exp(s - m) for s in s_new]
    denom = jnp.sum(e_sh, axis=-1, keepdims=True) + jnp.sum(e_own, axis=-1, keepdims=True) + sum(e_new)
    o = _dot_nt(e_sh.astype(BF16), slabs(0, 2, 1))
    for t in range(t_new):
        o = o + _dot_nt(jnp.where(row_t == t, e_own, 0.0).astype(BF16), slabs(2 + t * n_pick, n_pick, 1))
    for i in range(t_new):
        o = o + e_new[i].astype(BF16).astype(F32) * v_new[i:i + 1, :]
    o = o / denom

    gates = gate_ref[...]
    lane = _lane_iota((t_new, LANES))
    keep = _div_pow2(lane, HEAD_DIM) == h
    for g in range(GQA):
        sl = slice(g * LANES, (g + 1) * LANES)
        gate = jnp.sum(jnp.where(lane == h * GQA * 3 + g * 3 + 1, gates, 0.0), axis=-1, keepdims=True)
        o_g = o[g * t_new:(g + 1) * t_new]
        contrib = jnp.where(keep, gate * jnp.concatenate([o_g, o_g], axis=1), 0.0)

        @pl.when(h == 0)
        def _():
            out_ref[:, sl] = ocw_ref[:, sl] + contrib

        @pl.when(h == N_KV - 1)
        def _():
            out_ref[:, sl] = (out_ref[:, sl] + contrib) * szb_ref[:, sl]


def _sel_sample_call(sel_flat, pt_flat, cache_t, qr, rows, gates, ocw, szb, n_seq, t_new, n_pick, n_pages, tb):
    tile = lambda b, h, sel, pt: (b, 0)
    n_blocks = 2 + t_new * n_pick
    grid_spec = pltpu.PrefetchScalarGridSpec(
        num_scalar_prefetch=2,
        grid=(n_seq, N_KV),
        in_specs=[
            pl.BlockSpec(memory_space=pl.ANY),
            pl.BlockSpec((t_new, D_B), tile),
            pl.BlockSpec((t_new, 4 * LANES), tile),
            pl.BlockSpec((t_new, LANES), tile),
            pl.BlockSpec((t_new, D_B), tile),
            pl.BlockSpec((t_new, D_B), tile),
        ],
        out_specs=pl.BlockSpec((t_new, D_B), tile),
        scratch_shapes=[
            pltpu.VMEM((2, n_blocks, 2, HEAD_DIM, PAGE_SIZE), F32),
            pltpu.SemaphoreType.DMA((2,)),
        ],
    )
    return pl.pallas_call(
        functools.partial(_sel_sample_kernel, n_pick=n_pick, n_pages=n_pages, tb=tb),
        grid_spec=grid_spec,
        out_shape=jax.ShapeDtypeStruct((n_seq * t_new, D_B), F32),
        compiler_params=pltpu.CompilerParams(dimension_semantics=("arbitrary", "arbitrary"),
                                             vmem_limit_bytes=VMEM_LIMIT),
        name="sel_sample",
    )(sel_flat, pt_flat, cache_t, qr, rows, gates, ocw, szb)


def _merge_kernel(x_ref, a_ref, b_ref, wa_ref, wb_ref, fg_ref, y_ref):
    delta = _dot(a_ref[...].astype(BF16), wa_ref[...]) + _dot(b_ref[...].astype(BF16), wb_ref[...])
    x = x_ref[...] + delta
    ms = jnp.mean(x * x, axis=-1, keepdims=True)
    y_ref[...] = (x * lax.rsqrt(ms + EPS)) * fg_ref[...]


def _merge_call(x2d, a, b, wa, wb, fg):
    n_rows = x2d.shape[0]
    row = lambda i: (i, 0)
    const2 = lambda i: (0, 0)
    return pl.pallas_call(
        _merge_kernel,
        grid=(n_rows // TM,),
        in_specs=[
            pl.BlockSpec((TM, D_MODEL), row),
            pl.BlockSpec((TM, D_A), row),
            pl.BlockSpec((TM, D_B), row),
            pl.BlockSpec((D_A, D_MODEL), const2),
            pl.BlockSpec((D_B, D_MODEL), const2),
            pl.BlockSpec((1, D_MODEL), const2),
        ],
        out_specs=pl.BlockSpec((TM, D_MODEL), row),
        out_shape=jax.ShapeDtypeStruct((n_rows, D_MODEL), F32),
        compiler_params=pltpu.CompilerParams(dimension_semantics=("arbitrary",),
                                             vmem_limit_bytes=VMEM_LIMIT),
        name="merge",
    )(x2d, a, b, wa, wb, fg)


def _head_pair_perm(w_cols):
    lead = w_cols.shape[:-1]
    return w_cols.reshape(*lead, N_KV, GQA, HEAD_DIM).swapaxes(-3, -2).reshape(*lead, D_B)


def _prep_w_in(w_in):
    cuts = np.cumsum([D_A, D_A, D_A, D_B, 6 * N_KV * HEAD_DIM, 3 * N_HEADS]).tolist()
    u, v, za, q, kv, g, zb = jnp.split(w_in, cuts, axis=-1)
    g_pad = jnp.pad(g, ((0, 0), (0, LANES - g.shape[1])))
    return jnp.concatenate([u, v, za, _head_pair_perm(q), kv, _head_pair_perm(zb), g_pad], axis=-1).astype(BF16)


def _prep_mix(w_s, b_s, chunk_len):
    tril = w_s[:, :chunk_len, :chunk_len] * jnp.tril(jnp.ones((chunk_len, chunk_len), w_s.dtype))
    reps = TM // chunk_len
    eye = jnp.eye(reps, dtype=w_s.dtype)
    wmix = jnp.einsum("ab,gts->gatbs", eye, tril).reshape(A_GROUPS, TM, TM).astype(BF16)
    bias = jnp.repeat(b_s[:, :chunk_len].T, HEAD_DIM, axis=1)
    return wmix, jnp.tile(bias, (reps, 1))


def _rope_tables(pos):
    half = HEAD_DIM // 2
    inv = ROPE_THETA ** (-jnp.arange(half, dtype=F32) / half)
    ang = pos.astype(F32)[:, None] * inv
    cos = jnp.tile(jnp.cos(ang), (1, LANES // half))
    sin = jnp.tile(jnp.concatenate([-jnp.sin(ang), jnp.sin(ang)], axis=1), (1, LANES // HEAD_DIM))
    return cos, sin


def _prep_compress(cmp_pos, w_cmp1, b_cmp1, w_cmp2):
    ratio = CMP_BLOCK // CMP_STRIDE
    w1 = w_cmp1.reshape(2, ratio, CMP_STRIDE, HEAD_DIM, CMP_HIDDEN).transpose(0, 2, 3, 1, 4)
    w1 = w1.reshape(2, CMP_STRIDE, HEAD_DIM, ratio * CMP_HIDDEN)
    zeros = jnp.zeros_like(w1)
    wc = jnp.stack([jnp.concatenate([w1, zeros], axis=-1), jnp.concatenate([zeros, w1], axis=-1)], axis=2)
    wc = wc.reshape(2, CMP_STRIDE * LANES, N_KV * ratio * CMP_HIDDEN).astype(BF16)
    pe = cmp_pos.reshape(2, ratio, CMP_STRIDE, 1, HEAD_DIM)
    pe = jnp.broadcast_to(pe, (2, ratio, CMP_STRIDE, N_KV, HEAD_DIM)).reshape(2, ratio, CMP_STRIDE * LANES)
    pe = jnp.pad(pe, ((0, 0), (0, 16 - ratio), (0, 0))).astype(BF16)
    w2 = w_cmp2[:, None]
    zeros2 = jnp.zeros_like(w2)
    w2p = jnp.concatenate([jnp.concatenate([w2, zeros2], axis=-1), jnp.concatenate([zeros2, w2], axis=-1)],
                          axis=1).astype(BF16)
    return wc, pe, b_cmp1.reshape(2, 1, CMP_HIDDEN), w2p


def _overlap_matrix(n_cmp_slots, n_slc, n_lanes):
    start = (np.arange(n_cmp_slots)[:, None] - 1) * CMP_STRIDE
    j = np.arange(n_lanes)[None, :]
    ov = (start <= j * SLC_BLOCK + SLC_BLOCK - 1) & (start + CMP_BLOCK - 1 >= j * SLC_BLOCK)
    ov &= (np.arange(n_cmp_slots)[:, None] >= 1) & (j < n_slc)
    return jnp.asarray(ov, dtype=BF16)


def kernel(x_prompt, x_sample, cache_kv, state_win, page_table, norm_g, w_in, ln_g, ln_b, w_s, b_s,
           cmp_pos, w_cmp1, b_cmp1, w_cmp2, w_out, final_g):
    n_batch, seq, _ = x_prompt.shape
    n_seq, t_new, _ = x_sample.shape
    depth, n_phys = cache_kv.shape[:2]
    n_pages = page_table.shape[1]
    past_len = n_pages * PAGE_SIZE
    win_buf = state_win.shape[2]
    assert depth == 1 and seq % TQ == 0 and n_seq * t_new == TM and win_buf == WINDOW
    assert (past_len + t_new - 1) // SLC_BLOCK == past_len // SLC_BLOCK and past_len % CMP_STRIDE == 0

    w_all = _prep_w_in(w_in[0])
    ng = norm_g[0].reshape(1, D_MODEL)
    lng = ln_g[0].reshape(1, D_A)
    lnb = ln_b[0].reshape(1, D_A)
    fg = final_g.reshape(1, D_MODEL)
    wo = w_out[0]
    wo_a = wo[:D_A].astype(BF16)
    wo_b = _head_pair_perm(wo[D_A:].T).T.astype(BF16)
    wc, pe, b1, w2p = _prep_compress(cmp_pos[0], w_cmp1[0], b_cmp1[0], w_cmp2[0])

    xp = x_prompt.reshape(n_batch * seq, D_MODEL)
    wmix_p, bmix_p = _prep_mix(w_s[0], b_s[0], CHUNK)
    cos_p, sin_p = _rope_tables(jnp.arange(seq))
    ag_p, qc_p, qr_p, rows_p, kvw_pt, ksel_p, kwin_p, gate_p, szb_p, rows_pt = _proj_call(
        xp, ng, w_all, lng, lnb, wmix_p, bmix_p, cos_p, sin_p, BF16, seq)
    pages_p = seq // PAGE_SIZE
    ident = jnp.arange(n_batch * pages_p, dtype=jnp.int32).reshape(n_batch, pages_p)
    cmp_p = _compress_call(rows_p.reshape(n_batch * pages_p, PAGE_SIZE, 4 * LANES), ident, wc, pe, b1, w2p,
                           pages_p, False)
    ov_p = _overlap_matrix(seq // CMP_STRIDE, seq // SLC_BLOCK, LANES)
    bg_p = _attn_prompt_call(qc_p, qr_p, cmp_p, ov_p, ksel_p, kwin_p, gate_p, szb_p, n_batch, seq)
    y_p = _merge_call(xp, ag_p, bg_p, wo_a, wo_b, fg)

    xs = x_sample.reshape(n_seq * t_new, D_MODEL)
    wmix_s, bmix_s = _prep_mix(w_s[0], b_s[0], t_new)
    cos_s, sin_s = _rope_tables(jnp.tile(past_len + jnp.arange(t_new), n_seq))
    ag_s, qc_s, qr_s, rows_s, kvw_s, _, _, gate_s, szb_s, vln_s = _proj_call(
        xs, ng, w_all, lng, lnb, wmix_s, bmix_s, cos_s, sin_s, F32, None)
    cache_t = cache_kv[0].transpose(0, 2, 3, 4, 1)
    cmp_s = _compress_call(cache_t, page_table, wc, pe, b1, w2p, 32, True)
    n_slc_s = -(-(past_len + t_new) // SLC_BLOCK)
    tb = past_len // SLC_BLOCK
    n_pick = TOP_N - 3
    ov_s = _overlap_matrix(past_len // CMP_STRIDE, n_slc_s, -(-n_slc_s // LANES) * LANES)
    win_prev = state_win[0].reshape(n_seq, win_buf, 2 * LANES)
    ocw_s, sel_s = _cmpwin_sample_call(qc_s, qr_s, cmp_s, ov_s, win_prev, kvw_s, gate_s, t_new, past_len, n_pick)
    sel_flat = sel_s[:, :, :n_pick].reshape(-1)
    bg_s = _sel_sample_call(sel_flat, page_table.reshape(-1), cache_t, qr_s, rows_s, gate_s, ocw_s, szb_s,
                            n_seq, t_new, n_pick, n_pages, tb)
    y_s = _merge_call(xs, ag_s, bg_s, wo_a, wo_b, fg)

    new_kv_p = rows_pt.reshape(n_batch, 4, N_KV, HEAD_DIM, seq).transpose(0, 4, 1, 2, 3)
    new_win_p = kvw_pt[:, :, seq - win_buf:].reshape(n_batch, 2, N_KV, HEAD_DIM, win_buf).transpose(0, 4, 1, 2, 3)
    new_win_s = jnp.concatenate([state_win[0][:, t_new:], kvw_s.reshape(n_seq, t_new, 2, N_KV, HEAD_DIM)], axis=1)
    return (y_p.reshape(n_batch, seq, D_MODEL),
            y_s.reshape(n_seq, t_new, D_MODEL),
            new_kv_p[None],
            new_win_p[None],
            rows_s.reshape(1, n_seq, t_new, 4, N_KV, HEAD_DIM),
            new_win_s[None],
            vln_s.reshape(1, n_seq, t_new, D_A))
```

```python
import functools
import math

import jax
import jax.numpy as jnp
import numpy as np
from jax import lax
from jax.experimental import pallas as pl
from jax.experimental.pallas import tpu as pltpu

F32 = jnp.float32
BF16 = jnp.bfloat16

D_MODEL = 1024
HEAD_DIM = 64
D_A = 512
A_GROUPS = 8
CHUNK = 128
N_HEADS = 8
D_B = 512
N_KV = 2
GQA = 4
CMP_BLOCK = 32
CMP_STRIDE = 16
CMP_HIDDEN = 128
SLC_BLOCK = 64
TOP_N = 16
WINDOW = 512
ROPE_THETA = 10000.0
EPS = 1e-6
SCALE = HEAD_DIM ** -0.5
NEG = -1e30
FORCE = 1e9
PAGE_SIZE = 128

LANES = 128
VMEM_LIMIT = 56 * 1024 * 1024

C_U, C_V, C_ZA, C_Q, C_KV, C_ZB, C_G, C_END = 0, 512, 1024, 1536, 2048, 2816, 3328, 3456

TM = 256
TQ = 256
SEGS_PER_PAGE = PAGE_SIZE // CMP_STRIDE


def _lane_iota(shape):
    return lax.broadcasted_iota(jnp.int32, shape, len(shape) - 1)


def _row_iota(shape):
    return lax.broadcasted_iota(jnp.int32, shape, len(shape) - 2)


def _div_pow2(x, n):
    assert n & (n - 1) == 0
    return lax.shift_right_logical(x, int(math.log2(n))) if n > 1 else x


def _mod_pow2(x, n):
    assert n & (n - 1) == 0
    return x & (n - 1)


def _dot(a, b):
    return jnp.dot(a, b, preferred_element_type=F32)


def _dot_nt(a, b):
    return lax.dot_general(a, b, (((1,), (1,)), ((), ())), preferred_element_type=F32)


def _rope(x, cos, sin_signed):
    lo = _mod_pow2(_lane_iota(x.shape), HEAD_DIM) < (HEAD_DIM // 2)
    swapped = jnp.where(lo, pltpu.roll(x, LANES - HEAD_DIM // 2, 1), pltpu.roll(x, HEAD_DIM // 2, 1))
    return x * cos + swapped * sin_signed


def _proj_kernel(x_ref, ng_ref, w_ref, lng_ref, lnb_ref, wmix_ref, bmix_ref, cos_ref, sin_ref,
                 ag_ref, qc_ref, qr_ref, rows_ref, kvw_ref, ksel_ref, kwin_ref, gate_ref, szb_ref,
                 extra_ref, *, prompt):
    x = x_ref[...]
    ms = jnp.mean(x * x, axis=-1, keepdims=True)
    hb = ((x * lax.rsqrt(ms + EPS)) * ng_ref[...]).astype(BF16)
    cos = cos_ref[...]
    sin = sin_ref[...]
    pair = 2 * LANES

    def proj(c0, n):
        return _dot(hb, w_ref[:, c0:c0 + n])

    def halves(x2):
        return x2[:, 0:LANES], x2[:, LANES:pair]

    v = jax.nn.gelu(proj(C_V, D_A))
    mu = jnp.mean(v, axis=-1, keepdims=True)
    vc = v - mu
    var = jnp.mean(vc * vc, axis=-1, keepdims=True)
    vln = vc * lax.rsqrt(var + EPS) * lng_ref[...] + lnb_ref[...]
    if not prompt:
        extra_ref[...] = vln
    vb = vln.astype(BF16)
    first_head = _lane_iota((x.shape[0], LANES)) < HEAD_DIM
    for gpp in range(A_GROUPS // 4):
        u2 = halves(jax.nn.gelu(proj(C_U + gpp * pair, pair)))
        za2 = halves(proj(C_ZA + gpp * pair, pair))
        for k in range(2):
            gp = 2 * gpp + k
            sl = slice(gp * LANES, (gp + 1) * LANES)
            blk = vb[:, sl]
            mixed = jnp.where(first_head, _dot(wmix_ref[2 * gp], blk), _dot(wmix_ref[2 * gp + 1], blk))
            mixed = mixed + bmix_ref[:, sl]
            ag_ref[:, sl] = (u2[k] * mixed * jax.nn.silu(za2[k])).astype(ag_ref.dtype)

    for gpp in range(GQA // 2):
        q2 = halves(proj(C_Q + gpp * pair, pair))
        for k in range(2):
            sl = slice((2 * gpp + k) * LANES, (2 * gpp + k + 1) * LANES)
            qc_ref[:, sl] = (q2[k] * SCALE).astype(qc_ref.dtype)
            qr_ref[:, sl] = (_rope(q2[k], cos, sin) * SCALE).astype(qr_ref.dtype)

    kc, vcm = halves(proj(C_KV, pair))
    ks, vs = halves(proj(C_KV + pair, pair))
    kw, vw = halves(proj(C_KV + 2 * pair, pair))
    ks = _rope(ks, cos, sin)
    kw = _rope(kw, cos, sin)
    for i, blk in enumerate((kc, vcm, ks, vs)):
        rows_ref[:, i * LANES:(i + 1) * LANES] = blk
        if prompt:
            extra_ref[0, i * LANES:(i + 1) * LANES, :] = blk.T
    for i, blk in enumerate((kw, vw)):
        if prompt:
            kvw_ref[0, i * LANES:(i + 1) * LANES, :] = blk.T
        else:
            kvw_ref[:, i * LANES:(i + 1) * LANES] = blk
    ksel_ref[:, 0:LANES] = ks.astype(BF16)
    ksel_ref[:, LANES:2 * LANES] = vs.astype(BF16)
    kwin_ref[:, 0:LANES] = kw.astype(BF16)
    kwin_ref[:, LANES:2 * LANES] = vw.astype(BF16)

    gate_ref[...] = jax.nn.sigmoid(proj(C_G, LANES))
    for gpp in range(GQA // 2):
        zb2 = halves(proj(C_ZB + gpp * pair, pair))
        for k in range(2):
            sl = slice((2 * gpp + k) * LANES, (2 * gpp + k + 1) * LANES)
            szb_ref[:, sl] = jax.nn.silu(zb2[k]).astype(szb_ref.dtype)


def _proj_call(x2d, ng, w, lng, lnb, wmix, bmix, cos_t, sin_t, act_dtype, seq_len):
    n_rows = x2d.shape[0]
    n_tiles = n_rows // TM
    pos_tiles = cos_t.shape[0] // TM
    row = lambda i: (i, 0)
    const2 = lambda i: (0, 0)
    out_shapes = [
        jax.ShapeDtypeStruct((n_rows, D_A), act_dtype),
        jax.ShapeDtypeStruct((n_rows, D_B), act_dtype),
        jax.ShapeDtypeStruct((n_rows, D_B), act_dtype),
        jax.ShapeDtypeStruct((n_rows, 4 * LANES), F32),
        jax.ShapeDtypeStruct((n_rows, 2 * LANES), F32),
        jax.ShapeDtypeStruct((n_rows, 2 * LANES), BF16),
        jax.ShapeDtypeStruct((n_rows, 2 * LANES), BF16),
        jax.ShapeDtypeStruct((n_rows, LANES), F32),
        jax.ShapeDtypeStruct((n_rows, D_B), act_dtype),
        jax.ShapeDtypeStruct((n_rows, D_A), F32),
    ]
    out_specs = [pl.BlockSpec((TM, s.shape[1]), row) for s in out_shapes]
    if seq_len is not None:
        tiles_per_seq = seq_len // TM
        dim_major = lambda i: (i // tiles_per_seq, 0, i % tiles_per_seq)
        for idx, width in ((4, 2 * LANES), (9, 4 * LANES)):
            out_shapes[idx] = jax.ShapeDtypeStruct((n_rows // seq_len, width, seq_len), F32)
            out_specs[idx] = pl.BlockSpec((1, width, TM), dim_major)
    return pl.pallas_call(
        functools.partial(_proj_kernel, prompt=seq_len is not None),
        grid=(n_tiles,),
        in_specs=[
            pl.BlockSpec((TM, D_MODEL), row),
            pl.BlockSpec((1, D_MODEL), const2),
            pl.BlockSpec((D_MODEL, C_END), const2),
            pl.BlockSpec((1, D_A), const2),
            pl.BlockSpec((1, D_A), const2),
            pl.BlockSpec((A_GROUPS, TM, TM), lambda i: (0, 0, 0)),
            pl.BlockSpec((TM, D_A), const2),
            pl.BlockSpec((TM, LANES), lambda i: (i % pos_tiles, 0)),
            pl.BlockSpec((TM, LANES), lambda i: (i % pos_tiles, 0)),
        ],
        out_specs=out_specs,
        out_shape=out_shapes,
        compiler_params=pltpu.CompilerParams(dimension_semantics=("arbitrary",),
                                             vmem_limit_bytes=VMEM_LIMIT),
        name="proj",
    )(x2d, ng, w, lng, lnb, wmix, bmix, cos_t, sin_t)


def _compress_kernel(pt_ref, src_ref, wc_ref, pe_ref, b1_ref, w2_ref, out_ref, buf, sem, carry,
                     *maybe_stage, pages_per_step):
    transposed_src = bool(maybe_stage)
    n_steps = pl.num_programs(0) * pl.num_programs(1)
    step = pl.program_id(0) * pl.num_programs(1) + pl.program_id(1)
    m = pages_per_step * SEGS_PER_PAGE

    def page_copies(step_idx, slot, i):
        phys = pt_ref[step_idx * pages_per_step + i]
        if transposed_src:
            return [pltpu.make_async_copy(src_ref.at[phys, pl.ds(0, 2)], maybe_stage[0].at[slot, i], sem.at[slot])]
        return [pltpu.make_async_copy(src_ref.at[phys, :, pl.ds(c * LANES, LANES)],
                                      buf.at[slot, c, pl.ds(i * PAGE_SIZE, PAGE_SIZE), :],
                                      sem.at[slot]) for c in range(2)]

    def start_step(step_idx, slot):
        def body(i, carry_):
            for cp in page_copies(step_idx, slot, i):
                cp.start()
            return carry_
        lax.fori_loop(0, pages_per_step, body, 0)

    def wait_step(step_idx, slot):
        def body(i, carry_):
            for cp in page_copies(step_idx, slot, i):
                cp.wait()
            return carry_
        lax.fori_loop(0, pages_per_step, body, 0)

    slot = step % 2

    first_of_seq = pl.program_id(1) == 0
    first_row = _row_iota((m, LANES)) == 0

    def compress_rows(rows_slot, seg_pitch):
        for c in range(2):
            pieces = [buf[rows_slot, c, pl.ds(j, m, stride=seg_pitch), :].astype(BF16)
                      for j in range(CMP_STRIDE)]
            part = _dot(jnp.concatenate(pieces, axis=1), wc_ref[c])
            pe_part = _dot(pe_ref[c], wc_ref[c])
            acc_out = None
            for h in range(N_KV):
                lo = slice(h * 2 * CMP_HIDDEN, h * 2 * CMP_HIDDEN + CMP_HIDDEN)
                hi = slice(h * 2 * CMP_HIDDEN + CMP_HIDDEN, (h + 1) * 2 * CMP_HIDDEN)
                bias = pe_part[0:1, lo] + pe_part[1:2, hi] + b1_ref[c]
                part0 = part[:, lo]
                prev_last = jnp.where(first_of_seq, 0.0, carry[c, h, 0:1, :])
                prev0 = jnp.where(first_row, prev_last, pltpu.roll(part0, 1, 0))
                carry[c, h, 0:1, :] = part0[m - 1:m, :]
                hid = jax.nn.gelu(prev0 + part[:, hi] + bias).astype(BF16)
                contrib = _dot(hid, w2_ref[c, h])
                acc_out = contrib if acc_out is None else acc_out + contrib
            out_ref[0, :, c * LANES:(c + 1) * LANES] = acc_out.astype(out_ref.dtype)

    if transposed_src:
        stage = maybe_stage[0]
        seg_pitch = CMP_STRIDE + 1

        def transpose_step(stage_slot):
            for i in range(pages_per_step):
                for c in range(2):
                    rows = stage[stage_slot, i, c].reshape(N_KV * HEAD_DIM, PAGE_SIZE).T
                    for s in range(SEGS_PER_PAGE):
                        r0 = (i * SEGS_PER_PAGE + s) * seg_pitch
                        buf[stage_slot, c, r0:r0 + CMP_STRIDE, :] = rows[s * CMP_STRIDE:(s + 1) * CMP_STRIDE]

        @pl.when(step == 0)
        def _():
            start_step(0, 0)
            start_step(1, 1)
            wait_step(0, 0)
            transpose_step(0)

        @pl.when(step + 1 < n_steps)
        def _():
            wait_step(step + 1, (step + 1) % 2)

        @pl.when(step + 2 < n_steps)
        def _():
            start_step(step + 2, slot)

        for parity in range(2):
            @pl.when(slot == parity)
            def _():
                transpose_step(1 - parity)
                compress_rows(parity, seg_pitch)
    else:
        @pl.when(step == 0)
        def _():
            start_step(0, 0)

        @pl.when(step + 1 < n_steps)
        def _():
            start_step(step + 1, (step + 1) % 2)

        wait_step(step, slot)
        compress_rows(slot, CMP_STRIDE)


def _compress_call(src, page_table, wc, pe, b1, w2p, pages_per_step, transposed_src):
    n_seq, n_pages = page_table.shape
    n_chunks = n_pages // pages_per_step
    m = pages_per_step * SEGS_PER_PAGE
    assert not transposed_src or n_seq * n_chunks >= 2
    seg_pitch = CMP_STRIDE + 1 if transposed_src else CMP_STRIDE
    scratch = [
        pltpu.VMEM((2, 2, m * seg_pitch, LANES), F32),
        pltpu.SemaphoreType.DMA((2,)),
        pltpu.VMEM((2, N_KV, 8, CMP_HIDDEN), F32),
    ]
    if transposed_src:
        scratch.append(pltpu.VMEM((2, pages_per_step, 2, N_KV, HEAD_DIM, PAGE_SIZE), F32))
    grid_spec = pltpu.PrefetchScalarGridSpec(
        num_scalar_prefetch=1,
        grid=(n_seq, n_chunks),
        in_specs=[
            pl.BlockSpec(memory_space=pl.ANY),
            pl.BlockSpec((2, CMP_STRIDE * LANES, 4 * CMP_HIDDEN), lambda b, k, pt: (0, 0, 0)),
            pl.BlockSpec((2, 16, CMP_STRIDE * LANES), lambda b, k, pt: (0, 0, 0)),
            pl.BlockSpec((2, 1, CMP_HIDDEN), lambda b, k, pt: (0, 0, 0)),
            pl.BlockSpec((2, N_KV, CMP_HIDDEN, LANES), lambda b, k, pt: (0, 0, 0, 0)),
        ],
        out_specs=pl.BlockSpec((1, m, 2 * LANES), lambda b, k, pt: (b, k, 0)),
        scratch_shapes=scratch,
    )
    return pl.pallas_call(
        functools.partial(_compress_kernel, pages_per_step=pages_per_step),
        grid_spec=grid_spec,
        out_shape=jax.ShapeDtypeStruct((n_seq, n_pages * SEGS_PER_PAGE, 2 * LANES), BF16),
        compiler_params=pltpu.CompilerParams(dimension_semantics=("arbitrary", "arbitrary"),
                                             vmem_limit_bytes=VMEM_LIMIT),
        name="compress",
    )(page_table.reshape(-1), src, wc, pe, b1, w2p)


def _softmax_rows(s, mask):
    s = jnp.where(mask, s, NEG)
    e = jnp.where(mask, jnp.exp(s - jnp.max(s, axis=-1, keepdims=True)), 0.0)
    return e / jnp.maximum(jnp.sum(e, axis=-1, keepdims=True), 1.0)


def _dot_split(p, w):
    hi = p.astype(BF16)
    lo = (p - hi.astype(F32)).astype(BF16)
    return _dot(hi, w) + _dot(lo, w)


def _gate_col(gates, idx):
    return gates[:, idx:idx + 1]


def _attn_prompt_kernel(qc_ref, qr_ref, cmp_ref, ov_ref, eg_ref, ksel_ref, kwin_ref, gate_ref, szb_ref,
                        out_ref, q_sc, m_sc, acc_sc, cmp_sc):
    qi = pl.program_id(1)
    lane = _lane_iota((TQ, LANES))
    lo_half = lane < HEAD_DIM
    tq = qi * TQ + _row_iota((TQ, LANES))
    r_minus_c = _row_iota((TQ, TQ)) - _lane_iota((TQ, TQ))
    causal_bias = jnp.where(r_minus_c >= 0, 0.0, NEG)
    far_bias = jnp.where(r_minus_c < 0, 0.0, NEG)
    kcmp = cmp_ref[0, :, 0:LANES]
    vcmp = cmp_ref[0, :, LANES:2 * LANES]
    cmp_valid = (lane >= 1) & (lane * CMP_STRIDE + (CMP_STRIDE - 1) <= tq)
    n_slc = ksel_ref.shape[0] // SLC_BLOCK
    blocks_per_tile = TQ // SLC_BLOCK

    def flash_tile(br, h, k_tile, v_aug, bias):
        width = k_tile.shape[0]
        s = _dot_nt(q_sc[h], k_tile)
        if bias is not None:
            s = (s.reshape(GQA, TQ, width) + bias[None]).reshape(GQA * TQ, width)
        m_prev = m_sc[h]
        m_new = jnp.maximum(m_prev, jnp.max(s, axis=-1, keepdims=True))
        alpha = jnp.exp(m_prev - m_new)
        p = jnp.exp(s - jnp.concatenate([m_new] * (width // LANES), axis=1))
        acc_sc[br, h] = alpha * acc_sc[br, h] + _dot(p.astype(BF16), v_aug)
        m_sc[h] = m_new

    def flash_init(br):
        m_sc[...] = jnp.full(m_sc.shape, NEG, F32)
        acc_sc[br] = jnp.zeros(acc_sc.shape[1:], F32)

    for h in range(N_KV):
        keep = _div_pow2(lane, HEAD_DIM) == h
        other_off = HEAD_DIM * (1 - h)

        p_sum = jnp.zeros((TQ, LANES), F32)
        for g in range(GQA):
            sl = slice(g * LANES, (g + 1) * LANES)
            qc = jnp.where(keep, qc_ref[:, sl], 0)
            p = _softmax_rows(_dot_nt(qc, kcmp), cmp_valid)
            p_sum = p_sum + p
            o_cmp = _dot(p.astype(BF16), vcmp)
            cmp_sc[:, sl] = o_cmp if h == 0 else jnp.where(keep, o_cmp, cmp_sc[:, sl])
        imp = _dot_split(p_sum, ov_ref[...])
        tb = _div_pow2(tq, SLC_BLOCK)
        forced = (lane == 0) | (lane == tb) | (lane == tb - 1)
        valid = lane * SLC_BLOCK <= tq
        score = jnp.where(forced, FORCE, jnp.where(valid, imp, NEG))
        score_t = score.T[0:n_slc, :]
        blk = _row_iota((n_slc, TQ))
        rank = jnp.zeros((n_slc, TQ), jnp.int32)
        for j in range(n_slc):
            other = score_t[j:j + 1, :]
            ahead = (other > score_t) | ((other == score_t) & (blk > j))
            rank = rank + ahead.astype(jnp.int32)
        sel_bias_t = jnp.where(rank < TOP_N, 0.0, NEG)
        pieces = [jnp.zeros((other_off, TQ), F32)] if other_off else []
        pieces += [sel_bias_t, jnp.zeros((LANES - other_off - n_slc, TQ), F32)]
        sel_bias = jnp.concatenate(pieces, axis=0).T.astype(BF16)
        for g in range(GQA):
            sl = slice(g * LANES, (g + 1) * LANES)
            q_sc[h, g * TQ:(g + 1) * TQ, :] = jnp.where(keep, qr_ref[:, sl], sel_bias)

    def span_operands(kv_ref, kt, n_tiles, h, other_lanes):
        k = kv_ref[pl.ds(kt * TQ, n_tiles * TQ), 0:LANES]
        v = kv_ref[pl.ds(kt * TQ, n_tiles * TQ), LANES:2 * LANES]
        own = _div_pow2(_lane_iota(k.shape), HEAD_DIM) == h
        return jnp.where(own, k, other_lanes(k.shape)), jnp.where(own, v, 1.0)

    def sel_span(kt, n_tiles, bias):
        for h in range(N_KV):
            def indicator(shape, h=h):
                first = HEAD_DIM * (1 - h) + kt * blocks_per_tile
                hit = _lane_iota(shape) == first + _div_pow2(_row_iota(shape), SLC_BLOCK)
                return jnp.where(hit, 1, 0).astype(BF16)
            flash_tile(0, h, *span_operands(ksel_ref, kt, n_tiles, h, indicator), bias)

    flash_init(0)

    def sel_pair(i, carry_):
        sel_span(2 * i, 2, None)
        return carry_
    lax.fori_loop(0, _div_pow2(qi, 2), sel_pair, 0)

    @pl.when(_mod_pow2(qi, 2) == 1)
    def _():
        sel_span(qi - 1, 1, None)
    sel_span(qi, 1, causal_bias)

    def win_span(kt, n_tiles, bias):
        for h in range(N_KV):
            flash_tile(1, h, *span_operands(kwin_ref, kt, n_tiles, h, lambda shape: jnp.zeros(shape, BF16)), bias)

    flash_init(1)
    far = WINDOW // TQ
    assert far == 2

    @pl.when(qi >= far)
    def _():
        win_span(qi - far, 1, far_bias)

    @pl.when(qi >= 1)
    def _():
        win_span(qi - 1, 2, jnp.concatenate([jnp.zeros((TQ, TQ), F32), causal_bias], axis=1))

    @pl.when(qi == 0)
    def _():
        win_span(qi, 1, causal_bias)

    gates_x = _dot_split(gate_ref[...], eg_ref[...])
    for g in range(GQA):
        rows = slice(g * TQ, (g + 1) * TQ)
        sl = slice(g * LANES, (g + 1) * LANES)
        o = gates_x[:, (g * 3) * LANES:(g * 3 + 1) * LANES] * cmp_sc[:, sl]
        for br in range(2):
            a0 = acc_sc[br, 0, rows, :]
            a1 = acc_sc[br, 1, rows, :]
            num = jnp.where(lo_half, a0, a1)
            den = pltpu.roll(jnp.where(lo_half, a1, a0), HEAD_DIM, 1)
            o = o + gates_x[:, (g * 3 + 1 + br) * LANES:(g * 3 + 2 + br) * LANES] * (num / den)
        out_ref[:, sl] = (o * szb_ref[:, sl].astype(F32)).astype(out_ref.dtype)


def _gate_expansion():
    eg = np.zeros((LANES, GQA * 3 * LANES), np.float32)
    for h in range(N_KV):
        for g in range(GQA):
            for br in range(3):
                c0 = (g * 3 + br) * LANES + h * HEAD_DIM
                eg[h * GQA * 3 + g * 3 + br, c0:c0 + HEAD_DIM] = 1.0
    return jnp.asarray(eg, dtype=BF16)


def _attn_prompt_call(qc, qr, cmp_p, ov, ksel, kwin, gates, szb, n_batch, seq):
    nq = seq // TQ
    tile = lambda b, q: (b * nq + q, 0)
    whole = lambda b, q: (b, 0)
    return pl.pallas_call(
        _attn_prompt_kernel,
        grid=(n_batch, nq),
        in_specs=[
            pl.BlockSpec((TQ, D_B), tile),
            pl.BlockSpec((TQ, D_B), tile),
            pl.BlockSpec((1, LANES, 2 * LANES), lambda b, q: (b, 0, 0)),
            pl.BlockSpec((LANES, LANES), lambda b, q: (0, 0)),
            pl.BlockSpec((LANES, GQA * 3 * LANES), lambda b, q: (0, 0)),
            pl.BlockSpec((seq, 2 * LANES), whole),
            pl.BlockSpec((seq, 2 * LANES), whole),
            pl.BlockSpec((TQ, LANES), tile),
            pl.BlockSpec((TQ, D_B), tile),
        ],
        out_specs=pl.BlockSpec((TQ, D_B), tile),
        out_shape=jax.ShapeDtypeStruct((n_batch * seq, D_B), BF16),
        scratch_shapes=[
            pltpu.VMEM((N_KV, GQA * TQ, LANES), BF16),
            pltpu.VMEM((N_KV, GQA * TQ, LANES), F32),
            pltpu.VMEM((2, N_KV, GQA * TQ, LANES), F32),
            pltpu.VMEM((TQ, D_B), F32),
        ],
        compiler_params=pltpu.CompilerParams(dimension_semantics=("arbitrary", "arbitrary"),
                                             vmem_limit_bytes=VMEM_LIMIT),
        name="attn_prompt",
    )(qc, qr, cmp_p, ov, _gate_expansion(), ksel, kwin, gates, szb)


def _cmpwin_sample_kernel(qc_ref, qr_ref, cmp_ref, ov_ref, win_ref, kvw_ref, gate_ref,
                          ocw_ref, sel_ref, *, past_len, n_pick):
    t_new = qc_ref.shape[0]
    n_rows = N_KV * GQA * t_new
    gates = gate_ref[...]
    lane = _lane_iota((t_new, LANES))

    def q_rows(ref):
        return jnp.concatenate(
            [jnp.where(_div_pow2(lane, HEAD_DIM) == h, ref[:, g * LANES:(g + 1) * LANES], 0).astype(BF16)
             for h in range(N_KV) for g in range(GQA)], axis=0)

    n_cmp = cmp_ref.shape[1]
    row_t = _mod_pow2(_row_iota((n_rows, n_cmp)), t_new)
    slot = _lane_iota((n_rows, n_cmp))
    cmp_valid = (slot >= 1) & (slot * CMP_STRIDE + (CMP_STRIDE - 1) <= past_len + row_t)
    p = _softmax_rows(_dot_nt(q_rows(qc_ref), cmp_ref[0, :, 0:LANES]), cmp_valid)
    o_cmp = _dot(p.astype(BF16), cmp_ref[0, :, LANES:2 * LANES])

    p_sum = jnp.concatenate(
        [sum(p[(h * GQA + g) * t_new:(h * GQA + g + 1) * t_new] for g in range(GQA)) for h in range(N_KV)],
        axis=0)
    imp = _dot_split(p_sum, ov_ref[...])
    blk = _lane_iota(imp.shape)
    tb = (past_len + t_new - 1) // SLC_BLOCK
    candidate = (blk >= 1) & (blk < tb - 1)
    score = jnp.where(candidate, imp, -1.0)
    out_lane = _lane_iota((N_KV * t_new, LANES))
    picks = jnp.zeros((N_KV * t_new, LANES), jnp.int32)
    for k in range(n_pick):
        best = jnp.max(score, axis=-1, keepdims=True)
        idx = jnp.min(jnp.where(score == best, blk, 1 << 20), axis=-1, keepdims=True)
        picks = jnp.where(out_lane == k, idx, picks)
        score = jnp.where(blk == idx, -2.0, score)
    sel_ref[0] = picks

    qr = q_rows(qr_ref)
    n_buf = win_ref.shape[1]
    k_buf = win_ref[0, :, 0:LANES].astype(BF16)
    v_buf = win_ref[0, :, LANES:2 * LANES].astype(BF16)
    row_tb = _mod_pow2(_row_iota((n_rows, n_buf)), t_new)
    i_buf = _lane_iota((n_rows, n_buf))
    dist = n_buf + row_tb - i_buf
    s_buf = jnp.where((dist >= 0) & (dist < WINDOW), _dot_nt(qr, k_buf), NEG)
    k_new = kvw_ref[:, 0:LANES]
    v_new = kvw_ref[:, LANES:2 * LANES]
    qr32 = qr.astype(F32)
    row_t1 = _mod_pow2(_row_iota((n_rows, 1)), t_new)
    s_new = [jnp.where(row_t1 >= i, jnp.sum(qr32 * k_new[i:i + 1, :].astype(BF16).astype(F32), axis=-1, keepdims=True), NEG)
             for i in range(t_new)]
    m = jnp.max(s_buf, axis=-1, keepdims=True)
    for s in s_new:
        m = jnp.maximum(m, s)
    e_buf = jnp.exp(s_buf - m)
    e_new = [jnp.exp(s - m) for s in s_new]
    denom = jnp.sum(e_buf, axis=-1, keepdims=True) + sum(e_new)
    o_win = _dot(e_buf.astype(BF16), v_buf)
    for i in range(t_new):
        o_win = o_win + e_new[i].astype(BF16).astype(F32) * v_new[i:i + 1, :].astype(BF16).astype(F32)
    o_win = o_win / denom

    for g in range(GQA):
        parts = []
        for h in range(N_KV):
            rows = slice((h * GQA + g) * t_new, (h * GQA + g + 1) * t_new)
            base = h * GQA * 3 + g * 3
            parts.append(_gate_col(gates, base) * o_cmp[rows] + _gate_col(gates, base + 2) * o_win[rows])
        ocw_ref[:, g * LANES:(g + 1) * LANES] = jnp.where(lane < HEAD_DIM, parts[0], parts[1])


def _cmpwin_sample_call(qc, qr, cmp_s, ov, win, kvw, gates, t_new, past_len, n_pick):
    n_seq = cmp_s.shape[0]
    tile = lambda b: (b, 0)
    return pl.pallas_call(
        functools.partial(_cmpwin_sample_kernel, past_len=past_len, n_pick=n_pick),
        grid=(n_seq,),
        in_specs=[
            pl.BlockSpec((t_new, D_B), tile),
            pl.BlockSpec((t_new, D_B), tile),
            pl.BlockSpec((1, cmp_s.shape[1], 2 * LANES), lambda b: (b, 0, 0)),
            pl.BlockSpec(ov.shape, lambda b: (0, 0)),
            pl.BlockSpec((1, win.shape[1], 2 * LANES), lambda b: (b, 0, 0)),
            pl.BlockSpec((t_new, 2 * LANES), tile),
            pl.BlockSpec((t_new, LANES), tile),
        ],
        out_specs=[
            pl.BlockSpec((t_new, D_B), tile),
            pl.BlockSpec((1, N_KV * t_new, LANES), lambda b: (b, 0, 0)),
        ],
        out_shape=[
            jax.ShapeDtypeStruct((n_seq * t_new, D_B), F32),
            jax.ShapeDtypeStruct((n_seq, N_KV * t_new, LANES), jnp.int32),
        ],
        compiler_params=pltpu.CompilerParams(dimension_semantics=("arbitrary",),
                                             vmem_limit_bytes=VMEM_LIMIT),
        name="cmpwin_sample",
    )(qc, qr, cmp_s, ov, win, kvw, gates)


def _sel_sample_kernel(sel_ref, pt_ref, cache_ref, qr_ref, rows_ref, gate_ref, ocw_ref, szb_ref,
                       out_ref, buf, sem, *, n_pick, n_pages, tb):
    t_new = qr_ref.shape[0]
    n_own = t_new * n_pick
    n_steps = pl.num_programs(0) * pl.num_programs(1)
    h = pl.program_id(1)
    step = pl.program_id(0) * pl.num_programs(1) + h
    blocks_per_page = PAGE_SIZE // SLC_BLOCK

    def block_copy(step_idx, slot, i, blk):
        seq = _div_pow2(step_idx, N_KV)
        head = _mod_pow2(step_idx, N_KV)
        phys = pt_ref[seq * n_pages + _div_pow2(blk, blocks_per_page)]
        return pltpu.make_async_copy(cache_ref.at[phys, pl.ds(2, 2), head], buf.at[slot, i], sem.at[slot])

    def for_all_blocks(step_idx, slot, fn):
        fn(block_copy(step_idx, slot, 0, 0))
        fn(block_copy(step_idx, slot, 1, tb - 1))

        def body(i, carry_):
            fn(block_copy(step_idx, slot, 2 + i, sel_ref[step_idx * n_own + i]))
            return carry_
        lax.fori_loop(0, n_own, body, 0)

    @pl.when(step == 0)
    def _():
        for_all_blocks(0, 0, lambda cp: cp.start())

    @pl.when(step + 1 < n_steps)
    def _():
        for_all_blocks(step + 1, (step + 1) % 2, lambda cp: cp.start())

    slot = step % 2
    for_all_blocks(step, slot, lambda cp: cp.wait())

    n_rows = GQA * t_new
    half = _div_pow2(_lane_iota((1, LANES)), SLC_BLOCK)

    def head_half(x):
        return jnp.where(h == 0, x[:, 0:HEAD_DIM], x[:, HEAD_DIM:LANES])

    qr = jnp.concatenate([head_half(qr_ref[:, g * LANES:(g + 1) * LANES]) for g in range(GQA)], axis=0)
    qrb = qr.astype(BF16)
    qr32 = qrb.astype(F32)
    row_t = _mod_pow2(_row_iota((n_rows, 1)), t_new)

    def slabs(first, count, kind):
        return jnp.concatenate([buf[slot, first + k, kind].astype(BF16) for k in range(count)], axis=1)

    bias_sh = jnp.concatenate(
        [jnp.where(half == blk % blocks_per_page, 0.0, NEG) for blk in (0, tb - 1)], axis=1)
    s_sh = _dot(qrb, slabs(0, 2, 0)) + bias_sh
    k_new = head_half(rows_ref[:, 2 * LANES:3 * LANES]).astype(BF16).astype(F32)
    v_new = head_half(rows_ref[:, 3 * LANES:4 * LANES]).astype(BF16).astype(F32)
    s_new = [jnp.where(row_t >= i, jnp.sum(qr32 * k_new[i:i + 1, :], axis=-1, keepdims=True), NEG)
             for i in range(t_new)]
    own_keys = n_pick * PAGE_SIZE
    s_own = jnp.zeros((n_rows, own_keys), F32)
    for t in range(t_new):
        bias_t = jnp.concatenate(
            [jnp.where(half == _mod_pow2(sel_ref[step * n_own + t * n_pick + k], blocks_per_page), 0.0, NEG)
             for k in range(n_pick)], axis=1)
        s_t = _dot(qrb, slabs(2 + t * n_pick, n_pick, 0)) + bias_t
        s_own = jnp.where(row_t == t, s_t, s_own)

    m = jnp.maximum(jnp.max(s_sh, axis=-1, keepdims=True), jnp.max(s_own, axis=-1, keepdims=True))
    for s in s_new:
        m = jnp.maximum(m, s)
    e_sh = jnp.exp(s_sh - m)
    e_own = jnp.exp(s_own - m)
    e_new = [jnp.exp(s - m) for s in s_new]
    denom = jnp.sum(e_sh, axis=-1, keepdims=True) + jnp.sum(e_own, axis=-1, keepdims=True) + sum(e_new)
    o = _dot_nt(e_sh.astype(BF16), slabs(0, 2, 1))
    for t in range(t_new):
        o = o + _dot_nt(jnp.where(row_t == t, e_own, 0.0).astype(BF16), slabs(2 + t * n_pick, n_pick, 1))
    for i in range(t_new):
        o = o + e_new[i].astype(BF16).astype(F32) * v_new[i:i + 1, :]
    o = o / denom

    gates = gate_ref[...]
    lane = _lane_iota((t_new, LANES))
    keep = _div_pow2(lane, HEAD_DIM) == h
    for g in range(GQA):
        sl = slice(g * LANES, (g + 1) * LANES)
        gate = jnp.sum(jnp.where(lane == h * GQA * 3 + g * 3 + 1, gates, 0.0), axis=-1, keepdims=True)
        o_g = o[g * t_new:(g + 1) * t_new]
        contrib = jnp.where(keep, gate * jnp.concatenate([o_g, o_g], axis=1), 0.0)

        @pl.when(h == 0)
        def _():
            out_ref[:, sl] = ocw_ref[:, sl] + contrib

        @pl.when(h == N_KV - 1)
        def _():
            out_ref[:, sl] = (out_ref[:, sl] + contrib) * szb_ref[:, sl]


def _sel_sample_call(sel_flat, pt_flat, cache_t, qr, rows, gates, ocw, szb, n_seq, t_new, n_pick, n_pages, tb):
    tile = lambda b, h, sel, pt: (b, 0)
    n_blocks = 2 + t_new * n_pick
    grid_spec = pltpu.PrefetchScalarGridSpec(
        num_scalar_prefetch=2,
        grid=(n_seq, N_KV),
        in_specs=[
            pl.BlockSpec(memory_space=pl.ANY),
            pl.BlockSpec((t_new, D_B), tile),
            pl.BlockSpec((t_new, 4 * LANES), tile),
            pl.BlockSpec((t_new, LANES), tile),
            pl.BlockSpec((t_new, D_B), tile),
            pl.BlockSpec((t_new, D_B), tile),
        ],
        out_specs=pl.BlockSpec((t_new, D_B), tile),
        scratch_shapes=[
            pltpu.VMEM((2, n_blocks, 2, HEAD_DIM, PAGE_SIZE), F32),
            pltpu.SemaphoreType.DMA((2,)),
        ],
    )
    return pl.pallas_call(
        functools.partial(_sel_sample_kernel, n_pick=n_pick, n_pages=n_pages, tb=tb),
        grid_spec=grid_spec,
        out_shape=jax.ShapeDtypeStruct((n_seq * t_new, D_B), F32),
        compiler_params=pltpu.CompilerParams(dimension_semantics=("arbitrary", "arbitrary"),
                                             vmem_limit_bytes=VMEM_LIMIT),
        name="sel_sample",
    )(sel_flat, pt_flat, cache_t, qr, rows, gates, ocw, szb)


def _merge_kernel(x_ref, a_ref, b_ref, wa_ref, wb_ref, fg_ref, y_ref):
    delta = _dot(a_ref[...].astype(BF16), wa_ref[...]) + _dot(b_ref[...].astype(BF16), wb_ref[...])
    x = x_ref[...] + delta
    ms = jnp.mean(x * x, axis=-1, keepdims=True)
    y_ref[...] = (x * lax.rsqrt(ms + EPS)) * fg_ref[...]


def _merge_call(x2d, a, b, wa, wb, fg):
    n_rows = x2d.shape[0]
    row = lambda i: (i, 0)
    const2 = lambda i: (0, 0)
    return pl.pallas_call(
        _merge_kernel,
        grid=(n_rows // TM,),
        in_specs=[
            pl.BlockSpec((TM, D_MODEL), row),
            pl.BlockSpec((TM, D_A), row),
            pl.BlockSpec((TM, D_B), row),
            pl.BlockSpec((D_A, D_MODEL), const2),
            pl.BlockSpec((D_B, D_MODEL), const2),
            pl.BlockSpec((1, D_MODEL), const2),
        ],
        out_specs=pl.BlockSpec((TM, D_MODEL), row),
        out_shape=jax.ShapeDtypeStruct((n_rows, D_MODEL), F32),
        compiler_params=pltpu.CompilerParams(dimension_semantics=("arbitrary",),
                                             vmem_limit_bytes=VMEM_LIMIT),
        name="merge",
    )(x2d, a, b, wa, wb, fg)


def _head_pair_perm(w_cols):
    lead = w_cols.shape[:-1]
    return w_cols.reshape(*lead, N_KV, GQA, HEAD_DIM).swapaxes(-3, -2).reshape(*lead, D_B)


def _prep_w_in(w_in):
    cuts = np.cumsum([D_A, D_A, D_A, D_B, 6 * N_KV * HEAD_DIM, 3 * N_HEADS]).tolist()
    u, v, za, q, kv, g, zb = jnp.split(w_in, cuts, axis=-1)
    g_pad = jnp.pad(g, ((0, 0), (0, LANES - g.shape[1])))
    return jnp.concatenate([u, v, za, _head_pair_perm(q), kv, _head_pair_perm(zb), g_pad], axis=-1).astype(BF16)


def _prep_mix(w_s, b_s, chunk_len):
    tril = w_s[:, :chunk_len, :chunk_len] * jnp.tril(jnp.ones((chunk_len, chunk_len), w_s.dtype))
    reps = TM // chunk_len
    eye = jnp.eye(reps, dtype=w_s.dtype)
    wmix = jnp.einsum("ab,gts->gatbs", eye, tril).reshape(A_GROUPS, TM, TM).astype(BF16)
    bias = jnp.repeat(b_s[:, :chunk_len].T, HEAD_DIM, axis=1)
    return wmix, jnp.tile(bias, (reps, 1))


def _rope_tables(pos):
    half = HEAD_DIM // 2
    inv = ROPE_THETA ** (-jnp.arange(half, dtype=F32) / half)
    ang = pos.astype(F32)[:, None] * inv
    cos = jnp.tile(jnp.cos(ang), (1, LANES // half))
    sin = jnp.tile(jnp.concatenate([-jnp.sin(ang), jnp.sin(ang)], axis=1), (1, LANES // HEAD_DIM))
    return cos, sin


def _prep_compress(cmp_pos, w_cmp1, b_cmp1, w_cmp2):
    ratio = CMP_BLOCK // CMP_STRIDE
    w1 = w_cmp1.reshape(2, ratio, CMP_STRIDE, HEAD_DIM, CMP_HIDDEN).transpose(0, 2, 3, 1, 4)
    w1 = w1.reshape(2, CMP_STRIDE, HEAD_DIM, ratio * CMP_HIDDEN)
    zeros = jnp.zeros_like(w1)
    wc = jnp.stack([jnp.concatenate([w1, zeros], axis=-1), jnp.concatenate([zeros, w1], axis=-1)], axis=2)
    wc = wc.reshape(2, CMP_STRIDE * LANES, N_KV * ratio * CMP_HIDDEN).astype(BF16)
    pe = cmp_pos.reshape(2, ratio, CMP_STRIDE, 1, HEAD_DIM)
    pe = jnp.broadcast_to(pe, (2, ratio, CMP_STRIDE, N_KV, HEAD_DIM)).reshape(2, ratio, CMP_STRIDE * LANES)
    pe = jnp.pad(pe, ((0, 0), (0, 16 - ratio), (0, 0))).astype(BF16)
    w2 = w_cmp2[:, None]
    zeros2 = jnp.zeros_like(w2)
    w2p = jnp.concatenate([jnp.concatenate([w2, zeros2], axis=-1), jnp.concatenate([zeros2, w2], axis=-1)],
                          axis=1).astype(BF16)
    return wc, pe, b_cmp1.reshape(2, 1, CMP_HIDDEN), w2p


def _overlap_matrix(n_cmp_slots, n_slc, n_lanes):
    start = (np.arange(n_cmp_slots)[:, None] - 1) * CMP_STRIDE
    j = np.arange(n_lanes)[None, :]
    ov = (start <= j * SLC_BLOCK + SLC_BLOCK - 1) & (start + CMP_BLOCK - 1 >= j * SLC_BLOCK)
    ov &= (np.arange(n_cmp_slots)[:, None] >= 1) & (j < n_slc)
    return jnp.asarray(ov, dtype=BF16)


def kernel(x_prompt, x_sample, cache_kv, state_win, page_table, norm_g, w_in, ln_g, ln_b, w_s, b_s,
           cmp_pos, w_cmp1, b_cmp1, w_cmp2, w_out, final_g):
    n_batch, seq, _ = x_prompt.shape
    n_seq, t_new, _ = x_sample.shape
    depth, n_phys = cache_kv.shape[:2]
    n_pages = page_table.shape[1]
    past_len = n_pages * PAGE_SIZE
    win_buf = state_win.shape[2]
    assert depth == 1 and seq % TQ == 0 and n_seq * t_new == TM and win_buf == WINDOW
    assert (past_len + t_new - 1) // SLC_BLOCK == past_len // SLC_BLOCK and past_len % CMP_STRIDE == 0

    w_all = _prep_w_in(w_in[0])
    ng = norm_g[0].reshape(1, D_MODEL)
    lng = ln_g[0].reshape(1, D_A)
    lnb = ln_b[0].reshape(1, D_A)
    fg = final_g.reshape(1, D_MODEL)
    wo = w_out[0]
    wo_a = wo[:D_A].astype(BF16)
    wo_b = _head_pair_perm(wo[D_A:].T).T.astype(BF16)
    wc, pe, b1, w2p = _prep_compress(cmp_pos[0], w_cmp1[0], b_cmp1[0], w_cmp2[0])

    xp = x_prompt.reshape(n_batch * seq, D_MODEL)
    wmix_p, bmix_p = _prep_mix(w_s[0], b_s[0], CHUNK)
    cos_p, sin_p = _rope_tables(jnp.arange(seq))
    ag_p, qc_p, qr_p, rows_p, kvw_pt, ksel_p, kwin_p, gate_p, szb_p, rows_pt = _proj_call(
        xp, ng, w_all, lng, lnb, wmix_p, bmix_p, cos_p, sin_p, BF16, seq)
    pages_p = seq // PAGE_SIZE
    ident = jnp.arange(n_batch * pages_p, dtype=jnp.int32).reshape(n_batch, pages_p)
    cmp_p = _compress_call(rows_p.reshape(n_batch * pages_p, PAGE_SIZE, 4 * LANES), ident, wc, pe, b1, w2p,
                           pages_p, False)
    ov_p = _overlap_matrix(seq // CMP_STRIDE, seq // SLC_BLOCK, LANES)
    bg_p = _attn_prompt_call(qc_p, qr_p, cmp_p, ov_p, ksel_p, kwin_p, gate_p, szb_p, n_batch, seq)
    y_p = _merge_call(xp, ag_p, bg_p, wo_a, wo_b, fg)

    xs = x_sample.reshape(n_seq * t_new, D_MODEL)
    wmix_s, bmix_s = _prep_mix(w_s[0], b_s[0], t_new)
    cos_s, sin_s = _rope_tables(jnp.tile(past_len + jnp.arange(t_new), n_seq))
    ag_s, qc_s, qr_s, rows_s, kvw_s, _, _, gate_s, szb_s, vln_s = _proj_call(
        xs, ng, w_all, lng, lnb, wmix_s, bmix_s, cos_s, sin_s, F32, None)
    cache_t = cache_kv[0].transpose(0, 2, 3, 4, 1)
    cmp_s = _compress_call(cache_t, page_table, wc, pe, b1, w2p, 32, True)
    n_slc_s = -(-(past_len + t_new) // SLC_BLOCK)
    tb = past_len // SLC_BLOCK
    n_pick = TOP_N - 3
    ov_s = _overlap_matrix(past_len // CMP_STRIDE, n_slc_s, -(-n_slc_s // LANES) * LANES)
    win_prev = state_win[0].reshape(n_seq, win_buf, 2 * LANES)
    ocw_s, sel_s = _cmpwin_sample_call(qc_s, qr_s, cmp_s, ov_s, win_prev, kvw_s, gate_s, t_new, past_len, n_pick)
    sel_flat = sel_s[:, :, :n_pick].reshape(-1)
    bg_s = _sel_sample_call(sel_flat, page_table.reshape(-1), cache_t, qr_s, rows_s, gate_s, ocw_s, szb_s,
                            n_seq, t_new, n_pick, n_pages, tb)
    y_s = _merge_call(xs, ag_s, bg_s, wo_a, wo_b, fg)

    new_kv_p = rows_pt.reshape(n_batch, 4, N_KV, HEAD_DIM, seq).transpose(0, 4, 1, 2, 3)
    new_win_p = kvw_pt[:, :, seq - win_buf:].reshape(n_batch, 2, N_KV, HEAD_DIM, win_buf).transpose(0, 4, 1, 2, 3)
    new_win_s = jnp.concatenate([state_win[0][:, t_new:], kvw_s.reshape(n_seq, t_new, 2, N_KV, HEAD_DIM)], axis=1)
    return (y_p.reshape(n_batch, seq, D_MODEL),
            y_s.reshape(n_seq, t_new, D_MODEL),
            new_kv_p[None],
            new_win_p[None],
            rows_s.reshape(1, n_seq, t_new, 4, N_KV, HEAD_DIM),
            new_win_s[None],
            vln_s.reshape(1, n_seq, t_new, D_A))
```

```python
import functools
import math

import jax
import jax.numpy as jnp
import numpy as np
from jax import lax
from jax.experimental import pallas as pl
from jax.experimental.pallas import tpu as pltpu

F32 = jnp.float32
BF16 = jnp.bfloat16

D_MODEL = 1024
HEAD_DIM = 64
D_A = 512
A_GROUPS = 8
CHUNK = 128
N_HEADS = 8
D_B = 512
N_KV = 2
GQA = 4
CMP_BLOCK = 32
CMP_STRIDE = 16
CMP_HIDDEN = 128
SLC_BLOCK = 64
TOP_N = 16
WINDOW = 512
ROPE_THETA = 10000.0
EPS = 1e-6
SCALE = HEAD_DIM ** -0.5
NEG = -1e30
FORCE = 1e9
PAGE_SIZE = 128

LANES = 128
VMEM_LIMIT = 56 * 1024 * 1024

C_U, C_V, C_ZA, C_Q, C_KV, C_ZB, C_G, C_END = 0, 512, 1024, 1536, 2048, 2816, 3328, 3456

TM = 256
TM_MERGE = 512
DMA_LOOP_UNROLL = 8
SAMPLE_SEQS_PER_STEP = 8
TQ = 256
SEGS_PER_PAGE = PAGE_SIZE // CMP_STRIDE


def _lane_iota(shape):
    return lax.broadcasted_iota(jnp.int32, shape, len(shape) - 1)


def _row_iota(shape):
    return lax.broadcasted_iota(jnp.int32, shape, len(shape) - 2)


def _div_pow2(x, n):
    assert n & (n - 1) == 0
    return lax.shift_right_logical(x, int(math.log2(n))) if n > 1 else x


def _mod_pow2(x, n):
    assert n & (n - 1) == 0
    return x & (n - 1)


def _dot(a, b):
    return jnp.dot(a, b, preferred_element_type=F32)


def _dot_nt(a, b):
    return lax.dot_general(a, b, (((1,), (1,)), ((), ())), preferred_element_type=F32)


def _rope(x, cos, sin_signed):
    lo = _mod_pow2(_lane_iota(x.shape), HEAD_DIM) < (HEAD_DIM // 2)
    swapped = jnp.where(lo, pltpu.roll(x, LANES - HEAD_DIM // 2, 1), pltpu.roll(x, HEAD_DIM // 2, 1))
    return x * cos + swapped * sin_signed


def _proj_kernel(x_ref, ng_ref, w_ref, lng_ref, lnb_ref, wmix_ref, bmix_ref, cos_ref, sin_ref,
                 ag_ref, qc_ref, qr_ref, rows_ref, kvw_ref, ksel_ref, kwin_ref, gate_ref, szb_ref,
                 extra_ref, *, prompt):
    x = x_ref[...]
    ms = jnp.mean(x * x, axis=-1, keepdims=True)
    hb = ((x * lax.rsqrt(ms + EPS)) * ng_ref[...]).astype(BF16)
    cos = cos_ref[...]
    sin = sin_ref[...]
    pair = 2 * LANES

    def proj(c0, n):
        return _dot(hb, w_ref[:, c0:c0 + n])

    def halves(x2):
        return x2[:, 0:LANES], x2[:, LANES:pair]

    v = jax.nn.gelu(proj(C_V, D_A))
    mu = jnp.mean(v, axis=-1, keepdims=True)
    vc = v - mu
    var = jnp.mean(vc * vc, axis=-1, keepdims=True)
    vln = vc * lax.rsqrt(var + EPS) * lng_ref[...] + lnb_ref[...]
    if not prompt:
        extra_ref[...] = vln
    vb = vln.astype(BF16)
    first_head = _lane_iota((x.shape[0], LANES)) < HEAD_DIM
    for gpp in range(A_GROUPS // 4):
        u2 = halves(jax.nn.gelu(proj(C_U + gpp * pair, pair)))
        za2 = halves(proj(C_ZA + gpp * pair, pair))
        for k in range(2):
            gp = 2 * gpp + k
            sl = slice(gp * LANES, (gp + 1) * LANES)
            blk = vb[:, sl]
            mixed = jnp.where(first_head, _dot(wmix_ref[2 * gp], blk), _dot(wmix_ref[2 * gp + 1], blk))
            mixed = mixed + bmix_ref[:, sl]
            ag_ref[:, sl] = (u2[k] * mixed * jax.nn.silu(za2[k])).astype(ag_ref.dtype)

    for gpp in range(GQA // 2):
        q2 = halves(proj(C_Q + gpp * pair, pair))
        for k in range(2):
            sl = slice((2 * gpp + k) * LANES, (2 * gpp + k + 1) * LANES)
            qc_ref[:, sl] = (q2[k] * SCALE).astype(qc_ref.dtype)
            qr_ref[:, sl] = (_rope(q2[k], cos, sin) * SCALE).astype(qr_ref.dtype)

    kc, vcm = halves(proj(C_KV, pair))
    ks, vs = halves(proj(C_KV + pair, pair))
    kw, vw = halves(proj(C_KV + 2 * pair, pair))
    ks = _rope(ks, cos, sin)
    kw = _rope(kw, cos, sin)
    for i, blk in enumerate((kc, vcm, ks, vs)):
        rows_ref[:, i * LANES:(i + 1) * LANES] = blk
        if prompt:
            extra_ref[0, i * LANES:(i + 1) * LANES, :] = blk.T
    for i, blk in enumerate((kw, vw)):
        if prompt:
            kvw_ref[0, i * LANES:(i + 1) * LANES, :] = blk.T
        else:
            kvw_ref[:, i * LANES:(i + 1) * LANES] = blk
    ksel_ref[:, 0:LANES] = ks.astype(BF16)
    ksel_ref[:, LANES:2 * LANES] = vs.astype(BF16)
    kwin_ref[:, 0:LANES] = kw.astype(BF16)
    kwin_ref[:, LANES:2 * LANES] = vw.astype(BF16)

    gate_ref[...] = jax.nn.sigmoid(proj(C_G, LANES))
    for gpp in range(GQA // 2):
        zb2 = halves(proj(C_ZB + gpp * pair, pair))
        for k in range(2):
            sl = slice((2 * gpp + k) * LANES, (2 * gpp + k + 1) * LANES)
            szb_ref[:, sl] = jax.nn.silu(zb2[k]).astype(szb_ref.dtype)


def _proj_call(x2d, ng, w, lng, lnb, wmix, bmix, cos_t, sin_t, act_dtype, seq_len):
    n_rows = x2d.shape[0]
    n_tiles = n_rows // TM
    pos_tiles = cos_t.shape[0] // TM
    row = lambda i: (i, 0)
    const2 = lambda i: (0, 0)
    out_shapes = [
        jax.ShapeDtypeStruct((n_rows, D_A), act_dtype),
        jax.ShapeDtypeStruct((n_rows, D_B), act_dtype),
        jax.ShapeDtypeStruct((n_rows, D_B), act_dtype),
        jax.ShapeDtypeStruct((n_rows, 4 * LANES), F32),
        jax.ShapeDtypeStruct((n_rows, 2 * LANES), F32),
        jax.ShapeDtypeStruct((n_rows, 2 * LANES), BF16),
        jax.ShapeDtypeStruct((n_rows, 2 * LANES), BF16),
        jax.ShapeDtypeStruct((n_rows, LANES), F32),
        jax.ShapeDtypeStruct((n_rows, D_B), act_dtype),
        jax.ShapeDtypeStruct((n_rows, D_A), F32),
    ]
    out_specs = [pl.BlockSpec((TM, s.shape[1]), row) for s in out_shapes]
    if seq_len is not None:
        tiles_per_seq = seq_len // TM
        dim_major = lambda i: (i // tiles_per_seq, 0, i % tiles_per_seq)
        for idx, width in ((4, 2 * LANES), (9, 4 * LANES)):
            out_shapes[idx] = jax.ShapeDtypeStruct((n_rows // seq_len, width, seq_len), F32)
            out_specs[idx] = pl.BlockSpec((1, width, TM), dim_major)
    return pl.pallas_call(
        functools.partial(_proj_kernel, prompt=seq_len is not None),
        grid=(n_tiles,),
        in_specs=[
            pl.BlockSpec((TM, D_MODEL), row),
            pl.BlockSpec((1, D_MODEL), const2),
            pl.BlockSpec((D_MODEL, C_END), const2),
            pl.BlockSpec((1, D_A), const2),
            pl.BlockSpec((1, D_A), const2),
            pl.BlockSpec((A_GROUPS, TM, TM), lambda i: (0, 0, 0)),
            pl.BlockSpec((TM, D_A), const2),
            pl.BlockSpec((TM, LANES), lambda i: (i % pos_tiles, 0)),
            pl.BlockSpec((TM, LANES), lambda i: (i % pos_tiles, 0)),
        ],
        out_specs=out_specs,
        out_shape=out_shapes,
        compiler_params=pltpu.CompilerParams(dimension_semantics=("arbitrary",),
                                             vmem_limit_bytes=VMEM_LIMIT),
        name="proj",
    )(x2d, ng, w, lng, lnb, wmix, bmix, cos_t, sin_t)


def _compress_kernel(pt_ref, src_ref, wc_ref, pe_ref, b1_ref, w2_ref, out_ref, buf, sem, carry,
                     *maybe_stage, pages_per_step):
    transposed_src = bool(maybe_stage)
    n_steps = pl.num_programs(0) * pl.num_programs(1)
    step = pl.program_id(0) * pl.num_programs(1) + pl.program_id(1)
    m = pages_per_step * SEGS_PER_PAGE

    def page_copies(step_idx, slot, i):
        phys = pt_ref[step_idx * pages_per_step + i]
        if transposed_src:
            return [pltpu.make_async_copy(src_ref.at[phys, pl.ds(0, 2)], maybe_stage[0].at[slot, i], sem.at[slot])]
        return [pltpu.make_async_copy(src_ref.at[phys, :, pl.ds(c * LANES, LANES)],
                                      buf.at[slot, c, pl.ds(i * PAGE_SIZE, PAGE_SIZE), :],
                                      sem.at[slot]) for c in range(2)]

    def start_step(step_idx, slot):
        def body(i, carry_):
            for cp in page_copies(step_idx, slot, i):
                cp.start()
            return carry_
        lax.fori_loop(0, pages_per_step, body, 0, unroll=DMA_LOOP_UNROLL)

    def wait_step(step_idx, slot):
        def body(i, carry_):
            for cp in page_copies(step_idx, slot, i):
                cp.wait()
            return carry_
        lax.fori_loop(0, pages_per_step, body, 0, unroll=DMA_LOOP_UNROLL)

    slot = step % 2

    first_of_seq = pl.program_id(1) == 0
    first_row = _row_iota((m, LANES)) == 0

    def compress_rows(rows_slot, seg_pitch):
        for c in range(2):
            pieces = [buf[rows_slot, c, pl.ds(j, m, stride=seg_pitch), :].astype(BF16)
                      for j in range(CMP_STRIDE)]
            part = _dot(jnp.concatenate(pieces, axis=1), wc_ref[c])
            pe_part = _dot(pe_ref[c], wc_ref[c])
            acc_out = None
            for h in range(N_KV):
                lo = slice(h * 2 * CMP_HIDDEN, h * 2 * CMP_HIDDEN + CMP_HIDDEN)
                hi = slice(h * 2 * CMP_HIDDEN + CMP_HIDDEN, (h + 1) * 2 * CMP_HIDDEN)
                bias = pe_part[0:1, lo] + pe_part[1:2, hi] + b1_ref[c]
                part0 = part[:, lo]
                prev_last = jnp.where(first_of_seq, 0.0, carry[c, h, 0:1, :])
                prev0 = jnp.where(first_row, prev_last, pltpu.roll(part0, 1, 0))
                carry[c, h, 0:1, :] = part0[m - 1:m, :]
                hid = jax.nn.gelu(prev0 + part[:, hi] + bias).astype(BF16)
                contrib = _dot(hid, w2_ref[c, h])
                acc_out = contrib if acc_out is None else acc_out + contrib
            out_ref[0, :, c * LANES:(c + 1) * LANES] = acc_out.astype(out_ref.dtype)

    if transposed_src:
        stage = maybe_stage[0]
        seg_pitch = CMP_STRIDE + 1

        def transpose_step(stage_slot):
            for i in range(pages_per_step):
                for c in range(2):
                    rows = stage[stage_slot, i, c].reshape(N_KV * HEAD_DIM, PAGE_SIZE).T
                    for s in range(SEGS_PER_PAGE):
                        r0 = (i * SEGS_PER_PAGE + s) * seg_pitch
                        buf[stage_slot, c, r0:r0 + CMP_STRIDE, :] = rows[s * CMP_STRIDE:(s + 1) * CMP_STRIDE]

        @pl.when(step == 0)
        def _():
            start_step(0, 0)
            start_step(1, 1)
            wait_step(0, 0)
            transpose_step(0)

        @pl.when(step + 1 < n_steps)
        def _():
            wait_step(step + 1, (step + 1) % 2)

        @pl.when(step + 2 < n_steps)
        def _():
            start_step(step + 2, slot)

        for parity in range(2):
            @pl.when(slot == parity)
            def _():
                transpose_step(1 - parity)
                compress_rows(parity, seg_pitch)
    else:
        @pl.when(step == 0)
        def _():
            start_step(0, 0)

        @pl.when(step + 1 < n_steps)
        def _():
            start_step(step + 1, (step + 1) % 2)

        wait_step(step, slot)
        compress_rows(slot, CMP_STRIDE)


def _compress_call(src, page_table, wc, pe, b1, w2p, pages_per_step, transposed_src):
    n_seq, n_pages = page_table.shape
    n_chunks = n_pages // pages_per_step
    m = pages_per_step * SEGS_PER_PAGE
    assert not transposed_src or n_seq * n_chunks >= 2
    seg_pitch = CMP_STRIDE + 1 if transposed_src else CMP_STRIDE
    scratch = [
        pltpu.VMEM((2, 2, m * seg_pitch, LANES), F32),
        pltpu.SemaphoreType.DMA((2,)),
        pltpu.VMEM((2, N_KV, 8, CMP_HIDDEN), F32),
    ]
    if transposed_src:
        scratch.append(pltpu.VMEM((2, pages_per_step, 2, N_KV, HEAD_DIM, PAGE_SIZE), F32))
    grid_spec = pltpu.PrefetchScalarGridSpec(
        num_scalar_prefetch=1,
        grid=(n_seq, n_chunks),
        in_specs=[
            pl.BlockSpec(memory_space=pl.ANY),
            pl.BlockSpec((2, CMP_STRIDE * LANES, 4 * CMP_HIDDEN), lambda b, k, pt: (0, 0, 0)),
            pl.BlockSpec((2, 16, CMP_STRIDE * LANES), lambda b, k, pt: (0, 0, 0)),
            pl.BlockSpec((2, 1, CMP_HIDDEN), lambda b, k, pt: (0, 0, 0)),
            pl.BlockSpec((2, N_KV, CMP_HIDDEN, LANES), lambda b, k, pt: (0, 0, 0, 0)),
        ],
        out_specs=pl.BlockSpec((1, m, 2 * LANES), lambda b, k, pt: (b, k, 0)),
        scratch_shapes=scratch,
    )
    return pl.pallas_call(
        functools.partial(_compress_kernel, pages_per_step=pages_per_step),
        grid_spec=grid_spec,
        out_shape=jax.ShapeDtypeStruct((n_seq, n_pages * SEGS_PER_PAGE, 2 * LANES), BF16),
        compiler_params=pltpu.CompilerParams(dimension_semantics=("arbitrary", "arbitrary"),
                                             vmem_limit_bytes=VMEM_LIMIT),
        name="compress",
    )(page_table.reshape(-1), src, wc, pe, b1, w2p)


def _softmax_rows(s, mask):
    s = jnp.where(mask, s, NEG)
    e = jnp.where(mask, jnp.exp(s - jnp.max(s, axis=-1, keepdims=True)), 0.0)
    return e / jnp.maximum(jnp.sum(e, axis=-1, keepdims=True), 1.0)


def _dot_split(p, w):
    hi = p.astype(BF16)
    lo = (p - hi.astype(F32)).astype(BF16)
    return _dot(hi, w) + _dot(lo, w)


def _gate_col(gates, idx):
    return gates[:, idx:idx + 1]


def _attn_prompt_kernel(qc_ref, qr_ref, cmp_ref, ov_ref, eg_ref, ksel_ref, kwin_ref, gate_ref, szb_ref,
                        out_ref, q_sc, m_sc, acc_sc, cmp_sc):
    qi = pl.program_id(1)
    lane = _lane_iota((TQ, LANES))
    lo_half = lane < HEAD_DIM
    tq = qi * TQ + _row_iota((TQ, LANES))
    r_minus_c = _row_iota((TQ, TQ)) - _lane_iota((TQ, TQ))
    causal_bias = jnp.where(r_minus_c >= 0, 0.0, NEG)
    far_bias = jnp.where(r_minus_c < 0, 0.0, NEG)
    kcmp = cmp_ref[0, :, 0:LANES]
    vcmp = cmp_ref[0, :, LANES:2 * LANES]
    cmp_valid = (lane >= 1) & (lane * CMP_STRIDE + (CMP_STRIDE - 1) <= tq)
    n_slc = ksel_ref.shape[0] // SLC_BLOCK
    blocks_per_tile = TQ // SLC_BLOCK

    def flash_tile(br, h, k_tile, v_aug, bias):
        width = k_tile.shape[0]
        s = _dot_nt(q_sc[h], k_tile)
        if bias is not None:
            s = (s.reshape(GQA, TQ, width) + bias[None]).reshape(GQA * TQ, width)
        m_prev = m_sc[h]
        m_new = jnp.maximum(m_prev, jnp.max(s, axis=-1, keepdims=True))
        alpha = jnp.exp(m_prev - m_new)
        p = jnp.exp(s - jnp.concatenate([m_new] * (width // LANES), axis=1))
        acc_sc[br, h] = alpha * acc_sc[br, h] + _dot(p.astype(BF16), v_aug)
        m_sc[h] = m_new

    def flash_init(br):
        m_sc[...] = jnp.full(m_sc.shape, NEG, F32)
        acc_sc[br] = jnp.zeros(acc_sc.shape[1:], F32)

    for h in range(N_KV):
        keep = _div_pow2(lane, HEAD_DIM) == h
        other_off = HEAD_DIM * (1 - h)

        p_sum = jnp.zeros((TQ, LANES), F32)
        for g in range(GQA):
            sl = slice(g * LANES, (g + 1) * LANES)
            qc = jnp.where(keep, qc_ref[:, sl], 0)
            p = _softmax_rows(_dot_nt(qc, kcmp), cmp_valid)
            p_sum = p_sum + p
            o_cmp = _dot(p.astype(BF16), vcmp)
            cmp_sc[:, sl] = o_cmp if h == 0 else jnp.where(keep, o_cmp, cmp_sc[:, sl])
        imp = _dot_split(p_sum, ov_ref[...])
        tb = _div_pow2(tq, SLC_BLOCK)
        forced = (lane == 0) | (lane == tb) | (lane == tb - 1)
        valid = lane * SLC_BLOCK <= tq
        score = jnp.where(forced, FORCE, jnp.where(valid, imp, NEG))
        score_t = score.T[0:n_slc, :]
        blk = _row_iota((n_slc, TQ))
        rank = jnp.zeros((n_slc, TQ), jnp.int32)
        for j in range(n_slc):
            other = score_t[j:j + 1, :]
            ahead = (other > score_t) | ((other == score_t) & (blk > j))
            rank = rank + ahead.astype(jnp.int32)
        sel_bias_t = jnp.where(rank < TOP_N, 0.0, NEG)
        pieces = [jnp.zeros((other_off, TQ), F32)] if other_off else []
        pieces += [sel_bias_t, jnp.zeros((LANES - other_off - n_slc, TQ), F32)]
        sel_bias = jnp.concatenate(pieces, axis=0).T.astype(BF16)
        for g in range(GQA):
            sl = slice(g * LANES, (g + 1) * LANES)
            q_sc[h, g * TQ:(g + 1) * TQ, :] = jnp.where(keep, qr_ref[:, sl], sel_bias)

    def span_operands(kv_ref, kt, n_tiles, h, other_lanes):
        k = kv_ref[pl.ds(kt * TQ, n_tiles * TQ), 0:LANES]
        v = kv_ref[pl.ds(kt * TQ, n_tiles * TQ), LANES:2 * LANES]
        own = _div_pow2(_lane_iota(k.shape), HEAD_DIM) == h
        return jnp.where(own, k, other_lanes(k.shape)), jnp.where(own, v, 1.0)

    def sel_span(kt, n_tiles, bias):
        for h in range(N_KV):
            def indicator(shape, h=h):
                first = HEAD_DIM * (1 - h) + kt * blocks_per_tile
                hit = _lane_iota(shape) == first + _div_pow2(_row_iota(shape), SLC_BLOCK)
                return jnp.where(hit, 1, 0).astype(BF16)
            flash_tile(0, h, *span_operands(ksel_ref, kt, n_tiles, h, indicator), bias)

    flash_init(0)

    def sel_pair(i, carry_):
        sel_span(2 * i, 2, None)
        return carry_
    lax.fori_loop(0, _div_pow2(qi, 2), sel_pair, 0)

    @pl.when(_mod_pow2(qi, 2) == 1)
    def _():
        sel_span(qi - 1, 1, None)
    sel_span(qi, 1, causal_bias)

    def win_span(kt, n_tiles, bias):
        for h in range(N_KV):
            flash_tile(1, h, *span_operands(kwin_ref, kt, n_tiles, h, lambda shape: jnp.zeros(shape, BF16)), bias)

    flash_init(1)
    far = WINDOW // TQ
    assert far == 2

    @pl.when(qi >= far)
    def _():
        win_span(qi - far, 1, far_bias)

    @pl.when(qi >= 1)
    def _():
        win_span(qi - 1, 2, jnp.concatenate([jnp.zeros((TQ, TQ), F32), causal_bias], axis=1))

    @pl.when(qi == 0)
    def _():
        win_span(qi, 1, causal_bias)

    gates_x = _dot_split(gate_ref[...], eg_ref[...])
    for g in range(GQA):
        rows = slice(g * TQ, (g + 1) * TQ)
        sl = slice(g * LANES, (g + 1) * LANES)
        o = gates_x[:, (g * 3) * LANES:(g * 3 + 1) * LANES] * cmp_sc[:, sl]
        for br in range(2):
            a0 = acc_sc[br, 0, rows, :]
            a1 = acc_sc[br, 1, rows, :]
            num = jnp.where(lo_half, a0, a1)
            den = pltpu.roll(jnp.where(lo_half, a1, a0), HEAD_DIM, 1)
            o = o + gates_x[:, (g * 3 + 1 + br) * LANES:(g * 3 + 2 + br) * LANES] * (num / den)
        out_ref[:, sl] = (o * szb_ref[:, sl].astype(F32)).astype(out_ref.dtype)


def _gate_expansion():
    eg = np.zeros((LANES, GQA * 3 * LANES), np.float32)
    for h in range(N_KV):
        for g in range(GQA):
            for br in range(3):
                c0 = (g * 3 + br) * LANES + h * HEAD_DIM
                eg[h * GQA * 3 + g * 3 + br, c0:c0 + HEAD_DIM] = 1.0
    return jnp.asarray(eg, dtype=BF16)


def _attn_prompt_call(qc, qr, cmp_p, ov, ksel, kwin, gates, szb, n_batch, seq):
    nq = seq // TQ
    tile = lambda b, q: (b * nq + q, 0)
    whole = lambda b, q: (b, 0)
    return pl.pallas_call(
        _attn_prompt_kernel,
        grid=(n_batch, nq),
        in_specs=[
            pl.BlockSpec((TQ, D_B), tile),
            pl.BlockSpec((TQ, D_B), tile),
            pl.BlockSpec((1, LANES, 2 * LANES), lambda b, q: (b, 0, 0)),
            pl.BlockSpec((LANES, LANES), lambda b, q: (0, 0)),
            pl.BlockSpec((LANES, GQA * 3 * LANES), lambda b, q: (0, 0)),
            pl.BlockSpec((seq, 2 * LANES), whole),
            pl.BlockSpec((seq, 2 * LANES), whole),
            pl.BlockSpec((TQ, LANES), tile),
            pl.BlockSpec((TQ, D_B), tile),
        ],
        out_specs=pl.BlockSpec((TQ, D_B), tile),
        out_shape=jax.ShapeDtypeStruct((n_batch * seq, D_B), BF16),
        scratch_shapes=[
            pltpu.VMEM((N_KV, GQA * TQ, LANES), BF16),
            pltpu.VMEM((N_KV, GQA * TQ, LANES), F32),
            pltpu.VMEM((2, N_KV, GQA * TQ, LANES), F32),
            pltpu.VMEM((TQ, D_B), F32),
        ],
        compiler_params=pltpu.CompilerParams(dimension_semantics=("arbitrary", "arbitrary"),
                                             vmem_limit_bytes=VMEM_LIMIT),
        name="attn_prompt",
    )(qc, qr, cmp_p, ov, _gate_expansion(), ksel, kwin, gates, szb)


def _cmpwin_sample_kernel(qc_ref, qr_ref, cmp_ref, ov_ref, win_ref, kvw_ref, gate_ref,
                          ocw_ref, sel_ref, *, past_len, n_pick, t_new):
    seqs = cmp_ref.shape[0]
    n_rows = N_KV * GQA * t_new
    lane = _lane_iota((t_new, LANES))
    n_cmp = cmp_ref.shape[1]
    row_t = _mod_pow2(_row_iota((n_rows, n_cmp)), t_new)
    slot = _lane_iota((n_rows, n_cmp))
    cmp_valid = (slot >= 1) & (slot * CMP_STRIDE + (CMP_STRIDE - 1) <= past_len + row_t)
    n_buf = win_ref.shape[1]
    dist = n_buf + _mod_pow2(_row_iota((n_rows, n_buf)), t_new) - _lane_iota((n_rows, n_buf))
    win_valid = (dist >= 0) & (dist < WINDOW)
    row_t1 = _mod_pow2(_row_iota((n_rows, 1)), t_new)

    def one_sequence(sq):
        rows_sq = slice(sq * t_new, (sq + 1) * t_new)
        gates = gate_ref[rows_sq, :]

        def q_rows(ref):
            return jnp.concatenate(
                [jnp.where(_div_pow2(lane, HEAD_DIM) == h, ref[rows_sq, g * LANES:(g + 1) * LANES], 0).astype(BF16)
                 for h in range(N_KV) for g in range(GQA)], axis=0)

        p = _softmax_rows(_dot_nt(q_rows(qc_ref), cmp_ref[sq, :, 0:LANES]), cmp_valid)
        o_cmp = _dot(p.astype(BF16), cmp_ref[sq, :, LANES:2 * LANES])
        p_sum = jnp.concatenate(
            [sum(p[(h * GQA + g) * t_new:(h * GQA + g + 1) * t_new] for g in range(GQA)) for h in range(N_KV)],
            axis=0)
        imp = _dot_split(p_sum, ov_ref[...])

        qr = q_rows(qr_ref)
        k_buf = win_ref[sq, :, 0:LANES].astype(BF16)
        v_buf = win_ref[sq, :, LANES:2 * LANES].astype(BF16)
        s_buf = jnp.where(win_valid, _dot_nt(qr, k_buf), NEG)
        k_new = kvw_ref[rows_sq, 0:LANES].astype(BF16).astype(F32)
        v_new = kvw_ref[rows_sq, LANES:2 * LANES].astype(BF16).astype(F32)
        qr32 = qr.astype(F32)
        s_new = [jnp.where(row_t1 >= i, jnp.sum(qr32 * k_new[i:i + 1, :], axis=-1, keepdims=True), NEG)
                 for i in range(t_new)]
        m = jnp.max(s_buf, axis=-1, keepdims=True)
        for s in s_new:
            m = jnp.maximum(m, s)
        e_buf = jnp.exp(s_buf - m)
        e_new = [jnp.exp(s - m) for s in s_new]
        denom = jnp.sum(e_buf, axis=-1, keepdims=True) + sum(e_new)
        o_win = _dot(e_buf.astype(BF16), v_buf)
        for i in range(t_new):
            o_win = o_win + e_new[i].astype(BF16).astype(F32) * v_new[i:i + 1, :]
        o_win = o_win / denom

        for g in range(GQA):
            parts = []
            for h in range(N_KV):
                rows = slice((h * GQA + g) * t_new, (h * GQA + g + 1) * t_new)
                base = h * GQA * 3 + g * 3
                parts.append(_gate_col(gates, base) * o_cmp[rows] + _gate_col(gates, base + 2) * o_win[rows])
            ocw_ref[rows_sq, g * LANES:(g + 1) * LANES] = jnp.where(lane < HEAD_DIM, parts[0], parts[1])
        return imp

    imp = jnp.concatenate([one_sequence(sq) for sq in range(seqs)], axis=0)
    blk = _lane_iota(imp.shape)
    tb = (past_len + t_new - 1) // SLC_BLOCK
    candidate = (blk >= 1) & (blk < tb - 1)
    score = jnp.where(candidate, imp, -1.0)
    out_lane = _lane_iota((imp.shape[0], LANES))
    picks = jnp.zeros((imp.shape[0], LANES), jnp.int32)
    for k in range(n_pick):
        best = jnp.max(score, axis=-1, keepdims=True)
        idx = jnp.min(jnp.where(score == best, blk, 1 << 20), axis=-1, keepdims=True)
        picks = jnp.where(out_lane == k, idx, picks)
        score = jnp.where(blk == idx, -2.0, score)
    sel_ref[...] = picks.reshape(sel_ref.shape)


def _cmpwin_sample_call(qc, qr, cmp_s, ov, win, kvw, gates, t_new, past_len, n_pick):
    n_seq = cmp_s.shape[0]
    seqs = SAMPLE_SEQS_PER_STEP
    assert n_seq % seqs == 0
    tile = lambda b: (b, 0)
    per_seq = lambda b: (b, 0, 0)
    return pl.pallas_call(
        functools.partial(_cmpwin_sample_kernel, past_len=past_len, n_pick=n_pick, t_new=t_new),
        grid=(n_seq // seqs,),
        in_specs=[
            pl.BlockSpec((seqs * t_new, D_B), tile),
            pl.BlockSpec((seqs * t_new, D_B), tile),
            pl.BlockSpec((seqs, cmp_s.shape[1], 2 * LANES), per_seq),
            pl.BlockSpec(ov.shape, lambda b: (0, 0)),
            pl.BlockSpec((seqs, win.shape[1], 2 * LANES), per_seq),
            pl.BlockSpec((seqs * t_new, 2 * LANES), tile),
            pl.BlockSpec((seqs * t_new, LANES), tile),
        ],
        out_specs=[
            pl.BlockSpec((seqs * t_new, D_B), tile),
            pl.BlockSpec((seqs, N_KV * t_new, LANES), per_seq),
        ],
        out_shape=[
            jax.ShapeDtypeStruct((n_seq * t_new, D_B), F32),
            jax.ShapeDtypeStruct((n_seq, N_KV * t_new, LANES), jnp.int32),
        ],
        compiler_params=pltpu.CompilerParams(dimension_semantics=("arbitrary",),
                                             vmem_limit_bytes=VMEM_LIMIT),
        name="cmpwin_sample",
    )(qc, qr, cmp_s, ov, win, kvw, gates)


def _sel_sample_kernel(sel_ref, pt_ref, cache_ref, qr_ref, rows_ref, gate_ref, ocw_ref, szb_ref,
                       out_ref, buf, sem, *, n_pick, n_pages, tb):
    t_new = qr_ref.shape[0]
    n_own = t_new * n_pick
    n_steps = pl.num_programs(0) * pl.num_programs(1)
    h = pl.program_id(1)
    step = pl.program_id(0) * pl.num_programs(1) + h
    blocks_per_page = PAGE_SIZE // SLC_BLOCK

    def block_copy(step_idx, slot, i, blk):
        seq = _div_pow2(step_idx, N_KV)
        head = _mod_pow2(step_idx, N_KV)
        phys = pt_ref[seq * n_pages + _div_pow2(blk, blocks_per_page)]
        return pltpu.make_async_copy(cache_ref.at[phys, pl.ds(2, 2), head], buf.at[slot, i], sem.at[slot])

    def for_all_blocks(step_idx, slot, fn):
        fn(block_copy(step_idx, slot, 0, 0))
        fn(block_copy(step_idx, slot, 1, tb - 1))

        def body(i, carry_):
            fn(block_copy(step_idx, slot, 2 + i, sel_ref[step_idx * n_own + i]))
            return carry_
        lax.fori_loop(0, n_own, body, 0, unroll=DMA_LOOP_UNROLL)

    @pl.when(step == 0)
    def _():
        for_all_blocks(0, 0, lambda cp: cp.start())

    @pl.when(step + 1 < n_steps)
    def _():
        for_all_blocks(step + 1, (step + 1) % 2, lambda cp: cp.start())

    slot = step % 2
    for_all_blocks(step, slot, lambda cp: cp.wait())

    n_rows = GQA * t_new
    half = _div_pow2(_lane_iota((1, LANES)), SLC_BLOCK)

    def head_half(x):
        return jnp.where(h == 0, x[:, 0:HEAD_DIM], x[:, HEAD_DIM:LANES])

    qr = jnp.concatenate([head_half(qr_ref[:, g * LANES:(g + 1) * LANES]) for g in range(GQA)], axis=0)
    qrb = qr.astype(BF16)
    qr32 = qrb.astype(F32)
    row_t = _mod_pow2(_row_iota((n_rows, 1)), t_new)

    def slabs(first, count, kind):
        return jnp.concatenate([buf[slot, first + k, kind].astype(BF16) for k in range(count)], axis=1)

    bias_sh = jnp.concatenate(
        [jnp.where(half == blk % blocks_per_page, 0.0, NEG) for blk in (0, tb - 1)], axis=1)
    s_sh = _dot(qrb, slabs(0, 2, 0)) + bias_sh
    k_new = head_half(rows_ref[:, 2 * LANES:3 * LANES]).astype(BF16).astype(F32)
    v_new = head_half(rows_ref[:, 3 * LANES:4 * LANES]).astype(BF16).astype(F32)
    s_new = [jnp.where(row_t >= i, jnp.sum(qr32 * k_new[i:i + 1, :], axis=-1, keepdims=True), NEG)
             for i in range(t_new)]
    own_keys = n_pick * PAGE_SIZE
    s_own = jnp.zeros((n_rows, own_keys), F32)
    for t in range(t_new):
        bias_t = jnp.concatenate(
            [jnp.where(half == _mod_pow2(sel_ref[step * n_own + t * n_pick + k], blocks_per_page), 0.0, NEG)
             for k in range(n_pick)], axis=1)
        s_t = _dot(qrb, slabs(2 + t * n_pick, n_pick, 0)) + bias_t
        s_own = jnp.where(row_t == t, s_t, s_own)

    m = jnp.maximum(jnp.max(s_sh, axis=-1, keepdims=True), jnp.max(s_own, axis=-1, keepdims=True))
    for s in s_new:
        m = jnp.maximum(m, s)
    e_sh = jnp.exp(s_sh - m)
    e_own = jnp.exp(s_own - m)
    e_new = [jnp.exp(s - m) for s in s_new]
    denom = jnp.sum(e_sh, axis=-1, keepdims=True) + jnp.sum(e_own, axis=-1, keepdims=True) + sum(e_new)
    o = _dot_nt(e_sh.astype(BF16), slabs(0, 2, 1))
    for t in range(t_new):
        o = o + _dot_nt(jnp.where(row_t == t, e_own, 0.0).astype(BF16), slabs(2 + t * n_pick, n_pick, 1))
    for i in range(t_new):
        o = o + e_new[i].astype(BF16).astype(F32) * v_new[i:i + 1, :]
    o = o / denom

    gates = gate_ref[...]
    lane = _lane_iota((t_new, LANES))
    keep = _div_pow2(lane, HEAD_DIM) == h
    for g in range(GQA):
        sl = slice(g * LANES, (g + 1) * LANES)
        gate = jnp.sum(jnp.where(lane == h * GQA * 3 + g * 3 + 1, gates, 0.0), axis=-1, keepdims=True)
        o_g = o[g * t_new:(g + 1) * t_new]
        contrib = jnp.where(keep, gate * jnp.concatenate([o_g, o_g], axis=1), 0.0)

        @pl.when(h == 0)
        def _():
            out_ref[:, sl] = ocw_ref[:, sl] + contrib

        @pl.when(h == N_KV - 1)
        def _():
            out_ref[:, sl] = (out_ref[:, sl] + contrib) * szb_ref[:, sl]


def _sel_sample_call(sel_flat, pt_flat, cache_t, qr, rows, gates, ocw, szb, n_seq, t_new, n_pick, n_pages, tb):
    tile = lambda b, h, sel, pt: (b, 0)
    n_blocks = 2 + t_new * n_pick
    grid_spec = pltpu.PrefetchScalarGridSpec(
        num_scalar_prefetch=2,
        grid=(n_seq, N_KV),
        in_specs=[
            pl.BlockSpec(memory_space=pl.ANY),
            pl.BlockSpec((t_new, D_B), tile),
            pl.BlockSpec((t_new, 4 * LANES), tile),
            pl.BlockSpec((t_new, LANES), tile),
            pl.BlockSpec((t_new, D_B), tile),
            pl.BlockSpec((t_new, D_B), tile),
        ],
        out_specs=pl.BlockSpec((t_new, D_B), tile),
        scratch_shapes=[
            pltpu.VMEM((2, n_blocks, 2, HEAD_DIM, PAGE_SIZE), F32),
            pltpu.SemaphoreType.DMA((2,)),
        ],
    )
    return pl.pallas_call(
        functools.partial(_sel_sample_kernel, n_pick=n_pick, n_pages=n_pages, tb=tb),
        grid_spec=grid_spec,
        out_shape=jax.ShapeDtypeStruct((n_seq * t_new, D_B), F32),
        compiler_params=pltpu.CompilerParams(dimension_semantics=("arbitrary", "arbitrary"),
                                             vmem_limit_bytes=VMEM_LIMIT),
        name="sel_sample",
    )(sel_flat, pt_flat, cache_t, qr, rows, gates, ocw, szb)


def _merge_kernel(x_ref, a_ref, b_ref, wa_ref, wb_ref, fg_ref, y_ref):
    delta = _dot(a_ref[...].astype(BF16), wa_ref[...]) + _dot(b_ref[...].astype(BF16), wb_ref[...])
    x = x_ref[...] + delta
    ms = jnp.mean(x * x, axis=-1, keepdims=True)
    y_ref[...] = (x * lax.rsqrt(ms + EPS)) * fg_ref[...]


def _merge_call(x2d, a, b, wa, wb, fg):
    n_rows = x2d.shape[0]
    tm = min(TM_MERGE, n_rows)
    row = lambda i: (i, 0)
    const2 = lambda i: (0, 0)
    return pl.pallas_call(
        _merge_kernel,
        grid=(n_rows // tm,),
        in_specs=[
            pl.BlockSpec((tm, D_MODEL), row),
            pl.BlockSpec((tm, D_A), row),
            pl.BlockSpec((tm, D_B), row),
            pl.BlockSpec((D_A, D_MODEL), const2),
            pl.BlockSpec((D_B, D_MODEL), const2),
            pl.BlockSpec((1, D_MODEL), const2),
        ],
        out_specs=pl.BlockSpec((tm, D_MODEL), row),
        out_shape=jax.ShapeDtypeStruct((n_rows, D_MODEL), F32),
        compiler_params=pltpu.CompilerParams(dimension_semantics=("arbitrary",),
                                             vmem_limit_bytes=VMEM_LIMIT),
        name="merge",
    )(x2d, a, b, wa, wb, fg)


def _head_pair_perm(w_cols):
    lead = w_cols.shape[:-1]
    return w_cols.reshape(*lead, N_KV, GQA, HEAD_DIM).swapaxes(-3, -2).reshape(*lead, D_B)


def _prep_w_in(w_in):
    cuts = np.cumsum([D_A, D_A, D_A, D_B, 6 * N_KV * HEAD_DIM, 3 * N_HEADS]).tolist()
    u, v, za, q, kv, g, zb = jnp.split(w_in, cuts, axis=-1)
    g_pad = jnp.pad(g, ((0, 0), (0, LANES - g.shape[1])))
    return jnp.concatenate([u, v, za, _head_pair_perm(q), kv, _head_pair_perm(zb), g_pad], axis=-1).astype(BF16)


def _prep_mix(w_s, b_s, chunk_len):
    tril = w_s[:, :chunk_len, :chunk_len] * jnp.tril(jnp.ones((chunk_len, chunk_len), w_s.dtype))
    reps = TM // chunk_len
    eye = jnp.eye(reps, dtype=w_s.dtype)
    wmix = jnp.einsum("ab,gts->gatbs", eye, tril).reshape(A_GROUPS, TM, TM).astype(BF16)
    bias = jnp.repeat(b_s[:, :chunk_len].T, HEAD_DIM, axis=1)
    return wmix, jnp.tile(bias, (reps, 1))


def _rope_tables(pos):
    half = HEAD_DIM // 2
    inv = ROPE_THETA ** (-jnp.arange(half, dtype=F32) / half)
    ang = pos.astype(F32)[:, None] * inv
    cos = jnp.tile(jnp.cos(ang), (1, LANES // half))
    sin = jnp.tile(jnp.concatenate([-jnp.sin(ang), jnp.sin(ang)], axis=1), (1, LANES // HEAD_DIM))
    return cos, sin


def _prep_compress(cmp_pos, w_cmp1, b_cmp1, w_cmp2):
    ratio = CMP_BLOCK // CMP_STRIDE
    w1 = w_cmp1.reshape(2, ratio, CMP_STRIDE, HEAD_DIM, CMP_HIDDEN).transpose(0, 2, 3, 1, 4)
    w1 = w1.reshape(2, CMP_STRIDE, HEAD_DIM, ratio * CMP_HIDDEN)
    zeros = jnp.zeros_like(w1)
    wc = jnp.stack([jnp.concatenate([w1, zeros], axis=-1), jnp.concatenate([zeros, w1], axis=-1)], axis=2)
    wc = wc.reshape(2, CMP_STRIDE * LANES, N_KV * ratio * CMP_HIDDEN).astype(BF16)
    pe = cmp_pos.reshape(2, ratio, CMP_STRIDE, 1, HEAD_DIM)
    pe = jnp.broadcast_to(pe, (2, ratio, CMP_STRIDE, N_KV, HEAD_DIM)).reshape(2, ratio, CMP_STRIDE * LANES)
    pe = jnp.pad(pe, ((0, 0), (0, 16 - ratio), (0, 0))).astype(BF16)
    w2 = w_cmp2[:, None]
    zeros2 = jnp.zeros_like(w2)
    w2p = jnp.concatenate([jnp.concatenate([w2, zeros2], axis=-1), jnp.concatenate([zeros2, w2], axis=-1)],
                          axis=1).astype(BF16)
    return wc, pe, b_cmp1.reshape(2, 1, CMP_HIDDEN), w2p


def _overlap_matrix(n_cmp_slots, n_slc, n_lanes):
    start = (np.arange(n_cmp_slots)[:, None] - 1) * CMP_STRIDE
    j = np.arange(n_lanes)[None, :]
    ov = (start <= j * SLC_BLOCK + SLC_BLOCK - 1) & (start + CMP_BLOCK - 1 >= j * SLC_BLOCK)
    ov &= (np.arange(n_cmp_slots)[:, None] >= 1) & (j < n_slc)
    return jnp.asarray(ov, dtype=BF16)


def kernel(x_prompt, x_sample, cache_kv, state_win, page_table, norm_g, w_in, ln_g, ln_b, w_s, b_s,
           cmp_pos, w_cmp1, b_cmp1, w_cmp2, w_out, final_g):
    n_batch, seq, _ = x_prompt.shape
    n_seq, t_new, _ = x_sample.shape
    depth, n_phys = cache_kv.shape[:2]
    n_pages = page_table.shape[1]
    past_len = n_pages * PAGE_SIZE
    win_buf = state_win.shape[2]
    assert depth == 1 and seq % TQ == 0 and n_seq * t_new == TM and win_buf == WINDOW
    assert (past_len + t_new - 1) // SLC_BLOCK == past_len // SLC_BLOCK and past_len % CMP_STRIDE == 0

    w_all = _prep_w_in(w_in[0])
    ng = norm_g[0].reshape(1, D_MODEL)
    lng = ln_g[0].reshape(1, D_A)
    lnb = ln_b[0].reshape(1, D_A)
    fg = final_g.reshape(1, D_MODEL)
    wo = w_out[0]
    wo_a = wo[:D_A].astype(BF16)
    wo_b = _head_pair_perm(wo[D_A:].T).T.astype(BF16)
    wc, pe, b1, w2p = _prep_compress(cmp_pos[0], w_cmp1[0], b_cmp1[0], w_cmp2[0])

    xp = x_prompt.reshape(n_batch * seq, D_MODEL)
    wmix_p, bmix_p = _prep_mix(w_s[0], b_s[0], CHUNK)
    cos_p, sin_p = _rope_tables(jnp.arange(seq))
    ag_p, qc_p, qr_p, rows_p, kvw_pt, ksel_p, kwin_p, gate_p, szb_p, rows_pt = _proj_call(
        xp, ng, w_all, lng, lnb, wmix_p, bmix_p, cos_p, sin_p, BF16, seq)
    pages_p = seq // PAGE_SIZE
    ident = jnp.arange(n_batch * pages_p, dtype=jnp.int32).reshape(n_batch, pages_p)
    cmp_p = _compress_call(rows_p.reshape(n_batch * pages_p, PAGE_SIZE, 4 * LANES), ident, wc, pe, b1, w2p,
                           pages_p, False)
    ov_p = _overlap_matrix(seq // CMP_STRIDE, seq // SLC_BLOCK, LANES)
    bg_p = _attn_prompt_call(qc_p, qr_p, cmp_p, ov_p, ksel_p, kwin_p, gate_p, szb_p, n_batch, seq)
    y_p = _merge_call(xp, ag_p, bg_p, wo_a, wo_b, fg)

    xs = x_sample.reshape(n_seq * t_new, D_MODEL)
    wmix_s, bmix_s = _prep_mix(w_s[0], b_s[0], t_new)
    cos_s, sin_s = _rope_tables(jnp.tile(past_len + jnp.arange(t_new), n_seq))
    ag_s, qc_s, qr_s, rows_s, kvw_s, _, _, gate_s, szb_s, vln_s = _proj_call(
        xs, ng, w_all, lng, lnb, wmix_s, bmix_s, cos_s, sin_s, F32, None)
    cache_t = cache_kv[0].transpose(0, 2, 3, 4, 1)
    cmp_s = _compress_call(cache_t, page_table, wc, pe, b1, w2p, 32, True)
    n_slc_s = -(-(past_len + t_new) // SLC_BLOCK)
    tb = past_len // SLC_BLOCK
    n_pick = TOP_N - 3
    ov_s = _overlap_matrix(past_len // CMP_STRIDE, n_slc_s, -(-n_slc_s // LANES) * LANES)
    win_prev = state_win[0].reshape(n_seq, win_buf, 2 * LANES)
    ocw_s, sel_s = _cmpwin_sample_call(qc_s, qr_s, cmp_s, ov_s, win_prev, kvw_s, gate_s, t_new, past_len, n_pick)
    sel_flat = sel_s[:, :, :n_pick].reshape(-1)
    bg_s = _sel_sample_call(sel_flat, page_table.reshape(-1), cache_t, qr_s, rows_s, gate_s, ocw_s, szb_s,
                            n_seq, t_new, n_pick, n_pages, tb)
    y_s = _merge_call(xs, ag_s, bg_s, wo_a, wo_b, fg)

    new_kv_p = rows_pt.reshape(n_batch, 4, N_KV, HEAD_DIM, seq).transpose(0, 4, 1, 2, 3)
    new_win_p = kvw_pt[:, :, seq - win_buf:].reshape(n_batch, 2, N_KV, HEAD_DIM, win_buf).transpose(0, 4, 1, 2, 3)
    new_win_s = jnp.concatenate([state_win[0][:, t_new:], kvw_s.reshape(n_seq, t_new, 2, N_KV, HEAD_DIM)], axis=1)
    return (y_p.reshape(n_batch, seq, D_MODEL),
            y_s.reshape(n_seq, t_new, D_MODEL),
            new_kv_p[None],
            new_win_p[None],
            rows_s.reshape(1, n_seq, t_new, 4, N_KV, HEAD_DIM),
            new_win_s[None],
            vln_s.reshape(1, n_seq, t_new, D_A))
```

```python
import functools
import math

import jax
import jax.numpy as jnp
import numpy as np
from jax import lax
from jax.experimental import pallas as pl
from jax.experimental.pallas import tpu as pltpu

F32 = jnp.float32
BF16 = jnp.bfloat16

D_MODEL = 1024
HEAD_DIM = 64
D_A = 512
A_GROUPS = 8
CHUNK = 128
N_HEADS = 8
D_B = 512
N_KV = 2
GQA = 4
CMP_BLOCK = 32
CMP_STRIDE = 16
CMP_HIDDEN = 128
SLC_BLOCK = 64
TOP_N = 16
WINDOW = 512
ROPE_THETA = 10000.0
EPS = 1e-6
SCALE = HEAD_DIM ** -0.5
NEG = -1e30
FORCE = 1e9
PAGE_SIZE = 128

LANES = 128
VMEM_LIMIT = 56 * 1024 * 1024

C_U, C_V, C_ZA, C_Q, C_KV, C_ZB, C_G, C_END = 0, 512, 1024, 1536, 2048, 2816, 3328, 3456

TM = 256
TM_MERGE = 512
DMA_LOOP_UNROLL = 8
SAMPLE_SEQS_PER_STEP = 8
SAMPLE_PAGES_PER_STEP = 64
TQ = 256
SEGS_PER_PAGE = PAGE_SIZE // CMP_STRIDE


def _lane_iota(shape):
    return lax.broadcasted_iota(jnp.int32, shape, len(shape) - 1)


def _row_iota(shape):
    return lax.broadcasted_iota(jnp.int32, shape, len(shape) - 2)


def _div_pow2(x, n):
    assert n & (n - 1) == 0
    return lax.shift_right_logical(x, int(math.log2(n))) if n > 1 else x


def _mod_pow2(x, n):
    assert n & (n - 1) == 0
    return x & (n - 1)


def _dot(a, b):
    return jnp.dot(a, b, preferred_element_type=F32)


def _dot_nt(a, b):
    return lax.dot_general(a, b, (((1,), (1,)), ((), ())), preferred_element_type=F32)


def _rope(x, cos, sin_signed):
    lo = _mod_pow2(_lane_iota(x.shape), HEAD_DIM) < (HEAD_DIM // 2)
    swapped = jnp.where(lo, pltpu.roll(x, LANES - HEAD_DIM // 2, 1), pltpu.roll(x, HEAD_DIM // 2, 1))
    return x * cos + swapped * sin_signed


def _proj_kernel(x_ref, ng_ref, w_ref, lng_ref, lnb_ref, wmix_ref, bmix_ref, cos_ref, sin_ref,
                 ag_ref, qc_ref, qr_ref, rows_ref, kvw_ref, ksel_ref, kwin_ref, gate_ref, szb_ref,
                 extra_ref, *, prompt):
    x = x_ref[...]
    ms = jnp.mean(x * x, axis=-1, keepdims=True)
    hb = ((x * lax.rsqrt(ms + EPS)) * ng_ref[...]).astype(BF16)
    cos = cos_ref[...]
    sin = sin_ref[...]
    pair = 2 * LANES

    def proj(c0, n):
        return _dot(hb, w_ref[:, c0:c0 + n])

    def halves(x2):
        return x2[:, 0:LANES], x2[:, LANES:pair]

    v = jax.nn.gelu(proj(C_V, D_A))
    mu = jnp.mean(v, axis=-1, keepdims=True)
    vc = v - mu
    var = jnp.mean(vc * vc, axis=-1, keepdims=True)
    vln = vc * lax.rsqrt(var + EPS) * lng_ref[...] + lnb_ref[...]
    if not prompt:
        extra_ref[...] = vln
    vb = vln.astype(BF16)
    first_head = _lane_iota((x.shape[0], LANES)) < HEAD_DIM
    for gpp in range(A_GROUPS // 4):
        u2 = halves(jax.nn.gelu(proj(C_U + gpp * pair, pair)))
        za2 = halves(proj(C_ZA + gpp * pair, pair))
        for k in range(2):
            gp = 2 * gpp + k
            sl = slice(gp * LANES, (gp + 1) * LANES)
            blk = vb[:, sl]
            mixed = jnp.where(first_head, _dot(wmix_ref[2 * gp], blk), _dot(wmix_ref[2 * gp + 1], blk))
            mixed = mixed + bmix_ref[:, sl]
            ag_ref[:, sl] = (u2[k] * mixed * jax.nn.silu(za2[k])).astype(ag_ref.dtype)

    for gpp in range(GQA // 2):
        q2 = halves(proj(C_Q + gpp * pair, pair))
        for k in range(2):
            sl = slice((2 * gpp + k) * LANES, (2 * gpp + k + 1) * LANES)
            qc_ref[:, sl] = (q2[k] * SCALE).astype(qc_ref.dtype)
            qr_ref[:, sl] = (_rope(q2[k], cos, sin) * SCALE).astype(qr_ref.dtype)

    kc, vcm = halves(proj(C_KV, pair))
    ks, vs = halves(proj(C_KV + pair, pair))
    kw, vw = halves(proj(C_KV + 2 * pair, pair))
    ks = _rope(ks, cos, sin)
    kw = _rope(kw, cos, sin)
    for i, blk in enumerate((kc, vcm, ks, vs)):
        rows_ref[:, i * LANES:(i + 1) * LANES] = blk
        if prompt:
            extra_ref[0, i * LANES:(i + 1) * LANES, :] = blk.T
    for i, blk in enumerate((kw, vw)):
        if prompt:
            kvw_ref[0, i * LANES:(i + 1) * LANES, :] = blk.T
        else:
            kvw_ref[:, i * LANES:(i + 1) * LANES] = blk
    ksel_ref[:, 0:LANES] = ks.astype(BF16)
    ksel_ref[:, LANES:2 * LANES] = vs.astype(BF16)
    kwin_ref[:, 0:LANES] = kw.astype(BF16)
    kwin_ref[:, LANES:2 * LANES] = vw.astype(BF16)

    gate_ref[...] = jax.nn.sigmoid(proj(C_G, LANES))
    for gpp in range(GQA // 2):
        zb2 = halves(proj(C_ZB + gpp * pair, pair))
        for k in range(2):
            sl = slice((2 * gpp + k) * LANES, (2 * gpp + k + 1) * LANES)
            szb_ref[:, sl] = jax.nn.silu(zb2[k]).astype(szb_ref.dtype)


def _proj_call(x2d, ng, w, lng, lnb, wmix, bmix, cos_t, sin_t, act_dtype, seq_len):
    n_rows = x2d.shape[0]
    n_tiles = n_rows // TM
    pos_tiles = cos_t.shape[0] // TM
    row = lambda i: (i, 0)
    const2 = lambda i: (0, 0)
    out_shapes = [
        jax.ShapeDtypeStruct((n_rows, D_A), act_dtype),
        jax.ShapeDtypeStruct((n_rows, D_B), act_dtype),
        jax.ShapeDtypeStruct((n_rows, D_B), act_dtype),
        jax.ShapeDtypeStruct((n_rows, 4 * LANES), F32),
        jax.ShapeDtypeStruct((n_rows, 2 * LANES), F32),
        jax.ShapeDtypeStruct((n_rows, 2 * LANES), BF16),
        jax.ShapeDtypeStruct((n_rows, 2 * LANES), BF16),
        jax.ShapeDtypeStruct((n_rows, LANES), F32),
        jax.ShapeDtypeStruct((n_rows, D_B), act_dtype),
        jax.ShapeDtypeStruct((n_rows, D_A), F32),
    ]
    out_specs = [pl.BlockSpec((TM, s.shape[1]), row) for s in out_shapes]
    if seq_len is not None:
        tiles_per_seq = seq_len // TM
        dim_major = lambda i: (i // tiles_per_seq, 0, i % tiles_per_seq)
        for idx, width in ((4, 2 * LANES), (9, 4 * LANES)):
            out_shapes[idx] = jax.ShapeDtypeStruct((n_rows // seq_len, width, seq_len), F32)
            out_specs[idx] = pl.BlockSpec((1, width, TM), dim_major)
    return pl.pallas_call(
        functools.partial(_proj_kernel, prompt=seq_len is not None),
        grid=(n_tiles,),
        in_specs=[
            pl.BlockSpec((TM, D_MODEL), row),
            pl.BlockSpec((1, D_MODEL), const2),
            pl.BlockSpec((D_MODEL, C_END), const2),
            pl.BlockSpec((1, D_A), const2),
            pl.BlockSpec((1, D_A), const2),
            pl.BlockSpec((A_GROUPS, TM, TM), lambda i: (0, 0, 0)),
            pl.BlockSpec((TM, D_A), const2),
            pl.BlockSpec((TM, LANES), lambda i: (i % pos_tiles, 0)),
            pl.BlockSpec((TM, LANES), lambda i: (i % pos_tiles, 0)),
        ],
        out_specs=out_specs,
        out_shape=out_shapes,
        compiler_params=pltpu.CompilerParams(dimension_semantics=("arbitrary",),
                                             vmem_limit_bytes=VMEM_LIMIT),
        name="proj",
    )(x2d, ng, w, lng, lnb, wmix, bmix, cos_t, sin_t)


def _compress_kernel(pt_ref, src_ref, wc_ref, pe_ref, b1_ref, w2_ref, out_ref, buf, sem, carry,
                     *maybe_stage, pages_per_step):
    transposed_src = bool(maybe_stage)
    n_steps = pl.num_programs(0) * pl.num_programs(1)
    step = pl.program_id(0) * pl.num_programs(1) + pl.program_id(1)
    m = pages_per_step * SEGS_PER_PAGE

    def page_copies(step_idx, slot, i):
        phys = pt_ref[step_idx * pages_per_step + i]
        if transposed_src:
            return [pltpu.make_async_copy(src_ref.at[phys, pl.ds(0, 2)], maybe_stage[0].at[slot, i], sem.at[slot])]
        return [pltpu.make_async_copy(src_ref.at[phys, :, pl.ds(c * LANES, LANES)],
                                      buf.at[slot, c, pl.ds(i * PAGE_SIZE, PAGE_SIZE), :],
                                      sem.at[slot]) for c in range(2)]

    def start_step(step_idx, slot):
        def body(i, carry_):
            for cp in page_copies(step_idx, slot, i):
                cp.start()
            return carry_
        lax.fori_loop(0, pages_per_step, body, 0, unroll=DMA_LOOP_UNROLL)

    def wait_step(step_idx, slot):
        def body(i, carry_):
            for cp in page_copies(step_idx, slot, i):
                cp.wait()
            return carry_
        lax.fori_loop(0, pages_per_step, body, 0, unroll=DMA_LOOP_UNROLL)

    slot = step % 2

    first_of_seq = pl.program_id(1) == 0
    first_row = _row_iota((m, LANES)) == 0

    def compress_rows(rows_slot, seg_pitch):
        for c in range(2):
            pieces = [buf[rows_slot, c, pl.ds(j, m, stride=seg_pitch), :].astype(BF16)
                      for j in range(CMP_STRIDE)]
            part = _dot(jnp.concatenate(pieces, axis=1), wc_ref[c])
            pe_part = _dot(pe_ref[c], wc_ref[c])
            acc_out = None
            for h in range(N_KV):
                lo = slice(h * 2 * CMP_HIDDEN, h * 2 * CMP_HIDDEN + CMP_HIDDEN)
                hi = slice(h * 2 * CMP_HIDDEN + CMP_HIDDEN, (h + 1) * 2 * CMP_HIDDEN)
                bias = pe_part[0:1, lo] + pe_part[1:2, hi] + b1_ref[c]
                part0 = part[:, lo]
                prev_last = jnp.where(first_of_seq, 0.0, carry[c, h, 0:1, :])
                prev0 = jnp.where(first_row, prev_last, pltpu.roll(part0, 1, 0))
                carry[c, h, 0:1, :] = part0[m - 1:m, :]
                hid = jax.nn.gelu(prev0 + part[:, hi] + bias).astype(BF16)
                contrib = _dot(hid, w2_ref[c, h])
                acc_out = contrib if acc_out is None else acc_out + contrib
            out_ref[0, :, c * LANES:(c + 1) * LANES] = acc_out.astype(out_ref.dtype)

    if transposed_src:
        stage = maybe_stage[0]
        seg_pitch = CMP_STRIDE + 1

        def transpose_step(stage_slot):
            for i in range(pages_per_step):
                for c in range(2):
                    rows = stage[stage_slot, i, c].reshape(N_KV * HEAD_DIM, PAGE_SIZE).T
                    for s in range(SEGS_PER_PAGE):
                        r0 = (i * SEGS_PER_PAGE + s) * seg_pitch
                        buf[stage_slot, c, r0:r0 + CMP_STRIDE, :] = rows[s * CMP_STRIDE:(s + 1) * CMP_STRIDE]

        @pl.when(step == 0)
        def _():
            start_step(0, 0)
            start_step(1, 1)
            wait_step(0, 0)
            transpose_step(0)

        @pl.when(step + 1 < n_steps)
        def _():
            wait_step(step + 1, (step + 1) % 2)

        @pl.when(step + 2 < n_steps)
        def _():
            start_step(step + 2, slot)

        for parity in range(2):
            @pl.when(slot == parity)
            def _():
                transpose_step(1 - parity)
                compress_rows(parity, seg_pitch)
    else:
        @pl.when(step == 0)
        def _():
            start_step(0, 0)

        @pl.when(step + 1 < n_steps)
        def _():
            start_step(step + 1, (step + 1) % 2)

        wait_step(step, slot)
        compress_rows(slot, CMP_STRIDE)


def _compress_call(src, page_table, wc, pe, b1, w2p, pages_per_step, transposed_src):
    n_seq, n_pages = page_table.shape
    n_chunks = n_pages // pages_per_step
    m = pages_per_step * SEGS_PER_PAGE
    assert not transposed_src or n_seq * n_chunks >= 2
    seg_pitch = CMP_STRIDE + 1 if transposed_src else CMP_STRIDE
    scratch = [
        pltpu.VMEM((2, 2, m * seg_pitch, LANES), F32),
        pltpu.SemaphoreType.DMA((2,)),
        pltpu.VMEM((2, N_KV, 8, CMP_HIDDEN), F32),
    ]
    if transposed_src:
        scratch.append(pltpu.VMEM((2, pages_per_step, 2, N_KV, HEAD_DIM, PAGE_SIZE), F32))
    grid_spec = pltpu.PrefetchScalarGridSpec(
        num_scalar_prefetch=1,
        grid=(n_seq, n_chunks),
        in_specs=[
            pl.BlockSpec(memory_space=pl.ANY),
            pl.BlockSpec((2, CMP_STRIDE * LANES, 4 * CMP_HIDDEN), lambda b, k, pt: (0, 0, 0)),
            pl.BlockSpec((2, 16, CMP_STRIDE * LANES), lambda b, k, pt: (0, 0, 0)),
            pl.BlockSpec((2, 1, CMP_HIDDEN), lambda b, k, pt: (0, 0, 0)),
            pl.BlockSpec((2, N_KV, CMP_HIDDEN, LANES), lambda b, k, pt: (0, 0, 0, 0)),
        ],
        out_specs=pl.BlockSpec((1, m, 2 * LANES), lambda b, k, pt: (b, k, 0)),
        scratch_shapes=scratch,
    )
    return pl.pallas_call(
        functools.partial(_compress_kernel, pages_per_step=pages_per_step),
        grid_spec=grid_spec,
        out_shape=jax.ShapeDtypeStruct((n_seq, n_pages * SEGS_PER_PAGE, 2 * LANES), BF16),
        compiler_params=pltpu.CompilerParams(dimension_semantics=("arbitrary", "arbitrary"),
                                             vmem_limit_bytes=VMEM_LIMIT),
        name="compress",
    )(page_table.reshape(-1), src, wc, pe, b1, w2p)


def _softmax_rows(s, mask):
    s = jnp.where(mask, s, NEG)
    e = jnp.where(mask, jnp.exp(s - jnp.max(s, axis=-1, keepdims=True)), 0.0)
    return e / jnp.maximum(jnp.sum(e, axis=-1, keepdims=True), 1.0)


def _dot_split(p, w):
    hi = p.astype(BF16)
    lo = (p - hi.astype(F32)).astype(BF16)
    return _dot(hi, w) + _dot(lo, w)


def _gate_col(gates, idx):
    return gates[:, idx:idx + 1]


def _attn_prompt_kernel(qc_ref, qr_ref, cmp_ref, ov_ref, eg_ref, ksel_ref, kwin_ref, gate_ref, szb_ref,
                        out_ref, q_sc, m_sc, acc_sc, cmp_sc):
    qi = pl.program_id(1)
    lane = _lane_iota((TQ, LANES))
    lo_half = lane < HEAD_DIM
    tq = qi * TQ + _row_iota((TQ, LANES))
    r_minus_c = _row_iota((TQ, TQ)) - _lane_iota((TQ, TQ))
    causal_bias = jnp.where(r_minus_c >= 0, 0.0, NEG)
    far_bias = jnp.where(r_minus_c < 0, 0.0, NEG)
    pair_bias = jnp.concatenate([jnp.zeros((TQ, TQ), F32), causal_bias], axis=1)
    kcmp = cmp_ref[0, :, 0:LANES]
    vcmp = cmp_ref[0, :, LANES:2 * LANES]
    cmp_valid = (lane >= 1) & (lane * CMP_STRIDE + (CMP_STRIDE - 1) <= tq)
    n_slc = ksel_ref.shape[0] // SLC_BLOCK
    blocks_per_tile = TQ // SLC_BLOCK

    def flash_tile(br, h, k_tile, v_aug, bias):
        width = k_tile.shape[0]
        s = _dot_nt(q_sc[h], k_tile)
        if bias is not None:
            s = (s.reshape(GQA, TQ, width) + bias[None]).reshape(GQA * TQ, width)
        m_prev = m_sc[h]
        m_new = jnp.maximum(m_prev, jnp.max(s, axis=-1, keepdims=True))
        alpha = jnp.exp(m_prev - m_new)
        p = jnp.exp(s - jnp.concatenate([m_new] * (width // LANES), axis=1))
        acc_sc[br, h] = alpha * acc_sc[br, h] + _dot(p.astype(BF16), v_aug)
        m_sc[h] = m_new

    def flash_init(br):
        m_sc[...] = jnp.full(m_sc.shape, NEG, F32)
        acc_sc[br] = jnp.zeros(acc_sc.shape[1:], F32)

    for h in range(N_KV):
        keep = _div_pow2(lane, HEAD_DIM) == h
        other_off = HEAD_DIM * (1 - h)

        p_sum = jnp.zeros((TQ, LANES), F32)
        for g in range(GQA):
            sl = slice(g * LANES, (g + 1) * LANES)
            qc = jnp.where(keep, qc_ref[:, sl], 0)
            p = _softmax_rows(_dot_nt(qc, kcmp), cmp_valid)
            p_sum = p_sum + p
            o_cmp = _dot(p.astype(BF16), vcmp)
            cmp_sc[:, sl] = o_cmp if h == 0 else jnp.where(keep, o_cmp, cmp_sc[:, sl])
        imp = _dot_split(p_sum, ov_ref[...])
        tb = _div_pow2(tq, SLC_BLOCK)
        forced = (lane == 0) | (lane == tb) | (lane == tb - 1)
        valid = lane * SLC_BLOCK <= tq
        score = jnp.where(forced, FORCE, jnp.where(valid, imp, NEG))
        score_t = score.T[0:n_slc, :]
        blk = _row_iota((n_slc, TQ))
        rank = jnp.zeros((n_slc, TQ), jnp.int32)
        for j in range(n_slc):
            other = score_t[j:j + 1, :]
            ahead = (other > score_t) | ((other == score_t) & (blk > j))
            rank = rank + ahead.astype(jnp.int32)
        sel_bias_t = jnp.where(rank < TOP_N, 0.0, NEG)
        pieces = [jnp.zeros((other_off, TQ), F32)] if other_off else []
        pieces += [sel_bias_t, jnp.zeros((LANES - other_off - n_slc, TQ), F32)]
        sel_bias = jnp.concatenate(pieces, axis=0).T.astype(BF16)
        for g in range(GQA):
            sl = slice(g * LANES, (g + 1) * LANES)
            q_sc[h, g * TQ:(g + 1) * TQ, :] = jnp.where(keep, qr_ref[:, sl], sel_bias)

    def span_operands(kv_ref, kt, n_tiles, h, other_lanes):
        k = kv_ref[pl.ds(kt * TQ, n_tiles * TQ), 0:LANES]
        v = kv_ref[pl.ds(kt * TQ, n_tiles * TQ), LANES:2 * LANES]
        own = _div_pow2(_lane_iota(k.shape), HEAD_DIM) == h
        return jnp.where(own, k, other_lanes(k.shape)), jnp.where(own, v, 1.0)

    def sel_span(kt, n_tiles, bias):
        for h in range(N_KV):
            def indicator(shape, h=h):
                first = HEAD_DIM * (1 - h) + kt * blocks_per_tile
                hit = _lane_iota(shape) == first + _div_pow2(_row_iota(shape), SLC_BLOCK)
                return jnp.where(hit, 1, 0).astype(BF16)
            flash_tile(0, h, *span_operands(ksel_ref, kt, n_tiles, h, indicator), bias)

    flash_init(0)

    def sel_pair(i, carry_):
        sel_span(2 * i, 2, None)
        return carry_
    lax.fori_loop(0, _div_pow2(jnp.maximum(qi - 1, 0), 2), sel_pair, 0)

    @pl.when((qi >= 2) & (_mod_pow2(qi, 2) == 0))
    def _():
        sel_span(qi - 2, 1, None)

    @pl.when(qi >= 1)
    def _():
        sel_span(qi - 1, 2, pair_bias)

    @pl.when(qi == 0)
    def _():
        sel_span(qi, 1, causal_bias)

    def win_span(kt, n_tiles, bias):
        for h in range(N_KV):
            flash_tile(1, h, *span_operands(kwin_ref, kt, n_tiles, h, lambda shape: jnp.zeros(shape, BF16)), bias)

    flash_init(1)
    far = WINDOW // TQ
    assert far == 2

    @pl.when(qi >= far)
    def _():
        win_span(qi - far, 1, far_bias)

    @pl.when(qi >= 1)
    def _():
        win_span(qi - 1, 2, pair_bias)

    @pl.when(qi == 0)
    def _():
        win_span(qi, 1, causal_bias)

    gates_x = _dot_split(gate_ref[...], eg_ref[...])
    for g in range(GQA):
        rows = slice(g * TQ, (g + 1) * TQ)
        sl = slice(g * LANES, (g + 1) * LANES)
        o = gates_x[:, (g * 3) * LANES:(g * 3 + 1) * LANES] * cmp_sc[:, sl]
        for br in range(2):
            a0 = acc_sc[br, 0, rows, :]
            a1 = acc_sc[br, 1, rows, :]
            num = jnp.where(lo_half, a0, a1)
            den = pltpu.roll(jnp.where(lo_half, a1, a0), HEAD_DIM, 1)
            o = o + gates_x[:, (g * 3 + 1 + br) * LANES:(g * 3 + 2 + br) * LANES] * (num / den)
        out_ref[:, sl] = (o * szb_ref[:, sl].astype(F32)).astype(out_ref.dtype)


def _gate_expansion():
    eg = np.zeros((LANES, GQA * 3 * LANES), np.float32)
    for h in range(N_KV):
        for g in range(GQA):
            for br in range(3):
                c0 = (g * 3 + br) * LANES + h * HEAD_DIM
                eg[h * GQA * 3 + g * 3 + br, c0:c0 + HEAD_DIM] = 1.0
    return jnp.asarray(eg, dtype=BF16)


def _attn_prompt_call(qc, qr, cmp_p, ov, ksel, kwin, gates, szb, n_batch, seq):
    nq = seq // TQ
    tile = lambda b, q: (b * nq + q, 0)
    whole = lambda b, q: (b, 0)
    return pl.pallas_call(
        _attn_prompt_kernel,
        grid=(n_batch, nq),
        in_specs=[
            pl.BlockSpec((TQ, D_B), tile),
            pl.BlockSpec((TQ, D_B), tile),
            pl.BlockSpec((1, LANES, 2 * LANES), lambda b, q: (b, 0, 0)),
            pl.BlockSpec((LANES, LANES), lambda b, q: (0, 0)),
            pl.BlockSpec((LANES, GQA * 3 * LANES), lambda b, q: (0, 0)),
            pl.BlockSpec((seq, 2 * LANES), whole),
            pl.BlockSpec((seq, 2 * LANES), whole),
            pl.BlockSpec((TQ, LANES), tile),
            pl.BlockSpec((TQ, D_B), tile),
        ],
        out_specs=pl.BlockSpec((TQ, D_B), tile),
        out_shape=jax.ShapeDtypeStruct((n_batch * seq, D_B), BF16),
        scratch_shapes=[
            pltpu.VMEM((N_KV, GQA * TQ, LANES), BF16),
            pltpu.VMEM((N_KV, GQA * TQ, LANES), F32),
            pltpu.VMEM((2, N_KV, GQA * TQ, LANES), F32),
            pltpu.VMEM((TQ, D_B), F32),
        ],
        compiler_params=pltpu.CompilerParams(dimension_semantics=("arbitrary", "arbitrary"),
                                             vmem_limit_bytes=VMEM_LIMIT),
        name="attn_prompt",
    )(qc, qr, cmp_p, ov, _gate_expansion(), ksel, kwin, gates, szb)


def _cmpwin_sample_kernel(qc_ref, qr_ref, cmp_ref, ov_ref, win_ref, kvw_ref, gate_ref,
                          ocw_ref, sel_ref, *, past_len, n_pick, t_new):
    seqs = cmp_ref.shape[0]
    n_rows = N_KV * GQA * t_new
    lane = _lane_iota((t_new, LANES))
    n_cmp = cmp_ref.shape[1]
    row_t = _mod_pow2(_row_iota((n_rows, n_cmp)), t_new)
    slot = _lane_iota((n_rows, n_cmp))
    cmp_valid = (slot >= 1) & (slot * CMP_STRIDE + (CMP_STRIDE - 1) <= past_len + row_t)
    n_buf = win_ref.shape[1]
    dist = n_buf + _mod_pow2(_row_iota((n_rows, n_buf)), t_new) - _lane_iota((n_rows, n_buf))
    win_valid = (dist >= 0) & (dist < WINDOW)
    row_t1 = _mod_pow2(_row_iota((n_rows, 1)), t_new)

    def one_sequence(sq):
        rows_sq = slice(sq * t_new, (sq + 1) * t_new)
        gates = gate_ref[rows_sq, :]

        def q_rows(ref):
            return jnp.concatenate(
                [jnp.where(_div_pow2(lane, HEAD_DIM) == h, ref[rows_sq, g * LANES:(g + 1) * LANES], 0).astype(BF16)
                 for h in range(N_KV) for g in range(GQA)], axis=0)

        p = _softmax_rows(_dot_nt(q_rows(qc_ref), cmp_ref[sq, :, 0:LANES]), cmp_valid)
        o_cmp = _dot(p.astype(BF16), cmp_ref[sq, :, LANES:2 * LANES])
        p_sum = jnp.concatenate(
            [sum(p[(h * GQA + g) * t_new:(h * GQA + g + 1) * t_new] for g in range(GQA)) for h in range(N_KV)],
            axis=0)
        imp = _dot_split(p_sum, ov_ref[...])

        qr = q_rows(qr_ref)
        k_buf = win_ref[sq, :, 0:LANES].astype(BF16)
        v_buf = win_ref[sq, :, LANES:2 * LANES].astype(BF16)
        s_buf = jnp.where(win_valid, _dot_nt(qr, k_buf), NEG)
        k_new = kvw_ref[rows_sq, 0:LANES].astype(BF16).astype(F32)
        v_new = kvw_ref[rows_sq, LANES:2 * LANES].astype(BF16).astype(F32)
        qr32 = qr.astype(F32)
        s_new = [jnp.where(row_t1 >= i, jnp.sum(qr32 * k_new[i:i + 1, :], axis=-1, keepdims=True), NEG)
                 for i in range(t_new)]
        m = jnp.max(s_buf, axis=-1, keepdims=True)
        for s in s_new:
            m = jnp.maximum(m, s)
        e_buf = jnp.exp(s_buf - m)
        e_new = [jnp.exp(s - m) for s in s_new]
        denom = jnp.sum(e_buf, axis=-1, keepdims=True) + sum(e_new)
        o_win = _dot(e_buf.astype(BF16), v_buf)
        for i in range(t_new):
            o_win = o_win + e_new[i].astype(BF16).astype(F32) * v_new[i:i + 1, :]
        o_win = o_win / denom

        for g in range(GQA):
            parts = []
            for h in range(N_KV):
                rows = slice((h * GQA + g) * t_new, (h * GQA + g + 1) * t_new)
                base = h * GQA * 3 + g * 3
                parts.append(_gate_col(gates, base) * o_cmp[rows] + _gate_col(gates, base + 2) * o_win[rows])
            ocw_ref[rows_sq, g * LANES:(g + 1) * LANES] = jnp.where(lane < HEAD_DIM, parts[0], parts[1])
        return imp

    imp = jnp.concatenate([one_sequence(sq) for sq in range(seqs)], axis=0)
    blk = _lane_iota(imp.shape)
    tb = (past_len + t_new - 1) // SLC_BLOCK
    candidate = (blk >= 1) & (blk < tb - 1)
    score = jnp.where(candidate, imp, -1.0)
    out_lane = _lane_iota((imp.shape[0], LANES))
    picks = jnp.zeros((imp.shape[0], LANES), jnp.int32)
    for k in range(n_pick):
        best = jnp.max(score, axis=-1, keepdims=True)
        idx = jnp.min(jnp.where(score == best, blk, 1 << 20), axis=-1, keepdims=True)
        picks = jnp.where(out_lane == k, idx, picks)
        score = jnp.where(blk == idx, -2.0, score)
    sel_ref[...] = picks.reshape(sel_ref.shape)


def _cmpwin_sample_call(qc, qr, cmp_s, ov, win, kvw, gates, t_new, past_len, n_pick):
    n_seq = cmp_s.shape[0]
    seqs = SAMPLE_SEQS_PER_STEP
    assert n_seq % seqs == 0
    tile = lambda b: (b, 0)
    per_seq = lambda b: (b, 0, 0)
    return pl.pallas_call(
        functools.partial(_cmpwin_sample_kernel, past_len=past_len, n_pick=n_pick, t_new=t_new),
        grid=(n_seq // seqs,),
        in_specs=[
            pl.BlockSpec((seqs * t_new, D_B), tile),
            pl.BlockSpec((seqs * t_new, D_B), tile),
            pl.BlockSpec((seqs, cmp_s.shape[1], 2 * LANES), per_seq),
            pl.BlockSpec(ov.shape, lambda b: (0, 0)),
            pl.BlockSpec((seqs, win.shape[1], 2 * LANES), per_seq),
            pl.BlockSpec((seqs * t_new, 2 * LANES), tile),
            pl.BlockSpec((seqs * t_new, LANES), tile),
        ],
        out_specs=[
            pl.BlockSpec((seqs * t_new, D_B), tile),
            pl.BlockSpec((seqs, N_KV * t_new, LANES), per_seq),
        ],
        out_shape=[
            jax.ShapeDtypeStruct((n_seq * t_new, D_B), F32),
            jax.ShapeDtypeStruct((n_seq, N_KV * t_new, LANES), jnp.int32),
        ],
        compiler_params=pltpu.CompilerParams(dimension_semantics=("arbitrary",),
                                             vmem_limit_bytes=VMEM_LIMIT),
        name="cmpwin_sample",
    )(qc, qr, cmp_s, ov, win, kvw, gates)


def _sel_sample_kernel(sel_ref, pt_ref, cache_ref, qr_ref, rows_ref, gate_ref, ocw_ref, szb_ref,
                       out_ref, buf, sem, *, n_pick, n_pages, tb):
    t_new = qr_ref.shape[0]
    n_own = t_new * n_pick
    n_steps = pl.num_programs(0) * pl.num_programs(1)
    h = pl.program_id(1)
    step = pl.program_id(0) * pl.num_programs(1) + h
    blocks_per_page = PAGE_SIZE // SLC_BLOCK

    def block_copy(step_idx, slot, i, blk):
        seq = _div_pow2(step_idx, N_KV)
        head = _mod_pow2(step_idx, N_KV)
        phys = pt_ref[seq * n_pages + _div_pow2(blk, blocks_per_page)]
        return pltpu.make_async_copy(cache_ref.at[phys, pl.ds(2, 2), head], buf.at[slot, i], sem.at[slot])

    def for_all_blocks(step_idx, slot, fn):
        fn(block_copy(step_idx, slot, 0, 0))
        fn(block_copy(step_idx, slot, 1, tb - 1))

        def body(i, carry_):
            fn(block_copy(step_idx, slot, 2 + i, sel_ref[step_idx * n_own + i]))
            return carry_
        lax.fori_loop(0, n_own, body, 0, unroll=DMA_LOOP_UNROLL)

    @pl.when(step == 0)
    def _():
        for_all_blocks(0, 0, lambda cp: cp.start())

    @pl.when(step + 1 < n_steps)
    def _():
        for_all_blocks(step + 1, (step + 1) % 2, lambda cp: cp.start())

    slot = step % 2
    for_all_blocks(step, slot, lambda cp: cp.wait())

    n_rows = GQA * t_new
    half = _div_pow2(_lane_iota((1, LANES)), SLC_BLOCK)

    def head_half(x):
        return jnp.where(h == 0, x[:, 0:HEAD_DIM], x[:, HEAD_DIM:LANES])

    qr = jnp.concatenate([head_half(qr_ref[:, g * LANES:(g + 1) * LANES]) for g in range(GQA)], axis=0)
    qrb = qr.astype(BF16)
    qr32 = qrb.astype(F32)
    row_t = _mod_pow2(_row_iota((n_rows, 1)), t_new)

    def slabs(first, count, kind):
        return jnp.concatenate([buf[slot, first + k, kind].astype(BF16) for k in range(count)], axis=1)

    bias_sh = jnp.concatenate(
        [jnp.where(half == blk % blocks_per_page, 0.0, NEG) for blk in (0, tb - 1)], axis=1)
    s_sh = _dot(qrb, slabs(0, 2, 0)) + bias_sh
    k_new = head_half(rows_ref[:, 2 * LANES:3 * LANES]).astype(BF16).astype(F32)
    v_new = head_half(rows_ref[:, 3 * LANES:4 * LANES]).astype(BF16).astype(F32)
    s_new = [jnp.where(row_t >= i, jnp.sum(qr32 * k_new[i:i + 1, :], axis=-1, keepdims=True), NEG)
             for i in range(t_new)]
    own_keys = n_pick * PAGE_SIZE
    s_own = jnp.zeros((n_rows, own_keys), F32)
    for t in range(t_new):
        bias_t = jnp.concatenate(
            [jnp.where(half == _mod_pow2(sel_ref[step * n_own + t * n_pick + k], blocks_per_page), 0.0, NEG)
             for k in range(n_pick)], axis=1)
        s_t = _dot(qrb, slabs(2 + t * n_pick, n_pick, 0)) + bias_t
        s_own = jnp.where(row_t == t, s_t, s_own)

    m = jnp.maximum(jnp.max(s_sh, axis=-1, keepdims=True), jnp.max(s_own, axis=-1, keepdims=True))
    for s in s_new:
        m = jnp.maximum(m, s)
    e_sh = jnp.exp(s_sh - m)
    e_own = jnp.exp(s_own - m)
    e_new = [jnp.exp(s - m) for s in s_new]
    denom = jnp.sum(e_sh, axis=-1, keepdims=True) + jnp.sum(e_own, axis=-1, keepdims=True) + sum(e_new)
    o = _dot_nt(e_sh.astype(BF16), slabs(0, 2, 1))
    for t in range(t_new):
        o = o + _dot_nt(jnp.where(row_t == t, e_own, 0.0).astype(BF16), slabs(2 + t * n_pick, n_pick, 1))
    for i in range(t_new):
        o = o + e_new[i].astype(BF16).astype(F32) * v_new[i:i + 1, :]
    o = o / denom

    gates = gate_ref[...]
    lane = _lane_iota((t_new, LANES))
    keep = _div_pow2(lane, HEAD_DIM) == h
    for g in range(GQA):
        sl = slice(g * LANES, (g + 1) * LANES)
        gate = jnp.sum(jnp.where(lane == h * GQA * 3 + g * 3 + 1, gates, 0.0), axis=-1, keepdims=True)
        o_g = o[g * t_new:(g + 1) * t_new]
        contrib = jnp.where(keep, gate * jnp.concatenate([o_g, o_g], axis=1), 0.0)

        @pl.when(h == 0)
        def _():
            out_ref[:, sl] = ocw_ref[:, sl] + contrib

        @pl.when(h == N_KV - 1)
        def _():
            out_ref[:, sl] = (out_ref[:, sl] + contrib) * szb_ref[:, sl]


def _sel_sample_call(sel_flat, pt_flat, cache_t, qr, rows, gates, ocw, szb, n_seq, t_new, n_pick, n_pages, tb):
    tile = lambda b, h, sel, pt: (b, 0)
    n_blocks = 2 + t_new * n_pick
    grid_spec = pltpu.PrefetchScalarGridSpec(
        num_scalar_prefetch=2,
        grid=(n_seq, N_KV),
        in_specs=[
            pl.BlockSpec(memory_space=pl.ANY),
            pl.BlockSpec((t_new, D_B), tile),
            pl.BlockSpec((t_new, 4 * LANES), tile),
            pl.BlockSpec((t_new, LANES), tile),
            pl.BlockSpec((t_new, D_B), tile),
            pl.BlockSpec((t_new, D_B), tile),
        ],
        out_specs=pl.BlockSpec((t_new, D_B), tile),
        scratch_shapes=[
            pltpu.VMEM((2, n_blocks, 2, HEAD_DIM, PAGE_SIZE), F32),
            pltpu.SemaphoreType.DMA((2,)),
        ],
    )
    return pl.pallas_call(
        functools.partial(_sel_sample_kernel, n_pick=n_pick, n_pages=n_pages, tb=tb),
        grid_spec=grid_spec,
        out_shape=jax.ShapeDtypeStruct((n_seq * t_new, D_B), F32),
        compiler_params=pltpu.CompilerParams(dimension_semantics=("arbitrary", "arbitrary"),
                                             vmem_limit_bytes=VMEM_LIMIT),
        name="sel_sample",
    )(sel_flat, pt_flat, cache_t, qr, rows, gates, ocw, szb)


def _merge_kernel(x_ref, a_ref, b_ref, wa_ref, wb_ref, fg_ref, y_ref):
    delta = _dot(a_ref[...].astype(BF16), wa_ref[...]) + _dot(b_ref[...].astype(BF16), wb_ref[...])
    x = x_ref[...] + delta
    ms = jnp.mean(x * x, axis=-1, keepdims=True)
    y_ref[...] = (x * lax.rsqrt(ms + EPS)) * fg_ref[...]


def _merge_call(x2d, a, b, wa, wb, fg):
    n_rows = x2d.shape[0]
    tm = min(TM_MERGE, n_rows)
    row = lambda i: (i, 0)
    const2 = lambda i: (0, 0)
    return pl.pallas_call(
        _merge_kernel,
        grid=(n_rows // tm,),
        in_specs=[
            pl.BlockSpec((tm, D_MODEL), row),
            pl.BlockSpec((tm, D_A), row),
            pl.BlockSpec((tm, D_B), row),
            pl.BlockSpec((D_A, D_MODEL), const2),
            pl.BlockSpec((D_B, D_MODEL), const2),
            pl.BlockSpec((1, D_MODEL), const2),
        ],
        out_specs=pl.BlockSpec((tm, D_MODEL), row),
        out_shape=jax.ShapeDtypeStruct((n_rows, D_MODEL), F32),
        compiler_params=pltpu.CompilerParams(dimension_semantics=("arbitrary",),
                                             vmem_limit_bytes=VMEM_LIMIT),
        name="merge",
    )(x2d, a, b, wa, wb, fg)


def _head_pair_perm(w_cols):
    lead = w_cols.shape[:-1]
    return w_cols.reshape(*lead, N_KV, GQA, HEAD_DIM).swapaxes(-3, -2).reshape(*lead, D_B)


def _prep_w_in(w_in):
    cuts = np.cumsum([D_A, D_A, D_A, D_B, 6 * N_KV * HEAD_DIM, 3 * N_HEADS]).tolist()
    u, v, za, q, kv, g, zb = jnp.split(w_in, cuts, axis=-1)
    g_pad = jnp.pad(g, ((0, 0), (0, LANES - g.shape[1])))
    return jnp.concatenate([u, v, za, _head_pair_perm(q), kv, _head_pair_perm(zb), g_pad], axis=-1).astype(BF16)


def _prep_mix(w_s, b_s, chunk_len):
    tril = w_s[:, :chunk_len, :chunk_len] * jnp.tril(jnp.ones((chunk_len, chunk_len), w_s.dtype))
    reps = TM // chunk_len
    eye = jnp.eye(reps, dtype=w_s.dtype)
    wmix = jnp.einsum("ab,gts->gatbs", eye, tril).reshape(A_GROUPS, TM, TM).astype(BF16)
    bias = jnp.repeat(b_s[:, :chunk_len].T, HEAD_DIM, axis=1)
    return wmix, jnp.tile(bias, (reps, 1))


def _rope_tables(pos):
    half = HEAD_DIM // 2
    inv = ROPE_THETA ** (-jnp.arange(half, dtype=F32) / half)
    ang = pos.astype(F32)[:, None] * inv
    cos = jnp.tile(jnp.cos(ang), (1, LANES // half))
    sin = jnp.tile(jnp.concatenate([-jnp.sin(ang), jnp.sin(ang)], axis=1), (1, LANES // HEAD_DIM))
    return cos, sin


def _prep_compress(cmp_pos, w_cmp1, b_cmp1, w_cmp2):
    ratio = CMP_BLOCK // CMP_STRIDE
    w1 = w_cmp1.reshape(2, ratio, CMP_STRIDE, HEAD_DIM, CMP_HIDDEN).transpose(0, 2, 3, 1, 4)
    w1 = w1.reshape(2, CMP_STRIDE, HEAD_DIM, ratio * CMP_HIDDEN)
    zeros = jnp.zeros_like(w1)
    wc = jnp.stack([jnp.concatenate([w1, zeros], axis=-1), jnp.concatenate([zeros, w1], axis=-1)], axis=2)
    wc = wc.reshape(2, CMP_STRIDE * LANES, N_KV * ratio * CMP_HIDDEN).astype(BF16)
    pe = cmp_pos.reshape(2, ratio, CMP_STRIDE, 1, HEAD_DIM)
    pe = jnp.broadcast_to(pe, (2, ratio, CMP_STRIDE, N_KV, HEAD_DIM)).reshape(2, ratio, CMP_STRIDE * LANES)
    pe = jnp.pad(pe, ((0, 0), (0, 16 - ratio), (0, 0))).astype(BF16)
    w2 = w_cmp2[:, None]
    zeros2 = jnp.zeros_like(w2)
    w2p = jnp.concatenate([jnp.concatenate([w2, zeros2], axis=-1), jnp.concatenate([zeros2, w2], axis=-1)],
                          axis=1).astype(BF16)
    return wc, pe, b_cmp1.reshape(2, 1, CMP_HIDDEN), w2p


def _overlap_matrix(n_cmp_slots, n_slc, n_lanes):
    start = (np.arange(n_cmp_slots)[:, None] - 1) * CMP_STRIDE
    j = np.arange(n_lanes)[None, :]
    ov = (start <= j * SLC_BLOCK + SLC_BLOCK - 1) & (start + CMP_BLOCK - 1 >= j * SLC_BLOCK)
    ov &= (np.arange(n_cmp_slots)[:, None] >= 1) & (j < n_slc)
    return jnp.asarray(ov, dtype=BF16)


def kernel(x_prompt, x_sample, cache_kv, state_win, page_table, norm_g, w_in, ln_g, ln_b, w_s, b_s,
           cmp_pos, w_cmp1, b_cmp1, w_cmp2, w_out, final_g):
    n_batch, seq, _ = x_prompt.shape
    n_seq, t_new, _ = x_sample.shape
    depth, n_phys = cache_kv.shape[:2]
    n_pages = page_table.shape[1]
    past_len = n_pages * PAGE_SIZE
    win_buf = state_win.shape[2]
    assert depth == 1 and seq % TQ == 0 and n_seq * t_new == TM and win_buf == WINDOW
    assert (past_len + t_new - 1) // SLC_BLOCK == past_len // SLC_BLOCK and past_len % CMP_STRIDE == 0

    w_all = _prep_w_in(w_in[0])
    ng = norm_g[0].reshape(1, D_MODEL)
    lng = ln_g[0].reshape(1, D_A)
    lnb = ln_b[0].reshape(1, D_A)
    fg = final_g.reshape(1, D_MODEL)
    wo = w_out[0]
    wo_a = wo[:D_A].astype(BF16)
    wo_b = _head_pair_perm(wo[D_A:].T).T.astype(BF16)
    wc, pe, b1, w2p = _prep_compress(cmp_pos[0], w_cmp1[0], b_cmp1[0], w_cmp2[0])

    xp = x_prompt.reshape(n_batch * seq, D_MODEL)
    wmix_p, bmix_p = _prep_mix(w_s[0], b_s[0], CHUNK)
    cos_p, sin_p = _rope_tables(jnp.arange(seq))
    ag_p, qc_p, qr_p, rows_p, kvw_pt, ksel_p, kwin_p, gate_p, szb_p, rows_pt = _proj_call(
        xp, ng, w_all, lng, lnb, wmix_p, bmix_p, cos_p, sin_p, BF16, seq)
    pages_p = seq // PAGE_SIZE
    ident = jnp.arange(n_batch * pages_p, dtype=jnp.int32).reshape(n_batch, pages_p)
    cmp_p = _compress_call(rows_p.reshape(n_batch * pages_p, PAGE_SIZE, 4 * LANES), ident, wc, pe, b1, w2p,
                           pages_p, False)
    ov_p = _overlap_matrix(seq // CMP_STRIDE, seq // SLC_BLOCK, LANES)
    bg_p = _attn_prompt_call(qc_p, qr_p, cmp_p, ov_p, ksel_p, kwin_p, gate_p, szb_p, n_batch, seq)
    y_p = _merge_call(xp, ag_p, bg_p, wo_a, wo_b, fg)

    xs = x_sample.reshape(n_seq * t_new, D_MODEL)
    wmix_s, bmix_s = _prep_mix(w_s[0], b_s[0], t_new)
    cos_s, sin_s = _rope_tables(jnp.tile(past_len + jnp.arange(t_new), n_seq))
    ag_s, qc_s, qr_s, rows_s, kvw_s, _, _, gate_s, szb_s, vln_s = _proj_call(
        xs, ng, w_all, lng, lnb, wmix_s, bmix_s, cos_s, sin_s, F32, None)
    cache_t = cache_kv[0].transpose(0, 2, 3, 4, 1)
    cmp_s = _compress_call(cache_t, page_table, wc, pe, b1, w2p, SAMPLE_PAGES_PER_STEP, True)
    n_slc_s = -(-(past_len + t_new) // SLC_BLOCK)
    tb = past_len // SLC_BLOCK
    n_pick = TOP_N - 3
    ov_s = _overlap_matrix(past_len // CMP_STRIDE, n_slc_s, -(-n_slc_s // LANES) * LANES)
    win_prev = state_win[0].reshape(n_seq, win_buf, 2 * LANES)
    ocw_s, sel_s = _cmpwin_sample_call(qc_s, qr_s, cmp_s, ov_s, win_prev, kvw_s, gate_s, t_new, past_len, n_pick)
    sel_flat = sel_s[:, :, :n_pick].reshape(-1)
    bg_s = _sel_sample_call(sel_flat, page_table.reshape(-1), cache_t, qr_s, rows_s, gate_s, ocw_s, szb_s,
                            n_seq, t_new, n_pick, n_pages, tb)
    y_s = _merge_call(xs, ag_s, bg_s, wo_a, wo_b, fg)

    new_kv_p = rows_pt.reshape(n_batch, 4, N_KV, HEAD_DIM, seq).transpose(0, 4, 1, 2, 3)
    new_win_p = kvw_pt[:, :, seq - win_buf:].reshape(n_batch, 2, N_KV, HEAD_DIM, win_buf).transpose(0, 4, 1, 2, 3)
    new_win_s = jnp.concatenate([state_win[0][:, t_new:], kvw_s.reshape(n_seq, t_new, 2, N_KV, HEAD_DIM)], axis=1)
    return (y_p.reshape(n_batch, seq, D_MODEL),
            y_s.reshape(n_seq, t_new, D_MODEL),
            new_kv_p[None],
            new_win_p[None],
            rows_s.reshape(1, n_seq, t_new, 4, N_KV, HEAD_DIM),
            new_win_s[None],
            vln_s.reshape(1, n_seq, t_new, D_A))
```

```python
import functools
import math

import jax
import jax.numpy as jnp
import numpy as np
from jax import lax
from jax.experimental import pallas as pl
from jax.experimental.pallas import tpu as pltpu

F32 = jnp.float32
BF16 = jnp.bfloat16

D_MODEL = 1024
HEAD_DIM = 64
D_A = 512
A_GROUPS = 8
CHUNK = 128
N_HEADS = 8
D_B = 512
N_KV = 2
GQA = 4
CMP_BLOCK = 32
CMP_STRIDE = 16
CMP_HIDDEN = 128
SLC_BLOCK = 64
TOP_N = 16
WINDOW = 512
ROPE_THETA = 10000.0
EPS = 1e-6
SCALE = HEAD_DIM ** -0.5
NEG = -1e30
FORCE = 1e9
PAGE_SIZE = 128

LANES = 128
VMEM_LIMIT = 56 * 1024 * 1024

C_U, C_V, C_ZA, C_Q, C_KV, C_ZB, C_G, C_END = 0, 512, 1024, 1536, 2048, 2816, 3328, 3456

TM = 256
TM_MERGE = 512
DMA_LOOP_UNROLL = 8
SAMPLE_SEQS_PER_STEP = 8
SAMPLE_PAGES_PER_STEP = 64
TQ = 256
SEGS_PER_PAGE = PAGE_SIZE // CMP_STRIDE


def _lane_iota(shape):
    return lax.broadcasted_iota(jnp.int32, shape, len(shape) - 1)


def _row_iota(shape):
    return lax.broadcasted_iota(jnp.int32, shape, len(shape) - 2)


def _div_pow2(x, n):
    assert n & (n - 1) == 0
    return lax.shift_right_logical(x, int(math.log2(n))) if n > 1 else x


def _mod_pow2(x, n):
    assert n & (n - 1) == 0
    return x & (n - 1)


def _dot(a, b):
    return jnp.dot(a, b, preferred_element_type=F32)


def _dot_nt(a, b):
    return lax.dot_general(a, b, (((1,), (1,)), ((), ())), preferred_element_type=F32)


def _rope(x, cos, sin_signed):
    lo = _mod_pow2(_lane_iota(x.shape), HEAD_DIM) < (HEAD_DIM // 2)
    swapped = jnp.where(lo, pltpu.roll(x, LANES - HEAD_DIM // 2, 1), pltpu.roll(x, HEAD_DIM // 2, 1))
    return x * cos + swapped * sin_signed


def _proj_kernel(x_ref, ng_ref, w_ref, lng_ref, lnb_ref, wmix_ref, bmix_ref, cos_ref, sin_ref,
                 ag_ref, qc_ref, qr_ref, rows_ref, kvw_ref, ksel_ref, kwin_ref, gate_ref, szb_ref,
                 extra_ref, *, prompt):
    x = x_ref[...]
    ms = jnp.mean(x * x, axis=-1, keepdims=True)
    hb = ((x * lax.rsqrt(ms + EPS)) * ng_ref[...]).astype(BF16)
    cos = cos_ref[...]
    sin = sin_ref[...]
    pair = 2 * LANES

    def proj(c0, n):
        return _dot(hb, w_ref[:, c0:c0 + n])

    def halves(x2):
        return x2[:, 0:LANES], x2[:, LANES:pair]

    v = jax.nn.gelu(proj(C_V, D_A))
    mu = jnp.mean(v, axis=-1, keepdims=True)
    vc = v - mu
    var = jnp.mean(vc * vc, axis=-1, keepdims=True)
    vln = vc * lax.rsqrt(var + EPS) * lng_ref[...] + lnb_ref[...]
    if not prompt:
        extra_ref[...] = vln
    vb = vln.astype(BF16)
    first_head = _lane_iota((x.shape[0], LANES)) < HEAD_DIM
    for gpp in range(A_GROUPS // 4):
        u2 = halves(jax.nn.gelu(proj(C_U + gpp * pair, pair)))
        za2 = halves(proj(C_ZA + gpp * pair, pair))
        for k in range(2):
            gp = 2 * gpp + k
            sl = slice(gp * LANES, (gp + 1) * LANES)
            blk = vb[:, sl]
            mixed = jnp.where(first_head, _dot(wmix_ref[2 * gp], blk), _dot(wmix_ref[2 * gp + 1], blk))
            mixed = mixed + bmix_ref[:, sl]
            ag_ref[:, sl] = (u2[k] * mixed * jax.nn.silu(za2[k])).astype(ag_ref.dtype)

    for gpp in range(GQA // 2):
        q2 = halves(proj(C_Q + gpp * pair, pair))
        for k in range(2):
            sl = slice((2 * gpp + k) * LANES, (2 * gpp + k + 1) * LANES)
            qc_ref[:, sl] = (q2[k] * SCALE).astype(qc_ref.dtype)
            qr_ref[:, sl] = (_rope(q2[k], cos, sin) * SCALE).astype(qr_ref.dtype)

    kc, vcm = halves(proj(C_KV, pair))
    ks, vs = halves(proj(C_KV + pair, pair))
    kw, vw = halves(proj(C_KV + 2 * pair, pair))
    ks = _rope(ks, cos, sin)
    kw = _rope(kw, cos, sin)
    for i, blk in enumerate((kc, vcm, ks, vs)):
        rows_ref[:, i * LANES:(i + 1) * LANES] = blk
        if prompt:
            extra_ref[0, i * LANES:(i + 1) * LANES, :] = blk.T
    for i, blk in enumerate((kw, vw)):
        if prompt:
            kvw_ref[0, i * LANES:(i + 1) * LANES, :] = blk.T
        else:
            kvw_ref[:, i * LANES:(i + 1) * LANES] = blk
    ksel_ref[:, 0:LANES] = ks.astype(BF16)
    ksel_ref[:, LANES:2 * LANES] = vs.astype(BF16)
    kwin_ref[:, 0:LANES] = kw.astype(BF16)
    kwin_ref[:, LANES:2 * LANES] = vw.astype(BF16)

    gate_ref[...] = jax.nn.sigmoid(proj(C_G, LANES))
    for gpp in range(GQA // 2):
        zb2 = halves(proj(C_ZB + gpp * pair, pair))
        for k in range(2):
            sl = slice((2 * gpp + k) * LANES, (2 * gpp + k + 1) * LANES)
            szb_ref[:, sl] = jax.nn.silu(zb2[k]).astype(szb_ref.dtype)


def _proj_call(x2d, ng, w, lng, lnb, wmix, bmix, cos_t, sin_t, act_dtype, seq_len):
    n_rows = x2d.shape[0]
    n_tiles = n_rows // TM
    pos_tiles = cos_t.shape[0] // TM
    row = lambda i: (i, 0)
    const2 = lambda i: (0, 0)
    out_shapes = [
        jax.ShapeDtypeStruct((n_rows, D_A), act_dtype),
        jax.ShapeDtypeStruct((n_rows, D_B), act_dtype),
        jax.ShapeDtypeStruct((n_rows, D_B), act_dtype),
        jax.ShapeDtypeStruct((n_rows, 4 * LANES), F32),
        jax.ShapeDtypeStruct((n_rows, 2 * LANES), F32),
        jax.ShapeDtypeStruct((n_rows, 2 * LANES), BF16),
        jax.ShapeDtypeStruct((n_rows, 2 * LANES), BF16),
        jax.ShapeDtypeStruct((n_rows, LANES), F32),
        jax.ShapeDtypeStruct((n_rows, D_B), act_dtype),
        jax.ShapeDtypeStruct((n_rows, D_A), F32),
    ]
    out_specs = [pl.BlockSpec((TM, s.shape[1]), row) for s in out_shapes]
    if seq_len is not None:
        tiles_per_seq = seq_len // TM
        dim_major = lambda i: (i // tiles_per_seq, 0, i % tiles_per_seq)
        for idx, width in ((4, 2 * LANES), (9, 4 * LANES)):
            out_shapes[idx] = jax.ShapeDtypeStruct((n_rows // seq_len, width, seq_len), F32)
            out_specs[idx] = pl.BlockSpec((1, width, TM), dim_major)
    return pl.pallas_call(
        functools.partial(_proj_kernel, prompt=seq_len is not None),
        grid=(n_tiles,),
        in_specs=[
            pl.BlockSpec((TM, D_MODEL), row),
            pl.BlockSpec((1, D_MODEL), const2),
            pl.BlockSpec((D_MODEL, C_END), const2),
            pl.BlockSpec((1, D_A), const2),
            pl.BlockSpec((1, D_A), const2),
            pl.BlockSpec((A_GROUPS, TM, TM), lambda i: (0, 0, 0)),
            pl.BlockSpec((TM, D_A), const2),
            pl.BlockSpec((TM, LANES), lambda i: (i % pos_tiles, 0)),
            pl.BlockSpec((TM, LANES), lambda i: (i % pos_tiles, 0)),
        ],
        out_specs=out_specs,
        out_shape=out_shapes,
        compiler_params=pltpu.CompilerParams(dimension_semantics=("arbitrary",),
                                             vmem_limit_bytes=VMEM_LIMIT),
        name="proj",
    )(x2d, ng, w, lng, lnb, wmix, bmix, cos_t, sin_t)


def _compress_kernel(pt_ref, src_ref, wc_ref, pe_ref, b1_ref, w2_ref, out_ref, buf, sem, carry,
                     *maybe_stage, pages_per_step):
    transposed_src = bool(maybe_stage)
    n_steps = pl.num_programs(0) * pl.num_programs(1)
    step = pl.program_id(0) * pl.num_programs(1) + pl.program_id(1)
    m = pages_per_step * SEGS_PER_PAGE

    def page_copies(step_idx, slot, i):
        phys = pt_ref[step_idx * pages_per_step + i]
        if transposed_src:
            return [pltpu.make_async_copy(src_ref.at[phys, pl.ds(0, 2)], maybe_stage[0].at[slot, i], sem.at[slot])]
        return [pltpu.make_async_copy(src_ref.at[phys, :, pl.ds(c * LANES, LANES)],
                                      buf.at[slot, c, pl.ds(i * PAGE_SIZE, PAGE_SIZE), :],
                                      sem.at[slot]) for c in range(2)]

    def start_step(step_idx, slot):
        def body(i, carry_):
            for cp in page_copies(step_idx, slot, i):
                cp.start()
            return carry_
        lax.fori_loop(0, pages_per_step, body, 0, unroll=DMA_LOOP_UNROLL)

    def wait_step(step_idx, slot):
        def body(i, carry_):
            for cp in page_copies(step_idx, slot, i):
                cp.wait()
            return carry_
        lax.fori_loop(0, pages_per_step, body, 0, unroll=DMA_LOOP_UNROLL)

    slot = step % 2

    first_of_seq = pl.program_id(1) == 0
    first_row = _row_iota((m, LANES)) == 0

    def compress_rows(rows_slot, seg_pitch):
        for c in range(2):
            pieces = [buf[rows_slot, c, pl.ds(j, m, stride=seg_pitch), :].astype(BF16)
                      for j in range(CMP_STRIDE)]
            part = _dot(jnp.concatenate(pieces, axis=1), wc_ref[c])
            pe_part = _dot(pe_ref[c], wc_ref[c])
            acc_out = None
            for h in range(N_KV):
                lo = slice(h * 2 * CMP_HIDDEN, h * 2 * CMP_HIDDEN + CMP_HIDDEN)
                hi = slice(h * 2 * CMP_HIDDEN + CMP_HIDDEN, (h + 1) * 2 * CMP_HIDDEN)
                bias = pe_part[0:1, lo] + pe_part[1:2, hi] + b1_ref[c]
                part0 = part[:, lo]
                prev_last = jnp.where(first_of_seq, 0.0, carry[c, h, 0:1, :])
                prev0 = jnp.where(first_row, prev_last, pltpu.roll(part0, 1, 0))
                carry[c, h, 0:1, :] = part0[m - 1:m, :]
                hid = jax.nn.gelu(prev0 + part[:, hi] + bias).astype(BF16)
                contrib = _dot(hid, w2_ref[c, h])
                acc_out = contrib if acc_out is None else acc_out + contrib
            out_ref[0, :, c * LANES:(c + 1) * LANES] = acc_out.astype(out_ref.dtype)

    if transposed_src:
        stage = maybe_stage[0]
        seg_pitch = CMP_STRIDE + 1

        def transpose_step(stage_slot):
            for i in range(pages_per_step):
                for c in range(2):
                    rows = stage[stage_slot, i, c].reshape(N_KV * HEAD_DIM, PAGE_SIZE).T
                    for s in range(SEGS_PER_PAGE):
                        r0 = (i * SEGS_PER_PAGE + s) * seg_pitch
                        buf[stage_slot, c, r0:r0 + CMP_STRIDE, :] = rows[s * CMP_STRIDE:(s + 1) * CMP_STRIDE]

        @pl.when(step == 0)
        def _():
            start_step(0, 0)
            start_step(1, 1)
            wait_step(0, 0)
            transpose_step(0)

        @pl.when(step + 1 < n_steps)
        def _():
            wait_step(step + 1, (step + 1) % 2)

        @pl.when(step + 2 < n_steps)
        def _():
            start_step(step + 2, slot)

        for parity in range(2):
            @pl.when(slot == parity)
            def _():
                transpose_step(1 - parity)
                compress_rows(parity, seg_pitch)
    else:
        @pl.when(step == 0)
        def _():
            start_step(0, 0)

        @pl.when(step + 1 < n_steps)
        def _():
            start_step(step + 1, (step + 1) % 2)

        wait_step(step, slot)
        compress_rows(slot, CMP_STRIDE)


def _compress_call(src, page_table, wc, pe, b1, w2p, pages_per_step, transposed_src):
    n_seq, n_pages = page_table.shape
    n_chunks = n_pages // pages_per_step
    m = pages_per_step * SEGS_PER_PAGE
    assert not transposed_src or n_seq * n_chunks >= 2
    seg_pitch = CMP_STRIDE + 1 if transposed_src else CMP_STRIDE
    scratch = [
        pltpu.VMEM((2, 2, m * seg_pitch, LANES), F32),
        pltpu.SemaphoreType.DMA((2,)),
        pltpu.VMEM((2, N_KV, 8, CMP_HIDDEN), F32),
    ]
    if transposed_src:
        scratch.append(pltpu.VMEM((2, pages_per_step, 2, N_KV, HEAD_DIM, PAGE_SIZE), F32))
    grid_spec = pltpu.PrefetchScalarGridSpec(
        num_scalar_prefetch=1,
        grid=(n_seq, n_chunks),
        in_specs=[
            pl.BlockSpec(memory_space=pl.ANY),
            pl.BlockSpec((2, CMP_STRIDE * LANES, 4 * CMP_HIDDEN), lambda b, k, pt: (0, 0, 0)),
            pl.BlockSpec((2, 16, CMP_STRIDE * LANES), lambda b, k, pt: (0, 0, 0)),
            pl.BlockSpec((2, 1, CMP_HIDDEN), lambda b, k, pt: (0, 0, 0)),
            pl.BlockSpec((2, N_KV, CMP_HIDDEN, LANES), lambda b, k, pt: (0, 0, 0, 0)),
        ],
        out_specs=pl.BlockSpec((1, m, 2 * LANES), lambda b, k, pt: (b, k, 0)),
        scratch_shapes=scratch,
    )
    return pl.pallas_call(
        functools.partial(_compress_kernel, pages_per_step=pages_per_step),
        grid_spec=grid_spec,
        out_shape=jax.ShapeDtypeStruct((n_seq, n_pages * SEGS_PER_PAGE, 2 * LANES), BF16),
        compiler_params=pltpu.CompilerParams(dimension_semantics=("arbitrary", "arbitrary"),
                                             vmem_limit_bytes=VMEM_LIMIT),
        name="compress",
    )(page_table.reshape(-1), src, wc, pe, b1, w2p)


def _softmax_rows(s, mask):
    s = jnp.where(mask, s, NEG)
    e = jnp.where(mask, jnp.exp(s - jnp.max(s, axis=-1, keepdims=True)), 0.0)
    return e / jnp.maximum(jnp.sum(e, axis=-1, keepdims=True), 1.0)


def _dot_split(p, w):
    hi = p.astype(BF16)
    lo = (p - hi.astype(F32)).astype(BF16)
    return _dot(hi, w) + _dot(lo, w)


def _gate_col(gates, idx):
    return gates[:, idx:idx + 1]


def _attn_prompt_kernel(qc_ref, qr_ref, cmp_ref, ov_ref, eg_ref, ksel_ref, kwin_ref, gate_ref, szb_ref,
                        out_ref, q_sc, m_sc, acc_sc, cmp_sc):
    qi = pl.program_id(1)
    lane = _lane_iota((TQ, LANES))
    lo_half = lane < HEAD_DIM
    tq = qi * TQ + _row_iota((TQ, LANES))
    r_minus_c = _row_iota((TQ, TQ)) - _lane_iota((TQ, TQ))
    causal_bias = jnp.where(r_minus_c >= 0, 0.0, NEG)
    far_bias = jnp.where(r_minus_c < 0, 0.0, NEG)
    pair_bias = jnp.concatenate([jnp.zeros((TQ, TQ), F32), causal_bias], axis=1)
    kcmp = cmp_ref[0, :, 0:LANES]
    vcmp = cmp_ref[0, :, LANES:2 * LANES]
    cmp_valid = (lane >= 1) & (lane * CMP_STRIDE + (CMP_STRIDE - 1) <= tq)
    n_slc = ksel_ref.shape[0] // SLC_BLOCK
    blocks_per_tile = TQ // SLC_BLOCK

    def flash_tile(br, h, k_tile, v_aug, bias):
        width = k_tile.shape[0]
        s = _dot_nt(q_sc[h], k_tile)
        if bias is not None:
            s = (s.reshape(GQA, TQ, width) + bias[None]).reshape(GQA * TQ, width)
        m_prev = m_sc[h]
        m_new = jnp.maximum(m_prev, jnp.max(s, axis=-1, keepdims=True))
        alpha = jnp.exp(m_prev - m_new)
        p = jnp.exp(s - jnp.concatenate([m_new] * (width // LANES), axis=1))
        acc_sc[br, h] = alpha * acc_sc[br, h] + _dot(p.astype(BF16), v_aug)
        m_sc[h] = m_new

    def flash_init(br):
        m_sc[...] = jnp.full(m_sc.shape, NEG, F32)
        acc_sc[br] = jnp.zeros(acc_sc.shape[1:], F32)

    for h in range(N_KV):
        keep = _div_pow2(lane, HEAD_DIM) == h
        other_off = HEAD_DIM * (1 - h)

        p_sum = jnp.zeros((TQ, LANES), F32)
        for g in range(GQA):
            sl = slice(g * LANES, (g + 1) * LANES)
            qc = jnp.where(keep, qc_ref[:, sl], 0)
            p = _softmax_rows(_dot_nt(qc, kcmp), cmp_valid)
            p_sum = p_sum + p
            o_cmp = _dot(p.astype(BF16), vcmp)
            cmp_sc[:, sl] = o_cmp if h == 0 else jnp.where(keep, o_cmp, cmp_sc[:, sl])
        imp = _dot_split(p_sum, ov_ref[...])
        tb = _div_pow2(tq, SLC_BLOCK)
        forced = (lane == 0) | (lane == tb) | (lane == tb - 1)
        valid = lane * SLC_BLOCK <= tq
        score = jnp.where(forced, FORCE, jnp.where(valid, imp, NEG))
        score_t = score.T[0:n_slc, :]
        blk = _row_iota((n_slc, TQ))
        rank = jnp.zeros((n_slc, TQ), jnp.int32)
        for j in range(n_slc):
            other = score_t[j:j + 1, :]
            ahead = (other > score_t) | ((other == score_t) & (blk > j))
            rank = rank + ahead.astype(jnp.int32)
        sel_bias_t = jnp.where(rank < TOP_N, 0.0, NEG)
        pieces = [jnp.zeros((other_off, TQ), F32)] if other_off else []
        pieces += [sel_bias_t, jnp.zeros((LANES - other_off - n_slc, TQ), F32)]
        sel_bias = jnp.concatenate(pieces, axis=0).T.astype(BF16)
        for g in range(GQA):
            sl = slice(g * LANES, (g + 1) * LANES)
            q_sc[h, g * TQ:(g + 1) * TQ, :] = jnp.where(keep, qr_ref[:, sl], sel_bias)

    def span_operands(kv_ref, kt, n_tiles, h, other_lanes):
        k = kv_ref[pl.ds(kt * TQ, n_tiles * TQ), 0:LANES]
        v = kv_ref[pl.ds(kt * TQ, n_tiles * TQ), LANES:2 * LANES]
        own = _div_pow2(_lane_iota(k.shape), HEAD_DIM) == h
        return jnp.where(own, k, other_lanes(k.shape)), jnp.where(own, v, 1.0)

    def sel_span(kt, n_tiles, bias):
        for h in range(N_KV):
            def indicator(shape, h=h):
                first = HEAD_DIM * (1 - h) + kt * blocks_per_tile
                hit = _lane_iota(shape) == first + _div_pow2(_row_iota(shape), SLC_BLOCK)
                return jnp.where(hit, 1, 0).astype(BF16)
            flash_tile(0, h, *span_operands(ksel_ref, kt, n_tiles, h, indicator), bias)

    flash_init(0)

    def sel_pair(i, carry_):
        sel_span(2 * i, 2, None)
        return carry_
    lax.fori_loop(0, _div_pow2(jnp.maximum(qi - 1, 0), 2), sel_pair, 0)

    @pl.when((qi >= 2) & (_mod_pow2(qi, 2) == 0))
    def _():
        sel_span(qi - 2, 1, None)

    @pl.when(qi >= 1)
    def _():
        sel_span(qi - 1, 2, pair_bias)

    @pl.when(qi == 0)
    def _():
        sel_span(qi, 1, causal_bias)

    def win_span(kt, n_tiles, bias):
        for h in range(N_KV):
            flash_tile(1, h, *span_operands(kwin_ref, kt, n_tiles, h, lambda shape: jnp.zeros(shape, BF16)), bias)

    flash_init(1)
    far = WINDOW // TQ
    assert far == 2

    @pl.when(qi >= far)
    def _():
        win_span(qi - far, 1, far_bias)

    @pl.when(qi >= 1)
    def _():
        win_span(qi - 1, 2, pair_bias)

    @pl.when(qi == 0)
    def _():
        win_span(qi, 1, causal_bias)

    gates_x = _dot_split(gate_ref[...], eg_ref[...])
    for g in range(GQA):
        rows = slice(g * TQ, (g + 1) * TQ)
        sl = slice(g * LANES, (g + 1) * LANES)
        o = gates_x[:, (g * 3) * LANES:(g * 3 + 1) * LANES] * cmp_sc[:, sl]
        for br in range(2):
            a0 = acc_sc[br, 0, rows, :]
            a1 = acc_sc[br, 1, rows, :]
            num = jnp.where(lo_half, a0, a1)
            den = pltpu.roll(jnp.where(lo_half, a1, a0), HEAD_DIM, 1)
            o = o + gates_x[:, (g * 3 + 1 + br) * LANES:(g * 3 + 2 + br) * LANES] * (num / den)
        out_ref[:, sl] = (o * szb_ref[:, sl].astype(F32)).astype(out_ref.dtype)


def _gate_expansion():
    eg = np.zeros((LANES, GQA * 3 * LANES), np.float32)
    for h in range(N_KV):
        for g in range(GQA):
            for br in range(3):
                c0 = (g * 3 + br) * LANES + h * HEAD_DIM
                eg[h * GQA * 3 + g * 3 + br, c0:c0 + HEAD_DIM] = 1.0
    return jnp.asarray(eg, dtype=BF16)


def _attn_prompt_call(qc, qr, cmp_p, ov, ksel, kwin, gates, szb, n_batch, seq):
    nq = seq // TQ
    tile = lambda b, q: (b * nq + q, 0)
    whole = lambda b, q: (b, 0)
    return pl.pallas_call(
        _attn_prompt_kernel,
        grid=(n_batch, nq),
        in_specs=[
            pl.BlockSpec((TQ, D_B), tile),
            pl.BlockSpec((TQ, D_B), tile),
            pl.BlockSpec((1, LANES, 2 * LANES), lambda b, q: (b, 0, 0)),
            pl.BlockSpec((LANES, LANES), lambda b, q: (0, 0)),
            pl.BlockSpec((LANES, GQA * 3 * LANES), lambda b, q: (0, 0)),
            pl.BlockSpec((seq, 2 * LANES), whole),
            pl.BlockSpec((seq, 2 * LANES), whole),
            pl.BlockSpec((TQ, LANES), tile),
            pl.BlockSpec((TQ, D_B), tile),
        ],
        out_specs=pl.BlockSpec((TQ, D_B), tile),
        out_shape=jax.ShapeDtypeStruct((n_batch * seq, D_B), BF16),
        scratch_shapes=[
            pltpu.VMEM((N_KV, GQA * TQ, LANES), BF16),
            pltpu.VMEM((N_KV, GQA * TQ, LANES), F32),
            pltpu.VMEM((2, N_KV, GQA * TQ, LANES), F32),
            pltpu.VMEM((TQ, D_B), F32),
        ],
        compiler_params=pltpu.CompilerParams(dimension_semantics=("arbitrary", "arbitrary"),
                                             vmem_limit_bytes=VMEM_LIMIT),
        name="attn_prompt",
    )(qc, qr, cmp_p, ov, _gate_expansion(), ksel, kwin, gates, szb)


def _cmpwin_sample_kernel(qc_ref, qr_ref, cmp_ref, ov_ref, win_ref, kvw_ref, gate_ref,
                          ocw_ref, sel_ref, *, past_len, n_pick, t_new):
    seqs = cmp_ref.shape[0]
    n_rows = N_KV * GQA * t_new
    lane = _lane_iota((t_new, LANES))
    n_cmp = cmp_ref.shape[1]
    row_t = _mod_pow2(_row_iota((n_rows, n_cmp)), t_new)
    slot = _lane_iota((n_rows, n_cmp))
    cmp_valid = (slot >= 1) & (slot * CMP_STRIDE + (CMP_STRIDE - 1) <= past_len + row_t)
    n_buf = win_ref.shape[1]
    dist = n_buf + _mod_pow2(_row_iota((n_rows, n_buf)), t_new) - _lane_iota((n_rows, n_buf))
    win_valid = (dist >= 0) & (dist < WINDOW)
    row_t1 = _mod_pow2(_row_iota((n_rows, 1)), t_new)

    def one_sequence(sq):
        rows_sq = slice(sq * t_new, (sq + 1) * t_new)
        gates = gate_ref[rows_sq, :]

        def q_rows(ref):
            return jnp.concatenate(
                [jnp.where(_div_pow2(lane, HEAD_DIM) == h, ref[rows_sq, g * LANES:(g + 1) * LANES], 0).astype(BF16)
                 for h in range(N_KV) for g in range(GQA)], axis=0)

        p = _softmax_rows(_dot_nt(q_rows(qc_ref), cmp_ref[sq, :, 0:LANES]), cmp_valid)
        o_cmp = _dot(p.astype(BF16), cmp_ref[sq, :, LANES:2 * LANES])
        p_sum = jnp.concatenate(
            [sum(p[(h * GQA + g) * t_new:(h * GQA + g + 1) * t_new] for g in range(GQA)) for h in range(N_KV)],
            axis=0)
        imp = _dot_split(p_sum, ov_ref[...])

        qr = q_rows(qr_ref)
        k_buf = win_ref[sq, :, 0:LANES].astype(BF16)
        v_buf = win_ref[sq, :, LANES:2 * LANES].astype(BF16)
        s_buf = jnp.where(win_valid, _dot_nt(qr, k_buf), NEG)
        k_new = kvw_ref[rows_sq, 0:LANES].astype(BF16).astype(F32)
        v_new = kvw_ref[rows_sq, LANES:2 * LANES].astype(BF16).astype(F32)
        qr32 = qr.astype(F32)
        s_new = [jnp.where(row_t1 >= i, jnp.sum(qr32 * k_new[i:i + 1, :], axis=-1, keepdims=True), NEG)
                 for i in range(t_new)]
        m = jnp.max(s_buf, axis=-1, keepdims=True)
        for s in s_new:
            m = jnp.maximum(m, s)
        e_buf = jnp.exp(s_buf - m)
        e_new = [jnp.exp(s - m) for s in s_new]
        denom = jnp.sum(e_buf, axis=-1, keepdims=True) + sum(e_new)
        o_win = _dot(e_buf.astype(BF16), v_buf)
        for i in range(t_new):
            o_win = o_win + e_new[i].astype(BF16).astype(F32) * v_new[i:i + 1, :]
        o_win = o_win / denom

        for g in range(GQA):
            parts = []
            for h in range(N_KV):
                rows = slice((h * GQA + g) * t_new, (h * GQA + g + 1) * t_new)
                base = h * GQA * 3 + g * 3
                parts.append(_gate_col(gates, base) * o_cmp[rows] + _gate_col(gates, base + 2) * o_win[rows])
            ocw_ref[rows_sq, g * LANES:(g + 1) * LANES] = jnp.where(lane < HEAD_DIM, parts[0], parts[1])
        return imp

    imp = jnp.concatenate([one_sequence(sq) for sq in range(seqs)], axis=0)
    blk = _lane_iota(imp.shape)
    tb = (past_len + t_new - 1) // SLC_BLOCK
    candidate = (blk >= 1) & (blk < tb - 1)
    score = jnp.where(candidate, imp, -1.0)
    out_lane = _lane_iota((imp.shape[0], LANES))
    picks = jnp.zeros((imp.shape[0], LANES), jnp.int32)
    for k in range(n_pick):
        best = jnp.max(score, axis=-1, keepdims=True)
        idx = jnp.min(jnp.where(score == best, blk, 1 << 20), axis=-1, keepdims=True)
        picks = jnp.where(out_lane == k, idx, picks)
        score = jnp.where(blk == idx, -2.0, score)
    sel_ref[...] = picks.reshape(sel_ref.shape)


def _cmpwin_sample_call(qc, qr, cmp_s, ov, win, kvw, gates, t_new, past_len, n_pick):
    n_seq = cmp_s.shape[0]
    seqs = SAMPLE_SEQS_PER_STEP
    assert n_seq % seqs == 0
    tile = lambda b: (b, 0)
    per_seq = lambda b: (b, 0, 0)
    return pl.pallas_call(
        functools.partial(_cmpwin_sample_kernel, past_len=past_len, n_pick=n_pick, t_new=t_new),
        grid=(n_seq // seqs,),
        in_specs=[
            pl.BlockSpec((seqs * t_new, D_B), tile),
            pl.BlockSpec((seqs * t_new, D_B), tile),
            pl.BlockSpec((seqs, cmp_s.shape[1], 2 * LANES), per_seq),
            pl.BlockSpec(ov.shape, lambda b: (0, 0)),
            pl.BlockSpec((seqs, win.shape[1], 2 * LANES), per_seq),
            pl.BlockSpec((seqs * t_new, 2 * LANES), tile),
            pl.BlockSpec((seqs * t_new, LANES), tile),
        ],
        out_specs=[
            pl.BlockSpec((seqs * t_new, D_B), tile),
            pl.BlockSpec((seqs, N_KV * t_new, LANES), per_seq),
        ],
        out_shape=[
            jax.ShapeDtypeStruct((n_seq * t_new, D_B), F32),
            jax.ShapeDtypeStruct((n_seq, N_KV * t_new, LANES), jnp.int32),
        ],
        compiler_params=pltpu.CompilerParams(dimension_semantics=("arbitrary",),
                                             vmem_limit_bytes=VMEM_LIMIT),
        name="cmpwin_sample",
    )(qc, qr, cmp_s, ov, win, kvw, gates)


def _sel_sample_kernel(sel_ref, pt_ref, cache_ref, qr_ref, rows_ref, gate_ref, ocw_ref, szb_ref,
                       out_ref, buf, sem, *, n_pick, n_pages, tb):
    t_new = qr_ref.shape[0]
    n_own = t_new * n_pick
    n_steps = pl.num_programs(0) * pl.num_programs(1)
    h = pl.program_id(1)
    step = pl.program_id(0) * pl.num_programs(1) + h
    blocks_per_page = PAGE_SIZE // SLC_BLOCK

    def block_copy(step_idx, slot, i, blk):
        seq = _div_pow2(step_idx, N_KV)
        head = _mod_pow2(step_idx, N_KV)
        phys = pt_ref[seq * n_pages + _div_pow2(blk, blocks_per_page)]
        return pltpu.make_async_copy(cache_ref.at[phys, pl.ds(2, 2), head], buf.at[slot, i], sem.at[slot])

    def for_all_blocks(step_idx, slot, fn):
        fn(block_copy(step_idx, slot, 0, 0))
        fn(block_copy(step_idx, slot, 1, tb - 1))

        def body(i, carry_):
            fn(block_copy(step_idx, slot, 2 + i, sel_ref[step_idx * n_own + i]))
            return carry_
        lax.fori_loop(0, n_own, body, 0, unroll=DMA_LOOP_UNROLL)

    @pl.when(step == 0)
    def _():
        for_all_blocks(0, 0, lambda cp: cp.start())

    slot = step % 2
    for_all_blocks(step, slot, lambda cp: cp.wait())

    nxt_step = jnp.minimum(step + 1, n_steps - 1)
    nxt_slot = (step + 1) % 2

    def start_next(first, count):
        for i in range(first, first + count):
            blk = (0, tb - 1)[i] if i < 2 else sel_ref[nxt_step * n_own + i - 2]
            block_copy(nxt_step, nxt_slot, i, blk).start()

    n_rows = GQA * t_new
    half = _div_pow2(_lane_iota((1, LANES)), SLC_BLOCK)

    def head_half(x):
        return jnp.where(h == 0, x[:, 0:HEAD_DIM], x[:, HEAD_DIM:LANES])

    qr = jnp.concatenate([head_half(qr_ref[:, g * LANES:(g + 1) * LANES]) for g in range(GQA)], axis=0)
    qrb = qr.astype(BF16)
    qr32 = qrb.astype(F32)
    row_t = _mod_pow2(_row_iota((n_rows, 1)), t_new)

    def slabs(first, count, kind):
        return jnp.concatenate([buf[slot, first + k, kind].astype(BF16) for k in range(count)], axis=1)

    bias_sh = jnp.concatenate(
        [jnp.where(half == blk % blocks_per_page, 0.0, NEG) for blk in (0, tb - 1)], axis=1)
    s_sh = _dot(qrb, slabs(0, 2, 0)) + bias_sh
    start_next(0, 2)
    k_new = head_half(rows_ref[:, 2 * LANES:3 * LANES]).astype(BF16).astype(F32)
    v_new = head_half(rows_ref[:, 3 * LANES:4 * LANES]).astype(BF16).astype(F32)
    s_new = [jnp.where(row_t >= i, jnp.sum(qr32 * k_new[i:i + 1, :], axis=-1, keepdims=True), NEG)
             for i in range(t_new)]
    own_keys = n_pick * PAGE_SIZE
    s_own = jnp.zeros((n_rows, own_keys), F32)
    for t in range(t_new):
        bias_t = jnp.concatenate(
            [jnp.where(half == _mod_pow2(sel_ref[step * n_own + t * n_pick + k], blocks_per_page), 0.0, NEG)
             for k in range(n_pick)], axis=1)
        s_t = _dot(qrb, slabs(2 + t * n_pick, n_pick, 0)) + bias_t
        s_own = jnp.where(row_t == t, s_t, s_own)
        start_next(2 + t * n_pick, n_pick)

    m = jnp.maximum(jnp.max(s_sh, axis=-1, keepdims=True), jnp.max(s_own, axis=-1, keepdims=True))
    for s in s_new:
        m = jnp.maximum(m, s)
    e_sh = jnp.exp(s_sh - m)
    e_own = jnp.exp(s_own - m)
    e_new = [jnp.exp(s - m) for s in s_new]
    denom = jnp.sum(e_sh, axis=-1, keepdims=True) + jnp.sum(e_own, axis=-1, keepdims=True) + sum(e_new)
    o = _dot_nt(e_sh.astype(BF16), slabs(0, 2, 1))
    for t in range(t_new):
        o = o + _dot_nt(jnp.where(row_t == t, e_own, 0.0).astype(BF16), slabs(2 + t * n_pick, n_pick, 1))
    for i in range(t_new):
        o = o + e_new[i].astype(BF16).astype(F32) * v_new[i:i + 1, :]
    o = o / denom

    gates = gate_ref[...]
    lane = _lane_iota((t_new, LANES))
    keep = _div_pow2(lane, HEAD_DIM) == h
    for g in range(GQA):
        sl = slice(g * LANES, (g + 1) * LANES)
        gate = jnp.sum(jnp.where(lane == h * GQA * 3 + g * 3 + 1, gates, 0.0), axis=-1, keepdims=True)
        o_g = o[g * t_new:(g + 1) * t_new]
        contrib = jnp.where(keep, gate * jnp.concatenate([o_g, o_g], axis=1), 0.0)

        @pl.when(h == 0)
        def _():
            out_ref[:, sl] = ocw_ref[:, sl] + contrib

        @pl.when(h == N_KV - 1)
        def _():
            out_ref[:, sl] = (out_ref[:, sl] + contrib) * szb_ref[:, sl]

    @pl.when(step == n_steps - 1)
    def _():
        for_all_blocks(nxt_step, nxt_slot, lambda cp: cp.wait())


def _sel_sample_call(sel_flat, pt_flat, cache_t, qr, rows, gates, ocw, szb, n_seq, t_new, n_pick, n_pages, tb):
    tile = lambda b, h, sel, pt: (b, 0)
    n_blocks = 2 + t_new * n_pick
    grid_spec = pltpu.PrefetchScalarGridSpec(
        num_scalar_prefetch=2,
        grid=(n_seq, N_KV),
        in_specs=[
            pl.BlockSpec(memory_space=pl.ANY),
            pl.BlockSpec((t_new, D_B), tile),
            pl.BlockSpec((t_new, 4 * LANES), tile),
            pl.BlockSpec((t_new, LANES), tile),
            pl.BlockSpec((t_new, D_B), tile),
            pl.BlockSpec((t_new, D_B), tile),
        ],
        out_specs=pl.BlockSpec((t_new, D_B), tile),
        scratch_shapes=[
            pltpu.VMEM((2, n_blocks, 2, HEAD_DIM, PAGE_SIZE), F32),
            pltpu.SemaphoreType.DMA((2,)),
        ],
    )
    return pl.pallas_call(
        functools.partial(_sel_sample_kernel, n_pick=n_pick, n_pages=n_pages, tb=tb),
        grid_spec=grid_spec,
        out_shape=jax.ShapeDtypeStruct((n_seq * t_new, D_B), F32),
        compiler_params=pltpu.CompilerParams(dimension_semantics=("arbitrary", "arbitrary"),
                                             vmem_limit_bytes=VMEM_LIMIT),
        name="sel_sample",
    )(sel_flat, pt_flat, cache_t, qr, rows, gates, ocw, szb)


def _merge_kernel(x_ref, a_ref, b_ref, wa_ref, wb_ref, fg_ref, y_ref):
    delta = _dot(a_ref[...].astype(BF16), wa_ref[...]) + _dot(b_ref[...].astype(BF16), wb_ref[...])
    x = x_ref[...] + delta
    ms = jnp.mean(x * x, axis=-1, keepdims=True)
    y_ref[...] = (x * lax.rsqrt(ms + EPS)) * fg_ref[...]


def _merge_call(x2d, a, b, wa, wb, fg):
    n_rows = x2d.shape[0]
    tm = min(TM_MERGE, n_rows)
    row = lambda i: (i, 0)
    const2 = lambda i: (0, 0)
    return pl.pallas_call(
        _merge_kernel,
        grid=(n_rows // tm,),
        in_specs=[
            pl.BlockSpec((tm, D_MODEL), row),
            pl.BlockSpec((tm, D_A), row),
            pl.BlockSpec((tm, D_B), row),
            pl.BlockSpec((D_A, D_MODEL), const2),
            pl.BlockSpec((D_B, D_MODEL), const2),
            pl.BlockSpec((1, D_MODEL), const2),
        ],
        out_specs=pl.BlockSpec((tm, D_MODEL), row),
        out_shape=jax.ShapeDtypeStruct((n_rows, D_MODEL), F32),
        compiler_params=pltpu.CompilerParams(dimension_semantics=("arbitrary",),
                                             vmem_limit_bytes=VMEM_LIMIT),
        name="merge",
    )(x2d, a, b, wa, wb, fg)


def _head_pair_perm(w_cols):
    lead = w_cols.shape[:-1]
    return w_cols.reshape(*lead, N_KV, GQA, HEAD_DIM).swapaxes(-3, -2).reshape(*lead, D_B)


def _prep_w_in(w_in):
    cuts = np.cumsum([D_A, D_A, D_A, D_B, 6 * N_KV * HEAD_DIM, 3 * N_HEADS]).tolist()
    u, v, za, q, kv, g, zb = jnp.split(w_in, cuts, axis=-1)
    g_pad = jnp.pad(g, ((0, 0), (0, LANES - g.shape[1])))
    return jnp.concatenate([u, v, za, _head_pair_perm(q), kv, _head_pair_perm(zb), g_pad], axis=-1).astype(BF16)


def _prep_mix(w_s, b_s, chunk_len):
    reps = TM // chunk_len
    row = np.arange(TM)[:, None]
    col = np.arange(TM)[None, :]
    causal_same_chunk = (row // chunk_len == col // chunk_len) & (col <= row)
    wmix = jnp.where(causal_same_chunk, jnp.tile(w_s[:, :chunk_len, :chunk_len], (1, reps, reps)), 0).astype(BF16)
    bias = jnp.repeat(b_s[:, :chunk_len].T, HEAD_DIM, axis=1)
    return wmix, jnp.tile(bias, (reps, 1))


def _rope_tables(pos):
    half = HEAD_DIM // 2
    inv = ROPE_THETA ** (-jnp.arange(half, dtype=F32) / half)
    ang = pos.astype(F32)[:, None] * inv
    cos = jnp.tile(jnp.cos(ang), (1, LANES // half))
    sin = jnp.tile(jnp.concatenate([-jnp.sin(ang), jnp.sin(ang)], axis=1), (1, LANES // HEAD_DIM))
    return cos, sin


def _prep_compress(cmp_pos, w_cmp1, b_cmp1, w_cmp2):
    ratio = CMP_BLOCK // CMP_STRIDE
    w1 = w_cmp1.reshape(2, ratio, CMP_STRIDE, HEAD_DIM, CMP_HIDDEN).transpose(0, 2, 3, 1, 4)
    w1 = w1.reshape(2, CMP_STRIDE, HEAD_DIM, ratio * CMP_HIDDEN)
    zeros = jnp.zeros_like(w1)
    wc = jnp.stack([jnp.concatenate([w1, zeros], axis=-1), jnp.concatenate([zeros, w1], axis=-1)], axis=2)
    wc = wc.reshape(2, CMP_STRIDE * LANES, N_KV * ratio * CMP_HIDDEN).astype(BF16)
    pe = cmp_pos.reshape(2, ratio, CMP_STRIDE, 1, HEAD_DIM)
    pe = jnp.broadcast_to(pe, (2, ratio, CMP_STRIDE, N_KV, HEAD_DIM)).reshape(2, ratio, CMP_STRIDE * LANES)
    pe = jnp.pad(pe, ((0, 0), (0, 16 - ratio), (0, 0))).astype(BF16)
    w2 = w_cmp2[:, None]
    zeros2 = jnp.zeros_like(w2)
    w2p = jnp.concatenate([jnp.concatenate([w2, zeros2], axis=-1), jnp.concatenate([zeros2, w2], axis=-1)],
                          axis=1).astype(BF16)
    return wc, pe, b_cmp1.reshape(2, 1, CMP_HIDDEN), w2p


def _overlap_matrix(n_cmp_slots, n_slc, n_lanes):
    start = (np.arange(n_cmp_slots)[:, None] - 1) * CMP_STRIDE
    j = np.arange(n_lanes)[None, :]
    ov = (start <= j * SLC_BLOCK + SLC_BLOCK - 1) & (start + CMP_BLOCK - 1 >= j * SLC_BLOCK)
    ov &= (np.arange(n_cmp_slots)[:, None] >= 1) & (j < n_slc)
    return jnp.asarray(ov, dtype=BF16)


def kernel(x_prompt, x_sample, cache_kv, state_win, page_table, norm_g, w_in, ln_g, ln_b, w_s, b_s,
           cmp_pos, w_cmp1, b_cmp1, w_cmp2, w_out, final_g):
    n_batch, seq, _ = x_prompt.shape
    n_seq, t_new, _ = x_sample.shape
    depth, n_phys = cache_kv.shape[:2]
    n_pages = page_table.shape[1]
    past_len = n_pages * PAGE_SIZE
    win_buf = state_win.shape[2]
    assert depth == 1 and seq % TQ == 0 and n_seq * t_new == TM and win_buf == WINDOW
    assert (past_len + t_new - 1) // SLC_BLOCK == past_len // SLC_BLOCK and past_len % CMP_STRIDE == 0

    w_all = _prep_w_in(w_in[0])
    ng = norm_g[0].reshape(1, D_MODEL)
    lng = ln_g[0].reshape(1, D_A)
    lnb = ln_b[0].reshape(1, D_A)
    fg = final_g.reshape(1, D_MODEL)
    wo = w_out[0]
    wo_a = wo[:D_A].astype(BF16)
    wo_b = _head_pair_perm(wo[D_A:].T).T.astype(BF16)
    wc, pe, b1, w2p = _prep_compress(cmp_pos[0], w_cmp1[0], b_cmp1[0], w_cmp2[0])

    xp = x_prompt.reshape(n_batch * seq, D_MODEL)
    wmix_p, bmix_p = _prep_mix(w_s[0], b_s[0], CHUNK)
    cos_p, sin_p = _rope_tables(jnp.arange(seq))
    ag_p, qc_p, qr_p, rows_p, kvw_pt, ksel_p, kwin_p, gate_p, szb_p, rows_pt = _proj_call(
        xp, ng, w_all, lng, lnb, wmix_p, bmix_p, cos_p, sin_p, BF16, seq)
    pages_p = seq // PAGE_SIZE
    ident = jnp.arange(n_batch * pages_p, dtype=jnp.int32).reshape(n_batch, pages_p)
    cmp_p = _compress_call(rows_p.reshape(n_batch * pages_p, PAGE_SIZE, 4 * LANES), ident, wc, pe, b1, w2p,
                           pages_p, False)
    ov_p = _overlap_matrix(seq // CMP_STRIDE, seq // SLC_BLOCK, LANES)
    bg_p = _attn_prompt_call(qc_p, qr_p, cmp_p, ov_p, ksel_p, kwin_p, gate_p, szb_p, n_batch, seq)
    y_p = _merge_call(xp, ag_p, bg_p, wo_a, wo_b, fg)

    xs = x_sample.reshape(n_seq * t_new, D_MODEL)
    wmix_s, bmix_s = _prep_mix(w_s[0], b_s[0], t_new)
    cos_s, sin_s = _rope_tables(jnp.tile(past_len + jnp.arange(t_new), n_seq))
    ag_s, qc_s, qr_s, rows_s, kvw_s, _, _, gate_s, szb_s, vln_s = _proj_call(
        xs, ng, w_all, lng, lnb, wmix_s, bmix_s, cos_s, sin_s, F32, None)
    cache_t = cache_kv[0].transpose(0, 2, 3, 4, 1)
    cmp_s = _compress_call(cache_t, page_table, wc, pe, b1, w2p, SAMPLE_PAGES_PER_STEP, True)
    n_slc_s = -(-(past_len + t_new) // SLC_BLOCK)
    tb = past_len // SLC_BLOCK
    n_pick = TOP_N - 3
    ov_s = _overlap_matrix(past_len // CMP_STRIDE, n_slc_s, -(-n_slc_s // LANES) * LANES)
    win_prev = state_win[0].reshape(n_seq, win_buf, 2 * LANES)
    ocw_s, sel_s = _cmpwin_sample_call(qc_s, qr_s, cmp_s, ov_s, win_prev, kvw_s, gate_s, t_new, past_len, n_pick)
    sel_flat = sel_s[:, :, :n_pick].reshape(-1)
    bg_s = _sel_sample_call(sel_flat, page_table.reshape(-1), cache_t, qr_s, rows_s, gate_s, ocw_s, szb_s,
                            n_seq, t_new, n_pick, n_pages, tb)
    y_s = _merge_call(xs, ag_s, bg_s, wo_a, wo_b, fg)

    new_kv_p = rows_pt.reshape(n_batch, 4, N_KV, HEAD_DIM, seq).transpose(0, 4, 1, 2, 3)
    new_win_p = kvw_pt[:, :, seq - win_buf:].reshape(n_batch, 2, N_KV, HEAD_DIM, win_buf).transpose(0, 4, 1, 2, 3)
    new_win_s = jnp.concatenate([state_win[0][:, t_new:], kvw_s.reshape(n_seq, t_new, 2, N_KV, HEAD_DIM)], axis=1)
    return (y_p.reshape(n_batch, seq, D_MODEL),
            y_s.reshape(n_seq, t_new, D_MODEL),
            new_kv_p[None],
            new_win_p[None],
            rows_s.reshape(1, n_seq, t_new, 4, N_KV, HEAD_DIM),
            new_win_s[None],
            vln_s.reshape(1, n_seq, t_new, D_A))
```

```python
import functools
import math

import jax
import jax.numpy as jnp
import numpy as np
from jax import lax
from jax.experimental import pallas as pl
from jax.experimental.pallas import tpu as pltpu

F32 = jnp.float32
BF16 = jnp.bfloat16

D_MODEL = 1024
HEAD_DIM = 64
D_A = 512
A_GROUPS = 8
CHUNK = 128
N_HEADS = 8
D_B = 512
N_KV = 2
GQA = 4
CMP_BLOCK = 32
CMP_STRIDE = 16
CMP_HIDDEN = 128
SLC_BLOCK = 64
TOP_N = 16
WINDOW = 512
ROPE_THETA = 10000.0
EPS = 1e-6
SCALE = HEAD_DIM ** -0.5
NEG = -1e30
FORCE = 1e9
PAGE_SIZE = 128

LANES = 128
VMEM_LIMIT = 56 * 1024 * 1024

C_U, C_V, C_ZA, C_Q, C_KV, C_ZB, C_G, C_END = 0, 512, 1024, 1536, 2048, 2816, 3328, 3456

TM = 256
TM_MERGE = 512
DMA_LOOP_UNROLL = 8
SAMPLE_SEQS_PER_STEP = 8
SAMPLE_PAGES_PER_STEP = 64
TQ = 256
SEGS_PER_PAGE = PAGE_SIZE // CMP_STRIDE


def _lane_iota(shape):
    return lax.broadcasted_iota(jnp.int32, shape, len(shape) - 1)


def _row_iota(shape):
    return lax.broadcasted_iota(jnp.int32, shape, len(shape) - 2)


def _div_pow2(x, n):
    assert n & (n - 1) == 0
    return lax.shift_right_logical(x, int(math.log2(n))) if n > 1 else x


def _mod_pow2(x, n):
    assert n & (n - 1) == 0
    return x & (n - 1)


def _dot(a, b):
    return jnp.dot(a, b, preferred_element_type=F32)


def _dot_nt(a, b):
    return lax.dot_general(a, b, (((1,), (1,)), ((), ())), preferred_element_type=F32)


def _rope(x, cos, sin_signed):
    lo = _mod_pow2(_lane_iota(x.shape), HEAD_DIM) < (HEAD_DIM // 2)
    swapped = jnp.where(lo, pltpu.roll(x, LANES - HEAD_DIM // 2, 1), pltpu.roll(x, HEAD_DIM // 2, 1))
    return x * cos + swapped * sin_signed


def _proj_kernel(x_ref, ng_ref, w_ref, lng_ref, lnb_ref, wmix_ref, bmix_ref, cos_ref, sin_ref,
                 ag_ref, qc_ref, qr_ref, rows_ref, kvw_ref, ksel_ref, kwin_ref, gate_ref, szb_ref,
                 extra_ref, *, prompt):
    x = x_ref[...]
    ms = jnp.mean(x * x, axis=-1, keepdims=True)
    hb = ((x * lax.rsqrt(ms + EPS)) * ng_ref[...]).astype(BF16)
    cos = cos_ref[...]
    sin = sin_ref[...]
    pair = 2 * LANES

    def proj(c0, n):
        return _dot(hb, w_ref[:, c0:c0 + n])

    def halves(x2):
        return x2[:, 0:LANES], x2[:, LANES:pair]

    v = jax.nn.gelu(proj(C_V, D_A))
    mu = jnp.mean(v, axis=-1, keepdims=True)
    vc = v - mu
    var = jnp.mean(vc * vc, axis=-1, keepdims=True)
    vln = vc * lax.rsqrt(var + EPS) * lng_ref[...] + lnb_ref[...]
    if not prompt:
        extra_ref[...] = vln
    vb = vln.astype(BF16)
    first_head = _lane_iota((x.shape[0], LANES)) < HEAD_DIM
    for gpp in range(A_GROUPS // 4):
        u2 = halves(jax.nn.gelu(proj(C_U + gpp * pair, pair)))
        za2 = halves(proj(C_ZA + gpp * pair, pair))
        for k in range(2):
            gp = 2 * gpp + k
            sl = slice(gp * LANES, (gp + 1) * LANES)
            blk = vb[:, sl]
            mixed = jnp.where(first_head, _dot(wmix_ref[2 * gp], blk), _dot(wmix_ref[2 * gp + 1], blk))
            mixed = mixed + bmix_ref[:, sl]
            ag_ref[:, sl] = (u2[k] * mixed * jax.nn.silu(za2[k])).astype(ag_ref.dtype)

    for gpp in range(GQA // 2):
        q2 = halves(proj(C_Q + gpp * pair, pair))
        for k in range(2):
            sl = slice((2 * gpp + k) * LANES, (2 * gpp + k + 1) * LANES)
            qc_ref[:, sl] = (q2[k] * SCALE).astype(qc_ref.dtype)
            qr_ref[:, sl] = (_rope(q2[k], cos, sin) * SCALE).astype(qr_ref.dtype)

    kc, vcm = halves(proj(C_KV, pair))
    ks, vs = halves(proj(C_KV + pair, pair))
    kw, vw = halves(proj(C_KV + 2 * pair, pair))
    ks = _rope(ks, cos, sin)
    kw = _rope(kw, cos, sin)
    for i, blk in enumerate((kc, vcm, ks, vs)):
        rows_ref[:, i * LANES:(i + 1) * LANES] = blk
        if prompt:
            extra_ref[0, i * LANES:(i + 1) * LANES, :] = blk.T
    for i, blk in enumerate((kw, vw)):
        if prompt:
            kvw_ref[0, i * LANES:(i + 1) * LANES, :] = blk.T
        else:
            kvw_ref[:, i * LANES:(i + 1) * LANES] = blk
    ksel_ref[:, 0:LANES] = ks.astype(BF16)
    ksel_ref[:, LANES:2 * LANES] = vs.astype(BF16)
    kwin_ref[:, 0:LANES] = kw.astype(BF16)
    kwin_ref[:, LANES:2 * LANES] = vw.astype(BF16)

    gate_ref[...] = jax.nn.sigmoid(proj(C_G, LANES))
    for gpp in range(GQA // 2):
        zb2 = halves(proj(C_ZB + gpp * pair, pair))
        for k in range(2):
            sl = slice((2 * gpp + k) * LANES, (2 * gpp + k + 1) * LANES)
            szb_ref[:, sl] = jax.nn.silu(zb2[k]).astype(szb_ref.dtype)


def _proj_call(x2d, ng, w, lng, lnb, wmix, bmix, cos_t, sin_t, act_dtype, seq_len):
    n_rows = x2d.shape[0]
    n_tiles = n_rows // TM
    pos_tiles = cos_t.shape[0] // TM
    row = lambda i: (i, 0)
    const2 = lambda i: (0, 0)
    out_shapes = [
        jax.ShapeDtypeStruct((n_rows, D_A), act_dtype),
        jax.ShapeDtypeStruct((n_rows, D_B), act_dtype),
        jax.ShapeDtypeStruct((n_rows, D_B), act_dtype),
        jax.ShapeDtypeStruct((n_rows, 4 * LANES), F32),
        jax.ShapeDtypeStruct((n_rows, 2 * LANES), F32),
        jax.ShapeDtypeStruct((n_rows, 2 * LANES), BF16),
        jax.ShapeDtypeStruct((n_rows, 2 * LANES), BF16),
        jax.ShapeDtypeStruct((n_rows, LANES), F32),
        jax.ShapeDtypeStruct((n_rows, D_B), act_dtype),
        jax.ShapeDtypeStruct((n_rows, D_A), F32),
    ]
    out_specs = [pl.BlockSpec((TM, s.shape[1]), row) for s in out_shapes]
    if seq_len is not None:
        tiles_per_seq = seq_len // TM
        dim_major = lambda i: (i // tiles_per_seq, 0, i % tiles_per_seq)
        for idx, width in ((4, 2 * LANES), (9, 4 * LANES)):
            out_shapes[idx] = jax.ShapeDtypeStruct((n_rows // seq_len, width, seq_len), F32)
            out_specs[idx] = pl.BlockSpec((1, width, TM), dim_major)
    return pl.pallas_call(
        functools.partial(_proj_kernel, prompt=seq_len is not None),
        grid=(n_tiles,),
        in_specs=[
            pl.BlockSpec((TM, D_MODEL), row),
            pl.BlockSpec((1, D_MODEL), const2),
            pl.BlockSpec((D_MODEL, C_END), const2),
            pl.BlockSpec((1, D_A), const2),
            pl.BlockSpec((1, D_A), const2),
            pl.BlockSpec((A_GROUPS, TM, TM), lambda i: (0, 0, 0)),
            pl.BlockSpec((TM, D_A), const2),
            pl.BlockSpec((TM, LANES), lambda i: (i % pos_tiles, 0)),
            pl.BlockSpec((TM, LANES), lambda i: (i % pos_tiles, 0)),
        ],
        out_specs=out_specs,
        out_shape=out_shapes,
        compiler_params=pltpu.CompilerParams(dimension_semantics=("arbitrary",),
                                             vmem_limit_bytes=VMEM_LIMIT),
        name="proj",
    )(x2d, ng, w, lng, lnb, wmix, bmix, cos_t, sin_t)


def _compress_kernel(pt_ref, src_ref, wc_ref, pe_ref, b1_ref, w2_ref, out_ref, buf, sem, carry,
                     *maybe_stage, pages_per_step):
    transposed_src = bool(maybe_stage)
    n_steps = pl.num_programs(0) * pl.num_programs(1)
    step = pl.program_id(0) * pl.num_programs(1) + pl.program_id(1)
    m = pages_per_step * SEGS_PER_PAGE

    def page_copies(step_idx, slot, i):
        phys = pt_ref[step_idx * pages_per_step + i]
        if transposed_src:
            return [pltpu.make_async_copy(src_ref.at[phys, pl.ds(0, 2)], maybe_stage[0].at[slot, i], sem.at[slot])]
        return [pltpu.make_async_copy(src_ref.at[phys, :, pl.ds(c * LANES, LANES)],
                                      buf.at[slot, c, pl.ds(i * PAGE_SIZE, PAGE_SIZE), :],
                                      sem.at[slot]) for c in range(2)]

    def start_step(step_idx, slot):
        def body(i, carry_):
            for cp in page_copies(step_idx, slot, i):
                cp.start()
            return carry_
        lax.fori_loop(0, pages_per_step, body, 0, unroll=DMA_LOOP_UNROLL)

    def wait_step(step_idx, slot):
        def body(i, carry_):
            for cp in page_copies(step_idx, slot, i):
                cp.wait()
            return carry_
        lax.fori_loop(0, pages_per_step, body, 0, unroll=DMA_LOOP_UNROLL)

    slot = step % 2

    first_of_seq = pl.program_id(1) == 0
    first_row = _row_iota((m, LANES)) == 0

    def compress_rows(rows_slot, seg_pitch):
        for c in range(2):
            pieces = [buf[rows_slot, c, pl.ds(j, m, stride=seg_pitch), :].astype(BF16)
                      for j in range(CMP_STRIDE)]
            part = _dot(jnp.concatenate(pieces, axis=1), wc_ref[c])
            pe_part = _dot(pe_ref[c], wc_ref[c])
            acc_out = None
            for h in range(N_KV):
                lo = slice(h * 2 * CMP_HIDDEN, h * 2 * CMP_HIDDEN + CMP_HIDDEN)
                hi = slice(h * 2 * CMP_HIDDEN + CMP_HIDDEN, (h + 1) * 2 * CMP_HIDDEN)
                bias = pe_part[0:1, lo] + pe_part[1:2, hi] + b1_ref[c]
                part0 = part[:, lo]
                prev_last = jnp.where(first_of_seq, 0.0, carry[c, h, 0:1, :])
                prev0 = jnp.where(first_row, prev_last, pltpu.roll(part0, 1, 0))
                carry[c, h, 0:1, :] = part0[m - 1:m, :]
                hid = jax.nn.gelu(prev0 + part[:, hi] + bias).astype(BF16)
                contrib = _dot(hid, w2_ref[c, h])
                acc_out = contrib if acc_out is None else acc_out + contrib
            out_ref[0, :, c * LANES:(c + 1) * LANES] = acc_out.astype(out_ref.dtype)

    if transposed_src:
        stage = maybe_stage[0]
        seg_pitch = CMP_STRIDE + 1

        def transpose_step(stage_slot):
            for i in range(pages_per_step):
                for c in range(2):
                    rows = stage[stage_slot, i, c].reshape(N_KV * HEAD_DIM, PAGE_SIZE).T
                    for s in range(SEGS_PER_PAGE):
                        r0 = (i * SEGS_PER_PAGE + s) * seg_pitch
                        buf[stage_slot, c, r0:r0 + CMP_STRIDE, :] = rows[s * CMP_STRIDE:(s + 1) * CMP_STRIDE]

        @pl.when(step == 0)
        def _():
            start_step(0, 0)
            start_step(1, 1)
            wait_step(0, 0)
            transpose_step(0)

        @pl.when(step + 1 < n_steps)
        def _():
            wait_step(step + 1, (step + 1) % 2)

        @pl.when(step + 2 < n_steps)
        def _():
            start_step(step + 2, slot)

        for parity in range(2):
            @pl.when(slot == parity)
            def _():
                transpose_step(1 - parity)
                compress_rows(parity, seg_pitch)
    else:
        @pl.when(step == 0)
        def _():
            start_step(0, 0)

        @pl.when(step + 1 < n_steps)
        def _():
            start_step(step + 1, (step + 1) % 2)

        wait_step(step, slot)
        compress_rows(slot, CMP_STRIDE)


def _compress_call(src, page_table, wc, pe, b1, w2p, pages_per_step, transposed_src):
    n_seq, n_pages = page_table.shape
    n_chunks = n_pages // pages_per_step
    m = pages_per_step * SEGS_PER_PAGE
    assert not transposed_src or n_seq * n_chunks >= 2
    seg_pitch = CMP_STRIDE + 1 if transposed_src else CMP_STRIDE
    scratch = [
        pltpu.VMEM((2, 2, m * seg_pitch, LANES), F32),
        pltpu.SemaphoreType.DMA((2,)),
        pltpu.VMEM((2, N_KV, 8, CMP_HIDDEN), F32),
    ]
    if transposed_src:
        scratch.append(pltpu.VMEM((2, pages_per_step, 2, N_KV, HEAD_DIM, PAGE_SIZE), F32))
    grid_spec = pltpu.PrefetchScalarGridSpec(
        num_scalar_prefetch=1,
        grid=(n_seq, n_chunks),
        in_specs=[
            pl.BlockSpec(memory_space=pl.ANY),
            pl.BlockSpec((2, CMP_STRIDE * LANES, 4 * CMP_HIDDEN), lambda b, k, pt: (0, 0, 0)),
            pl.BlockSpec((2, 16, CMP_STRIDE * LANES), lambda b, k, pt: (0, 0, 0)),
            pl.BlockSpec((2, 1, CMP_HIDDEN), lambda b, k, pt: (0, 0, 0)),
            pl.BlockSpec((2, N_KV, CMP_HIDDEN, LANES), lambda b, k, pt: (0, 0, 0, 0)),
        ],
        out_specs=pl.BlockSpec((1, m, 2 * LANES), lambda b, k, pt: (b, k, 0)),
        scratch_shapes=scratch,
    )
    return pl.pallas_call(
        functools.partial(_compress_kernel, pages_per_step=pages_per_step),
        grid_spec=grid_spec,
        out_shape=jax.ShapeDtypeStruct((n_seq, n_pages * SEGS_PER_PAGE, 2 * LANES), BF16),
        compiler_params=pltpu.CompilerParams(dimension_semantics=("arbitrary", "arbitrary"),
                                             vmem_limit_bytes=VMEM_LIMIT),
        name="compress",
    )(page_table.reshape(-1), src, wc, pe, b1, w2p)


def _softmax_rows(s, mask):
    s = jnp.where(mask, s, NEG)
    e = jnp.where(mask, jnp.exp(s - jnp.max(s, axis=-1, keepdims=True)), 0.0)
    return e / jnp.maximum(jnp.sum(e, axis=-1, keepdims=True), 1.0)


def _dot_split(p, w):
    hi = p.astype(BF16)
    lo = (p - hi.astype(F32)).astype(BF16)
    return _dot(hi, w) + _dot(lo, w)


def _gate_col(gates, idx):
    return gates[:, idx:idx + 1]


def _attn_prompt_kernel(qc_ref, qr_ref, cmp_ref, ov_ref, eg_ref, ksel_ref, kwin_ref, gate_ref, szb_ref,
                        out_ref, q_sc, m_sc, acc_sc, cmp_sc):
    qi = pl.program_id(1)
    lane = _lane_iota((TQ, LANES))
    lo_half = lane < HEAD_DIM
    tq = qi * TQ + _row_iota((TQ, LANES))
    r_minus_c = _row_iota((TQ, TQ)) - _lane_iota((TQ, TQ))
    causal_bias = jnp.where(r_minus_c >= 0, 0.0, NEG)
    far_bias = jnp.where(r_minus_c < 0, 0.0, NEG)
    pair_bias = jnp.concatenate([jnp.zeros((TQ, TQ), F32), causal_bias], axis=1)
    kcmp = cmp_ref[0, :, 0:LANES]
    vcmp = cmp_ref[0, :, LANES:2 * LANES]
    cmp_valid = (lane >= 1) & (lane * CMP_STRIDE + (CMP_STRIDE - 1) <= tq)
    n_slc = ksel_ref.shape[0] // SLC_BLOCK
    blocks_per_tile = TQ // SLC_BLOCK

    def flash_tile(br, h, k_tile, v_aug, bias):
        width = k_tile.shape[0]
        s = _dot_nt(q_sc[h], k_tile)
        if bias is not None:
            s = (s.reshape(GQA, TQ, width) + bias[None]).reshape(GQA * TQ, width)
        m_prev = m_sc[h]
        m_new = jnp.maximum(m_prev, jnp.max(s, axis=-1, keepdims=True))
        alpha = jnp.exp(m_prev - m_new)
        p = jnp.exp(s - jnp.concatenate([m_new] * (width // LANES), axis=1))
        acc_sc[br, h] = alpha * acc_sc[br, h] + _dot(p.astype(BF16), v_aug)
        m_sc[h] = m_new

    def flash_init(br):
        m_sc[...] = jnp.full(m_sc.shape, NEG, F32)
        acc_sc[br] = jnp.zeros(acc_sc.shape[1:], F32)

    for h in range(N_KV):
        keep = _div_pow2(lane, HEAD_DIM) == h
        other_off = HEAD_DIM * (1 - h)

        p_sum = jnp.zeros((TQ, LANES), F32)
        for g in range(GQA):
            sl = slice(g * LANES, (g + 1) * LANES)
            qc = jnp.where(keep, qc_ref[:, sl], 0)
            p = _softmax_rows(_dot_nt(qc, kcmp), cmp_valid)
            p_sum = p_sum + p
            o_cmp = _dot(p.astype(BF16), vcmp)
            cmp_sc[:, sl] = o_cmp if h == 0 else jnp.where(keep, o_cmp, cmp_sc[:, sl])
        imp = _dot_split(p_sum, ov_ref[...])
        tb = _div_pow2(tq, SLC_BLOCK)
        forced = (lane == 0) | (lane == tb) | (lane == tb - 1)
        valid = lane * SLC_BLOCK <= tq
        score = jnp.where(forced, FORCE, jnp.where(valid, imp, NEG))
        score_t = score.T[0:n_slc, :]
        blk = _row_iota((n_slc, TQ))
        rank = jnp.zeros((n_slc, TQ), jnp.int32)
        for j in range(n_slc):
            other = score_t[j:j + 1, :]
            ahead = (other > score_t) | ((other == score_t) & (blk > j))
            rank = rank + ahead.astype(jnp.int32)
        sel_bias_t = jnp.where(rank < TOP_N, 0.0, NEG)
        pieces = [jnp.zeros((other_off, TQ), F32)] if other_off else []
        pieces += [sel_bias_t, jnp.zeros((LANES - other_off - n_slc, TQ), F32)]
        sel_bias = jnp.concatenate(pieces, axis=0).T.astype(BF16)
        for g in range(GQA):
            sl = slice(g * LANES, (g + 1) * LANES)
            q_sc[h, g * TQ:(g + 1) * TQ, :] = jnp.where(keep, qr_ref[:, sl], sel_bias)

    def span_operands(kv_ref, kt, n_tiles, h, other_lanes):
        k = kv_ref[pl.ds(kt * TQ, n_tiles * TQ), 0:LANES]
        v = kv_ref[pl.ds(kt * TQ, n_tiles * TQ), LANES:2 * LANES]
        own = _div_pow2(_lane_iota(k.shape), HEAD_DIM) == h
        return jnp.where(own, k, other_lanes(k.shape)), jnp.where(own, v, 1.0)

    def sel_span(kt, n_tiles, bias):
        for h in range(N_KV):
            def indicator(shape, h=h):
                first = HEAD_DIM * (1 - h) + kt * blocks_per_tile
                hit = _lane_iota(shape) == first + _div_pow2(_row_iota(shape), SLC_BLOCK)
                return jnp.where(hit, 1, 0).astype(BF16)
            flash_tile(0, h, *span_operands(ksel_ref, kt, n_tiles, h, indicator), bias)

    flash_init(0)

    def sel_pair(i, carry_):
        sel_span(2 * i, 2, None)
        return carry_
    lax.fori_loop(0, _div_pow2(jnp.maximum(qi - 1, 0), 2), sel_pair, 0)

    @pl.when((qi >= 2) & (_mod_pow2(qi, 2) == 0))
    def _():
        sel_span(qi - 2, 1, None)

    @pl.when(qi >= 1)
    def _():
        sel_span(qi - 1, 2, pair_bias)

    @pl.when(qi == 0)
    def _():
        sel_span(qi, 1, causal_bias)

    def win_span(kt, n_tiles, bias):
        for h in range(N_KV):
            flash_tile(1, h, *span_operands(kwin_ref, kt, n_tiles, h, lambda shape: jnp.zeros(shape, BF16)), bias)

    flash_init(1)
    far = WINDOW // TQ
    assert far == 2

    @pl.when(qi >= far)
    def _():
        win_span(qi - far, 1, far_bias)

    @pl.when(qi >= 1)
    def _():
        win_span(qi - 1, 2, pair_bias)

    @pl.when(qi == 0)
    def _():
        win_span(qi, 1, causal_bias)

    gates_x = _dot_split(gate_ref[...], eg_ref[...])
    for g in range(GQA):
        rows = slice(g * TQ, (g + 1) * TQ)
        sl = slice(g * LANES, (g + 1) * LANES)
        o = gates_x[:, (g * 3) * LANES:(g * 3 + 1) * LANES] * cmp_sc[:, sl]
        for br in range(2):
            a0 = acc_sc[br, 0, rows, :]
            a1 = acc_sc[br, 1, rows, :]
            num = jnp.where(lo_half, a0, a1)
            den = pltpu.roll(jnp.where(lo_half, a1, a0), HEAD_DIM, 1)
            o = o + gates_x[:, (g * 3 + 1 + br) * LANES:(g * 3 + 2 + br) * LANES] * (num / den)
        out_ref[:, sl] = (o * szb_ref[:, sl].astype(F32)).astype(out_ref.dtype)


def _gate_expansion():
    eg = np.zeros((LANES, GQA * 3 * LANES), np.float32)
    for h in range(N_KV):
        for g in range(GQA):
            for br in range(3):
                c0 = (g * 3 + br) * LANES + h * HEAD_DIM
                eg[h * GQA * 3 + g * 3 + br, c0:c0 + HEAD_DIM] = 1.0
    return jnp.asarray(eg, dtype=BF16)


def _attn_prompt_call(qc, qr, cmp_p, ov, ksel, kwin, gates, szb, n_batch, seq):
    nq = seq // TQ
    tile = lambda b, q: (b * nq + q, 0)
    whole = lambda b, q: (b, 0)
    return pl.pallas_call(
        _attn_prompt_kernel,
        grid=(n_batch, nq),
        in_specs=[
            pl.BlockSpec((TQ, D_B), tile),
            pl.BlockSpec((TQ, D_B), tile),
            pl.BlockSpec((1, LANES, 2 * LANES), lambda b, q: (b, 0, 0)),
            pl.BlockSpec((LANES, LANES), lambda b, q: (0, 0)),
            pl.BlockSpec((LANES, GQA * 3 * LANES), lambda b, q: (0, 0)),
            pl.BlockSpec((seq, 2 * LANES), whole),
            pl.BlockSpec((seq, 2 * LANES), whole),
            pl.BlockSpec((TQ, LANES), tile),
            pl.BlockSpec((TQ, D_B), tile),
        ],
        out_specs=pl.BlockSpec((TQ, D_B), tile),
        out_shape=jax.ShapeDtypeStruct((n_batch * seq, D_B), BF16),
        scratch_shapes=[
            pltpu.VMEM((N_KV, GQA * TQ, LANES), BF16),
            pltpu.VMEM((N_KV, GQA * TQ, LANES), F32),
            pltpu.VMEM((2, N_KV, GQA * TQ, LANES), F32),
            pltpu.VMEM((TQ, D_B), F32),
        ],
        compiler_params=pltpu.CompilerParams(dimension_semantics=("arbitrary", "arbitrary"),
                                             vmem_limit_bytes=VMEM_LIMIT),
        name="attn_prompt",
    )(qc, qr, cmp_p, ov, _gate_expansion(), ksel, kwin, gates, szb)


def _cmpwin_sample_kernel(qc_ref, qr_ref, cmp_ref, ov_ref, win_ref, kvw_ref, gate_ref,
                          ocw_ref, sel_ref, *, past_len, n_pick, t_new):
    seqs = cmp_ref.shape[0]
    n_rows = N_KV * GQA * t_new
    lane = _lane_iota((t_new, LANES))
    n_cmp = cmp_ref.shape[1]
    row_t = _mod_pow2(_row_iota((n_rows, n_cmp)), t_new)
    slot = _lane_iota((n_rows, n_cmp))
    cmp_valid = (slot >= 1) & (slot * CMP_STRIDE + (CMP_STRIDE - 1) <= past_len + row_t)
    n_buf = win_ref.shape[1]
    dist = n_buf + _mod_pow2(_row_iota((n_rows, n_buf)), t_new) - _lane_iota((n_rows, n_buf))
    win_valid = (dist >= 0) & (dist < WINDOW)
    row_t1 = _mod_pow2(_row_iota((n_rows, 1)), t_new)

    def one_sequence(sq):
        rows_sq = slice(sq * t_new, (sq + 1) * t_new)
        gates = gate_ref[rows_sq, :]

        def q_rows(ref):
            return jnp.concatenate(
                [jnp.where(_div_pow2(lane, HEAD_DIM) == h, ref[rows_sq, g * LANES:(g + 1) * LANES], 0).astype(BF16)
                 for h in range(N_KV) for g in range(GQA)], axis=0)

        p = _softmax_rows(_dot_nt(q_rows(qc_ref), cmp_ref[sq, :, 0:LANES]), cmp_valid)
        o_cmp = _dot(p.astype(BF16), cmp_ref[sq, :, LANES:2 * LANES])
        p_sum = jnp.concatenate(
            [sum(p[(h * GQA + g) * t_new:(h * GQA + g + 1) * t_new] for g in range(GQA)) for h in range(N_KV)],
            axis=0)
        imp = _dot_split(p_sum, ov_ref[...])

        qr = q_rows(qr_ref)
        k_buf = win_ref[sq, :, 0:LANES].astype(BF16)
        v_buf = win_ref[sq, :, LANES:2 * LANES].astype(BF16)
        s_buf = jnp.where(win_valid, _dot_nt(qr, k_buf), NEG)
        k_new = kvw_ref[rows_sq, 0:LANES].astype(BF16).astype(F32)
        v_new = kvw_ref[rows_sq, LANES:2 * LANES].astype(BF16).astype(F32)
        qr32 = qr.astype(F32)
        s_new = [jnp.where(row_t1 >= i, jnp.sum(qr32 * k_new[i:i + 1, :], axis=-1, keepdims=True), NEG)
                 for i in range(t_new)]
        m = jnp.max(s_buf, axis=-1, keepdims=True)
        for s in s_new:
            m = jnp.maximum(m, s)
        e_buf = jnp.exp(s_buf - m)
        e_new = [jnp.exp(s - m) for s in s_new]
        denom = jnp.sum(e_buf, axis=-1, keepdims=True) + sum(e_new)
        o_win = _dot(e_buf.astype(BF16), v_buf)
        for i in range(t_new):
            o_win = o_win + e_new[i].astype(BF16).astype(F32) * v_new[i:i + 1, :]
        o_win = o_win / denom

        for g in range(GQA):
            parts = []
            for h in range(N_KV):
                rows = slice((h * GQA + g) * t_new, (h * GQA + g + 1) * t_new)
                base = h * GQA * 3 + g * 3
                parts.append(_gate_col(gates, base) * o_cmp[rows] + _gate_col(gates, base + 2) * o_win[rows])
            ocw_ref[rows_sq, g * LANES:(g + 1) * LANES] = jnp.where(lane < HEAD_DIM, parts[0], parts[1])
        return imp

    imp = jnp.concatenate([one_sequence(sq) for sq in range(seqs)], axis=0)
    blk = _lane_iota(imp.shape)
    tb = (past_len + t_new - 1) // SLC_BLOCK
    candidate = (blk >= 1) & (blk < tb - 1)
    score = jnp.where(candidate, imp, -1.0)
    out_lane = _lane_iota((imp.shape[0], LANES))
    picks = jnp.zeros((imp.shape[0], LANES), jnp.int32)
    for k in range(n_pick):
        best = jnp.max(score, axis=-1, keepdims=True)
        idx = jnp.min(jnp.where(score == best, blk, 1 << 20), axis=-1, keepdims=True)
        picks = jnp.where(out_lane == k, idx, picks)
        score = jnp.where(blk == idx, -2.0, score)
    sel_ref[...] = picks.reshape(sel_ref.shape)


def _cmpwin_sample_call(qc, qr, cmp_s, ov, win, kvw, gates, t_new, past_len, n_pick):
    n_seq = cmp_s.shape[0]
    seqs = SAMPLE_SEQS_PER_STEP
    assert n_seq % seqs == 0
    tile = lambda b: (b, 0)
    per_seq = lambda b: (b, 0, 0)
    return pl.pallas_call(
        functools.partial(_cmpwin_sample_kernel, past_len=past_len, n_pick=n_pick, t_new=t_new),
        grid=(n_seq // seqs,),
        in_specs=[
            pl.BlockSpec((seqs * t_new, D_B), tile),
            pl.BlockSpec((seqs * t_new, D_B), tile),
            pl.BlockSpec((seqs, cmp_s.shape[1], 2 * LANES), per_seq),
            pl.BlockSpec(ov.shape, lambda b: (0, 0)),
            pl.BlockSpec((seqs, win.shape[1], 2 * LANES), per_seq),
            pl.BlockSpec((seqs * t_new, 2 * LANES), tile),
            pl.BlockSpec((seqs * t_new, LANES), tile),
        ],
        out_specs=[
            pl.BlockSpec((seqs * t_new, D_B), tile),
            pl.BlockSpec((seqs, N_KV * t_new, LANES), per_seq),
        ],
        out_shape=[
            jax.ShapeDtypeStruct((n_seq * t_new, D_B), F32),
            jax.ShapeDtypeStruct((n_seq, N_KV * t_new, LANES), jnp.int32),
        ],
        compiler_params=pltpu.CompilerParams(dimension_semantics=("arbitrary",),
                                             vmem_limit_bytes=VMEM_LIMIT),
        name="cmpwin_sample",
    )(qc, qr, cmp_s, ov, win, kvw, gates)


def _sel_sample_kernel(sel_ref, pt_ref, cache_ref, qr_ref, rows_ref, gate_ref, ocw_ref, szb_ref,
                       out_ref, buf, sem, *, n_pick, n_pages, tb):
    t_new = qr_ref.shape[0]
    n_own = t_new * n_pick
    n_blocks = 2 + n_own
    n_steps = pl.num_programs(0)
    step = pl.program_id(0)
    blocks_per_page = PAGE_SIZE // SLC_BLOCK

    def block_copy(seq, slot, h, i, blk):
        phys = pt_ref[seq * n_pages + _div_pow2(blk, blocks_per_page)]
        return pltpu.make_async_copy(cache_ref.at[phys, pl.ds(2, 2), h], buf.at[slot, h * n_blocks + i],
                                     sem.at[slot])

    def for_all_blocks(seq, slot, fn):
        for h in range(N_KV):
            fn(block_copy(seq, slot, h, 0, 0))
            fn(block_copy(seq, slot, h, 1, tb - 1))

            def body(i, carry_, h=h):
                fn(block_copy(seq, slot, h, 2 + i, sel_ref[(seq * N_KV + h) * n_own + i]))
                return carry_
            lax.fori_loop(0, n_own, body, 0, unroll=DMA_LOOP_UNROLL)

    @pl.when(step == 0)
    def _():
        for_all_blocks(0, 0, lambda cp: cp.start())

    @pl.when(step + 1 < n_steps)
    def _():
        for_all_blocks(step + 1, (step + 1) % 2, lambda cp: cp.start())

    slot = step % 2
    for_all_blocks(step, slot, lambda cp: cp.wait())

    n_rows = GQA * t_new
    half = _div_pow2(_lane_iota((1, LANES)), SLC_BLOCK)
    row_t = _mod_pow2(_row_iota((n_rows, 1)), t_new)
    gates = gate_ref[...]

    def one_head(h):
        def head_half(x):
            return x[:, h * HEAD_DIM:(h + 1) * HEAD_DIM]

        qr = jnp.concatenate([head_half(qr_ref[:, g * LANES:(g + 1) * LANES]) for g in range(GQA)], axis=0)
        qrb = qr.astype(BF16)
        qr32 = qrb.astype(F32)

        def slabs(first, count, kind):
            return jnp.concatenate([buf[slot, h * n_blocks + first + k, kind].astype(BF16) for k in range(count)],
                                   axis=1)

        bias_sh = jnp.concatenate(
            [jnp.where(half == blk % blocks_per_page, 0.0, NEG) for blk in (0, tb - 1)], axis=1)
        s_sh = _dot(qrb, slabs(0, 2, 0)) + bias_sh
        k_new = head_half(rows_ref[:, 2 * LANES:3 * LANES]).astype(BF16).astype(F32)
        v_new = head_half(rows_ref[:, 3 * LANES:4 * LANES]).astype(BF16).astype(F32)
        s_new = [jnp.where(row_t >= i, jnp.sum(qr32 * k_new[i:i + 1, :], axis=-1, keepdims=True), NEG)
                 for i in range(t_new)]
        own_keys = n_pick * PAGE_SIZE
        s_own = jnp.zeros((n_rows, own_keys), F32)
        for t in range(t_new):
            bias_t = jnp.concatenate(
                [jnp.where(half == _mod_pow2(sel_ref[(step * N_KV + h) * n_own + t * n_pick + k], blocks_per_page),
                           0.0, NEG) for k in range(n_pick)], axis=1)
            s_t = _dot(qrb, slabs(2 + t * n_pick, n_pick, 0)) + bias_t
            s_own = jnp.where(row_t == t, s_t, s_own)

        m = jnp.maximum(jnp.max(s_sh, axis=-1, keepdims=True), jnp.max(s_own, axis=-1, keepdims=True))
        for s in s_new:
            m = jnp.maximum(m, s)
        e_sh = jnp.exp(s_sh - m)
        e_own = jnp.exp(s_own - m)
        e_new = [jnp.exp(s - m) for s in s_new]
        denom = jnp.sum(e_sh, axis=-1, keepdims=True) + jnp.sum(e_own, axis=-1, keepdims=True) + sum(e_new)
        o = _dot_nt(e_sh.astype(BF16), slabs(0, 2, 1))
        for t in range(t_new):
            o = o + _dot_nt(jnp.where(row_t == t, e_own, 0.0).astype(BF16), slabs(2 + t * n_pick, n_pick, 1))
        for i in range(t_new):
            o = o + e_new[i].astype(BF16).astype(F32) * v_new[i:i + 1, :]
        o = o / denom
        return [_gate_col(gates, h * GQA * 3 + g * 3 + 1) * o[g * t_new:(g + 1) * t_new] for g in range(GQA)]

    gated = [one_head(h) for h in range(N_KV)]
    for g in range(GQA):
        sl = slice(g * LANES, (g + 1) * LANES)
        o_slc = jnp.concatenate([gated[h][g] for h in range(N_KV)], axis=1)
        out_ref[:, sl] = (ocw_ref[:, sl] + o_slc) * szb_ref[:, sl]


def _sel_sample_call(sel_flat, pt_flat, cache_t, qr, rows, gates, ocw, szb, n_seq, t_new, n_pick, n_pages, tb):
    tile = lambda b, sel, pt: (b, 0)
    n_blocks = N_KV * (2 + t_new * n_pick)
    grid_spec = pltpu.PrefetchScalarGridSpec(
        num_scalar_prefetch=2,
        grid=(n_seq,),
        in_specs=[
            pl.BlockSpec(memory_space=pl.ANY),
            pl.BlockSpec((t_new, D_B), tile),
            pl.BlockSpec((t_new, 4 * LANES), tile),
            pl.BlockSpec((t_new, LANES), tile),
            pl.BlockSpec((t_new, D_B), tile),
            pl.BlockSpec((t_new, D_B), tile),
        ],
        out_specs=pl.BlockSpec((t_new, D_B), tile),
        scratch_shapes=[
            pltpu.VMEM((2, n_blocks, 2, HEAD_DIM, PAGE_SIZE), F32),
            pltpu.SemaphoreType.DMA((2,)),
        ],
    )
    return pl.pallas_call(
        functools.partial(_sel_sample_kernel, n_pick=n_pick, n_pages=n_pages, tb=tb),
        grid_spec=grid_spec,
        out_shape=jax.ShapeDtypeStruct((n_seq * t_new, D_B), F32),
        compiler_params=pltpu.CompilerParams(dimension_semantics=("arbitrary",),
                                             vmem_limit_bytes=VMEM_LIMIT),
        name="sel_sample",
    )(sel_flat, pt_flat, cache_t, qr, rows, gates, ocw, szb)


def _merge_kernel(x_ref, a_ref, b_ref, wa_ref, wb_ref, fg_ref, y_ref):
    delta = _dot(a_ref[...].astype(BF16), wa_ref[...]) + _dot(b_ref[...].astype(BF16), wb_ref[...])
    x = x_ref[...] + delta
    ms = jnp.mean(x * x, axis=-1, keepdims=True)
    y_ref[...] = (x * lax.rsqrt(ms + EPS)) * fg_ref[...]


def _merge_call(x2d, a, b, wa, wb, fg):
    n_rows = x2d.shape[0]
    tm = min(TM_MERGE, n_rows)
    row = lambda i: (i, 0)
    const2 = lambda i: (0, 0)
    return pl.pallas_call(
        _merge_kernel,
        grid=(n_rows // tm,),
        in_specs=[
            pl.BlockSpec((tm, D_MODEL), row),
            pl.BlockSpec((tm, D_A), row),
            pl.BlockSpec((tm, D_B), row),
            pl.BlockSpec((D_A, D_MODEL), const2),
            pl.BlockSpec((D_B, D_MODEL), const2),
            pl.BlockSpec((1, D_MODEL), const2),
        ],
        out_specs=pl.BlockSpec((tm, D_MODEL), row),
        out_shape=jax.ShapeDtypeStruct((n_rows, D_MODEL), F32),
        compiler_params=pltpu.CompilerParams(dimension_semantics=("arbitrary",),
                                             vmem_limit_bytes=VMEM_LIMIT),
        name="merge",
    )(x2d, a, b, wa, wb, fg)


def _head_pair_perm(w_cols):
    lead = w_cols.shape[:-1]
    return w_cols.reshape(*lead, N_KV, GQA, HEAD_DIM).swapaxes(-3, -2).reshape(*lead, D_B)


def _prep_w_in(w_in):
    cuts = np.cumsum([D_A, D_A, D_A, D_B, 6 * N_KV * HEAD_DIM, 3 * N_HEADS]).tolist()
    u, v, za, q, kv, g, zb = jnp.split(w_in, cuts, axis=-1)
    g_pad = jnp.pad(g, ((0, 0), (0, LANES - g.shape[1])))
    return jnp.concatenate([u, v, za, _head_pair_perm(q), kv, _head_pair_perm(zb), g_pad], axis=-1).astype(BF16)


def _prep_mix(w_s, b_s, chunk_len):
    reps = TM // chunk_len
    row = np.arange(TM)[:, None]
    col = np.arange(TM)[None, :]
    causal_same_chunk = (row // chunk_len == col // chunk_len) & (col <= row)
    spread = jnp.asarray(np.arange(chunk_len)[:, None] == np.arange(TM)[None, :] % chunk_len, dtype=w_s.dtype)
    wide = jnp.einsum("gts,sc->gtc", w_s[:, :chunk_len, :chunk_len], spread)
    wmix = jnp.where(causal_same_chunk, jnp.tile(wide, (1, reps, 1)), 0).astype(BF16)
    bias = jnp.repeat(b_s[:, :chunk_len].T, HEAD_DIM, axis=1)
    return wmix, jnp.tile(bias, (reps, 1))


def _rope_tables(pos):
    half = HEAD_DIM // 2
    inv = ROPE_THETA ** (-jnp.arange(half, dtype=F32) / half)
    ang = pos.astype(F32)[:, None] * inv
    cos = jnp.tile(jnp.cos(ang), (1, LANES // half))
    sin = jnp.tile(jnp.concatenate([-jnp.sin(ang), jnp.sin(ang)], axis=1), (1, LANES // HEAD_DIM))
    return cos, sin


def _prep_compress(cmp_pos, w_cmp1, b_cmp1, w_cmp2):
    ratio = CMP_BLOCK // CMP_STRIDE
    w1 = w_cmp1.reshape(2, ratio, CMP_STRIDE, HEAD_DIM, CMP_HIDDEN).transpose(0, 2, 3, 1, 4)
    w1 = w1.reshape(2, CMP_STRIDE, HEAD_DIM, ratio * CMP_HIDDEN)
    zeros = jnp.zeros_like(w1)
    wc = jnp.stack([jnp.concatenate([w1, zeros], axis=-1), jnp.concatenate([zeros, w1], axis=-1)], axis=2)
    wc = wc.reshape(2, CMP_STRIDE * LANES, N_KV * ratio * CMP_HIDDEN).astype(BF16)
    pe = cmp_pos.reshape(2, ratio, CMP_STRIDE, 1, HEAD_DIM)
    pe = jnp.broadcast_to(pe, (2, ratio, CMP_STRIDE, N_KV, HEAD_DIM)).reshape(2, ratio, CMP_STRIDE * LANES)
    pe = jnp.pad(pe, ((0, 0), (0, 16 - ratio), (0, 0))).astype(BF16)
    w2 = w_cmp2[:, None]
    zeros2 = jnp.zeros_like(w2)
    w2p = jnp.concatenate([jnp.concatenate([w2, zeros2], axis=-1), jnp.concatenate([zeros2, w2], axis=-1)],
                          axis=1).astype(BF16)
    return wc, pe, b_cmp1.reshape(2, 1, CMP_HIDDEN), w2p


def _overlap_matrix(n_cmp_slots, n_slc, n_lanes):
    start = (np.arange(n_cmp_slots)[:, None] - 1) * CMP_STRIDE
    j = np.arange(n_lanes)[None, :]
    ov = (start <= j * SLC_BLOCK + SLC_BLOCK - 1) & (start + CMP_BLOCK - 1 >= j * SLC_BLOCK)
    ov &= (np.arange(n_cmp_slots)[:, None] >= 1) & (j < n_slc)
    return jnp.asarray(ov, dtype=BF16)


def kernel(x_prompt, x_sample, cache_kv, state_win, page_table, norm_g, w_in, ln_g, ln_b, w_s, b_s,
           cmp_pos, w_cmp1, b_cmp1, w_cmp2, w_out, final_g):
    n_batch, seq, _ = x_prompt.shape
    n_seq, t_new, _ = x_sample.shape
    depth, n_phys = cache_kv.shape[:2]
    n_pages = page_table.shape[1]
    past_len = n_pages * PAGE_SIZE
    win_buf = state_win.shape[2]
    assert depth == 1 and seq % TQ == 0 and n_seq * t_new == TM and win_buf == WINDOW
    assert (past_len + t_new - 1) // SLC_BLOCK == past_len // SLC_BLOCK and past_len % CMP_STRIDE == 0

    w_all = _prep_w_in(w_in[0])
    ng = norm_g[0].reshape(1, D_MODEL)
    lng = ln_g[0].reshape(1, D_A)
    lnb = ln_b[0].reshape(1, D_A)
    fg = final_g.reshape(1, D_MODEL)
    wo = w_out[0]
    wo_a = wo[:D_A].astype(BF16)
    wo_b = _head_pair_perm(wo[D_A:].T).T.astype(BF16)
    wc, pe, b1, w2p = _prep_compress(cmp_pos[0], w_cmp1[0], b_cmp1[0], w_cmp2[0])

    xp = x_prompt.reshape(n_batch * seq, D_MODEL)
    wmix_p, bmix_p = _prep_mix(w_s[0], b_s[0], CHUNK)
    cos_p, sin_p = _rope_tables(jnp.arange(seq))
    ag_p, qc_p, qr_p, rows_p, kvw_pt, ksel_p, kwin_p, gate_p, szb_p, rows_pt = _proj_call(
        xp, ng, w_all, lng, lnb, wmix_p, bmix_p, cos_p, sin_p, BF16, seq)
    pages_p = seq // PAGE_SIZE
    ident = jnp.arange(n_batch * pages_p, dtype=jnp.int32).reshape(n_batch, pages_p)
    cmp_p = _compress_call(rows_p.reshape(n_batch * pages_p, PAGE_SIZE, 4 * LANES), ident, wc, pe, b1, w2p,
                           pages_p, False)
    ov_p = _overlap_matrix(seq // CMP_STRIDE, seq // SLC_BLOCK, LANES)
    bg_p = _attn_prompt_call(qc_p, qr_p, cmp_p, ov_p, ksel_p, kwin_p, gate_p, szb_p, n_batch, seq)
    y_p = _merge_call(xp, ag_p, bg_p, wo_a, wo_b, fg)

    xs = x_sample.reshape(n_seq * t_new, D_MODEL)
    wmix_s, bmix_s = _prep_mix(w_s[0], b_s[0], t_new)
    cos_s, sin_s = _rope_tables(jnp.tile(past_len + jnp.arange(t_new), n_seq))
    ag_s, qc_s, qr_s, rows_s, kvw_s, _, _, gate_s, szb_s, vln_s = _proj_call(
        xs, ng, w_all, lng, lnb, wmix_s, bmix_s, cos_s, sin_s, F32, None)
    cache_t = cache_kv[0].transpose(0, 2, 3, 4, 1)
    cmp_s = _compress_call(cache_t, page_table, wc, pe, b1, w2p, SAMPLE_PAGES_PER_STEP, True)
    n_slc_s = -(-(past_len + t_new) // SLC_BLOCK)
    tb = past_len // SLC_BLOCK
    n_pick = TOP_N - 3
    ov_s = _overlap_matrix(past_len // CMP_STRIDE, n_slc_s, -(-n_slc_s // LANES) * LANES)
    win_prev = state_win[0].reshape(n_seq, win_buf, 2 * LANES)
    ocw_s, sel_s = _cmpwin_sample_call(qc_s, qr_s, cmp_s, ov_s, win_prev, kvw_s, gate_s, t_new, past_len, n_pick)
    sel_flat = sel_s[:, :, :n_pick].reshape(-1)
    bg_s = _sel_sample_call(sel_flat, page_table.reshape(-1), cache_t, qr_s, rows_s, gate_s, ocw_s, szb_s,
                            n_seq, t_new, n_pick, n_pages, tb)
    y_s = _merge_call(xs, ag_s, bg_s, wo_a, wo_b, fg)

    new_kv_p = rows_pt.reshape(n_batch, 4, N_KV, HEAD_DIM, seq).transpose(0, 4, 1, 2, 3)
    new_win_p = kvw_pt[:, :, seq - win_buf:].reshape(n_batch, 2, N_KV, HEAD_DIM, win_buf).transpose(0, 4, 1, 2, 3)
    new_win_s = jnp.concatenate([state_win[0][:, t_new:], kvw_s.reshape(n_seq, t_new, 2, N_KV, HEAD_DIM)], axis=1)
    return (y_p.reshape(n_batch, seq, D_MODEL),
            y_s.reshape(n_seq, t_new, D_MODEL),
            new_kv_p[None],
            new_win_p[None],
            rows_s.reshape(1, n_seq, t_new, 4, N_KV, HEAD_DIM),
            new_win_s[None],
            vln_s.reshape(1, n_seq, t_new, D_A))
```

```python
import functools
import math

import jax
import jax.numpy as jnp
import numpy as np
from jax import lax
from jax.experimental import pallas as pl
from jax.experimental.pallas import tpu as pltpu

F32 = jnp.float32
BF16 = jnp.bfloat16

D_MODEL = 1024
HEAD_DIM = 64
D_A = 512
A_GROUPS = 8
CHUNK = 128
N_HEADS = 8
D_B = 512
N_KV = 2
GQA = 4
CMP_BLOCK = 32
CMP_STRIDE = 16
CMP_HIDDEN = 128
SLC_BLOCK = 64
TOP_N = 16
WINDOW = 512
ROPE_THETA = 10000.0
EPS = 1e-6
SCALE = HEAD_DIM ** -0.5
NEG = -1e30
FORCE = 1e9
PAGE_SIZE = 128

LANES = 128
VMEM_LIMIT = 56 * 1024 * 1024

C_U, C_V, C_ZA, C_Q, C_KV, C_ZB, C_G, C_END = 0, 512, 1024, 1536, 2048, 2816, 3328, 3456

TM = 256
TM_MERGE = 512
DMA_LOOP_UNROLL = 8
SAMPLE_SEQS_PER_STEP = 8
SAMPLE_PAGES_PER_STEP = 64
TQ = 256
SEGS_PER_PAGE = PAGE_SIZE // CMP_STRIDE


def _lane_iota(shape):
    return lax.broadcasted_iota(jnp.int32, shape, len(shape) - 1)


def _row_iota(shape):
    return lax.broadcasted_iota(jnp.int32, shape, len(shape) - 2)


def _div_pow2(x, n):
    assert n & (n - 1) == 0
    return lax.shift_right_logical(x, int(math.log2(n))) if n > 1 else x


def _mod_pow2(x, n):
    assert n & (n - 1) == 0
    return x & (n - 1)


def _dot(a, b):
    return jnp.dot(a, b, preferred_element_type=F32)


def _dot_nt(a, b):
    return lax.dot_general(a, b, (((1,), (1,)), ((), ())), preferred_element_type=F32)


def _rope(x, cos, sin_signed):
    lo = _mod_pow2(_lane_iota(x.shape), HEAD_DIM) < (HEAD_DIM // 2)
    swapped = jnp.where(lo, pltpu.roll(x, LANES - HEAD_DIM // 2, 1), pltpu.roll(x, HEAD_DIM // 2, 1))
    return x * cos + swapped * sin_signed


def _proj_kernel(x_ref, ng_ref, w_ref, lng_ref, lnb_ref, wmix_ref, bmix_ref, cos_ref, sin_ref,
                 ag_ref, qc_ref, qr_ref, rows_ref, kvw_ref, ksel_ref, kwin_ref, gate_ref, szb_ref,
                 extra_ref, *, prompt):
    x = x_ref[...]
    ms = jnp.mean(x * x, axis=-1, keepdims=True)
    hb = ((x * lax.rsqrt(ms + EPS)) * ng_ref[...]).astype(BF16)
    cos = cos_ref[...]
    sin = sin_ref[...]
    pair = 2 * LANES

    def proj(c0, n):
        return _dot(hb, w_ref[:, c0:c0 + n])

    def halves(x2):
        return x2[:, 0:LANES], x2[:, LANES:pair]

    v = jax.nn.gelu(proj(C_V, D_A))
    mu = jnp.mean(v, axis=-1, keepdims=True)
    vc = v - mu
    var = jnp.mean(vc * vc, axis=-1, keepdims=True)
    vln = vc * lax.rsqrt(var + EPS) * lng_ref[...] + lnb_ref[...]
    if not prompt:
        extra_ref[...] = vln
    vb = vln.astype(BF16)
    first_head = _lane_iota((x.shape[0], LANES)) < HEAD_DIM
    for gpp in range(A_GROUPS // 4):
        u2 = halves(jax.nn.gelu(proj(C_U + gpp * pair, pair)))
        za2 = halves(proj(C_ZA + gpp * pair, pair))
        for k in range(2):
            gp = 2 * gpp + k
            sl = slice(gp * LANES, (gp + 1) * LANES)
            blk = vb[:, sl]
            mixed = jnp.where(first_head, _dot(wmix_ref[2 * gp], blk), _dot(wmix_ref[2 * gp + 1], blk))
            mixed = mixed + bmix_ref[:, sl]
            ag_ref[:, sl] = (u2[k] * mixed * jax.nn.silu(za2[k])).astype(ag_ref.dtype)

    for gpp in range(GQA // 2):
        q2 = halves(proj(C_Q + gpp * pair, pair))
        for k in range(2):
            sl = slice((2 * gpp + k) * LANES, (2 * gpp + k + 1) * LANES)
            qc_ref[:, sl] = (q2[k] * SCALE).astype(qc_ref.dtype)
            qr_ref[:, sl] = (_rope(q2[k], cos, sin) * SCALE).astype(qr_ref.dtype)

    kc, vcm = halves(proj(C_KV, pair))
    ks, vs = halves(proj(C_KV + pair, pair))
    kw, vw = halves(proj(C_KV + 2 * pair, pair))
    ks = _rope(ks, cos, sin)
    kw = _rope(kw, cos, sin)
    for i, blk in enumerate((kc, vcm, ks, vs)):
        rows_ref[:, i * LANES:(i + 1) * LANES] = blk
        if prompt:
            extra_ref[0, i * LANES:(i + 1) * LANES, :] = blk.T
    for i, blk in enumerate((kw, vw)):
        if prompt:
            kvw_ref[0, i * LANES:(i + 1) * LANES, :] = blk.T
        else:
            kvw_ref[:, i * LANES:(i + 1) * LANES] = blk
    ksel_ref[:, 0:LANES] = ks.astype(BF16)
    ksel_ref[:, LANES:2 * LANES] = vs.astype(BF16)
    kwin_ref[:, 0:LANES] = kw.astype(BF16)
    kwin_ref[:, LANES:2 * LANES] = vw.astype(BF16)

    gate_ref[...] = jax.nn.sigmoid(proj(C_G, LANES))
    for gpp in range(GQA // 2):
        zb2 = halves(proj(C_ZB + gpp * pair, pair))
        for k in range(2):
            sl = slice((2 * gpp + k) * LANES, (2 * gpp + k + 1) * LANES)
            szb_ref[:, sl] = jax.nn.silu(zb2[k]).astype(szb_ref.dtype)


def _proj_call(x2d, ng, w, lng, lnb, wmix, bmix, cos_t, sin_t, act_dtype, seq_len):
    n_rows = x2d.shape[0]
    n_tiles = n_rows // TM
    pos_tiles = cos_t.shape[0] // TM
    row = lambda i: (i, 0)
    const2 = lambda i: (0, 0)
    out_shapes = [
        jax.ShapeDtypeStruct((n_rows, D_A), act_dtype),
        jax.ShapeDtypeStruct((n_rows, D_B), act_dtype),
        jax.ShapeDtypeStruct((n_rows, D_B), act_dtype),
        jax.ShapeDtypeStruct((n_rows, 4 * LANES), F32),
        jax.ShapeDtypeStruct((n_rows, 2 * LANES), F32),
        jax.ShapeDtypeStruct((n_rows, 2 * LANES), BF16),
        jax.ShapeDtypeStruct((n_rows, 2 * LANES), BF16),
        jax.ShapeDtypeStruct((n_rows, LANES), F32),
        jax.ShapeDtypeStruct((n_rows, D_B), act_dtype),
        jax.ShapeDtypeStruct((n_rows, D_A), F32),
    ]
    out_specs = [pl.BlockSpec((TM, s.shape[1]), row) for s in out_shapes]
    if seq_len is not None:
        tiles_per_seq = seq_len // TM
        dim_major = lambda i: (i // tiles_per_seq, 0, i % tiles_per_seq)
        for idx, width in ((4, 2 * LANES), (9, 4 * LANES)):
            out_shapes[idx] = jax.ShapeDtypeStruct((n_rows // seq_len, width, seq_len), F32)
            out_specs[idx] = pl.BlockSpec((1, width, TM), dim_major)
    return pl.pallas_call(
        functools.partial(_proj_kernel, prompt=seq_len is not None),
        grid=(n_tiles,),
        in_specs=[
            pl.BlockSpec((TM, D_MODEL), row),
            pl.BlockSpec((1, D_MODEL), const2),
            pl.BlockSpec((D_MODEL, C_END), const2),
            pl.BlockSpec((1, D_A), const2),
            pl.BlockSpec((1, D_A), const2),
            pl.BlockSpec((A_GROUPS, TM, TM), lambda i: (0, 0, 0)),
            pl.BlockSpec((TM, D_A), const2),
            pl.BlockSpec((TM, LANES), lambda i: (i % pos_tiles, 0)),
            pl.BlockSpec((TM, LANES), lambda i: (i % pos_tiles, 0)),
        ],
        out_specs=out_specs,
        out_shape=out_shapes,
        compiler_params=pltpu.CompilerParams(dimension_semantics=("arbitrary",),
                                             vmem_limit_bytes=VMEM_LIMIT),
        name="proj",
    )(x2d, ng, w, lng, lnb, wmix, bmix, cos_t, sin_t)


def _compress_kernel(pt_ref, src_ref, wc_ref, pe_ref, b1_ref, w2_ref, out_ref, buf, sem, carry,
                     *maybe_stage, pages_per_step):
    transposed_src = bool(maybe_stage)
    n_steps = pl.num_programs(0) * pl.num_programs(1)
    step = pl.program_id(0) * pl.num_programs(1) + pl.program_id(1)
    m = pages_per_step * SEGS_PER_PAGE

    def page_copies(step_idx, slot, i):
        phys = pt_ref[step_idx * pages_per_step + i]
        if transposed_src:
            return [pltpu.make_async_copy(src_ref.at[phys, pl.ds(0, 2)], maybe_stage[0].at[slot, i], sem.at[slot])]
        return [pltpu.make_async_copy(src_ref.at[phys, :, pl.ds(c * LANES, LANES)],
                                      buf.at[slot, c, pl.ds(i * PAGE_SIZE, PAGE_SIZE), :],
                                      sem.at[slot]) for c in range(2)]

    def start_step(step_idx, slot):
        def body(i, carry_):
            for cp in page_copies(step_idx, slot, i):
                cp.start()
            return carry_
        lax.fori_loop(0, pages_per_step, body, 0, unroll=DMA_LOOP_UNROLL)

    def wait_step(step_idx, slot):
        def body(i, carry_):
            for cp in page_copies(step_idx, slot, i):
                cp.wait()
            return carry_
        lax.fori_loop(0, pages_per_step, body, 0, unroll=DMA_LOOP_UNROLL)

    slot = step % 2

    first_of_seq = pl.program_id(1) == 0
    first_row = _row_iota((m, LANES)) == 0

    def compress_rows(rows_slot, seg_pitch):
        for c in range(2):
            pieces = [buf[rows_slot, c, pl.ds(j, m, stride=seg_pitch), :].astype(BF16)
                      for j in range(CMP_STRIDE)]
            part = _dot(jnp.concatenate(pieces, axis=1), wc_ref[c])
            pe_part = _dot(pe_ref[c], wc_ref[c])
            acc_out = None
            for h in range(N_KV):
                lo = slice(h * 2 * CMP_HIDDEN, h * 2 * CMP_HIDDEN + CMP_HIDDEN)
                hi = slice(h * 2 * CMP_HIDDEN + CMP_HIDDEN, (h + 1) * 2 * CMP_HIDDEN)
                bias = pe_part[0:1, lo] + pe_part[1:2, hi] + b1_ref[c]
                part0 = part[:, lo]
                prev_last = jnp.where(first_of_seq, 0.0, carry[c, h, 0:1, :])
                prev0 = jnp.where(first_row, prev_last, pltpu.roll(part0, 1, 0))
                carry[c, h, 0:1, :] = part0[m - 1:m, :]
                hid = jax.nn.gelu(prev0 + part[:, hi] + bias).astype(BF16)
                contrib = _dot(hid, w2_ref[c, h])
                acc_out = contrib if acc_out is None else acc_out + contrib
            out_ref[0, :, c * LANES:(c + 1) * LANES] = acc_out.astype(out_ref.dtype)

    if transposed_src:
        stage = maybe_stage[0]
        seg_pitch = CMP_STRIDE + 1

        def transpose_step(stage_slot):
            for i in range(pages_per_step):
                for c in range(2):
                    rows = stage[stage_slot, i, c].reshape(N_KV * HEAD_DIM, PAGE_SIZE).T
                    for s in range(SEGS_PER_PAGE):
                        r0 = (i * SEGS_PER_PAGE + s) * seg_pitch
                        buf[stage_slot, c, r0:r0 + CMP_STRIDE, :] = rows[s * CMP_STRIDE:(s + 1) * CMP_STRIDE]

        @pl.when(step == 0)
        def _():
            start_step(0, 0)
            start_step(1, 1)
            wait_step(0, 0)
            transpose_step(0)

        @pl.when(step + 1 < n_steps)
        def _():
            wait_step(step + 1, (step + 1) % 2)

        @pl.when(step + 2 < n_steps)
        def _():
            start_step(step + 2, slot)

        for parity in range(2):
            @pl.when(slot == parity)
            def _():
                transpose_step(1 - parity)
                compress_rows(parity, seg_pitch)
    else:
        @pl.when(step == 0)
        def _():
            start_step(0, 0)

        @pl.when(step + 1 < n_steps)
        def _():
            start_step(step + 1, (step + 1) % 2)

        wait_step(step, slot)
        compress_rows(slot, CMP_STRIDE)


def _compress_call(src, page_table, wc, pe, b1, w2p, pages_per_step, transposed_src):
    n_seq, n_pages = page_table.shape
    n_chunks = n_pages // pages_per_step
    m = pages_per_step * SEGS_PER_PAGE
    assert not transposed_src or n_seq * n_chunks >= 2
    seg_pitch = CMP_STRIDE + 1 if transposed_src else CMP_STRIDE
    scratch = [
        pltpu.VMEM((2, 2, m * seg_pitch, LANES), F32),
        pltpu.SemaphoreType.DMA((2,)),
        pltpu.VMEM((2, N_KV, 8, CMP_HIDDEN), F32),
    ]
    if transposed_src:
        scratch.append(pltpu.VMEM((2, pages_per_step, 2, N_KV, HEAD_DIM, PAGE_SIZE), F32))
    grid_spec = pltpu.PrefetchScalarGridSpec(
        num_scalar_prefetch=1,
        grid=(n_seq, n_chunks),
        in_specs=[
            pl.BlockSpec(memory_space=pl.ANY),
            pl.BlockSpec((2, CMP_STRIDE * LANES, 4 * CMP_HIDDEN), lambda b, k, pt: (0, 0, 0)),
            pl.BlockSpec((2, 16, CMP_STRIDE * LANES), lambda b, k, pt: (0, 0, 0)),
            pl.BlockSpec((2, 1, CMP_HIDDEN), lambda b, k, pt: (0, 0, 0)),
            pl.BlockSpec((2, N_KV, CMP_HIDDEN, LANES), lambda b, k, pt: (0, 0, 0, 0)),
        ],
        out_specs=pl.BlockSpec((1, m, 2 * LANES), lambda b, k, pt: (b, k, 0)),
        scratch_shapes=scratch,
    )
    return pl.pallas_call(
        functools.partial(_compress_kernel, pages_per_step=pages_per_step),
        grid_spec=grid_spec,
        out_shape=jax.ShapeDtypeStruct((n_seq, n_pages * SEGS_PER_PAGE, 2 * LANES), BF16),
        compiler_params=pltpu.CompilerParams(dimension_semantics=("arbitrary", "arbitrary"),
                                             vmem_limit_bytes=VMEM_LIMIT),
        name="compress",
    )(page_table.reshape(-1), src, wc, pe, b1, w2p)


def _softmax_rows(s, mask):
    s = jnp.where(mask, s, NEG)
    e = jnp.where(mask, jnp.exp(s - jnp.max(s, axis=-1, keepdims=True)), 0.0)
    return e / jnp.maximum(jnp.sum(e, axis=-1, keepdims=True), 1.0)


def _dot_split(p, w):
    hi = p.astype(BF16)
    lo = (p - hi.astype(F32)).astype(BF16)
    return _dot(hi, w) + _dot(lo, w)


def _gate_col(gates, idx):
    return gates[:, idx:idx + 1]


def _attn_prompt_kernel(qc_ref, qr_ref, cmp_ref, ov_ref, eg_ref, ksel_ref, kwin_ref, gate_ref, szb_ref,
                        out_ref, q_sc, m_sc, acc_sc, cmp_sc):
    qi = pl.program_id(1)
    lane = _lane_iota((TQ, LANES))
    lo_half = lane < HEAD_DIM
    r_minus_c = _row_iota((TQ, TQ)) - _lane_iota((TQ, TQ))
    causal_bias = jnp.where(r_minus_c >= 0, 0.0, NEG)
    far_bias = jnp.where(r_minus_c < 0, 0.0, NEG)
    pair_bias = jnp.concatenate([jnp.zeros((TQ, TQ), F32), causal_bias], axis=1)
    kcmp = cmp_ref[0, :, 0:LANES]
    vcmp = cmp_ref[0, :, LANES:2 * LANES]
    n_slc = ksel_ref.shape[0] // SLC_BLOCK
    blocks_per_tile = TQ // SLC_BLOCK

    def flash_tile(br, h, k_tile, v_aug, bias):
        width = k_tile.shape[0]
        s = _dot_nt(q_sc[h], k_tile)
        if bias is not None:
            s = (s.reshape(GQA, TQ, width) + bias[None]).reshape(GQA * TQ, width)
        m_prev = m_sc[h]
        m_new = jnp.maximum(m_prev, jnp.max(s, axis=-1, keepdims=True))
        alpha = jnp.exp(m_prev - m_new)
        p = jnp.exp(s - jnp.concatenate([m_new] * (width // LANES), axis=1))
        acc_sc[br, h] = alpha * acc_sc[br, h] + _dot(p.astype(BF16), v_aug)
        m_sc[h] = m_new

    def flash_init(br):
        m_sc[...] = jnp.full(m_sc.shape, NEG, F32)
        acc_sc[br] = jnp.zeros(acc_sc.shape[1:], F32)

    n_hg = N_KV * GQA
    q_cmp = jnp.concatenate([jnp.where(_div_pow2(lane, HEAD_DIM) == h, qc_ref[:, g * LANES:(g + 1) * LANES], 0)
                             for h in range(N_KV) for g in range(GQA)], axis=0)
    lane_all = _lane_iota((n_hg * TQ, LANES))
    tq_all = qi * TQ + _mod_pow2(_row_iota((n_hg * TQ, LANES)), TQ)
    cmp_valid = (lane_all >= 1) & (lane_all * CMP_STRIDE + (CMP_STRIDE - 1) <= tq_all)
    p_all = _softmax_rows(_dot_nt(q_cmp, kcmp), cmp_valid)
    o_all = _dot(p_all.astype(BF16), vcmp)
    for g in range(GQA):
        cmp_sc[:, g * LANES:(g + 1) * LANES] = jnp.where(
            lo_half, o_all[g * TQ:(g + 1) * TQ], o_all[(GQA + g) * TQ:(GQA + g + 1) * TQ])
    p4 = p_all.reshape(N_KV, GQA, TQ, LANES)
    p_sum = sum(p4[:, g] for g in range(GQA)).reshape(N_KV * TQ, LANES)
    imp = _dot_split(p_sum, ov_ref[...])
    lane_ht = _lane_iota((N_KV * TQ, LANES))
    tq_ht = qi * TQ + _mod_pow2(_row_iota((N_KV * TQ, LANES)), TQ)
    tb = _div_pow2(tq_ht, SLC_BLOCK)
    forced = (lane_ht == 0) | (lane_ht == tb) | (lane_ht == tb - 1)
    valid = lane_ht * SLC_BLOCK <= tq_ht
    score = jnp.where(forced, FORCE, jnp.where(valid, imp, NEG))
    score_t = score.T[0:n_slc, :]
    blk = _row_iota((n_slc, N_KV * TQ))
    rank = jnp.zeros((n_slc, N_KV * TQ), jnp.int32)
    for j in range(n_slc):
        other = score_t[j:j + 1, :]
        ahead = (other > score_t) | ((other == score_t) & (blk > j))
        rank = rank + ahead.astype(jnp.int32)
    sel_bias_t = jnp.where(rank < TOP_N, 0.0, NEG)

    for h in range(N_KV):
        keep = _div_pow2(lane, HEAD_DIM) == h
        other_off = HEAD_DIM * (1 - h)
        pieces = [jnp.zeros((other_off, TQ), F32)] if other_off else []
        pieces += [sel_bias_t[:, h * TQ:(h + 1) * TQ], jnp.zeros((LANES - other_off - n_slc, TQ), F32)]
        sel_bias = jnp.concatenate(pieces, axis=0).T.astype(BF16)
        for g in range(GQA):
            sl = slice(g * LANES, (g + 1) * LANES)
            q_sc[h, g * TQ:(g + 1) * TQ, :] = jnp.where(keep, qr_ref[:, sl], sel_bias)

    def span_operands(kv_ref, kt, n_tiles, h, other_lanes):
        k = kv_ref[pl.ds(kt * TQ, n_tiles * TQ), 0:LANES]
        v = kv_ref[pl.ds(kt * TQ, n_tiles * TQ), LANES:2 * LANES]
        own = _div_pow2(_lane_iota(k.shape), HEAD_DIM) == h
        return jnp.where(own, k, other_lanes(k.shape)), jnp.where(own, v, 1.0)

    def sel_span(kt, n_tiles, bias):
        for h in range(N_KV):
            def indicator(shape, h=h):
                first = HEAD_DIM * (1 - h) + kt * blocks_per_tile
                hit = _lane_iota(shape) == first + _div_pow2(_row_iota(shape), SLC_BLOCK)
                return jnp.where(hit, 1, 0).astype(BF16)
            flash_tile(0, h, *span_operands(ksel_ref, kt, n_tiles, h, indicator), bias)

    flash_init(0)

    def sel_pair(i, carry_):
        sel_span(2 * i, 2, None)
        return carry_
    lax.fori_loop(0, _div_pow2(jnp.maximum(qi - 1, 0), 2), sel_pair, 0)

    @pl.when((qi >= 2) & (_mod_pow2(qi, 2) == 0))
    def _():
        sel_span(qi - 2, 1, None)

    @pl.when(qi >= 1)
    def _():
        sel_span(qi - 1, 2, pair_bias)

    @pl.when(qi == 0)
    def _():
        sel_span(qi, 1, causal_bias)

    def win_span(kt, n_tiles, bias):
        for h in range(N_KV):
            flash_tile(1, h, *span_operands(kwin_ref, kt, n_tiles, h, lambda shape: jnp.zeros(shape, BF16)), bias)

    flash_init(1)
    far = WINDOW // TQ
    assert far == 2

    @pl.when(qi >= far)
    def _():
        win_span(qi - far, 1, far_bias)

    @pl.when(qi >= 1)
    def _():
        win_span(qi - 1, 2, pair_bias)

    @pl.when(qi == 0)
    def _():
        win_span(qi, 1, causal_bias)

    gates_x = _dot_split(gate_ref[...], eg_ref[...])
    for g in range(GQA):
        rows = slice(g * TQ, (g + 1) * TQ)
        sl = slice(g * LANES, (g + 1) * LANES)
        o = gates_x[:, (g * 3) * LANES:(g * 3 + 1) * LANES] * cmp_sc[:, sl]
        for br in range(2):
            a0 = acc_sc[br, 0, rows, :]
            a1 = acc_sc[br, 1, rows, :]
            num = jnp.where(lo_half, a0, a1)
            den = pltpu.roll(jnp.where(lo_half, a1, a0), HEAD_DIM, 1)
            o = o + gates_x[:, (g * 3 + 1 + br) * LANES:(g * 3 + 2 + br) * LANES] * (num / den)
        out_ref[:, sl] = (o * szb_ref[:, sl].astype(F32)).astype(out_ref.dtype)


def _gate_expansion():
    eg = np.zeros((LANES, GQA * 3 * LANES), np.float32)
    for h in range(N_KV):
        for g in range(GQA):
            for br in range(3):
                c0 = (g * 3 + br) * LANES + h * HEAD_DIM
                eg[h * GQA * 3 + g * 3 + br, c0:c0 + HEAD_DIM] = 1.0
    return jnp.asarray(eg, dtype=BF16)


def _attn_prompt_call(qc, qr, cmp_p, ov, ksel, kwin, gates, szb, n_batch, seq):
    nq = seq // TQ
    tile = lambda b, q: (b * nq + q, 0)
    whole = lambda b, q: (b, 0)
    return pl.pallas_call(
        _attn_prompt_kernel,
        grid=(n_batch, nq),
        in_specs=[
            pl.BlockSpec((TQ, D_B), tile),
            pl.BlockSpec((TQ, D_B), tile),
            pl.BlockSpec((1, LANES, 2 * LANES), lambda b, q: (b, 0, 0)),
            pl.BlockSpec((LANES, LANES), lambda b, q: (0, 0)),
            pl.BlockSpec((LANES, GQA * 3 * LANES), lambda b, q: (0, 0)),
            pl.BlockSpec((seq, 2 * LANES), whole),
            pl.BlockSpec((seq, 2 * LANES), whole),
            pl.BlockSpec((TQ, LANES), tile),
            pl.BlockSpec((TQ, D_B), tile),
        ],
        out_specs=pl.BlockSpec((TQ, D_B), tile),
        out_shape=jax.ShapeDtypeStruct((n_batch * seq, D_B), BF16),
        scratch_shapes=[
            pltpu.VMEM((N_KV, GQA * TQ, LANES), BF16),
            pltpu.VMEM((N_KV, GQA * TQ, LANES), F32),
            pltpu.VMEM((2, N_KV, GQA * TQ, LANES), F32),
            pltpu.VMEM((TQ, D_B), F32),
        ],
        compiler_params=pltpu.CompilerParams(dimension_semantics=("arbitrary", "arbitrary"),
                                             vmem_limit_bytes=VMEM_LIMIT),
        name="attn_prompt",
    )(qc, qr, cmp_p, ov, _gate_expansion(), ksel, kwin, gates, szb)


def _cmpwin_sample_kernel(qc_ref, qr_ref, cmp_ref, ov_ref, win_ref, kvw_ref, gate_ref,
                          ocw_ref, sel_ref, *, past_len, n_pick, t_new):
    seqs = cmp_ref.shape[0]
    n_rows = N_KV * GQA * t_new
    lane = _lane_iota((t_new, LANES))
    n_cmp = cmp_ref.shape[1]
    row_t = _mod_pow2(_row_iota((n_rows, n_cmp)), t_new)
    slot = _lane_iota((n_rows, n_cmp))
    cmp_valid = (slot >= 1) & (slot * CMP_STRIDE + (CMP_STRIDE - 1) <= past_len + row_t)
    n_buf = win_ref.shape[1]
    dist = n_buf + _mod_pow2(_row_iota((n_rows, n_buf)), t_new) - _lane_iota((n_rows, n_buf))
    win_valid = (dist >= 0) & (dist < WINDOW)
    row_t1 = _mod_pow2(_row_iota((n_rows, 1)), t_new)

    def one_sequence(sq):
        rows_sq = slice(sq * t_new, (sq + 1) * t_new)
        gates = gate_ref[rows_sq, :]

        def q_rows(ref):
            return jnp.concatenate(
                [jnp.where(_div_pow2(lane, HEAD_DIM) == h, ref[rows_sq, g * LANES:(g + 1) * LANES], 0).astype(BF16)
                 for h in range(N_KV) for g in range(GQA)], axis=0)

        p = _softmax_rows(_dot_nt(q_rows(qc_ref), cmp_ref[sq, :, 0:LANES]), cmp_valid)
        o_cmp = _dot(p.astype(BF16), cmp_ref[sq, :, LANES:2 * LANES])
        p_sum = jnp.concatenate(
            [sum(p[(h * GQA + g) * t_new:(h * GQA + g + 1) * t_new] for g in range(GQA)) for h in range(N_KV)],
            axis=0)
        imp = _dot_split(p_sum, ov_ref[...])

        qr = q_rows(qr_ref)
        k_buf = win_ref[sq, :, 0:LANES].astype(BF16)
        v_buf = win_ref[sq, :, LANES:2 * LANES].astype(BF16)
        s_buf = jnp.where(win_valid, _dot_nt(qr, k_buf), NEG)
        k_new = kvw_ref[rows_sq, 0:LANES].astype(BF16).astype(F32)
        v_new = kvw_ref[rows_sq, LANES:2 * LANES].astype(BF16).astype(F32)
        qr32 = qr.astype(F32)
        s_new = [jnp.where(row_t1 >= i, jnp.sum(qr32 * k_new[i:i + 1, :], axis=-1, keepdims=True), NEG)
                 for i in range(t_new)]
        m = jnp.max(s_buf, axis=-1, keepdims=True)
        for s in s_new:
            m = jnp.maximum(m, s)
        e_buf = jnp.exp(s_buf - m)
        e_new = [jnp.exp(s - m) for s in s_new]
        denom = jnp.sum(e_buf, axis=-1, keepdims=True) + sum(e_new)
        o_win = _dot(e_buf.astype(BF16), v_buf)
        for i in range(t_new):
            o_win = o_win + e_new[i].astype(BF16).astype(F32) * v_new[i:i + 1, :]
        o_win = o_win / denom

        for g in range(GQA):
            parts = []
            for h in range(N_KV):
                rows = slice((h * GQA + g) * t_new, (h * GQA + g + 1) * t_new)
                base = h * GQA * 3 + g * 3
                parts.append(_gate_col(gates, base) * o_cmp[rows] + _gate_col(gates, base + 2) * o_win[rows])
            ocw_ref[rows_sq, g * LANES:(g + 1) * LANES] = jnp.where(lane < HEAD_DIM, parts[0], parts[1])
        return imp

    imp = jnp.concatenate([one_sequence(sq) for sq in range(seqs)], axis=0)
    blk = _lane_iota(imp.shape)
    tb = (past_len + t_new - 1) // SLC_BLOCK
    candidate = (blk >= 1) & (blk < tb - 1)
    score = jnp.where(candidate, imp, -1.0)
    out_lane = _lane_iota((imp.shape[0], LANES))
    picks = jnp.zeros((imp.shape[0], LANES), jnp.int32)
    for k in range(n_pick):
        best = jnp.max(score, axis=-1, keepdims=True)
        idx = jnp.min(jnp.where(score == best, blk, 1 << 20), axis=-1, keepdims=True)
        picks = jnp.where(out_lane == k, idx, picks)
        score = jnp.where(blk == idx, -2.0, score)
    sel_ref[...] = picks.reshape(sel_ref.shape)


def _cmpwin_sample_call(qc, qr, cmp_s, ov, win, kvw, gates, t_new, past_len, n_pick):
    n_seq = cmp_s.shape[0]
    seqs = SAMPLE_SEQS_PER_STEP
    assert n_seq % seqs == 0
    tile = lambda b: (b, 0)
    per_seq = lambda b: (b, 0, 0)
    return pl.pallas_call(
        functools.partial(_cmpwin_sample_kernel, past_len=past_len, n_pick=n_pick, t_new=t_new),
        grid=(n_seq // seqs,),
        in_specs=[
            pl.BlockSpec((seqs * t_new, D_B), tile),
            pl.BlockSpec((seqs * t_new, D_B), tile),
            pl.BlockSpec((seqs, cmp_s.shape[1], 2 * LANES), per_seq),
            pl.BlockSpec(ov.shape, lambda b: (0, 0)),
            pl.BlockSpec((seqs, win.shape[1], 2 * LANES), per_seq),
            pl.BlockSpec((seqs * t_new, 2 * LANES), tile),
            pl.BlockSpec((seqs * t_new, LANES), tile),
        ],
        out_specs=[
            pl.BlockSpec((seqs * t_new, D_B), tile),
            pl.BlockSpec((seqs, N_KV * t_new, LANES), per_seq),
        ],
        out_shape=[
            jax.ShapeDtypeStruct((n_seq * t_new, D_B), F32),
            jax.ShapeDtypeStruct((n_seq, N_KV * t_new, LANES), jnp.int32),
        ],
        compiler_params=pltpu.CompilerParams(dimension_semantics=("arbitrary",),
                                             vmem_limit_bytes=VMEM_LIMIT),
        name="cmpwin_sample",
    )(qc, qr, cmp_s, ov, win, kvw, gates)


def _sel_sample_kernel(sel_ref, pt_ref, cache_ref, qr_ref, rows_ref, gate_ref, ocw_ref, szb_ref,
                       out_ref, buf, sem, *, n_pick, n_pages, tb):
    t_new = qr_ref.shape[0]
    n_own = t_new * n_pick
    n_blocks = 2 + n_own
    n_steps = pl.num_programs(0)
    step = pl.program_id(0)
    blocks_per_page = PAGE_SIZE // SLC_BLOCK

    def block_copy(seq, slot, h, i, blk):
        phys = pt_ref[seq * n_pages + _div_pow2(blk, blocks_per_page)]
        return pltpu.make_async_copy(cache_ref.at[phys, pl.ds(2, 2), h], buf.at[slot, h * n_blocks + i],
                                     sem.at[slot])

    def for_all_blocks(seq, slot, fn):
        for h in range(N_KV):
            fn(block_copy(seq, slot, h, 0, 0))
            fn(block_copy(seq, slot, h, 1, tb - 1))

            def body(i, carry_, h=h):
                fn(block_copy(seq, slot, h, 2 + i, sel_ref[(seq * N_KV + h) * n_own + i]))
                return carry_
            lax.fori_loop(0, n_own, body, 0, unroll=DMA_LOOP_UNROLL)

    @pl.when(step == 0)
    def _():
        for_all_blocks(0, 0, lambda cp: cp.start())

    @pl.when(step + 1 < n_steps)
    def _():
        for_all_blocks(step + 1, (step + 1) % 2, lambda cp: cp.start())

    slot = step % 2
    for_all_blocks(step, slot, lambda cp: cp.wait())

    n_rows = GQA * t_new
    half = _div_pow2(_lane_iota((1, LANES)), SLC_BLOCK)
    row_t = _mod_pow2(_row_iota((n_rows, 1)), t_new)
    gates = gate_ref[...]

    def one_head(h):
        def head_half(x):
            return x[:, h * HEAD_DIM:(h + 1) * HEAD_DIM]

        qr = jnp.concatenate([head_half(qr_ref[:, g * LANES:(g + 1) * LANES]) for g in range(GQA)], axis=0)
        qrb = qr.astype(BF16)
        qr32 = qrb.astype(F32)

        def slabs(first, count, kind):
            return jnp.concatenate([buf[slot, h * n_blocks + first + k, kind].astype(BF16) for k in range(count)],
                                   axis=1)

        bias_sh = jnp.concatenate(
            [jnp.where(half == blk % blocks_per_page, 0.0, NEG) for blk in (0, tb - 1)], axis=1)
        s_sh = _dot(qrb, slabs(0, 2, 0)) + bias_sh
        k_new = head_half(rows_ref[:, 2 * LANES:3 * LANES]).astype(BF16).astype(F32)
        v_new = head_half(rows_ref[:, 3 * LANES:4 * LANES]).astype(BF16).astype(F32)
        s_new = [jnp.where(row_t >= i, jnp.sum(qr32 * k_new[i:i + 1, :], axis=-1, keepdims=True), NEG)
                 for i in range(t_new)]
        own_keys = n_pick * PAGE_SIZE
        s_own = jnp.zeros((n_rows, own_keys), F32)
        for t in range(t_new):
            bias_t = jnp.concatenate(
                [jnp.where(half == _mod_pow2(sel_ref[(step * N_KV + h) * n_own + t * n_pick + k], blocks_per_page),
                           0.0, NEG) for k in range(n_pick)], axis=1)
            s_t = _dot(qrb, slabs(2 + t * n_pick, n_pick, 0)) + bias_t
            s_own = jnp.where(row_t == t, s_t, s_own)

        m = jnp.maximum(jnp.max(s_sh, axis=-1, keepdims=True), jnp.max(s_own, axis=-1, keepdims=True))
        for s in s_new:
            m = jnp.maximum(m, s)
        e_sh = jnp.exp(s_sh - m)
        e_own = jnp.exp(s_own - m)
        e_new = [jnp.exp(s - m) for s in s_new]
        denom = jnp.sum(e_sh, axis=-1, keepdims=True) + jnp.sum(e_own, axis=-1, keepdims=True) + sum(e_new)
        o = _dot_nt(e_sh.astype(BF16), slabs(0, 2, 1))
        for t in range(t_new):
            o = o + _dot_nt(jnp.where(row_t == t, e_own, 0.0).astype(BF16), slabs(2 + t * n_pick, n_pick, 1))
        for i in range(t_new):
            o = o + e_new[i].astype(BF16).astype(F32) * v_new[i:i + 1, :]
        o = o / denom
        return [_gate_col(gates, h * GQA * 3 + g * 3 + 1) * o[g * t_new:(g + 1) * t_new] for g in range(GQA)]

    gated = [one_head(h) for h in range(N_KV)]
    for g in range(GQA):
        sl = slice(g * LANES, (g + 1) * LANES)
        o_slc = jnp.concatenate([gated[h][g] for h in range(N_KV)], axis=1)
        out_ref[:, sl] = (ocw_ref[:, sl] + o_slc) * szb_ref[:, sl]


def _sel_sample_call(sel_flat, pt_flat, cache_t, qr, rows, gates, ocw, szb, n_seq, t_new, n_pick, n_pages, tb):
    tile = lambda b, sel, pt: (b, 0)
    n_blocks = N_KV * (2 + t_new * n_pick)
    grid_spec = pltpu.PrefetchScalarGridSpec(
        num_scalar_prefetch=2,
        grid=(n_seq,),
        in_specs=[
            pl.BlockSpec(memory_space=pl.ANY),
            pl.BlockSpec((t_new, D_B), tile),
            pl.BlockSpec((t_new, 4 * LANES), tile),
            pl.BlockSpec((t_new, LANES), tile),
            pl.BlockSpec((t_new, D_B), tile),
            pl.BlockSpec((t_new, D_B), tile),
        ],
        out_specs=pl.BlockSpec((t_new, D_B), tile),
        scratch_shapes=[
            pltpu.VMEM((2, n_blocks, 2, HEAD_DIM, PAGE_SIZE), F32),
            pltpu.SemaphoreType.DMA((2,)),
        ],
    )
    return pl.pallas_call(
        functools.partial(_sel_sample_kernel, n_pick=n_pick, n_pages=n_pages, tb=tb),
        grid_spec=grid_spec,
        out_shape=jax.ShapeDtypeStruct((n_seq * t_new, D_B), F32),
        compiler_params=pltpu.CompilerParams(dimension_semantics=("arbitrary",),
                                             vmem_limit_bytes=VMEM_LIMIT),
        name="sel_sample",
    )(sel_flat, pt_flat, cache_t, qr, rows, gates, ocw, szb)


def _merge_kernel(x_ref, a_ref, b_ref, wa_ref, wb_ref, fg_ref, y_ref):
    delta = _dot(a_ref[...].astype(BF16), wa_ref[...]) + _dot(b_ref[...].astype(BF16), wb_ref[...])
    x = x_ref[...] + delta
    ms = jnp.mean(x * x, axis=-1, keepdims=True)
    y_ref[...] = (x * lax.rsqrt(ms + EPS)) * fg_ref[...]


def _merge_call(x2d, a, b, wa, wb, fg):
    n_rows = x2d.shape[0]
    tm = min(TM_MERGE, n_rows)
    row = lambda i: (i, 0)
    const2 = lambda i: (0, 0)
    return pl.pallas_call(
        _merge_kernel,
        grid=(n_rows // tm,),
        in_specs=[
            pl.BlockSpec((tm, D_MODEL), row),
            pl.BlockSpec((tm, D_A), row),
            pl.BlockSpec((tm, D_B), row),
            pl.BlockSpec((D_A, D_MODEL), const2),
            pl.BlockSpec((D_B, D_MODEL), const2),
            pl.BlockSpec((1, D_MODEL), const2),
        ],
        out_specs=pl.BlockSpec((tm, D_MODEL), row),
        out_shape=jax.ShapeDtypeStruct((n_rows, D_MODEL), F32),
        compiler_params=pltpu.CompilerParams(dimension_semantics=("arbitrary",),
                                             vmem_limit_bytes=VMEM_LIMIT),
        name="merge",
    )(x2d, a, b, wa, wb, fg)


def _head_pair_perm(w_cols):
    lead = w_cols.shape[:-1]
    return w_cols.reshape(*lead, N_KV, GQA, HEAD_DIM).swapaxes(-3, -2).reshape(*lead, D_B)


def _prep_w_in(w_in):
    cuts = np.cumsum([D_A, D_A, D_A, D_B, 6 * N_KV * HEAD_DIM, 3 * N_HEADS]).tolist()
    u, v, za, q, kv, g, zb = jnp.split(w_in, cuts, axis=-1)
    g_pad = jnp.pad(g, ((0, 0), (0, LANES - g.shape[1])))
    return jnp.concatenate([u, v, za, _head_pair_perm(q), kv, _head_pair_perm(zb), g_pad], axis=-1).astype(BF16)


def _prep_mix(w_s, b_s, chunk_len):
    reps = TM // chunk_len
    row = np.arange(TM)[:, None]
    col = np.arange(TM)[None, :]
    causal_same_chunk = (row // chunk_len == col // chunk_len) & (col <= row)
    spread = jnp.asarray(np.arange(chunk_len)[:, None] == np.arange(TM)[None, :] % chunk_len, dtype=w_s.dtype)
    wide = jnp.einsum("gts,sc->gtc", w_s[:, :chunk_len, :chunk_len], spread)
    wmix = jnp.where(causal_same_chunk, jnp.tile(wide, (1, reps, 1)), 0).astype(BF16)
    bias = jnp.repeat(b_s[:, :chunk_len].T, HEAD_DIM, axis=1)
    return wmix, jnp.tile(bias, (reps, 1))


def _rope_tables(pos):
    half = HEAD_DIM // 2
    inv = ROPE_THETA ** (-jnp.arange(half, dtype=F32) / half)
    ang = pos.astype(F32)[:, None] * inv
    cos = jnp.tile(jnp.cos(ang), (1, LANES // half))
    sin = jnp.tile(jnp.concatenate([-jnp.sin(ang), jnp.sin(ang)], axis=1), (1, LANES // HEAD_DIM))
    return cos, sin


def _prep_compress(cmp_pos, w_cmp1, b_cmp1, w_cmp2):
    ratio = CMP_BLOCK // CMP_STRIDE
    w1 = w_cmp1.reshape(2, ratio, CMP_STRIDE, HEAD_DIM, CMP_HIDDEN).transpose(0, 2, 3, 1, 4)
    w1 = w1.reshape(2, CMP_STRIDE, HEAD_DIM, ratio * CMP_HIDDEN)
    zeros = jnp.zeros_like(w1)
    wc = jnp.stack([jnp.concatenate([w1, zeros], axis=-1), jnp.concatenate([zeros, w1], axis=-1)], axis=2)
    wc = wc.reshape(2, CMP_STRIDE * LANES, N_KV * ratio * CMP_HIDDEN).astype(BF16)
    pe = cmp_pos.reshape(2, ratio, CMP_STRIDE, 1, HEAD_DIM)
    pe = jnp.broadcast_to(pe, (2, ratio, CMP_STRIDE, N_KV, HEAD_DIM)).reshape(2, ratio, CMP_STRIDE * LANES)
    pe = jnp.pad(pe, ((0, 0), (0, 16 - ratio), (0, 0))).astype(BF16)
    w2 = w_cmp2[:, None]
    zeros2 = jnp.zeros_like(w2)
    w2p = jnp.concatenate([jnp.concatenate([w2, zeros2], axis=-1), jnp.concatenate([zeros2, w2], axis=-1)],
                          axis=1).astype(BF16)
    return wc, pe, b_cmp1.reshape(2, 1, CMP_HIDDEN), w2p


def _overlap_matrix(n_cmp_slots, n_slc, n_lanes):
    start = (np.arange(n_cmp_slots)[:, None] - 1) * CMP_STRIDE
    j = np.arange(n_lanes)[None, :]
    ov = (start <= j * SLC_BLOCK + SLC_BLOCK - 1) & (start + CMP_BLOCK - 1 >= j * SLC_BLOCK)
    ov &= (np.arange(n_cmp_slots)[:, None] >= 1) & (j < n_slc)
    return jnp.asarray(ov, dtype=BF16)


def kernel(x_prompt, x_sample, cache_kv, state_win, page_table, norm_g, w_in, ln_g, ln_b, w_s, b_s,
           cmp_pos, w_cmp1, b_cmp1, w_cmp2, w_out, final_g):
    n_batch, seq, _ = x_prompt.shape
    n_seq, t_new, _ = x_sample.shape
    depth, n_phys = cache_kv.shape[:2]
    n_pages = page_table.shape[1]
    past_len = n_pages * PAGE_SIZE
    win_buf = state_win.shape[2]
    assert depth == 1 and seq % TQ == 0 and n_seq * t_new == TM and win_buf == WINDOW
    assert (past_len + t_new - 1) // SLC_BLOCK == past_len // SLC_BLOCK and past_len % CMP_STRIDE == 0

    w_all = _prep_w_in(w_in[0])
    ng = norm_g[0].reshape(1, D_MODEL)
    lng = ln_g[0].reshape(1, D_A)
    lnb = ln_b[0].reshape(1, D_A)
    fg = final_g.reshape(1, D_MODEL)
    wo = w_out[0]
    wo_a = wo[:D_A].astype(BF16)
    wo_b = _head_pair_perm(wo[D_A:].T).T.astype(BF16)
    wc, pe, b1, w2p = _prep_compress(cmp_pos[0], w_cmp1[0], b_cmp1[0], w_cmp2[0])

    xp = x_prompt.reshape(n_batch * seq, D_MODEL)
    wmix_p, bmix_p = _prep_mix(w_s[0], b_s[0], CHUNK)
    cos_p, sin_p = _rope_tables(jnp.arange(seq))
    ag_p, qc_p, qr_p, rows_p, kvw_pt, ksel_p, kwin_p, gate_p, szb_p, rows_pt = _proj_call(
        xp, ng, w_all, lng, lnb, wmix_p, bmix_p, cos_p, sin_p, BF16, seq)
    pages_p = seq // PAGE_SIZE
    ident = jnp.arange(n_batch * pages_p, dtype=jnp.int32).reshape(n_batch, pages_p)
    cmp_p = _compress_call(rows_p.reshape(n_batch * pages_p, PAGE_SIZE, 4 * LANES), ident, wc, pe, b1, w2p,
                           pages_p, False)
    ov_p = _overlap_matrix(seq // CMP_STRIDE, seq // SLC_BLOCK, LANES)
    bg_p = _attn_prompt_call(qc_p, qr_p, cmp_p, ov_p, ksel_p, kwin_p, gate_p, szb_p, n_batch, seq)
    y_p = _merge_call(xp, ag_p, bg_p, wo_a, wo_b, fg)

    xs = x_sample.reshape(n_seq * t_new, D_MODEL)
    wmix_s, bmix_s = _prep_mix(w_s[0], b_s[0], t_new)
    cos_s, sin_s = _rope_tables(jnp.tile(past_len + jnp.arange(t_new), n_seq))
    ag_s, qc_s, qr_s, rows_s, kvw_s, _, _, gate_s, szb_s, vln_s = _proj_call(
        xs, ng, w_all, lng, lnb, wmix_s, bmix_s, cos_s, sin_s, F32, None)
    cache_t = cache_kv[0].transpose(0, 2, 3, 4, 1)
    cmp_s = _compress_call(cache_t, page_table, wc, pe, b1, w2p, SAMPLE_PAGES_PER_STEP, True)
    n_slc_s = -(-(past_len + t_new) // SLC_BLOCK)
    tb = past_len // SLC_BLOCK
    n_pick = TOP_N - 3
    ov_s = _overlap_matrix(past_len // CMP_STRIDE, n_slc_s, -(-n_slc_s // LANES) * LANES)
    win_prev = state_win[0].reshape(n_seq, win_buf, 2 * LANES)
    ocw_s, sel_s = _cmpwin_sample_call(qc_s, qr_s, cmp_s, ov_s, win_prev, kvw_s, gate_s, t_new, past_len, n_pick)
    sel_flat = sel_s[:, :, :n_pick].reshape(-1)
    bg_s = _sel_sample_call(sel_flat, page_table.reshape(-1), cache_t, qr_s, rows_s, gate_s, ocw_s, szb_s,
                            n_seq, t_new, n_pick, n_pages, tb)
    y_s = _merge_call(xs, ag_s, bg_s, wo_a, wo_b, fg)

    new_kv_p = rows_pt.reshape(n_batch, 4, N_KV, HEAD_DIM, seq).transpose(0, 4, 1, 2, 3)
    new_win_p = kvw_pt[:, :, seq - win_buf:].reshape(n_batch, 2, N_KV, HEAD_DIM, win_buf).transpose(0, 4, 1, 2, 3)
    new_win_s = jnp.concatenate([state_win[0][:, t_new:], kvw_s.reshape(n_seq, t_new, 2, N_KV, HEAD_DIM)], axis=1)
    return (y_p.reshape(n_batch, seq, D_MODEL),
            y_s.reshape(n_seq, t_new, D_MODEL),
            new_kv_p[None],
            new_win_p[None],
            rows_s.reshape(1, n_seq, t_new, 4, N_KV, HEAD_DIM),
            new_win_s[None],
            vln_s.reshape(1, n_seq, t_new, D_A))
```

```python
import functools
import math

import jax
import jax.numpy as jnp
import numpy as np
from jax import lax
from jax.experimental import pallas as pl
from jax.experimental.pallas import tpu as pltpu

F32 = jnp.float32
BF16 = jnp.bfloat16

D_MODEL = 1024
HEAD_DIM = 64
D_A = 512
A_GROUPS = 8
CHUNK = 128
N_HEADS = 8
D_B = 512
N_KV = 2
GQA = 4
CMP_BLOCK = 32
CMP_STRIDE = 16
CMP_HIDDEN = 128
SLC_BLOCK = 64
TOP_N = 16
WINDOW = 512
ROPE_THETA = 10000.0
EPS = 1e-6
SCALE = HEAD_DIM ** -0.5
NEG = -1e30
FORCE = 1e9
PAGE_SIZE = 128

LANES = 128
VMEM_LIMIT = 56 * 1024 * 1024

C_U, C_V, C_ZA, C_Q, C_KV, C_ZB, C_G, C_END = 0, 512, 1024, 1536, 2048, 2816, 3328, 3456

MIX_ROWS = 256
TM = 512
TM_MERGE = 512
DMA_LOOP_UNROLL = 8
SAMPLE_SEQS_PER_STEP = 8
SAMPLE_PAGES_PER_STEP = 64
TQ = 256
SEGS_PER_PAGE = PAGE_SIZE // CMP_STRIDE


def _lane_iota(shape):
    return lax.broadcasted_iota(jnp.int32, shape, len(shape) - 1)


def _row_iota(shape):
    return lax.broadcasted_iota(jnp.int32, shape, len(shape) - 2)


def _div_pow2(x, n):
    assert n & (n - 1) == 0
    return lax.shift_right_logical(x, int(math.log2(n))) if n > 1 else x


def _mod_pow2(x, n):
    assert n & (n - 1) == 0
    return x & (n - 1)


def _dot(a, b):
    return jnp.dot(a, b, preferred_element_type=F32)


def _dot_nt(a, b):
    return lax.dot_general(a, b, (((1,), (1,)), ((), ())), preferred_element_type=F32)


def _rope(x, cos, sin_signed):
    lo = _mod_pow2(_lane_iota(x.shape), HEAD_DIM) < (HEAD_DIM // 2)
    swapped = jnp.where(lo, pltpu.roll(x, LANES - HEAD_DIM // 2, 1), pltpu.roll(x, HEAD_DIM // 2, 1))
    return x * cos + swapped * sin_signed


def _proj_kernel(x_ref, ng_ref, w_ref, lng_ref, lnb_ref, wmix_ref, bmix_ref, cos_ref, sin_ref,
                 ag_ref, qc_ref, qr_ref, rows_ref, kvw_ref, ksel_ref, kwin_ref, gate_ref, szb_ref,
                 extra_ref, *, prompt):
    x = x_ref[...]
    ms = jnp.mean(x * x, axis=-1, keepdims=True)
    hb = ((x * lax.rsqrt(ms + EPS)) * ng_ref[...]).astype(BF16)
    cos = cos_ref[...]
    sin = sin_ref[...]
    pair = 2 * LANES

    def proj(c0, n):
        return _dot(hb, w_ref[:, c0:c0 + n])

    def halves(x2):
        return x2[:, 0:LANES], x2[:, LANES:pair]

    v = jax.nn.gelu(proj(C_V, D_A))
    mu = jnp.mean(v, axis=-1, keepdims=True)
    vc = v - mu
    var = jnp.mean(vc * vc, axis=-1, keepdims=True)
    vln = vc * lax.rsqrt(var + EPS) * lng_ref[...] + lnb_ref[...]
    if not prompt:
        extra_ref[...] = vln
    vb = vln.astype(BF16)
    first_head = _lane_iota((MIX_ROWS, LANES)) < HEAD_DIM
    for gpp in range(A_GROUPS // 4):
        u2 = halves(jax.nn.gelu(proj(C_U + gpp * pair, pair)))
        za2 = halves(proj(C_ZA + gpp * pair, pair))
        for k in range(2):
            gp = 2 * gpp + k
            sl = slice(gp * LANES, (gp + 1) * LANES)
            mixed = jnp.concatenate(
                [jnp.where(first_head, _dot(wmix_ref[2 * gp], vb[r0:r0 + MIX_ROWS, sl]),
                           _dot(wmix_ref[2 * gp + 1], vb[r0:r0 + MIX_ROWS, sl])) + bmix_ref[:, sl]
                 for r0 in range(0, x.shape[0], MIX_ROWS)], axis=0)
            ag_ref[:, sl] = (u2[k] * mixed * jax.nn.silu(za2[k])).astype(ag_ref.dtype)

    for gpp in range(GQA // 2):
        q2 = halves(proj(C_Q + gpp * pair, pair))
        for k in range(2):
            sl = slice((2 * gpp + k) * LANES, (2 * gpp + k + 1) * LANES)
            qc_ref[:, sl] = (q2[k] * SCALE).astype(qc_ref.dtype)
            qr_ref[:, sl] = (_rope(q2[k], cos, sin) * SCALE).astype(qr_ref.dtype)

    kc, vcm = halves(proj(C_KV, pair))
    ks, vs = halves(proj(C_KV + pair, pair))
    kw, vw = halves(proj(C_KV + 2 * pair, pair))
    ks = _rope(ks, cos, sin)
    kw = _rope(kw, cos, sin)
    for i, blk in enumerate((kc, vcm, ks, vs)):
        rows_ref[:, i * LANES:(i + 1) * LANES] = blk
        if prompt:
            extra_ref[0, i * LANES:(i + 1) * LANES, :] = blk.T
    for i, blk in enumerate((kw, vw)):
        if prompt:
            kvw_ref[0, i * LANES:(i + 1) * LANES, :] = blk.T
        else:
            kvw_ref[:, i * LANES:(i + 1) * LANES] = blk
    ksel_ref[:, 0:LANES] = ks.astype(BF16)
    ksel_ref[:, LANES:2 * LANES] = vs.astype(BF16)
    kwin_ref[:, 0:LANES] = kw.astype(BF16)
    kwin_ref[:, LANES:2 * LANES] = vw.astype(BF16)

    gate_ref[...] = jax.nn.sigmoid(proj(C_G, LANES))
    for gpp in range(GQA // 2):
        zb2 = halves(proj(C_ZB + gpp * pair, pair))
        for k in range(2):
            sl = slice((2 * gpp + k) * LANES, (2 * gpp + k + 1) * LANES)
            szb_ref[:, sl] = jax.nn.silu(zb2[k]).astype(szb_ref.dtype)


def _proj_call(x2d, ng, w, lng, lnb, wmix, bmix, cos_t, sin_t, act_dtype, seq_len):
    n_rows = x2d.shape[0]
    tm = min(TM, n_rows)
    n_tiles = n_rows // tm
    pos_tiles = cos_t.shape[0] // tm
    row = lambda i: (i, 0)
    const2 = lambda i: (0, 0)
    out_shapes = [
        jax.ShapeDtypeStruct((n_rows, D_A), act_dtype),
        jax.ShapeDtypeStruct((n_rows, D_B), act_dtype),
        jax.ShapeDtypeStruct((n_rows, D_B), act_dtype),
        jax.ShapeDtypeStruct((n_rows, 4 * LANES), F32),
        jax.ShapeDtypeStruct((n_rows, 2 * LANES), F32),
        jax.ShapeDtypeStruct((n_rows, 2 * LANES), BF16),
        jax.ShapeDtypeStruct((n_rows, 2 * LANES), BF16),
        jax.ShapeDtypeStruct((n_rows, LANES), F32),
        jax.ShapeDtypeStruct((n_rows, D_B), act_dtype),
        jax.ShapeDtypeStruct((n_rows, D_A), F32),
    ]
    out_specs = [pl.BlockSpec((tm, s.shape[1]), row) for s in out_shapes]
    if seq_len is not None:
        tiles_per_seq = seq_len // tm
        dim_major = lambda i: (i // tiles_per_seq, 0, i % tiles_per_seq)
        for idx, width in ((4, 2 * LANES), (9, 4 * LANES)):
            out_shapes[idx] = jax.ShapeDtypeStruct((n_rows // seq_len, width, seq_len), F32)
            out_specs[idx] = pl.BlockSpec((1, width, tm), dim_major)
    return pl.pallas_call(
        functools.partial(_proj_kernel, prompt=seq_len is not None),
        grid=(n_tiles,),
        in_specs=[
            pl.BlockSpec((tm, D_MODEL), row),
            pl.BlockSpec((1, D_MODEL), const2),
            pl.BlockSpec((D_MODEL, C_END), const2),
            pl.BlockSpec((1, D_A), const2),
            pl.BlockSpec((1, D_A), const2),
            pl.BlockSpec((A_GROUPS, MIX_ROWS, MIX_ROWS), lambda i: (0, 0, 0)),
            pl.BlockSpec((MIX_ROWS, D_A), const2),
            pl.BlockSpec((tm, LANES), lambda i: (i % pos_tiles, 0)),
            pl.BlockSpec((tm, LANES), lambda i: (i % pos_tiles, 0)),
        ],
        out_specs=out_specs,
        out_shape=out_shapes,
        compiler_params=pltpu.CompilerParams(dimension_semantics=("arbitrary",),
                                             vmem_limit_bytes=VMEM_LIMIT),
        name="proj",
    )(x2d, ng, w, lng, lnb, wmix, bmix, cos_t, sin_t)


def _compress_kernel(pt_ref, src_ref, wc_ref, pe_ref, b1_ref, w2_ref, out_ref, buf, sem, carry,
                     *maybe_stage, pages_per_step):
    transposed_src = bool(maybe_stage)
    n_steps = pl.num_programs(0) * pl.num_programs(1)
    step = pl.program_id(0) * pl.num_programs(1) + pl.program_id(1)
    m = pages_per_step * SEGS_PER_PAGE

    def page_copies(step_idx, slot, i):
        phys = pt_ref[step_idx * pages_per_step + i]
        if transposed_src:
            return [pltpu.make_async_copy(src_ref.at[phys, pl.ds(0, 2)], maybe_stage[0].at[slot, i], sem.at[slot])]
        return [pltpu.make_async_copy(src_ref.at[phys, :, pl.ds(c * LANES, LANES)],
                                      buf.at[slot, c, pl.ds(i * PAGE_SIZE, PAGE_SIZE), :],
                                      sem.at[slot]) for c in range(2)]

    def start_step(step_idx, slot):
        def body(i, carry_):
            for cp in page_copies(step_idx, slot, i):
                cp.start()
            return carry_
        lax.fori_loop(0, pages_per_step, body, 0, unroll=DMA_LOOP_UNROLL)

    def wait_step(step_idx, slot):
        def body(i, carry_):
            for cp in page_copies(step_idx, slot, i):
                cp.wait()
            return carry_
        lax.fori_loop(0, pages_per_step, body, 0, unroll=DMA_LOOP_UNROLL)

    slot = step % 2

    first_of_seq = pl.program_id(1) == 0
    first_row = _row_iota((m, LANES)) == 0

    def compress_rows(rows_slot, seg_pitch):
        for c in range(2):
            pieces = [buf[rows_slot, c, pl.ds(j, m, stride=seg_pitch), :].astype(BF16)
                      for j in range(CMP_STRIDE)]
            part = _dot(jnp.concatenate(pieces, axis=1), wc_ref[c])
            pe_part = _dot(pe_ref[c], wc_ref[c])
            acc_out = None
            for h in range(N_KV):
                lo = slice(h * 2 * CMP_HIDDEN, h * 2 * CMP_HIDDEN + CMP_HIDDEN)
                hi = slice(h * 2 * CMP_HIDDEN + CMP_HIDDEN, (h + 1) * 2 * CMP_HIDDEN)
                bias = pe_part[0:1, lo] + pe_part[1:2, hi] + b1_ref[c]
                part0 = part[:, lo]
                prev_last = jnp.where(first_of_seq, 0.0, carry[c, h, 0:1, :])
                prev0 = jnp.where(first_row, prev_last, pltpu.roll(part0, 1, 0))
                carry[c, h, 0:1, :] = part0[m - 1:m, :]
                hid = jax.nn.gelu(prev0 + part[:, hi] + bias).astype(BF16)
                contrib = _dot(hid, w2_ref[c, h])
                acc_out = contrib if acc_out is None else acc_out + contrib
            out_ref[0, :, c * LANES:(c + 1) * LANES] = acc_out.astype(out_ref.dtype)

    if transposed_src:
        stage = maybe_stage[0]
        seg_pitch = CMP_STRIDE + 1

        def transpose_step(stage_slot):
            for i in range(pages_per_step):
                for c in range(2):
                    rows = stage[stage_slot, i, c].reshape(N_KV * HEAD_DIM, PAGE_SIZE).T
                    for s in range(SEGS_PER_PAGE):
                        r0 = (i * SEGS_PER_PAGE + s) * seg_pitch
                        buf[stage_slot, c, r0:r0 + CMP_STRIDE, :] = rows[s * CMP_STRIDE:(s + 1) * CMP_STRIDE]

        @pl.when(step == 0)
        def _():
            start_step(0, 0)
            start_step(1, 1)
            wait_step(0, 0)
            transpose_step(0)

        @pl.when(step + 1 < n_steps)
        def _():
            wait_step(step + 1, (step + 1) % 2)

        @pl.when(step + 2 < n_steps)
        def _():
            start_step(step + 2, slot)

        for parity in range(2):
            @pl.when(slot == parity)
            def _():
                transpose_step(1 - parity)
                compress_rows(parity, seg_pitch)
    else:
        @pl.when(step == 0)
        def _():
            start_step(0, 0)

        @pl.when(step + 1 < n_steps)
        def _():
            start_step(step + 1, (step + 1) % 2)

        wait_step(step, slot)
        compress_rows(slot, CMP_STRIDE)


def _compress_call(src, page_table, wc, pe, b1, w2p, pages_per_step, transposed_src):
    n_seq, n_pages = page_table.shape
    n_chunks = n_pages // pages_per_step
    m = pages_per_step * SEGS_PER_PAGE
    assert not transposed_src or n_seq * n_chunks >= 2
    seg_pitch = CMP_STRIDE + 1 if transposed_src else CMP_STRIDE
    scratch = [
        pltpu.VMEM((2, 2, m * seg_pitch, LANES), F32),
        pltpu.SemaphoreType.DMA((2,)),
        pltpu.VMEM((2, N_KV, 8, CMP_HIDDEN), F32),
    ]
    if transposed_src:
        scratch.append(pltpu.VMEM((2, pages_per_step, 2, N_KV, HEAD_DIM, PAGE_SIZE), F32))
    grid_spec = pltpu.PrefetchScalarGridSpec(
        num_scalar_prefetch=1,
        grid=(n_seq, n_chunks),
        in_specs=[
            pl.BlockSpec(memory_space=pl.ANY),
            pl.BlockSpec((2, CMP_STRIDE * LANES, 4 * CMP_HIDDEN), lambda b, k, pt: (0, 0, 0)),
            pl.BlockSpec((2, 16, CMP_STRIDE * LANES), lambda b, k, pt: (0, 0, 0)),
            pl.BlockSpec((2, 1, CMP_HIDDEN), lambda b, k, pt: (0, 0, 0)),
            pl.BlockSpec((2, N_KV, CMP_HIDDEN, LANES), lambda b, k, pt: (0, 0, 0, 0)),
        ],
        out_specs=pl.BlockSpec((1, m, 2 * LANES), lambda b, k, pt: (b, k, 0)),
        scratch_shapes=scratch,
    )
    return pl.pallas_call(
        functools.partial(_compress_kernel, pages_per_step=pages_per_step),
        grid_spec=grid_spec,
        out_shape=jax.ShapeDtypeStruct((n_seq, n_pages * SEGS_PER_PAGE, 2 * LANES), BF16),
        compiler_params=pltpu.CompilerParams(dimension_semantics=("arbitrary", "arbitrary"),
                                             vmem_limit_bytes=VMEM_LIMIT),
        name="compress",
    )(page_table.reshape(-1), src, wc, pe, b1, w2p)


def _softmax_rows(s, mask):
    s = jnp.where(mask, s, NEG)
    e = jnp.where(mask, jnp.exp(s - jnp.max(s, axis=-1, keepdims=True)), 0.0)
    return e / jnp.maximum(jnp.sum(e, axis=-1, keepdims=True), 1.0)


def _dot_split(p, w):
    hi = p.astype(BF16)
    lo = (p - hi.astype(F32)).astype(BF16)
    return _dot(hi, w) + _dot(lo, w)


def _gate_col(gates, idx):
    return gates[:, idx:idx + 1]


def _attn_prompt_kernel(qc_ref, qr_ref, cmp_ref, ov_ref, eg_ref, ksel_ref, kwin_ref, gate_ref, szb_ref,
                        out_ref, q_sc, m_sc, acc_sc, cmp_sc):
    qi = pl.program_id(1)
    lane = _lane_iota((TQ, LANES))
    lo_half = lane < HEAD_DIM
    r_minus_c = _row_iota((TQ, TQ)) - _lane_iota((TQ, TQ))
    causal_bias = jnp.where(r_minus_c >= 0, 0.0, NEG)
    far_bias = jnp.where(r_minus_c < 0, 0.0, NEG)
    pair_bias = jnp.concatenate([jnp.zeros((TQ, TQ), F32), causal_bias], axis=1)
    kcmp = cmp_ref[0, :, 0:LANES]
    vcmp = cmp_ref[0, :, LANES:2 * LANES]
    n_slc = ksel_ref.shape[0] // SLC_BLOCK
    blocks_per_tile = TQ // SLC_BLOCK

    def flash_tile(br, h, k_tile, v_aug, bias):
        width = k_tile.shape[0]
        s = _dot_nt(q_sc[h], k_tile)
        if bias is not None:
            s = (s.reshape(GQA, TQ, width) + bias[None]).reshape(GQA * TQ, width)
        m_prev = m_sc[h]
        m_new = jnp.maximum(m_prev, jnp.max(s, axis=-1, keepdims=True))
        alpha = jnp.exp(m_prev - m_new)
        p = jnp.exp(s - jnp.concatenate([m_new] * (width // LANES), axis=1))
        acc_sc[br, h] = alpha * acc_sc[br, h] + _dot(p.astype(BF16), v_aug)
        m_sc[h] = m_new

    def flash_init(br):
        m_sc[...] = jnp.full(m_sc.shape, NEG, F32)
        acc_sc[br] = jnp.zeros(acc_sc.shape[1:], F32)

    n_hg = N_KV * GQA
    q_cmp = jnp.concatenate([jnp.where(_div_pow2(lane, HEAD_DIM) == h, qc_ref[:, g * LANES:(g + 1) * LANES], 0)
                             for h in range(N_KV) for g in range(GQA)], axis=0)
    lane_all = _lane_iota((n_hg * TQ, LANES))
    tq_all = qi * TQ + _mod_pow2(_row_iota((n_hg * TQ, LANES)), TQ)
    cmp_valid = (lane_all >= 1) & (lane_all * CMP_STRIDE + (CMP_STRIDE - 1) <= tq_all)
    p_all = _softmax_rows(_dot_nt(q_cmp, kcmp), cmp_valid)
    o_all = _dot(p_all.astype(BF16), vcmp)
    for g in range(GQA):
        cmp_sc[:, g * LANES:(g + 1) * LANES] = jnp.where(
            lo_half, o_all[g * TQ:(g + 1) * TQ], o_all[(GQA + g) * TQ:(GQA + g + 1) * TQ])
    p4 = p_all.reshape(N_KV, GQA, TQ, LANES)
    p_sum = sum(p4[:, g] for g in range(GQA)).reshape(N_KV * TQ, LANES)
    imp = _dot_split(p_sum, ov_ref[...])
    lane_ht = _lane_iota((N_KV * TQ, LANES))
    tq_ht = qi * TQ + _mod_pow2(_row_iota((N_KV * TQ, LANES)), TQ)
    tb = _div_pow2(tq_ht, SLC_BLOCK)
    forced = (lane_ht == 0) | (lane_ht == tb) | (lane_ht == tb - 1)
    valid = lane_ht * SLC_BLOCK <= tq_ht
    score = jnp.where(forced, FORCE, jnp.where(valid, imp, NEG))
    score_t = score.T[0:n_slc, :]
    blk = _row_iota((n_slc, N_KV * TQ))
    rank = jnp.zeros((n_slc, N_KV * TQ), jnp.int32)
    for j in range(n_slc):
        other = score_t[j:j + 1, :]
        ahead = (other > score_t) | ((other == score_t) & (blk > j))
        rank = rank + ahead.astype(jnp.int32)
    sel_bias_t = jnp.where(rank < TOP_N, 0.0, NEG)

    for h in range(N_KV):
        keep = _div_pow2(lane, HEAD_DIM) == h
        other_off = HEAD_DIM * (1 - h)
        pieces = [jnp.zeros((other_off, TQ), F32)] if other_off else []
        pieces += [sel_bias_t[:, h * TQ:(h + 1) * TQ], jnp.zeros((LANES - other_off - n_slc, TQ), F32)]
        sel_bias = jnp.concatenate(pieces, axis=0).T.astype(BF16)
        for g in range(GQA):
            sl = slice(g * LANES, (g + 1) * LANES)
            q_sc[h, g * TQ:(g + 1) * TQ, :] = jnp.where(keep, qr_ref[:, sl], sel_bias)

    def span_operands(kv_ref, kt, n_tiles, h, other_lanes):
        k = kv_ref[pl.ds(kt * TQ, n_tiles * TQ), 0:LANES]
        v = kv_ref[pl.ds(kt * TQ, n_tiles * TQ), LANES:2 * LANES]
        own = _div_pow2(_lane_iota(k.shape), HEAD_DIM) == h
        return jnp.where(own, k, other_lanes(k.shape)), jnp.where(own, v, 1.0)

    def sel_span(kt, n_tiles, bias):
        for h in range(N_KV):
            def indicator(shape, h=h):
                first = HEAD_DIM * (1 - h) + kt * blocks_per_tile
                hit = _lane_iota(shape) == first + _div_pow2(_row_iota(shape), SLC_BLOCK)
                return jnp.where(hit, 1, 0).astype(BF16)
            flash_tile(0, h, *span_operands(ksel_ref, kt, n_tiles, h, indicator), bias)

    flash_init(0)

    def sel_pair(i, carry_):
        sel_span(2 * i, 2, None)
        return carry_
    lax.fori_loop(0, _div_pow2(jnp.maximum(qi - 1, 0), 2), sel_pair, 0)

    @pl.when((qi >= 2) & (_mod_pow2(qi, 2) == 0))
    def _():
        sel_span(qi - 2, 1, None)

    @pl.when(qi >= 1)
    def _():
        sel_span(qi - 1, 2, pair_bias)

    @pl.when(qi == 0)
    def _():
        sel_span(qi, 1, causal_bias)

    def win_span(kt, n_tiles, bias):
        for h in range(N_KV):
            flash_tile(1, h, *span_operands(kwin_ref, kt, n_tiles, h, lambda shape: jnp.zeros(shape, BF16)), bias)

    flash_init(1)
    far = WINDOW // TQ
    assert far == 2

    @pl.when(qi >= far)
    def _():
        win_span(qi - far, 1, far_bias)

    @pl.when(qi >= 1)
    def _():
        win_span(qi - 1, 2, pair_bias)

    @pl.when(qi == 0)
    def _():
        win_span(qi, 1, causal_bias)

    gates_x = _dot_split(gate_ref[...], eg_ref[...])
    for g in range(GQA):
        rows = slice(g * TQ, (g + 1) * TQ)
        sl = slice(g * LANES, (g + 1) * LANES)
        o = gates_x[:, (g * 3) * LANES:(g * 3 + 1) * LANES] * cmp_sc[:, sl]
        for br in range(2):
            a0 = acc_sc[br, 0, rows, :]
            a1 = acc_sc[br, 1, rows, :]
            num = jnp.where(lo_half, a0, a1)
            den = pltpu.roll(jnp.where(lo_half, a1, a0), HEAD_DIM, 1)
            o = o + gates_x[:, (g * 3 + 1 + br) * LANES:(g * 3 + 2 + br) * LANES] * (num / den)
        out_ref[:, sl] = (o * szb_ref[:, sl].astype(F32)).astype(out_ref.dtype)


def _gate_expansion():
    eg = np.zeros((LANES, GQA * 3 * LANES), np.float32)
    for h in range(N_KV):
        for g in range(GQA):
            for br in range(3):
                c0 = (g * 3 + br) * LANES + h * HEAD_DIM
                eg[h * GQA * 3 + g * 3 + br, c0:c0 + HEAD_DIM] = 1.0
    return jnp.asarray(eg, dtype=BF16)


def _attn_prompt_call(qc, qr, cmp_p, ov, ksel, kwin, gates, szb, n_batch, seq):
    nq = seq // TQ
    tile = lambda b, q: (b * nq + q, 0)
    whole = lambda b, q: (b, 0)
    return pl.pallas_call(
        _attn_prompt_kernel,
        grid=(n_batch, nq),
        in_specs=[
            pl.BlockSpec((TQ, D_B), tile),
            pl.BlockSpec((TQ, D_B), tile),
            pl.BlockSpec((1, LANES, 2 * LANES), lambda b, q: (b, 0, 0)),
            pl.BlockSpec((LANES, LANES), lambda b, q: (0, 0)),
            pl.BlockSpec((LANES, GQA * 3 * LANES), lambda b, q: (0, 0)),
            pl.BlockSpec((seq, 2 * LANES), whole),
            pl.BlockSpec((seq, 2 * LANES), whole),
            pl.BlockSpec((TQ, LANES), tile),
            pl.BlockSpec((TQ, D_B), tile),
        ],
        out_specs=pl.BlockSpec((TQ, D_B), tile),
        out_shape=jax.ShapeDtypeStruct((n_batch * seq, D_B), BF16),
        scratch_shapes=[
            pltpu.VMEM((N_KV, GQA * TQ, LANES), BF16),
            pltpu.VMEM((N_KV, GQA * TQ, LANES), F32),
            pltpu.VMEM((2, N_KV, GQA * TQ, LANES), F32),
            pltpu.VMEM((TQ, D_B), F32),
        ],
        compiler_params=pltpu.CompilerParams(dimension_semantics=("arbitrary", "arbitrary"),
                                             vmem_limit_bytes=VMEM_LIMIT),
        name="attn_prompt",
    )(qc, qr, cmp_p, ov, _gate_expansion(), ksel, kwin, gates, szb)


def _cmpwin_sample_kernel(qc_ref, qr_ref, cmp_ref, ov_ref, win_ref, kvw_ref, gate_ref,
                          ocw_ref, sel_ref, *, past_len, n_pick, t_new):
    seqs = cmp_ref.shape[0]
    n_rows = N_KV * GQA * t_new
    lane = _lane_iota((t_new, LANES))
    n_cmp = cmp_ref.shape[1]
    row_t = _mod_pow2(_row_iota((n_rows, n_cmp)), t_new)
    slot = _lane_iota((n_rows, n_cmp))
    cmp_valid = (slot >= 1) & (slot * CMP_STRIDE + (CMP_STRIDE - 1) <= past_len + row_t)
    n_buf = win_ref.shape[1]
    dist = n_buf + _mod_pow2(_row_iota((n_rows, n_buf)), t_new) - _lane_iota((n_rows, n_buf))
    win_valid = (dist >= 0) & (dist < WINDOW)
    row_t1 = _mod_pow2(_row_iota((n_rows, 1)), t_new)

    def one_sequence(sq):
        rows_sq = slice(sq * t_new, (sq + 1) * t_new)
        gates = gate_ref[rows_sq, :]

        def q_rows(ref):
            return jnp.concatenate(
                [jnp.where(_div_pow2(lane, HEAD_DIM) == h, ref[rows_sq, g * LANES:(g + 1) * LANES], 0).astype(BF16)
                 for h in range(N_KV) for g in range(GQA)], axis=0)

        p = _softmax_rows(_dot_nt(q_rows(qc_ref), cmp_ref[sq, :, 0:LANES]), cmp_valid)
        o_cmp = _dot(p.astype(BF16), cmp_ref[sq, :, LANES:2 * LANES])
        p_sum = jnp.concatenate(
            [sum(p[(h * GQA + g) * t_new:(h * GQA + g + 1) * t_new] for g in range(GQA)) for h in range(N_KV)],
            axis=0)
        imp = _dot_split(p_sum, ov_ref[...])

        qr = q_rows(qr_ref)
        k_buf = win_ref[sq, :, 0:LANES].astype(BF16)
        v_buf = win_ref[sq, :, LANES:2 * LANES].astype(BF16)
        s_buf = jnp.where(win_valid, _dot_nt(qr, k_buf), NEG)
        k_new = kvw_ref[rows_sq, 0:LANES].astype(BF16).astype(F32)
        v_new = kvw_ref[rows_sq, LANES:2 * LANES].astype(BF16).astype(F32)
        qr32 = qr.astype(F32)
        s_new = [jnp.where(row_t1 >= i, jnp.sum(qr32 * k_new[i:i + 1, :], axis=-1, keepdims=True), NEG)
                 for i in range(t_new)]
        m = jnp.max(s_buf, axis=-1, keepdims=True)
        for s in s_new:
            m = jnp.maximum(m, s)
        e_buf = jnp.exp(s_buf - m)
        e_new = [jnp.exp(s - m) for s in s_new]
        denom = jnp.sum(e_buf, axis=-1, keepdims=True) + sum(e_new)
        o_win = _dot(e_buf.astype(BF16), v_buf)
        for i in range(t_new):
            o_win = o_win + e_new[i].astype(BF16).astype(F32) * v_new[i:i + 1, :]
        o_win = o_win / denom

        for g in range(GQA):
            parts = []
            for h in range(N_KV):
                rows = slice((h * GQA + g) * t_new, (h * GQA + g + 1) * t_new)
                base = h * GQA * 3 + g * 3
                parts.append(_gate_col(gates, base) * o_cmp[rows] + _gate_col(gates, base + 2) * o_win[rows])
            ocw_ref[rows_sq, g * LANES:(g + 1) * LANES] = jnp.where(lane < HEAD_DIM, parts[0], parts[1])
        return imp

    imp = jnp.concatenate([one_sequence(sq) for sq in range(seqs)], axis=0)
    blk = _lane_iota(imp.shape)
    tb = (past_len + t_new - 1) // SLC_BLOCK
    candidate = (blk >= 1) & (blk < tb - 1)
    score = jnp.where(candidate, imp, -1.0)
    out_lane = _lane_iota((imp.shape[0], LANES))
    picks = jnp.zeros((imp.shape[0], LANES), jnp.int32)
    for k in range(n_pick):
        best = jnp.max(score, axis=-1, keepdims=True)
        idx = jnp.min(jnp.where(score == best, blk, 1 << 20), axis=-1, keepdims=True)
        picks = jnp.where(out_lane == k, idx, picks)
        score = jnp.where(blk == idx, -2.0, score)
    sel_ref[...] = picks.reshape(sel_ref.shape)


def _cmpwin_sample_call(qc, qr, cmp_s, ov, win, kvw, gates, t_new, past_len, n_pick):
    n_seq = cmp_s.shape[0]
    seqs = SAMPLE_SEQS_PER_STEP
    assert n_seq % seqs == 0
    tile = lambda b: (b, 0)
    per_seq = lambda b: (b, 0, 0)
    return pl.pallas_call(
        functools.partial(_cmpwin_sample_kernel, past_len=past_len, n_pick=n_pick, t_new=t_new),
        grid=(n_seq // seqs,),
        in_specs=[
            pl.BlockSpec((seqs * t_new, D_B), tile),
            pl.BlockSpec((seqs * t_new, D_B), tile),
            pl.BlockSpec((seqs, cmp_s.shape[1], 2 * LANES), per_seq),
            pl.BlockSpec(ov.shape, lambda b: (0, 0)),
            pl.BlockSpec((seqs, win.shape[1], 2 * LANES), per_seq),
            pl.BlockSpec((seqs * t_new, 2 * LANES), tile),
            pl.BlockSpec((seqs * t_new, LANES), tile),
        ],
        out_specs=[
            pl.BlockSpec((seqs * t_new, D_B), tile),
            pl.BlockSpec((seqs, N_KV * t_new, LANES), per_seq),
        ],
        out_shape=[
            jax.ShapeDtypeStruct((n_seq * t_new, D_B), F32),
            jax.ShapeDtypeStruct((n_seq, N_KV * t_new, LANES), jnp.int32),
        ],
        compiler_params=pltpu.CompilerParams(dimension_semantics=("arbitrary",),
                                             vmem_limit_bytes=VMEM_LIMIT),
        name="cmpwin_sample",
    )(qc, qr, cmp_s, ov, win, kvw, gates)


def _sel_sample_kernel(sel_ref, pt_ref, cache_ref, qr_ref, rows_ref, gate_ref, ocw_ref, szb_ref,
                       out_ref, buf, sem, *, n_pick, n_pages, tb):
    t_new = qr_ref.shape[0]
    n_own = t_new * n_pick
    n_blocks = 2 + n_own
    n_steps = pl.num_programs(0)
    step = pl.program_id(0)
    blocks_per_page = PAGE_SIZE // SLC_BLOCK

    def block_copy(seq, slot, h, i, blk):
        phys = pt_ref[seq * n_pages + _div_pow2(blk, blocks_per_page)]
        return pltpu.make_async_copy(cache_ref.at[phys, pl.ds(2, 2), h], buf.at[slot, h * n_blocks + i],
                                     sem.at[slot])

    def for_all_blocks(seq, slot, fn):
        for h in range(N_KV):
            fn(block_copy(seq, slot, h, 0, 0))
            fn(block_copy(seq, slot, h, 1, tb - 1))

            def body(i, carry_, h=h):
                fn(block_copy(seq, slot, h, 2 + i, sel_ref[(seq * N_KV + h) * n_own + i]))
                return carry_
            lax.fori_loop(0, n_own, body, 0, unroll=DMA_LOOP_UNROLL)

    @pl.when(step == 0)
    def _():
        for_all_blocks(0, 0, lambda cp: cp.start())

    @pl.when(step + 1 < n_steps)
    def _():
        for_all_blocks(step + 1, (step + 1) % 2, lambda cp: cp.start())

    slot = step % 2
    for_all_blocks(step, slot, lambda cp: cp.wait())

    n_rows = GQA * t_new
    half = _div_pow2(_lane_iota((1, LANES)), SLC_BLOCK)
    row_t = _mod_pow2(_row_iota((n_rows, 1)), t_new)
    gates = gate_ref[...]

    def one_head(h):
        def head_half(x):
            return x[:, h * HEAD_DIM:(h + 1) * HEAD_DIM]

        qr = jnp.concatenate([head_half(qr_ref[:, g * LANES:(g + 1) * LANES]) for g in range(GQA)], axis=0)
        qrb = qr.astype(BF16)
        qr32 = qrb.astype(F32)

        def slabs(first, count, kind):
            return jnp.concatenate([buf[slot, h * n_blocks + first + k, kind].astype(BF16) for k in range(count)],
                                   axis=1)

        bias_sh = jnp.concatenate(
            [jnp.where(half == blk % blocks_per_page, 0.0, NEG) for blk in (0, tb - 1)], axis=1)
        s_sh = _dot(qrb, slabs(0, 2, 0)) + bias_sh
        k_new = head_half(rows_ref[:, 2 * LANES:3 * LANES]).astype(BF16).astype(F32)
        v_new = head_half(rows_ref[:, 3 * LANES:4 * LANES]).astype(BF16).astype(F32)
        s_new = [jnp.where(row_t >= i, jnp.sum(qr32 * k_new[i:i + 1, :], axis=-1, keepdims=True), NEG)
                 for i in range(t_new)]
        own_keys = n_pick * PAGE_SIZE
        s_own = jnp.zeros((n_rows, own_keys), F32)
        for t in range(t_new):
            bias_t = jnp.concatenate(
                [jnp.where(half == _mod_pow2(sel_ref[(step * N_KV + h) * n_own + t * n_pick + k], blocks_per_page),
                           0.0, NEG) for k in range(n_pick)], axis=1)
            s_t = _dot(qrb, slabs(2 + t * n_pick, n_pick, 0)) + bias_t
            s_own = jnp.where(row_t == t, s_t, s_own)

        m = jnp.maximum(jnp.max(s_sh, axis=-1, keepdims=True), jnp.max(s_own, axis=-1, keepdims=True))
        for s in s_new:
            m = jnp.maximum(m, s)
        e_sh = jnp.exp(s_sh - m)
        e_own = jnp.exp(s_own - m)
        e_new = [jnp.exp(s - m) for s in s_new]
        denom = jnp.sum(e_sh, axis=-1, keepdims=True) + jnp.sum(e_own, axis=-1, keepdims=True) + sum(e_new)
        o = _dot_nt(e_sh.astype(BF16), slabs(0, 2, 1))
        for t in range(t_new):
            o = o + _dot_nt(jnp.where(row_t == t, e_own, 0.0).astype(BF16), slabs(2 + t * n_pick, n_pick, 1))
        for i in range(t_new):
            o = o + e_new[i].astype(BF16).astype(F32) * v_new[i:i + 1, :]
        o = o / denom
        return [_gate_col(gates, h * GQA * 3 + g * 3 + 1) * o[g * t_new:(g + 1) * t_new] for g in range(GQA)]

    gated = [one_head(h) for h in range(N_KV)]
    for g in range(GQA):
        sl = slice(g * LANES, (g + 1) * LANES)
        o_slc = jnp.concatenate([gated[h][g] for h in range(N_KV)], axis=1)
        out_ref[:, sl] = (ocw_ref[:, sl] + o_slc) * szb_ref[:, sl]


def _sel_sample_call(sel_flat, pt_flat, cache_t, qr, rows, gates, ocw, szb, n_seq, t_new, n_pick, n_pages, tb):
    tile = lambda b, sel, pt: (b, 0)
    n_blocks = N_KV * (2 + t_new * n_pick)
    grid_spec = pltpu.PrefetchScalarGridSpec(
        num_scalar_prefetch=2,
        grid=(n_seq,),
        in_specs=[
            pl.BlockSpec(memory_space=pl.ANY),
            pl.BlockSpec((t_new, D_B), tile),
            pl.BlockSpec((t_new, 4 * LANES), tile),
            pl.BlockSpec((t_new, LANES), tile),
            pl.BlockSpec((t_new, D_B), tile),
            pl.BlockSpec((t_new, D_B), tile),
        ],
        out_specs=pl.BlockSpec((t_new, D_B), tile),
        scratch_shapes=[
            pltpu.VMEM((2, n_blocks, 2, HEAD_DIM, PAGE_SIZE), F32),
            pltpu.SemaphoreType.DMA((2,)),
        ],
    )
    return pl.pallas_call(
        functools.partial(_sel_sample_kernel, n_pick=n_pick, n_pages=n_pages, tb=tb),
        grid_spec=grid_spec,
        out_shape=jax.ShapeDtypeStruct((n_seq * t_new, D_B), F32),
        compiler_params=pltpu.CompilerParams(dimension_semantics=("arbitrary",),
                                             vmem_limit_bytes=VMEM_LIMIT),
        name="sel_sample",
    )(sel_flat, pt_flat, cache_t, qr, rows, gates, ocw, szb)


def _merge_kernel(x_ref, a_ref, b_ref, wa_ref, wb_ref, fg_ref, y_ref):
    delta = _dot(a_ref[...].astype(BF16), wa_ref[...]) + _dot(b_ref[...].astype(BF16), wb_ref[...])
    x = x_ref[...] + delta
    ms = jnp.mean(x * x, axis=-1, keepdims=True)
    y_ref[...] = (x * lax.rsqrt(ms + EPS)) * fg_ref[...]


def _merge_call(x2d, a, b, wa, wb, fg):
    n_rows = x2d.shape[0]
    tm = min(TM_MERGE, n_rows)
    row = lambda i: (i, 0)
    const2 = lambda i: (0, 0)
    return pl.pallas_call(
        _merge_kernel,
        grid=(n_rows // tm,),
        in_specs=[
            pl.BlockSpec((tm, D_MODEL), row),
            pl.BlockSpec((tm, D_A), row),
            pl.BlockSpec((tm, D_B), row),
            pl.BlockSpec((D_A, D_MODEL), const2),
            pl.BlockSpec((D_B, D_MODEL), const2),
            pl.BlockSpec((1, D_MODEL), const2),
        ],
        out_specs=pl.BlockSpec((tm, D_MODEL), row),
        out_shape=jax.ShapeDtypeStruct((n_rows, D_MODEL), F32),
        compiler_params=pltpu.CompilerParams(dimension_semantics=("arbitrary",),
                                             vmem_limit_bytes=VMEM_LIMIT),
        name="merge",
    )(x2d, a, b, wa, wb, fg)


def _head_pair_perm(w_cols):
    lead = w_cols.shape[:-1]
    return w_cols.reshape(*lead, N_KV, GQA, HEAD_DIM).swapaxes(-3, -2).reshape(*lead, D_B)


def _prep_w_in(w_in):
    cuts = np.cumsum([D_A, D_A, D_A, D_B, 6 * N_KV * HEAD_DIM, 3 * N_HEADS]).tolist()
    u, v, za, q, kv, g, zb = jnp.split(w_in, cuts, axis=-1)
    g_pad = jnp.pad(g, ((0, 0), (0, LANES - g.shape[1])))
    return jnp.concatenate([u, v, za, _head_pair_perm(q), kv, _head_pair_perm(zb), g_pad], axis=-1).astype(BF16)


def _prep_mix(w_s, b_s, chunk_len):
    reps = MIX_ROWS // chunk_len
    row = np.arange(MIX_ROWS)[:, None]
    col = np.arange(MIX_ROWS)[None, :]
    causal_same_chunk = (row // chunk_len == col // chunk_len) & (col <= row)
    spread = jnp.asarray(np.arange(chunk_len)[:, None] == np.arange(MIX_ROWS)[None, :] % chunk_len, dtype=w_s.dtype)
    wide = jnp.einsum("gts,sc->gtc", w_s[:, :chunk_len, :chunk_len], spread)
    wmix = jnp.where(causal_same_chunk, jnp.tile(wide, (1, reps, 1)), 0).astype(BF16)
    bias = jnp.repeat(b_s[:, :chunk_len].T, HEAD_DIM, axis=1)
    return wmix, jnp.tile(bias, (reps, 1))


def _rope_tables(pos):
    half = HEAD_DIM // 2
    inv = ROPE_THETA ** (-jnp.arange(half, dtype=F32) / half)
    ang = pos.astype(F32)[:, None] * inv
    cos = jnp.tile(jnp.cos(ang), (1, LANES // half))
    sin = jnp.tile(jnp.concatenate([-jnp.sin(ang), jnp.sin(ang)], axis=1), (1, LANES // HEAD_DIM))
    return cos, sin


def _prep_compress(cmp_pos, w_cmp1, b_cmp1, w_cmp2):
    ratio = CMP_BLOCK // CMP_STRIDE
    w1 = w_cmp1.reshape(2, ratio, CMP_STRIDE, HEAD_DIM, CMP_HIDDEN).transpose(0, 2, 3, 1, 4)
    w1 = w1.reshape(2, CMP_STRIDE, HEAD_DIM, ratio * CMP_HIDDEN)
    zeros = jnp.zeros_like(w1)
    wc = jnp.stack([jnp.concatenate([w1, zeros], axis=-1), jnp.concatenate([zeros, w1], axis=-1)], axis=2)
    wc = wc.reshape(2, CMP_STRIDE * LANES, N_KV * ratio * CMP_HIDDEN).astype(BF16)
    pe = cmp_pos.reshape(2, ratio, CMP_STRIDE, 1, HEAD_DIM)
    pe = jnp.broadcast_to(pe, (2, ratio, CMP_STRIDE, N_KV, HEAD_DIM)).reshape(2, ratio, CMP_STRIDE * LANES)
    pe = jnp.pad(pe, ((0, 0), (0, 16 - ratio), (0, 0))).astype(BF16)
    w2 = w_cmp2[:, None]
    zeros2 = jnp.zeros_like(w2)
    w2p = jnp.concatenate([jnp.concatenate([w2, zeros2], axis=-1), jnp.concatenate([zeros2, w2], axis=-1)],
                          axis=1).astype(BF16)
    return wc, pe, b_cmp1.reshape(2, 1, CMP_HIDDEN), w2p


def _overlap_matrix(n_cmp_slots, n_slc, n_lanes):
    start = (np.arange(n_cmp_slots)[:, None] - 1) * CMP_STRIDE
    j = np.arange(n_lanes)[None, :]
    ov = (start <= j * SLC_BLOCK + SLC_BLOCK - 1) & (start + CMP_BLOCK - 1 >= j * SLC_BLOCK)
    ov &= (np.arange(n_cmp_slots)[:, None] >= 1) & (j < n_slc)
    return jnp.asarray(ov, dtype=BF16)


def kernel(x_prompt, x_sample, cache_kv, state_win, page_table, norm_g, w_in, ln_g, ln_b, w_s, b_s,
           cmp_pos, w_cmp1, b_cmp1, w_cmp2, w_out, final_g):
    n_batch, seq, _ = x_prompt.shape
    n_seq, t_new, _ = x_sample.shape
    depth, n_phys = cache_kv.shape[:2]
    n_pages = page_table.shape[1]
    past_len = n_pages * PAGE_SIZE
    win_buf = state_win.shape[2]
    assert depth == 1 and seq % TQ == 0 and n_seq * t_new == MIX_ROWS and win_buf == WINDOW
    assert (past_len + t_new - 1) // SLC_BLOCK == past_len // SLC_BLOCK and past_len % CMP_STRIDE == 0

    w_all = _prep_w_in(w_in[0])
    ng = norm_g[0].reshape(1, D_MODEL)
    lng = ln_g[0].reshape(1, D_A)
    lnb = ln_b[0].reshape(1, D_A)
    fg = final_g.reshape(1, D_MODEL)
    wo = w_out[0]
    wo_a = wo[:D_A].astype(BF16)
    wo_b = _head_pair_perm(wo[D_A:].T).T.astype(BF16)
    wc, pe, b1, w2p = _prep_compress(cmp_pos[0], w_cmp1[0], b_cmp1[0], w_cmp2[0])

    xp = x_prompt.reshape(n_batch * seq, D_MODEL)
    wmix_p, bmix_p = _prep_mix(w_s[0], b_s[0], CHUNK)
    cos_p, sin_p = _rope_tables(jnp.arange(seq))
    ag_p, qc_p, qr_p, rows_p, kvw_pt, ksel_p, kwin_p, gate_p, szb_p, rows_pt = _proj_call(
        xp, ng, w_all, lng, lnb, wmix_p, bmix_p, cos_p, sin_p, BF16, seq)
    pages_p = seq // PAGE_SIZE
    ident = jnp.arange(n_batch * pages_p, dtype=jnp.int32).reshape(n_batch, pages_p)
    cmp_p = _compress_call(rows_p.reshape(n_batch * pages_p, PAGE_SIZE, 4 * LANES), ident, wc, pe, b1, w2p,
                           pages_p, False)
    ov_p = _overlap_matrix(seq // CMP_STRIDE, seq // SLC_BLOCK, LANES)
    bg_p = _attn_prompt_call(qc_p, qr_p, cmp_p, ov_p, ksel_p, kwin_p, gate_p, szb_p, n_batch, seq)
    y_p = _merge_call(xp, ag_p, bg_p, wo_a, wo_b, fg)

    xs = x_sample.reshape(n_seq * t_new, D_MODEL)
    wmix_s, bmix_s = _prep_mix(w_s[0], b_s[0], t_new)
    cos_s, sin_s = _rope_tables(jnp.tile(past_len + jnp.arange(t_new), n_seq))
    ag_s, qc_s, qr_s, rows_s, kvw_s, _, _, gate_s, szb_s, vln_s = _proj_call(
        xs, ng, w_all, lng, lnb, wmix_s, bmix_s, cos_s, sin_s, F32, None)
    cache_t = cache_kv[0].transpose(0, 2, 3, 4, 1)
    cmp_s = _compress_call(cache_t, page_table, wc, pe, b1, w2p, SAMPLE_PAGES_PER_STEP, True)
    n_slc_s = -(-(past_len + t_new) // SLC_BLOCK)
    tb = past_len // SLC_BLOCK
    n_pick = TOP_N - 3
    ov_s = _overlap_matrix(past_len // CMP_STRIDE, n_slc_s, -(-n_slc_s // LANES) * LANES)
    win_prev = state_win[0].reshape(n_seq, win_buf, 2 * LANES)
    ocw_s, sel_s = _cmpwin_sample_call(qc_s, qr_s, cmp_s, ov_s, win_prev, kvw_s, gate_s, t_new, past_len, n_pick)
    sel_flat = sel_s[:, :, :n_pick].reshape(-1)
    bg_s = _sel_sample_call(sel_flat, page_table.reshape(-1), cache_t, qr_s, rows_s, gate_s, ocw_s, szb_s,
                            n_seq, t_new, n_pick, n_pages, tb)
    y_s = _merge_call(xs, ag_s, bg_s, wo_a, wo_b, fg)

    new_kv_p = rows_pt.reshape(n_batch, 4, N_KV, HEAD_DIM, seq).transpose(0, 4, 1, 2, 3)
    new_win_p = kvw_pt[:, :, seq - win_buf:].reshape(n_batch, 2, N_KV, HEAD_DIM, win_buf).transpose(0, 4, 1, 2, 3)
    new_win_s = jnp.concatenate([state_win[0][:, t_new:], kvw_s.reshape(n_seq, t_new, 2, N_KV, HEAD_DIM)], axis=1)
    return (y_p.reshape(n_batch, seq, D_MODEL),
            y_s.reshape(n_seq, t_new, D_MODEL),
            new_kv_p[None],
            new_win_p[None],
            rows_s.reshape(1, n_seq, t_new, 4, N_KV, HEAD_DIM),
            new_win_s[None],
            vln_s.reshape(1, n_seq, t_new, D_A))
```

```python
import functools
import math

import jax
import jax.numpy as jnp
import numpy as np
from jax import lax
from jax.experimental import pallas as pl
from jax.experimental.pallas import tpu as pltpu

F32 = jnp.float32
BF16 = jnp.bfloat16

D_MODEL = 1024
HEAD_DIM = 64
D_A = 512
A_GROUPS = 8
CHUNK = 128
N_HEADS = 8
D_B = 512
N_KV = 2
GQA = 4
CMP_BLOCK = 32
CMP_STRIDE = 16
CMP_HIDDEN = 128
SLC_BLOCK = 64
TOP_N = 16
WINDOW = 512
ROPE_THETA = 10000.0
EPS = 1e-6
SCALE = HEAD_DIM ** -0.5
NEG = -1e30
FORCE = 1e9
PAGE_SIZE = 128

LANES = 128
VMEM_LIMIT = 56 * 1024 * 1024

C_U, C_V, C_ZA, C_Q, C_KV, C_ZB, C_G, C_END = 0, 512, 1024, 1536, 2048, 2816, 3328, 3456

MIX_ROWS = 256
TM = 512
TM_MERGE = 512
DMA_LOOP_UNROLL = 8
SAMPLE_SEQS_PER_STEP = 8
SAMPLE_PAGES_PER_STEP = 64
TQ = 256
SEGS_PER_PAGE = PAGE_SIZE // CMP_STRIDE


def _lane_iota(shape):
    return lax.broadcasted_iota(jnp.int32, shape, len(shape) - 1)


def _row_iota(shape):
    return lax.broadcasted_iota(jnp.int32, shape, len(shape) - 2)


def _div_pow2(x, n):
    assert n & (n - 1) == 0
    return lax.shift_right_logical(x, int(math.log2(n))) if n > 1 else x


def _mod_pow2(x, n):
    assert n & (n - 1) == 0
    return x & (n - 1)


def _dot(a, b):
    return jnp.dot(a, b, preferred_element_type=F32)


def _dot_nt(a, b):
    return lax.dot_general(a, b, (((1,), (1,)), ((), ())), preferred_element_type=F32)


def _rope(x, cos, sin_signed):
    lo = _mod_pow2(_lane_iota(x.shape), HEAD_DIM) < (HEAD_DIM // 2)
    swapped = jnp.where(lo, pltpu.roll(x, LANES - HEAD_DIM // 2, 1), pltpu.roll(x, HEAD_DIM // 2, 1))
    return x * cos + swapped * sin_signed


def _proj_kernel(x_ref, ng_ref, w_ref, lng_ref, lnb_ref, wmix_ref, bmix_ref, cos_ref, sin_ref,
                 ag_ref, qc_ref, qr_ref, rows_ref, kvw_ref, ksel_ref, kwin_ref, gate_ref, szb_ref,
                 extra_ref, *, prompt):
    x = x_ref[...]
    ms = jnp.mean(x * x, axis=-1, keepdims=True)
    hb = ((x * lax.rsqrt(ms + EPS)) * ng_ref[...]).astype(BF16)
    cos = cos_ref[...]
    sin = sin_ref[...]
    pair = 2 * LANES

    def proj(c0, n):
        return _dot(hb, w_ref[:, c0:c0 + n])

    def halves(x2):
        return x2[:, 0:LANES], x2[:, LANES:pair]

    v = jax.nn.gelu(proj(C_V, D_A))
    mu = jnp.mean(v, axis=-1, keepdims=True)
    vc = v - mu
    var = jnp.mean(vc * vc, axis=-1, keepdims=True)
    vln = vc * lax.rsqrt(var + EPS) * lng_ref[...] + lnb_ref[...]
    if not prompt:
        extra_ref[...] = vln
    vb = vln.astype(BF16)
    first_head = _lane_iota((MIX_ROWS, LANES)) < HEAD_DIM
    for gpp in range(A_GROUPS // 4):
        u2 = halves(jax.nn.gelu(proj(C_U + gpp * pair, pair)))
        za2 = halves(proj(C_ZA + gpp * pair, pair))
        for k in range(2):
            gp = 2 * gpp + k
            sl = slice(gp * LANES, (gp + 1) * LANES)
            mixed = jnp.concatenate(
                [jnp.where(first_head, _dot(wmix_ref[2 * gp], vb[r0:r0 + MIX_ROWS, sl]),
                           _dot(wmix_ref[2 * gp + 1], vb[r0:r0 + MIX_ROWS, sl])) + bmix_ref[:, sl]
                 for r0 in range(0, x.shape[0], MIX_ROWS)], axis=0)
            ag_ref[:, sl] = (u2[k] * mixed * jax.nn.silu(za2[k])).astype(ag_ref.dtype)

    for gpp in range(GQA // 2):
        q2 = halves(proj(C_Q + gpp * pair, pair))
        for k in range(2):
            sl = slice((2 * gpp + k) * LANES, (2 * gpp + k + 1) * LANES)
            qc_ref[:, sl] = (q2[k] * SCALE).astype(qc_ref.dtype)
            qr_ref[:, sl] = (_rope(q2[k], cos, sin) * SCALE).astype(qr_ref.dtype)

    kc, vcm = halves(proj(C_KV, pair))
    ks, vs = halves(proj(C_KV + pair, pair))
    kw, vw = halves(proj(C_KV + 2 * pair, pair))
    ks = _rope(ks, cos, sin)
    kw = _rope(kw, cos, sin)
    for i, blk in enumerate((kc, vcm, ks, vs)):
        rows_ref[:, i * LANES:(i + 1) * LANES] = blk
        if prompt:
            extra_ref[0, i * LANES:(i + 1) * LANES, :] = blk.T
    for i, blk in enumerate((kw, vw)):
        if prompt:
            kvw_ref[0, i * LANES:(i + 1) * LANES, :] = blk.T
        else:
            kvw_ref[:, i * LANES:(i + 1) * LANES] = blk
    ksel_ref[:, 0:LANES] = ks.astype(BF16)
    ksel_ref[:, LANES:2 * LANES] = vs.astype(BF16)
    kwin_ref[:, 0:LANES] = kw.astype(BF16)
    kwin_ref[:, LANES:2 * LANES] = vw.astype(BF16)

    gate_ref[...] = jax.nn.sigmoid(proj(C_G, LANES))
    for gpp in range(GQA // 2):
        zb2 = halves(proj(C_ZB + gpp * pair, pair))
        for k in range(2):
            sl = slice((2 * gpp + k) * LANES, (2 * gpp + k + 1) * LANES)
            szb_ref[:, sl] = jax.nn.silu(zb2[k]).astype(szb_ref.dtype)


def _proj_call(x2d, ng, w, lng, lnb, wmix, bmix, cos_t, sin_t, act_dtype, seq_len):
    n_rows = x2d.shape[0]
    tm = min(TM, n_rows)
    n_tiles = n_rows // tm
    pos_tiles = cos_t.shape[0] // tm
    row = lambda i: (i, 0)
    const2 = lambda i: (0, 0)
    out_shapes = [
        jax.ShapeDtypeStruct((n_rows, D_A), act_dtype),
        jax.ShapeDtypeStruct((n_rows, D_B), act_dtype),
        jax.ShapeDtypeStruct((n_rows, D_B), act_dtype),
        jax.ShapeDtypeStruct((n_rows, 4 * LANES), F32),
        jax.ShapeDtypeStruct((n_rows, 2 * LANES), F32),
        jax.ShapeDtypeStruct((n_rows, 2 * LANES), BF16),
        jax.ShapeDtypeStruct((n_rows, 2 * LANES), BF16),
        jax.ShapeDtypeStruct((n_rows, LANES), F32),
        jax.ShapeDtypeStruct((n_rows, D_B), act_dtype),
        jax.ShapeDtypeStruct((n_rows, D_A), F32),
    ]
    out_specs = [pl.BlockSpec((tm, s.shape[1]), row) for s in out_shapes]
    if seq_len is not None:
        tiles_per_seq = seq_len // tm
        dim_major = lambda i: (i // tiles_per_seq, 0, i % tiles_per_seq)
        for idx, width in ((4, 2 * LANES), (9, 4 * LANES)):
            out_shapes[idx] = jax.ShapeDtypeStruct((n_rows // seq_len, width, seq_len), F32)
            out_specs[idx] = pl.BlockSpec((1, width, tm), dim_major)
    return pl.pallas_call(
        functools.partial(_proj_kernel, prompt=seq_len is not None),
        grid=(n_tiles,),
        in_specs=[
            pl.BlockSpec((tm, D_MODEL), row),
            pl.BlockSpec((1, D_MODEL), const2),
            pl.BlockSpec((D_MODEL, C_END), const2),
            pl.BlockSpec((1, D_A), const2),
            pl.BlockSpec((1, D_A), const2),
            pl.BlockSpec((A_GROUPS, MIX_ROWS, MIX_ROWS), lambda i: (0, 0, 0)),
            pl.BlockSpec((MIX_ROWS, D_A), const2),
            pl.BlockSpec((tm, LANES), lambda i: (i % pos_tiles, 0)),
            pl.BlockSpec((tm, LANES), lambda i: (i % pos_tiles, 0)),
        ],
        out_specs=out_specs,
        out_shape=out_shapes,
        compiler_params=pltpu.CompilerParams(dimension_semantics=("arbitrary",),
                                             vmem_limit_bytes=VMEM_LIMIT),
        name="proj",
    )(x2d, ng, w, lng, lnb, wmix, bmix, cos_t, sin_t)


def _compress_kernel(pt_ref, src_ref, wc_ref, pe_ref, b1_ref, w2_ref, out_ref, buf, sem, carry,
                     *maybe_stage, pages_per_step):
    transposed_src = bool(maybe_stage)
    n_steps = pl.num_programs(0) * pl.num_programs(1)
    step = pl.program_id(0) * pl.num_programs(1) + pl.program_id(1)
    m = pages_per_step * SEGS_PER_PAGE

    def page_copies(step_idx, slot, i):
        phys = pt_ref[step_idx * pages_per_step + i]
        if transposed_src:
            return [pltpu.make_async_copy(src_ref.at[phys, pl.ds(0, 2)], maybe_stage[0].at[slot, i], sem.at[slot])]
        return [pltpu.make_async_copy(src_ref.at[phys, :, pl.ds(c * LANES, LANES)],
                                      buf.at[slot, c, pl.ds(i * PAGE_SIZE, PAGE_SIZE), :],
                                      sem.at[slot]) for c in range(2)]

    def start_step(step_idx, slot):
        def body(i, carry_):
            for cp in page_copies(step_idx, slot, i):
                cp.start()
            return carry_
        lax.fori_loop(0, pages_per_step, body, 0, unroll=DMA_LOOP_UNROLL)

    def wait_step(step_idx, slot):
        def body(i, carry_):
            for cp in page_copies(step_idx, slot, i):
                cp.wait()
            return carry_
        lax.fori_loop(0, pages_per_step, body, 0, unroll=DMA_LOOP_UNROLL)

    slot = step % 2

    first_of_seq = pl.program_id(1) == 0
    first_row = _row_iota((m, LANES)) == 0

    def compress_rows(rows_slot, seg_pitch):
        for c in range(2):
            pieces = [buf[rows_slot, c, pl.ds(j, m, stride=seg_pitch), :].astype(BF16)
                      for j in range(CMP_STRIDE)]
            part = _dot(jnp.concatenate(pieces, axis=1), wc_ref[c])
            pe_part = _dot(pe_ref[c], wc_ref[c])
            acc_out = None
            for h in range(N_KV):
                lo = slice(h * 2 * CMP_HIDDEN, h * 2 * CMP_HIDDEN + CMP_HIDDEN)
                hi = slice(h * 2 * CMP_HIDDEN + CMP_HIDDEN, (h + 1) * 2 * CMP_HIDDEN)
                bias = pe_part[0:1, lo] + pe_part[1:2, hi] + b1_ref[c]
                part0 = part[:, lo]
                prev_last = jnp.where(first_of_seq, 0.0, carry[c, h, 0:1, :])
                prev0 = jnp.where(first_row, prev_last, pltpu.roll(part0, 1, 0))
                carry[c, h, 0:1, :] = part0[m - 1:m, :]
                hid = jax.nn.gelu(prev0 + part[:, hi] + bias).astype(BF16)
                contrib = _dot(hid, w2_ref[c, h])
                acc_out = contrib if acc_out is None else acc_out + contrib
            out_ref[0, :, c * LANES:(c + 1) * LANES] = acc_out.astype(out_ref.dtype)

    if transposed_src:
        stage = maybe_stage[0]
        seg_pitch = CMP_STRIDE + 1

        def transpose_step(stage_slot):
            for i in range(pages_per_step):
                for c in range(2):
                    rows = stage[stage_slot, i, c].reshape(N_KV * HEAD_DIM, PAGE_SIZE).T
                    for s in range(SEGS_PER_PAGE):
                        r0 = (i * SEGS_PER_PAGE + s) * seg_pitch
                        buf[stage_slot, c, r0:r0 + CMP_STRIDE, :] = rows[s * CMP_STRIDE:(s + 1) * CMP_STRIDE]

        @pl.when(step == 0)
        def _():
            start_step(0, 0)
            start_step(1, 1)
            wait_step(0, 0)
            transpose_step(0)

        @pl.when(step + 1 < n_steps)
        def _():
            wait_step(step + 1, (step + 1) % 2)

        @pl.when(step + 2 < n_steps)
        def _():
            start_step(step + 2, slot)

        for parity in range(2):
            @pl.when(slot == parity)
            def _():
                transpose_step(1 - parity)
                compress_rows(parity, seg_pitch)
    else:
        @pl.when(step == 0)
        def _():
            start_step(0, 0)

        @pl.when(step + 1 < n_steps)
        def _():
            start_step(step + 1, (step + 1) % 2)

        wait_step(step, slot)
        compress_rows(slot, CMP_STRIDE)


def _compress_call(src, page_table, wc, pe, b1, w2p, pages_per_step, transposed_src):
    n_seq, n_pages = page_table.shape
    n_chunks = n_pages // pages_per_step
    m = pages_per_step * SEGS_PER_PAGE
    assert not transposed_src or n_seq * n_chunks >= 2
    seg_pitch = CMP_STRIDE + 1 if transposed_src else CMP_STRIDE
    scratch = [
        pltpu.VMEM((2, 2, m * seg_pitch, LANES), F32),
        pltpu.SemaphoreType.DMA((2,)),
        pltpu.VMEM((2, N_KV, 8, CMP_HIDDEN), F32),
    ]
    if transposed_src:
        scratch.append(pltpu.VMEM((2, pages_per_step, 2, N_KV, HEAD_DIM, PAGE_SIZE), F32))
    grid_spec = pltpu.PrefetchScalarGridSpec(
        num_scalar_prefetch=1,
        grid=(n_seq, n_chunks),
        in_specs=[
            pl.BlockSpec(memory_space=pl.ANY),
            pl.BlockSpec((2, CMP_STRIDE * LANES, 4 * CMP_HIDDEN), lambda b, k, pt: (0, 0, 0)),
            pl.BlockSpec((2, 16, CMP_STRIDE * LANES), lambda b, k, pt: (0, 0, 0)),
            pl.BlockSpec((2, 1, CMP_HIDDEN), lambda b, k, pt: (0, 0, 0)),
            pl.BlockSpec((2, N_KV, CMP_HIDDEN, LANES), lambda b, k, pt: (0, 0, 0, 0)),
        ],
        out_specs=pl.BlockSpec((1, m, 2 * LANES), lambda b, k, pt: (b, k, 0)),
        scratch_shapes=scratch,
    )
    return pl.pallas_call(
        functools.partial(_compress_kernel, pages_per_step=pages_per_step),
        grid_spec=grid_spec,
        out_shape=jax.ShapeDtypeStruct((n_seq, n_pages * SEGS_PER_PAGE, 2 * LANES), BF16),
        compiler_params=pltpu.CompilerParams(dimension_semantics=("arbitrary", "arbitrary"),
                                             vmem_limit_bytes=VMEM_LIMIT),
        name="compress",
    )(page_table.reshape(-1), src, wc, pe, b1, w2p)


def _softmax_rows(s, mask):
    s = jnp.where(mask, s, NEG)
    e = jnp.where(mask, jnp.exp(s - jnp.max(s, axis=-1, keepdims=True)), 0.0)
    return e / jnp.maximum(jnp.sum(e, axis=-1, keepdims=True), 1.0)


def _dot_split(p, w):
    hi = p.astype(BF16)
    lo = (p - hi.astype(F32)).astype(BF16)
    return _dot(hi, w) + _dot(lo, w)


def _gate_col(gates, idx):
    return gates[:, idx:idx + 1]


def _attn_prompt_kernel(qc_ref, qr_ref, cmp_ref, ov_ref, eg_ref, ksel_ref, kwin_ref, gate_ref, szb_ref,
                        out_ref, q_sc, m_sc, acc_sc, cmp_sc):
    qi = pl.program_id(1)
    lane = _lane_iota((TQ, LANES))
    lo_half = lane < HEAD_DIM
    r_minus_c = _row_iota((TQ, TQ)) - _lane_iota((TQ, TQ))
    causal_bias = jnp.where(r_minus_c >= 0, 0.0, NEG)
    far_bias = jnp.where(r_minus_c < 0, 0.0, NEG)
    pair_bias = jnp.concatenate([jnp.zeros((TQ, TQ), F32), causal_bias], axis=1)
    kcmp = cmp_ref[0, :, 0:LANES]
    vcmp = cmp_ref[0, :, LANES:2 * LANES]
    n_slc = ksel_ref.shape[0] // SLC_BLOCK
    blocks_per_tile = TQ // SLC_BLOCK

    def flash_tile(br, h, k_tile, v_aug, bias):
        width = k_tile.shape[0]
        s = _dot_nt(q_sc[h], k_tile)
        if bias is not None:
            s = (s.reshape(GQA, TQ, width) + bias[None]).reshape(GQA * TQ, width)
        m_prev = m_sc[h]
        m_new = jnp.maximum(m_prev, jnp.max(s, axis=-1, keepdims=True))
        alpha = jnp.exp(m_prev - m_new)
        p = jnp.exp(s - jnp.concatenate([m_new] * (width // LANES), axis=1))
        acc_sc[br, h] = alpha * acc_sc[br, h] + _dot(p.astype(BF16), v_aug)
        m_sc[h] = m_new

    def flash_init(br):
        m_sc[...] = jnp.full(m_sc.shape, NEG, F32)
        acc_sc[br] = jnp.zeros(acc_sc.shape[1:], F32)

    n_hg = N_KV * GQA
    q_cmp = jnp.concatenate([jnp.where(_div_pow2(lane, HEAD_DIM) == h, qc_ref[:, g * LANES:(g + 1) * LANES], 0)
                             for h in range(N_KV) for g in range(GQA)], axis=0)
    lane_all = _lane_iota((n_hg * TQ, LANES))
    tq_all = qi * TQ + _mod_pow2(_row_iota((n_hg * TQ, LANES)), TQ)
    cmp_valid = (lane_all >= 1) & (lane_all * CMP_STRIDE + (CMP_STRIDE - 1) <= tq_all)
    p_all = _softmax_rows(_dot_nt(q_cmp, kcmp), cmp_valid)
    o_all = _dot(p_all.astype(BF16), vcmp)
    for g in range(GQA):
        cmp_sc[:, g * LANES:(g + 1) * LANES] = jnp.where(
            lo_half, o_all[g * TQ:(g + 1) * TQ], o_all[(GQA + g) * TQ:(GQA + g + 1) * TQ])
    p4 = p_all.reshape(N_KV, GQA, TQ, LANES)
    p_sum = sum(p4[:, g] for g in range(GQA)).reshape(N_KV * TQ, LANES)
    imp = _dot_split(p_sum, ov_ref[...])
    lane_ht = _lane_iota((N_KV * TQ, LANES))
    tq_ht = qi * TQ + _mod_pow2(_row_iota((N_KV * TQ, LANES)), TQ)
    tb = _div_pow2(tq_ht, SLC_BLOCK)
    forced = (lane_ht == 0) | (lane_ht == tb) | (lane_ht == tb - 1)
    valid = lane_ht * SLC_BLOCK <= tq_ht
    score = jnp.where(forced, FORCE, jnp.where(valid, imp, NEG))
    score_t = score.T[0:n_slc, :]
    blk = _row_iota((n_slc, N_KV * TQ))
    rank = jnp.zeros((n_slc, N_KV * TQ), jnp.int32)
    for j in range(n_slc):
        other = score_t[j:j + 1, :]
        ahead = (other > score_t) | ((other == score_t) & (blk > j))
        rank = rank + ahead.astype(jnp.int32)
    sel_bias_t = jnp.where(rank < TOP_N, 0.0, NEG)

    for h in range(N_KV):
        keep = _div_pow2(lane, HEAD_DIM) == h
        other_off = HEAD_DIM * (1 - h)
        pieces = [jnp.zeros((other_off, TQ), F32)] if other_off else []
        pieces += [sel_bias_t[:, h * TQ:(h + 1) * TQ], jnp.zeros((LANES - other_off - n_slc, TQ), F32)]
        sel_bias = jnp.concatenate(pieces, axis=0).T.astype(BF16)
        for g in range(GQA):
            sl = slice(g * LANES, (g + 1) * LANES)
            q_sc[h, g * TQ:(g + 1) * TQ, :] = jnp.where(keep, qr_ref[:, sl], sel_bias)

    def span_operands(kv_ref, kt, n_tiles, h, other_lanes):
        k = kv_ref[pl.ds(kt * TQ, n_tiles * TQ), 0:LANES]
        v = kv_ref[pl.ds(kt * TQ, n_tiles * TQ), LANES:2 * LANES]
        own = _div_pow2(_lane_iota(k.shape), HEAD_DIM) == h
        return jnp.where(own, k, other_lanes(k.shape)), jnp.where(own, v, 1.0)

    def sel_span(kt, n_tiles, bias):
        for h in range(N_KV):
            def indicator(shape, h=h):
                first = HEAD_DIM * (1 - h) + kt * blocks_per_tile
                hit = _lane_iota(shape) == first + _div_pow2(_row_iota(shape), SLC_BLOCK)
                return jnp.where(hit, 1, 0).astype(BF16)
            flash_tile(0, h, *span_operands(ksel_ref, kt, n_tiles, h, indicator), bias)

    flash_init(0)

    def sel_pair(i, carry_):
        sel_span(2 * i, 2, None)
        return carry_
    lax.fori_loop(0, _div_pow2(jnp.maximum(qi - 1, 0), 2), sel_pair, 0)

    @pl.when((qi >= 2) & (_mod_pow2(qi, 2) == 0))
    def _():
        sel_span(qi - 2, 1, None)

    @pl.when(qi >= 1)
    def _():
        sel_span(qi - 1, 2, pair_bias)

    @pl.when(qi == 0)
    def _():
        sel_span(qi, 1, causal_bias)

    def win_span(kt, n_tiles, bias):
        for h in range(N_KV):
            flash_tile(1, h, *span_operands(kwin_ref, kt, n_tiles, h, lambda shape: jnp.zeros(shape, BF16)), bias)

    flash_init(1)
    far = WINDOW // TQ
    assert far == 2

    @pl.when(qi >= far)
    def _():
        win_span(qi - far, 1, far_bias)

    @pl.when(qi >= 1)
    def _():
        win_span(qi - 1, 2, pair_bias)

    @pl.when(qi == 0)
    def _():
        win_span(qi, 1, causal_bias)

    gates_x = _dot_split(gate_ref[...], eg_ref[...])
    for g in range(GQA):
        rows = slice(g * TQ, (g + 1) * TQ)
        sl = slice(g * LANES, (g + 1) * LANES)
        o = gates_x[:, (g * 3) * LANES:(g * 3 + 1) * LANES] * cmp_sc[:, sl]
        for br in range(2):
            a0 = acc_sc[br, 0, rows, :]
            a1 = acc_sc[br, 1, rows, :]
            num = jnp.where(lo_half, a0, a1)
            den = pltpu.roll(jnp.where(lo_half, a1, a0), HEAD_DIM, 1)
            o = o + gates_x[:, (g * 3 + 1 + br) * LANES:(g * 3 + 2 + br) * LANES] * (num / den)
        out_ref[:, sl] = (o * szb_ref[:, sl].astype(F32)).astype(out_ref.dtype)


def _gate_expansion():
    eg = np.zeros((LANES, GQA * 3 * LANES), np.float32)
    for h in range(N_KV):
        for g in range(GQA):
            for br in range(3):
                c0 = (g * 3 + br) * LANES + h * HEAD_DIM
                eg[h * GQA * 3 + g * 3 + br, c0:c0 + HEAD_DIM] = 1.0
    return jnp.asarray(eg, dtype=BF16)


def _attn_prompt_call(qc, qr, cmp_p, ov, ksel, kwin, gates, szb, n_batch, seq):
    nq = seq // TQ
    tile = lambda b, q: (b * nq + q, 0)
    whole = lambda b, q: (b, 0)
    return pl.pallas_call(
        _attn_prompt_kernel,
        grid=(n_batch, nq),
        in_specs=[
            pl.BlockSpec((TQ, D_B), tile),
            pl.BlockSpec((TQ, D_B), tile),
            pl.BlockSpec((1, LANES, 2 * LANES), lambda b, q: (b, 0, 0)),
            pl.BlockSpec((LANES, LANES), lambda b, q: (0, 0)),
            pl.BlockSpec((LANES, GQA * 3 * LANES), lambda b, q: (0, 0)),
            pl.BlockSpec((seq, 2 * LANES), whole),
            pl.BlockSpec((seq, 2 * LANES), whole),
            pl.BlockSpec((TQ, LANES), tile),
            pl.BlockSpec((TQ, D_B), tile),
        ],
        out_specs=pl.BlockSpec((TQ, D_B), tile),
        out_shape=jax.ShapeDtypeStruct((n_batch * seq, D_B), BF16),
        scratch_shapes=[
            pltpu.VMEM((N_KV, GQA * TQ, LANES), BF16),
            pltpu.VMEM((N_KV, GQA * TQ, LANES), F32),
            pltpu.VMEM((2, N_KV, GQA * TQ, LANES), F32),
            pltpu.VMEM((TQ, D_B), F32),
        ],
        compiler_params=pltpu.CompilerParams(dimension_semantics=("arbitrary", "arbitrary"),
                                             vmem_limit_bytes=VMEM_LIMIT),
        name="attn_prompt",
    )(qc, qr, cmp_p, ov, _gate_expansion(), ksel, kwin, gates, szb)


def _cmpwin_sample_kernel(qc_ref, qr_ref, cmp_ref, ov_ref, win_ref, kvw_ref, gate_ref,
                          ocw_ref, sel_ref, need_ref, *, past_len, n_pick, t_new):
    seqs = cmp_ref.shape[0]
    n_rows = N_KV * GQA * t_new
    lane = _lane_iota((t_new, LANES))
    n_cmp = cmp_ref.shape[1]
    row_t = _mod_pow2(_row_iota((n_rows, n_cmp)), t_new)
    slot = _lane_iota((n_rows, n_cmp))
    cmp_valid = (slot >= 1) & (slot * CMP_STRIDE + (CMP_STRIDE - 1) <= past_len + row_t)
    n_buf = win_ref.shape[1]
    dist = n_buf + _mod_pow2(_row_iota((n_rows, n_buf)), t_new) - _lane_iota((n_rows, n_buf))
    win_valid = (dist >= 0) & (dist < WINDOW)
    row_t1 = _mod_pow2(_row_iota((n_rows, 1)), t_new)

    def one_sequence(sq):
        rows_sq = slice(sq * t_new, (sq + 1) * t_new)
        gates = gate_ref[rows_sq, :]

        def q_rows(ref):
            return jnp.concatenate(
                [jnp.where(_div_pow2(lane, HEAD_DIM) == h, ref[rows_sq, g * LANES:(g + 1) * LANES], 0).astype(BF16)
                 for h in range(N_KV) for g in range(GQA)], axis=0)

        p = _softmax_rows(_dot_nt(q_rows(qc_ref), cmp_ref[sq, :, 0:LANES]), cmp_valid)
        o_cmp = _dot(p.astype(BF16), cmp_ref[sq, :, LANES:2 * LANES])
        p_sum = jnp.concatenate(
            [sum(p[(h * GQA + g) * t_new:(h * GQA + g + 1) * t_new] for g in range(GQA)) for h in range(N_KV)],
            axis=0)
        imp = _dot_split(p_sum, ov_ref[...])

        qr = q_rows(qr_ref)
        k_buf = win_ref[sq, :, 0:LANES].astype(BF16)
        v_buf = win_ref[sq, :, LANES:2 * LANES].astype(BF16)
        s_buf = jnp.where(win_valid, _dot_nt(qr, k_buf), NEG)
        k_new = kvw_ref[rows_sq, 0:LANES].astype(BF16).astype(F32)
        v_new = kvw_ref[rows_sq, LANES:2 * LANES].astype(BF16).astype(F32)
        qr32 = qr.astype(F32)
        s_new = [jnp.where(row_t1 >= i, jnp.sum(qr32 * k_new[i:i + 1, :], axis=-1, keepdims=True), NEG)
                 for i in range(t_new)]
        m = jnp.max(s_buf, axis=-1, keepdims=True)
        for s in s_new:
            m = jnp.maximum(m, s)
        e_buf = jnp.exp(s_buf - m)
        e_new = [jnp.exp(s - m) for s in s_new]
        denom = jnp.sum(e_buf, axis=-1, keepdims=True) + sum(e_new)
        o_win = _dot(e_buf.astype(BF16), v_buf)
        for i in range(t_new):
            o_win = o_win + e_new[i].astype(BF16).astype(F32) * v_new[i:i + 1, :]
        o_win = o_win / denom

        for g in range(GQA):
            parts = []
            for h in range(N_KV):
                rows = slice((h * GQA + g) * t_new, (h * GQA + g + 1) * t_new)
                base = h * GQA * 3 + g * 3
                parts.append(_gate_col(gates, base) * o_cmp[rows] + _gate_col(gates, base + 2) * o_win[rows])
            ocw_ref[rows_sq, g * LANES:(g + 1) * LANES] = jnp.where(lane < HEAD_DIM, parts[0], parts[1])
        return imp

    imp = jnp.concatenate([one_sequence(sq) for sq in range(seqs)], axis=0)
    blk = _lane_iota(imp.shape)
    tb = (past_len + t_new - 1) // SLC_BLOCK
    candidate = (blk >= 1) & (blk < tb - 1)
    score = jnp.where(candidate, imp, -1.0)
    out_lane = _lane_iota((imp.shape[0], LANES))
    picks = jnp.zeros((imp.shape[0], LANES), jnp.int32)
    blocks_per_page = PAGE_SIZE // SLC_BLOCK
    page_used = jnp.zeros((imp.shape[0], LANES), jnp.int32)
    for k in range(n_pick):
        best = jnp.max(score, axis=-1, keepdims=True)
        idx = jnp.min(jnp.where(score == best, blk, 1 << 20), axis=-1, keepdims=True)
        picks = jnp.where(out_lane == k, idx, picks)
        page_used = jnp.where(out_lane == _div_pow2(idx, blocks_per_page), 1, page_used)
        score = jnp.where(blk == idx, -2.0, score)
    sel_ref[...] = picks.reshape(sel_ref.shape)
    used = jnp.max(page_used.reshape(seqs * N_KV, t_new, LANES), axis=1)
    page = _lane_iota(used.shape)
    used = jnp.where((page == 0) | (page == (tb - 1) // blocks_per_page), 1, used)
    need_ref[...] = jnp.broadcast_to(used.reshape(seqs, N_KV, 1, LANES), need_ref.shape)


def _cmpwin_sample_call(qc, qr, cmp_s, ov, win, kvw, gates, t_new, past_len, n_pick):
    n_seq = cmp_s.shape[0]
    seqs = SAMPLE_SEQS_PER_STEP
    assert n_seq % seqs == 0
    tile = lambda b: (b, 0)
    per_seq = lambda b: (b, 0, 0)
    return pl.pallas_call(
        functools.partial(_cmpwin_sample_kernel, past_len=past_len, n_pick=n_pick, t_new=t_new),
        grid=(n_seq // seqs,),
        in_specs=[
            pl.BlockSpec((seqs * t_new, D_B), tile),
            pl.BlockSpec((seqs * t_new, D_B), tile),
            pl.BlockSpec((seqs, cmp_s.shape[1], 2 * LANES), per_seq),
            pl.BlockSpec(ov.shape, lambda b: (0, 0)),
            pl.BlockSpec((seqs, win.shape[1], 2 * LANES), per_seq),
            pl.BlockSpec((seqs * t_new, 2 * LANES), tile),
            pl.BlockSpec((seqs * t_new, LANES), tile),
        ],
        out_specs=[
            pl.BlockSpec((seqs * t_new, D_B), tile),
            pl.BlockSpec((seqs, N_KV * t_new, LANES), per_seq),
            pl.BlockSpec((seqs, N_KV, 8, LANES), lambda b: (b, 0, 0, 0)),
        ],
        out_shape=[
            jax.ShapeDtypeStruct((n_seq * t_new, D_B), F32),
            jax.ShapeDtypeStruct((n_seq, N_KV * t_new, LANES), jnp.int32),
            jax.ShapeDtypeStruct((n_seq, N_KV, 8, LANES), jnp.int32),
        ],
        compiler_params=pltpu.CompilerParams(dimension_semantics=("arbitrary",),
                                             vmem_limit_bytes=VMEM_LIMIT),
        name="cmpwin_sample",
    )(qc, qr, cmp_s, ov, win, kvw, gates)


def _sel_sample_kernel(sel_ref, need_ref, pt_ref, cache_ref, qr_ref, rows_ref, gate_ref, ocw_ref, szb_ref,
                       out_ref, buf, sem, *, n_pick, n_pages, tb):
    t_new = qr_ref.shape[0]
    n_own = t_new * n_pick
    n_steps = pl.num_programs(0)
    step = pl.program_id(0)
    blocks_per_page = PAGE_SIZE // SLC_BLOCK

    def for_all_blocks(seq, slot, fn):
        for h in range(N_KV):
            def body(p, carry_, h=h):
                @pl.when(need_ref[(seq * N_KV + h) * n_pages + p] != 0)
                def _():
                    fn(pltpu.make_async_copy(cache_ref.at[pt_ref[seq * n_pages + p], pl.ds(2, 2), h],
                                             buf.at[slot, h * n_pages + p], sem.at[slot]))
                return carry_
            lax.fori_loop(0, n_pages, body, 0, unroll=DMA_LOOP_UNROLL)

    @pl.when(step == 0)
    def _():
        for_all_blocks(0, 0, lambda cp: cp.start())

    @pl.when(step + 1 < n_steps)
    def _():
        for_all_blocks(step + 1, (step + 1) % 2, lambda cp: cp.start())

    slot = step % 2
    n_copies = lax.fori_loop(0, N_KV * n_pages, lambda i, acc: acc + need_ref[step * N_KV * n_pages + i], 0,
                             unroll=DMA_LOOP_UNROLL)
    for bit in range((N_KV * n_pages).bit_length()):
        @pl.when((n_copies & (1 << bit)) != 0)
        def _():
            pltpu.make_async_copy(cache_ref.at[pl.ds(0, 1 << bit), pl.ds(2, 2), 0],
                                  buf.at[slot, pl.ds(0, 1 << bit)], sem.at[slot]).wait()

    n_rows = GQA * t_new
    half = _div_pow2(_lane_iota((1, LANES)), SLC_BLOCK)
    row_t = _mod_pow2(_row_iota((n_rows, 1)), t_new)
    gates = gate_ref[...]

    def one_head(h):
        def head_half(x):
            return x[:, h * HEAD_DIM:(h + 1) * HEAD_DIM]

        qr = jnp.concatenate([head_half(qr_ref[:, g * LANES:(g + 1) * LANES]) for g in range(GQA)], axis=0)
        qrb = qr.astype(BF16)
        qr32 = qrb.astype(F32)

        def slabs(blocks, kind):
            return jnp.concatenate(
                [buf[slot, h * n_pages + _div_pow2(blk, blocks_per_page), kind].astype(BF16) for blk in blocks],
                axis=1)

        def own_blocks(t):
            return [sel_ref[(step * N_KV + h) * n_own + t * n_pick + k] for k in range(n_pick)]

        bias_sh = jnp.concatenate(
            [jnp.where(half == blk % blocks_per_page, 0.0, NEG) for blk in (0, tb - 1)], axis=1)
        s_sh = _dot(qrb, slabs((0, tb - 1), 0)) + bias_sh
        k_new = head_half(rows_ref[:, 2 * LANES:3 * LANES]).astype(BF16).astype(F32)
        v_new = head_half(rows_ref[:, 3 * LANES:4 * LANES]).astype(BF16).astype(F32)
        s_new = [jnp.where(row_t >= i, jnp.sum(qr32 * k_new[i:i + 1, :], axis=-1, keepdims=True), NEG)
                 for i in range(t_new)]
        own_keys = n_pick * PAGE_SIZE
        s_own = jnp.zeros((n_rows, own_keys), F32)
        for t in range(t_new):
            bias_t = jnp.concatenate(
                [jnp.where(half == _mod_pow2(blk, blocks_per_page), 0.0, NEG) for blk in own_blocks(t)], axis=1)
            s_t = _dot(qrb, slabs(own_blocks(t), 0)) + bias_t
            s_own = jnp.where(row_t == t, s_t, s_own)

        m = jnp.maximum(jnp.max(s_sh, axis=-1, keepdims=True), jnp.max(s_own, axis=-1, keepdims=True))
        for s in s_new:
            m = jnp.maximum(m, s)
        e_sh = jnp.exp(s_sh - m)
        e_own = jnp.exp(s_own - m)
        e_new = [jnp.exp(s - m) for s in s_new]
        denom = jnp.sum(e_sh, axis=-1, keepdims=True) + jnp.sum(e_own, axis=-1, keepdims=True) + sum(e_new)
        o = _dot_nt(e_sh.astype(BF16), slabs((0, tb - 1), 1))
        for t in range(t_new):
            o = o + _dot_nt(jnp.where(row_t == t, e_own, 0.0).astype(BF16), slabs(own_blocks(t), 1))
        for i in range(t_new):
            o = o + e_new[i].astype(BF16).astype(F32) * v_new[i:i + 1, :]
        o = o / denom
        return [_gate_col(gates, h * GQA * 3 + g * 3 + 1) * o[g * t_new:(g + 1) * t_new] for g in range(GQA)]

    gated = [one_head(h) for h in range(N_KV)]
    for g in range(GQA):
        sl = slice(g * LANES, (g + 1) * LANES)
        o_slc = jnp.concatenate([gated[h][g] for h in range(N_KV)], axis=1)
        out_ref[:, sl] = (ocw_ref[:, sl] + o_slc) * szb_ref[:, sl]


def _sel_sample_call(sel_flat, need_flat, pt_flat, cache_t, qr, rows, gates, ocw, szb, n_seq, t_new, n_pick, n_pages,
                     tb):
    tile = lambda b, sel, need, pt: (b, 0)
    grid_spec = pltpu.PrefetchScalarGridSpec(
        num_scalar_prefetch=3,
        grid=(n_seq,),
        in_specs=[
            pl.BlockSpec(memory_space=pl.ANY),
            pl.BlockSpec((t_new, D_B), tile),
            pl.BlockSpec((t_new, 4 * LANES), tile),
            pl.BlockSpec((t_new, LANES), tile),
            pl.BlockSpec((t_new, D_B), tile),
            pl.BlockSpec((t_new, D_B), tile),
        ],
        out_specs=pl.BlockSpec((t_new, D_B), tile),
        scratch_shapes=[
            pltpu.VMEM((2, N_KV * n_pages, 2, HEAD_DIM, PAGE_SIZE), F32),
            pltpu.SemaphoreType.DMA((2,)),
        ],
    )
    return pl.pallas_call(
        functools.partial(_sel_sample_kernel, n_pick=n_pick, n_pages=n_pages, tb=tb),
        grid_spec=grid_spec,
        out_shape=jax.ShapeDtypeStruct((n_seq * t_new, D_B), F32),
        compiler_params=pltpu.CompilerParams(dimension_semantics=("arbitrary",),
                                             vmem_limit_bytes=VMEM_LIMIT),
        name="sel_sample",
    )(sel_flat, need_flat, pt_flat, cache_t, qr, rows, gates, ocw, szb)


def _merge_kernel(x_ref, a_ref, b_ref, wa_ref, wb_ref, fg_ref, y_ref):
    delta = _dot(a_ref[...].astype(BF16), wa_ref[...]) + _dot(b_ref[...].astype(BF16), wb_ref[...])
    x = x_ref[...] + delta
    ms = jnp.mean(x * x, axis=-1, keepdims=True)
    y_ref[...] = (x * lax.rsqrt(ms + EPS)) * fg_ref[...]


def _merge_call(x2d, a, b, wa, wb, fg):
    n_rows = x2d.shape[0]
    tm = min(TM_MERGE, n_rows)
    row = lambda i: (i, 0)
    const2 = lambda i: (0, 0)
    return pl.pallas_call(
        _merge_kernel,
        grid=(n_rows // tm,),
        in_specs=[
            pl.BlockSpec((tm, D_MODEL), row),
            pl.BlockSpec((tm, D_A), row),
            pl.BlockSpec((tm, D_B), row),
            pl.BlockSpec((D_A, D_MODEL), const2),
            pl.BlockSpec((D_B, D_MODEL), const2),
            pl.BlockSpec((1, D_MODEL), const2),
        ],
        out_specs=pl.BlockSpec((tm, D_MODEL), row),
        out_shape=jax.ShapeDtypeStruct((n_rows, D_MODEL), F32),
        compiler_params=pltpu.CompilerParams(dimension_semantics=("arbitrary",),
                                             vmem_limit_bytes=VMEM_LIMIT),
        name="merge",
    )(x2d, a, b, wa, wb, fg)


def _head_pair_perm(w_cols):
    lead = w_cols.shape[:-1]
    return w_cols.reshape(*lead, N_KV, GQA, HEAD_DIM).swapaxes(-3, -2).reshape(*lead, D_B)


def _prep_w_in(w_in):
    cuts = np.cumsum([D_A, D_A, D_A, D_B, 6 * N_KV * HEAD_DIM, 3 * N_HEADS]).tolist()
    u, v, za, q, kv, g, zb = jnp.split(w_in, cuts, axis=-1)
    g_pad = jnp.pad(g, ((0, 0), (0, LANES - g.shape[1])))
    return jnp.concatenate([u, v, za, _head_pair_perm(q), kv, _head_pair_perm(zb), g_pad], axis=-1).astype(BF16)


def _prep_mix(w_s, b_s, chunk_len):
    reps = MIX_ROWS // chunk_len
    row = np.arange(MIX_ROWS)[:, None]
    col = np.arange(MIX_ROWS)[None, :]
    causal_same_chunk = (row // chunk_len == col // chunk_len) & (col <= row)
    spread = jnp.asarray(np.arange(chunk_len)[:, None] == np.arange(MIX_ROWS)[None, :] % chunk_len, dtype=w_s.dtype)
    wide = jnp.einsum("gts,sc->gtc", w_s[:, :chunk_len, :chunk_len], spread)
    wmix = jnp.where(causal_same_chunk, jnp.tile(wide, (1, reps, 1)), 0).astype(BF16)
    bias = jnp.repeat(b_s[:, :chunk_len].T, HEAD_DIM, axis=1)
    return wmix, jnp.tile(bias, (reps, 1))


def _rope_tables(pos):
    half = HEAD_DIM // 2
    inv = ROPE_THETA ** (-jnp.arange(half, dtype=F32) / half)
    ang = pos.astype(F32)[:, None] * inv
    cos = jnp.tile(jnp.cos(ang), (1, LANES // half))
    sin = jnp.tile(jnp.concatenate([-jnp.sin(ang), jnp.sin(ang)], axis=1), (1, LANES // HEAD_DIM))
    return cos, sin


def _prep_compress(cmp_pos, w_cmp1, b_cmp1, w_cmp2):
    ratio = CMP_BLOCK // CMP_STRIDE
    w1 = w_cmp1.reshape(2, ratio, CMP_STRIDE, HEAD_DIM, CMP_HIDDEN).transpose(0, 2, 3, 1, 4)
    w1 = w1.reshape(2, CMP_STRIDE, HEAD_DIM, ratio * CMP_HIDDEN)
    zeros = jnp.zeros_like(w1)
    wc = jnp.stack([jnp.concatenate([w1, zeros], axis=-1), jnp.concatenate([zeros, w1], axis=-1)], axis=2)
    wc = wc.reshape(2, CMP_STRIDE * LANES, N_KV * ratio * CMP_HIDDEN).astype(BF16)
    pe = cmp_pos.reshape(2, ratio, CMP_STRIDE, 1, HEAD_DIM)
    pe = jnp.broadcast_to(pe, (2, ratio, CMP_STRIDE, N_KV, HEAD_DIM)).reshape(2, ratio, CMP_STRIDE * LANES)
    pe = jnp.pad(pe, ((0, 0), (0, 16 - ratio), (0, 0))).astype(BF16)
    w2 = w_cmp2[:, None]
    zeros2 = jnp.zeros_like(w2)
    w2p = jnp.concatenate([jnp.concatenate([w2, zeros2], axis=-1), jnp.concatenate([zeros2, w2], axis=-1)],
                          axis=1).astype(BF16)
    return wc, pe, b_cmp1.reshape(2, 1, CMP_HIDDEN), w2p


def _overlap_matrix(n_cmp_slots, n_slc, n_lanes):
    start = (np.arange(n_cmp_slots)[:, None] - 1) * CMP_STRIDE
    j = np.arange(n_lanes)[None, :]
    ov = (start <= j * SLC_BLOCK + SLC_BLOCK - 1) & (start + CMP_BLOCK - 1 >= j * SLC_BLOCK)
    ov &= (np.arange(n_cmp_slots)[:, None] >= 1) & (j < n_slc)
    return jnp.asarray(ov, dtype=BF16)


def kernel(x_prompt, x_sample, cache_kv, state_win, page_table, norm_g, w_in, ln_g, ln_b, w_s, b_s,
           cmp_pos, w_cmp1, b_cmp1, w_cmp2, w_out, final_g):
    n_batch, seq, _ = x_prompt.shape
    n_seq, t_new, _ = x_sample.shape
    depth, n_phys = cache_kv.shape[:2]
    n_pages = page_table.shape[1]
    past_len = n_pages * PAGE_SIZE
    win_buf = state_win.shape[2]
    assert depth == 1 and seq % TQ == 0 and n_seq * t_new == MIX_ROWS and win_buf == WINDOW
    assert (past_len + t_new - 1) // SLC_BLOCK == past_len // SLC_BLOCK and past_len % CMP_STRIDE == 0

    w_all = _prep_w_in(w_in[0])
    ng = norm_g[0].reshape(1, D_MODEL)
    lng = ln_g[0].reshape(1, D_A)
    lnb = ln_b[0].reshape(1, D_A)
    fg = final_g.reshape(1, D_MODEL)
    wo = w_out[0]
    wo_a = wo[:D_A].astype(BF16)
    wo_b = _head_pair_perm(wo[D_A:].T).T.astype(BF16)
    wc, pe, b1, w2p = _prep_compress(cmp_pos[0], w_cmp1[0], b_cmp1[0], w_cmp2[0])

    xp = x_prompt.reshape(n_batch * seq, D_MODEL)
    wmix_p, bmix_p = _prep_mix(w_s[0], b_s[0], CHUNK)
    cos_p, sin_p = _rope_tables(jnp.arange(seq))
    ag_p, qc_p, qr_p, rows_p, kvw_pt, ksel_p, kwin_p, gate_p, szb_p, rows_pt = _proj_call(
        xp, ng, w_all, lng, lnb, wmix_p, bmix_p, cos_p, sin_p, BF16, seq)
    pages_p = seq // PAGE_SIZE
    ident = jnp.arange(n_batch * pages_p, dtype=jnp.int32).reshape(n_batch, pages_p)
    cmp_p = _compress_call(rows_p.reshape(n_batch * pages_p, PAGE_SIZE, 4 * LANES), ident, wc, pe, b1, w2p,
                           pages_p, False)
    ov_p = _overlap_matrix(seq // CMP_STRIDE, seq // SLC_BLOCK, LANES)
    bg_p = _attn_prompt_call(qc_p, qr_p, cmp_p, ov_p, ksel_p, kwin_p, gate_p, szb_p, n_batch, seq)
    y_p = _merge_call(xp, ag_p, bg_p, wo_a, wo_b, fg)

    xs = x_sample.reshape(n_seq * t_new, D_MODEL)
    wmix_s, bmix_s = _prep_mix(w_s[0], b_s[0], t_new)
    cos_s, sin_s = _rope_tables(jnp.tile(past_len + jnp.arange(t_new), n_seq))
    ag_s, qc_s, qr_s, rows_s, kvw_s, _, _, gate_s, szb_s, vln_s = _proj_call(
        xs, ng, w_all, lng, lnb, wmix_s, bmix_s, cos_s, sin_s, F32, None)
    cache_t = cache_kv[0].transpose(0, 2, 3, 4, 1)
    cmp_s = _compress_call(cache_t, page_table, wc, pe, b1, w2p, SAMPLE_PAGES_PER_STEP, True)
    n_slc_s = -(-(past_len + t_new) // SLC_BLOCK)
    tb = past_len // SLC_BLOCK
    n_pick = TOP_N - 3
    ov_s = _overlap_matrix(past_len // CMP_STRIDE, n_slc_s, -(-n_slc_s // LANES) * LANES)
    win_prev = state_win[0].reshape(n_seq, win_buf, 2 * LANES)
    ocw_s, sel_s, need_s = _cmpwin_sample_call(qc_s, qr_s, cmp_s, ov_s, win_prev, kvw_s, gate_s, t_new, past_len, n_pick)
    sel_flat = sel_s[:, :, :n_pick].reshape(-1)
    assert n_pages == LANES
    bg_s = _sel_sample_call(sel_flat, need_s[:, :, 0, :].reshape(-1), page_table.reshape(-1), cache_t, qr_s, rows_s, gate_s, ocw_s, szb_s,
                            n_seq, t_new, n_pick, n_pages, tb)
    y_s = _merge_call(xs, ag_s, bg_s, wo_a, wo_b, fg)

    new_kv_p = rows_pt.reshape(n_batch, 4, N_KV, HEAD_DIM, seq).transpose(0, 4, 1, 2, 3)
    new_win_p = kvw_pt[:, :, seq - win_buf:].reshape(n_batch, 2, N_KV, HEAD_DIM, win_buf).transpose(0, 4, 1, 2, 3)
    new_win_s = jnp.concatenate([state_win[0][:, t_new:], kvw_s.reshape(n_seq, t_new, 2, N_KV, HEAD_DIM)], axis=1)
    return (y_p.reshape(n_batch, seq, D_MODEL),
            y_s.reshape(n_seq, t_new, D_MODEL),
            new_kv_p[None],
            new_win_p[None],
            rows_s.reshape(1, n_seq, t_new, 4, N_KV, HEAD_DIM),
            new_win_s[None],
            vln_s.reshape(1, n_seq, t_new, D_A))
```

```python
import functools
import math

import jax
import jax.numpy as jnp
import numpy as np
from jax import lax
from jax.experimental import pallas as pl
from jax.experimental.pallas import tpu as pltpu

F32 = jnp.float32
BF16 = jnp.bfloat16

D_MODEL = 1024
HEAD_DIM = 64
D_A = 512
A_GROUPS = 8
CHUNK = 128
N_HEADS = 8
D_B = 512
N_KV = 2
GQA = 4
CMP_BLOCK = 32
CMP_STRIDE = 16
CMP_HIDDEN = 128
SLC_BLOCK = 64
TOP_N = 16
WINDOW = 512
ROPE_THETA = 10000.0
EPS = 1e-6
SCALE = HEAD_DIM ** -0.5
NEG = -1e30
FORCE = 1e9
PAGE_SIZE = 128

LANES = 128
VMEM_LIMIT = 56 * 1024 * 1024

C_U, C_V, C_ZA, C_Q, C_KV, C_ZB, C_G, C_END = 0, 512, 1024, 1536, 2048, 2816, 3328, 3456

MIX_ROWS = 256
TM = 512
TM_MERGE = 1024
DMA_LOOP_UNROLL = 16
SAMPLE_SEQS_PER_STEP = 16
SAMPLE_PAGES_PER_STEP = 64
TQ = 256
SEGS_PER_PAGE = PAGE_SIZE // CMP_STRIDE


def _lane_iota(shape):
    return lax.broadcasted_iota(jnp.int32, shape, len(shape) - 1)


def _row_iota(shape):
    return lax.broadcasted_iota(jnp.int32, shape, len(shape) - 2)


def _div_pow2(x, n):
    assert n & (n - 1) == 0
    return lax.shift_right_logical(x, int(math.log2(n))) if n > 1 else x


def _mod_pow2(x, n):
    assert n & (n - 1) == 0
    return x & (n - 1)


def _dot(a, b):
    return jnp.dot(a, b, preferred_element_type=F32)


def _dot_nt(a, b):
    return lax.dot_general(a, b, (((1,), (1,)), ((), ())), preferred_element_type=F32)


def _rope(x, cos, sin_signed):
    lo = _mod_pow2(_lane_iota(x.shape), HEAD_DIM) < (HEAD_DIM // 2)
    swapped = jnp.where(lo, pltpu.roll(x, LANES - HEAD_DIM // 2, 1), pltpu.roll(x, HEAD_DIM // 2, 1))
    return x * cos + swapped * sin_signed


def _proj_kernel(x_ref, ng_ref, w_ref, lng_ref, lnb_ref, wmix_ref, bmix_ref, cos_ref, sin_ref,
                 ag_ref, qc_ref, qr_ref, rows_ref, kvw_ref, ksel_ref, kwin_ref, gate_ref, szb_ref,
                 extra_ref, *, prompt):
    x = x_ref[...]
    ms = jnp.mean(x * x, axis=-1, keepdims=True)
    hb = ((x * lax.rsqrt(ms + EPS)) * ng_ref[...]).astype(BF16)
    cos = cos_ref[...]
    sin = sin_ref[...]
    pair = 2 * LANES

    def proj(c0, n):
        return _dot(hb, w_ref[:, c0:c0 + n])

    def halves(x2):
        return x2[:, 0:LANES], x2[:, LANES:pair]

    v = jax.nn.gelu(proj(C_V, D_A))
    mu = jnp.mean(v, axis=-1, keepdims=True)
    vc = v - mu
    var = jnp.mean(vc * vc, axis=-1, keepdims=True)
    vln = vc * lax.rsqrt(var + EPS) * lng_ref[...] + lnb_ref[...]
    if not prompt:
        extra_ref[...] = vln
    vb = vln.astype(BF16)
    first_head = _lane_iota((MIX_ROWS, LANES)) < HEAD_DIM
    for gpp in range(A_GROUPS // 4):
        u2 = halves(jax.nn.gelu(proj(C_U + gpp * pair, pair)))
        za2 = halves(proj(C_ZA + gpp * pair, pair))
        for k in range(2):
            gp = 2 * gpp + k
            sl = slice(gp * LANES, (gp + 1) * LANES)
            mixed = jnp.concatenate(
                [jnp.where(first_head, _dot(wmix_ref[2 * gp], vb[r0:r0 + MIX_ROWS, sl]),
                           _dot(wmix_ref[2 * gp + 1], vb[r0:r0 + MIX_ROWS, sl])) + bmix_ref[:, sl]
                 for r0 in range(0, x.shape[0], MIX_ROWS)], axis=0)
            ag_ref[:, sl] = (u2[k] * mixed * jax.nn.silu(za2[k])).astype(ag_ref.dtype)

    for gpp in range(GQA // 2):
        q2 = halves(proj(C_Q + gpp * pair, pair))
        for k in range(2):
            sl = slice((2 * gpp + k) * LANES, (2 * gpp + k + 1) * LANES)
            qc_ref[:, sl] = (q2[k] * SCALE).astype(qc_ref.dtype)
            qr_ref[:, sl] = (_rope(q2[k], cos, sin) * SCALE).astype(qr_ref.dtype)

    kc, vcm = halves(proj(C_KV, pair))
    ks, vs = halves(proj(C_KV + pair, pair))
    kw, vw = halves(proj(C_KV + 2 * pair, pair))
    ks = _rope(ks, cos, sin)
    kw = _rope(kw, cos, sin)
    for i, blk in enumerate((kc, vcm, ks, vs)):
        rows_ref[:, i * LANES:(i + 1) * LANES] = blk
        if prompt:
            extra_ref[0, i * LANES:(i + 1) * LANES, :] = blk.T
    for i, blk in enumerate((kw, vw)):
        if prompt:
            kvw_ref[0, i * LANES:(i + 1) * LANES, :] = blk.T
        else:
            kvw_ref[:, i * LANES:(i + 1) * LANES] = blk
    ksel_ref[:, 0:LANES] = ks.astype(BF16)
    ksel_ref[:, LANES:2 * LANES] = vs.astype(BF16)
    kwin_ref[:, 0:LANES] = kw.astype(BF16)
    kwin_ref[:, LANES:2 * LANES] = vw.astype(BF16)

    gate_ref[...] = jax.nn.sigmoid(proj(C_G, LANES))
    for gpp in range(GQA // 2):
        zb2 = halves(proj(C_ZB + gpp * pair, pair))
        for k in range(2):
            sl = slice((2 * gpp + k) * LANES, (2 * gpp + k + 1) * LANES)
            szb_ref[:, sl] = jax.nn.silu(zb2[k]).astype(szb_ref.dtype)


def _proj_call(x2d, ng, w, lng, lnb, wmix, bmix, cos_t, sin_t, act_dtype, seq_len):
    n_rows = x2d.shape[0]
    tm = min(TM, n_rows)
    n_tiles = n_rows // tm
    pos_tiles = cos_t.shape[0] // tm
    row = lambda i: (i, 0)
    const2 = lambda i: (0, 0)
    out_shapes = [
        jax.ShapeDtypeStruct((n_rows, D_A), act_dtype),
        jax.ShapeDtypeStruct((n_rows, D_B), act_dtype),
        jax.ShapeDtypeStruct((n_rows, D_B), act_dtype),
        jax.ShapeDtypeStruct((n_rows, 4 * LANES), F32),
        jax.ShapeDtypeStruct((n_rows, 2 * LANES), F32),
        jax.ShapeDtypeStruct((n_rows, 2 * LANES), BF16),
        jax.ShapeDtypeStruct((n_rows, 2 * LANES), BF16),
        jax.ShapeDtypeStruct((n_rows, LANES), F32),
        jax.ShapeDtypeStruct((n_rows, D_B), act_dtype),
        jax.ShapeDtypeStruct((n_rows, D_A), F32),
    ]
    out_specs = [pl.BlockSpec((tm, s.shape[1]), row) for s in out_shapes]
    if seq_len is not None:
        tiles_per_seq = seq_len // tm
        dim_major = lambda i: (i // tiles_per_seq, 0, i % tiles_per_seq)
        for idx, width in ((4, 2 * LANES), (9, 4 * LANES)):
            out_shapes[idx] = jax.ShapeDtypeStruct((n_rows // seq_len, width, seq_len), F32)
            out_specs[idx] = pl.BlockSpec((1, width, tm), dim_major)
    return pl.pallas_call(
        functools.partial(_proj_kernel, prompt=seq_len is not None),
        grid=(n_tiles,),
        in_specs=[
            pl.BlockSpec((tm, D_MODEL), row),
            pl.BlockSpec((1, D_MODEL), const2),
            pl.BlockSpec((D_MODEL, C_END), const2),
            pl.BlockSpec((1, D_A), const2),
            pl.BlockSpec((1, D_A), const2),
            pl.BlockSpec((A_GROUPS, MIX_ROWS, MIX_ROWS), lambda i: (0, 0, 0)),
            pl.BlockSpec((MIX_ROWS, D_A), const2),
            pl.BlockSpec((tm, LANES), lambda i: (i % pos_tiles, 0)),
            pl.BlockSpec((tm, LANES), lambda i: (i % pos_tiles, 0)),
        ],
        out_specs=out_specs,
        out_shape=out_shapes,
        compiler_params=pltpu.CompilerParams(dimension_semantics=("arbitrary",),
                                             vmem_limit_bytes=VMEM_LIMIT),
        name="proj",
    )(x2d, ng, w, lng, lnb, wmix, bmix, cos_t, sin_t)


def _compress_kernel(pt_ref, src_ref, wc_ref, pe_ref, b1_ref, w2_ref, out_ref, buf, sem, carry,
                     *maybe_stage, pages_per_step):
    transposed_src = bool(maybe_stage)
    n_steps = pl.num_programs(0) * pl.num_programs(1)
    step = pl.program_id(0) * pl.num_programs(1) + pl.program_id(1)
    m = pages_per_step * SEGS_PER_PAGE

    def page_copies(step_idx, slot, i):
        phys = pt_ref[step_idx * pages_per_step + i]
        if transposed_src:
            return [pltpu.make_async_copy(src_ref.at[phys, pl.ds(0, 2)], maybe_stage[0].at[slot, i], sem.at[slot])]
        return [pltpu.make_async_copy(src_ref.at[phys, :, pl.ds(c * LANES, LANES)],
                                      buf.at[slot, c, pl.ds(i * PAGE_SIZE, PAGE_SIZE), :],
                                      sem.at[slot]) for c in range(2)]

    def start_step(step_idx, slot):
        def body(i, carry_):
            for cp in page_copies(step_idx, slot, i):
                cp.start()
            return carry_
        lax.fori_loop(0, pages_per_step, body, 0, unroll=DMA_LOOP_UNROLL)

    def wait_step(step_idx, slot):
        def body(i, carry_):
            for cp in page_copies(step_idx, slot, i):
                cp.wait()
            return carry_
        lax.fori_loop(0, pages_per_step, body, 0, unroll=DMA_LOOP_UNROLL)

    slot = step % 2

    first_of_seq = pl.program_id(1) == 0
    first_row = _row_iota((m, LANES)) == 0

    def compress_rows(rows_slot, seg_pitch):
        for c in range(2):
            pieces = [buf[rows_slot, c, pl.ds(j, m, stride=seg_pitch), :].astype(BF16)
                      for j in range(CMP_STRIDE)]
            part = _dot(jnp.concatenate(pieces, axis=1), wc_ref[c])
            pe_part = _dot(pe_ref[c], wc_ref[c])
            acc_out = None
            for h in range(N_KV):
                lo = slice(h * 2 * CMP_HIDDEN, h * 2 * CMP_HIDDEN + CMP_HIDDEN)
                hi = slice(h * 2 * CMP_HIDDEN + CMP_HIDDEN, (h + 1) * 2 * CMP_HIDDEN)
                bias = pe_part[0:1, lo] + pe_part[1:2, hi] + b1_ref[c]
                part0 = part[:, lo]
                prev_last = jnp.where(first_of_seq, 0.0, carry[c, h, 0:1, :])
                prev0 = jnp.where(first_row, prev_last, pltpu.roll(part0, 1, 0))
                carry[c, h, 0:1, :] = part0[m - 1:m, :]
                hid = jax.nn.gelu(prev0 + part[:, hi] + bias).astype(BF16)
                contrib = _dot(hid, w2_ref[c, h])
                acc_out = contrib if acc_out is None else acc_out + contrib
            out_ref[0, :, c * LANES:(c + 1) * LANES] = acc_out.astype(out_ref.dtype)

    if transposed_src:
        stage = maybe_stage[0]
        seg_pitch = CMP_STRIDE + 1

        def transpose_step(stage_slot):
            for i in range(pages_per_step):
                for c in range(2):
                    rows = stage[stage_slot, i, c].reshape(N_KV * HEAD_DIM, PAGE_SIZE).T
                    for s in range(SEGS_PER_PAGE):
                        r0 = (i * SEGS_PER_PAGE + s) * seg_pitch
                        buf[stage_slot, c, r0:r0 + CMP_STRIDE, :] = rows[s * CMP_STRIDE:(s + 1) * CMP_STRIDE]

        @pl.when(step == 0)
        def _():
            start_step(0, 0)
            start_step(1, 1)
            wait_step(0, 0)
            transpose_step(0)

        @pl.when(step + 1 < n_steps)
        def _():
            wait_step(step + 1, (step + 1) % 2)

        @pl.when(step + 2 < n_steps)
        def _():
            start_step(step + 2, slot)

        for parity in range(2):
            @pl.when(slot == parity)
            def _():
                transpose_step(1 - parity)
                compress_rows(parity, seg_pitch)
    else:
        @pl.when(step == 0)
        def _():
            start_step(0, 0)

        @pl.when(step + 1 < n_steps)
        def _():
            start_step(step + 1, (step + 1) % 2)

        wait_step(step, slot)
        compress_rows(slot, CMP_STRIDE)


def _compress_call(src, page_table, wc, pe, b1, w2p, pages_per_step, transposed_src):
    n_seq, n_pages = page_table.shape
    n_chunks = n_pages // pages_per_step
    m = pages_per_step * SEGS_PER_PAGE
    assert not transposed_src or n_seq * n_chunks >= 2
    seg_pitch = CMP_STRIDE + 1 if transposed_src else CMP_STRIDE
    scratch = [
        pltpu.VMEM((2, 2, m * seg_pitch, LANES), F32),
        pltpu.SemaphoreType.DMA((2,)),
        pltpu.VMEM((2, N_KV, 8, CMP_HIDDEN), F32),
    ]
    if transposed_src:
        scratch.append(pltpu.VMEM((2, pages_per_step, 2, N_KV, HEAD_DIM, PAGE_SIZE), F32))
    grid_spec = pltpu.PrefetchScalarGridSpec(
        num_scalar_prefetch=1,
        grid=(n_seq, n_chunks),
        in_specs=[
            pl.BlockSpec(memory_space=pl.ANY),
            pl.BlockSpec((2, CMP_STRIDE * LANES, 4 * CMP_HIDDEN), lambda b, k, pt: (0, 0, 0)),
            pl.BlockSpec((2, 16, CMP_STRIDE * LANES), lambda b, k, pt: (0, 0, 0)),
            pl.BlockSpec((2, 1, CMP_HIDDEN), lambda b, k, pt: (0, 0, 0)),
            pl.BlockSpec((2, N_KV, CMP_HIDDEN, LANES), lambda b, k, pt: (0, 0, 0, 0)),
        ],
        out_specs=pl.BlockSpec((1, m, 2 * LANES), lambda b, k, pt: (b, k, 0)),
        scratch_shapes=scratch,
    )
    return pl.pallas_call(
        functools.partial(_compress_kernel, pages_per_step=pages_per_step),
        grid_spec=grid_spec,
        out_shape=jax.ShapeDtypeStruct((n_seq, n_pages * SEGS_PER_PAGE, 2 * LANES), BF16),
        compiler_params=pltpu.CompilerParams(dimension_semantics=("arbitrary", "arbitrary"),
                                             vmem_limit_bytes=VMEM_LIMIT),
        name="compress",
    )(page_table.reshape(-1), src, wc, pe, b1, w2p)


def _softmax_rows(s, mask):
    s = jnp.where(mask, s, NEG)
    e = jnp.where(mask, jnp.exp(s - jnp.max(s, axis=-1, keepdims=True)), 0.0)
    return e / jnp.maximum(jnp.sum(e, axis=-1, keepdims=True), 1.0)


def _dot_split(p, w):
    hi = p.astype(BF16)
    lo = (p - hi.astype(F32)).astype(BF16)
    return _dot(hi, w) + _dot(lo, w)


def _gate_col(gates, idx):
    return gates[:, idx:idx + 1]


def _attn_prompt_kernel(qc_ref, qr_ref, cmp_ref, ov_ref, eg_ref, ksel_ref, kwin_ref, gate_ref, szb_ref,
                        out_ref, q_sc, m_sc, acc_sc, cmp_sc):
    qi = pl.program_id(1)
    lane = _lane_iota((TQ, LANES))
    lo_half = lane < HEAD_DIM
    r_minus_c = _row_iota((TQ, TQ)) - _lane_iota((TQ, TQ))
    causal_bias = jnp.where(r_minus_c >= 0, 0.0, NEG)
    far_bias = jnp.where(r_minus_c < 0, 0.0, NEG)
    pair_bias = jnp.concatenate([jnp.zeros((TQ, TQ), F32), causal_bias], axis=1)
    kcmp = cmp_ref[0, :, 0:LANES]
    vcmp = cmp_ref[0, :, LANES:2 * LANES]
    n_slc = ksel_ref.shape[0] // SLC_BLOCK
    blocks_per_tile = TQ // SLC_BLOCK

    def flash_tile(br, h, k_tile, v_aug, bias):
        width = k_tile.shape[0]
        s = _dot_nt(q_sc[h], k_tile)
        if bias is not None:
            s = (s.reshape(GQA, TQ, width) + bias[None]).reshape(GQA * TQ, width)
        m_prev = m_sc[h]
        m_new = jnp.maximum(m_prev, jnp.max(s, axis=-1, keepdims=True))
        alpha = jnp.exp(m_prev - m_new)
        p = jnp.exp(s - jnp.concatenate([m_new] * (width // LANES), axis=1))
        acc_sc[br, h] = alpha * acc_sc[br, h] + _dot(p.astype(BF16), v_aug)
        m_sc[h] = m_new

    def flash_init(br):
        m_sc[...] = jnp.full(m_sc.shape, NEG, F32)
        acc_sc[br] = jnp.zeros(acc_sc.shape[1:], F32)

    n_hg = N_KV * GQA
    q_cmp = jnp.concatenate([jnp.where(_div_pow2(lane, HEAD_DIM) == h, qc_ref[:, g * LANES:(g + 1) * LANES], 0)
                             for h in range(N_KV) for g in range(GQA)], axis=0)
    lane_all = _lane_iota((n_hg * TQ, LANES))
    tq_all = qi * TQ + _mod_pow2(_row_iota((n_hg * TQ, LANES)), TQ)
    cmp_valid = (lane_all >= 1) & (lane_all * CMP_STRIDE + (CMP_STRIDE - 1) <= tq_all)
    p_all = _softmax_rows(_dot_nt(q_cmp, kcmp), cmp_valid)
    o_all = _dot(p_all.astype(BF16), vcmp)
    for g in range(GQA):
        cmp_sc[:, g * LANES:(g + 1) * LANES] = jnp.where(
            lo_half, o_all[g * TQ:(g + 1) * TQ], o_all[(GQA + g) * TQ:(GQA + g + 1) * TQ])
    p4 = p_all.reshape(N_KV, GQA, TQ, LANES)
    p_sum = sum(p4[:, g] for g in range(GQA)).reshape(N_KV * TQ, LANES)
    imp = _dot_split(p_sum, ov_ref[...])
    lane_ht = _lane_iota((N_KV * TQ, LANES))
    tq_ht = qi * TQ + _mod_pow2(_row_iota((N_KV * TQ, LANES)), TQ)
    tb = _div_pow2(tq_ht, SLC_BLOCK)
    forced = (lane_ht == 0) | (lane_ht == tb) | (lane_ht == tb - 1)
    valid = lane_ht * SLC_BLOCK <= tq_ht
    score = jnp.where(forced, FORCE, jnp.where(valid, imp, NEG))
    score_t = score.T[0:n_slc, :]
    blk = _row_iota((n_slc, N_KV * TQ))
    rank = jnp.zeros((n_slc, N_KV * TQ), jnp.int32)
    for j in range(n_slc):
        other = score_t[j:j + 1, :]
        ahead = (other > score_t) | ((other == score_t) & (blk > j))
        rank = rank + ahead.astype(jnp.int32)
    sel_bias_t = jnp.where(rank < TOP_N, 0.0, NEG)

    for h in range(N_KV):
        keep = _div_pow2(lane, HEAD_DIM) == h
        other_off = HEAD_DIM * (1 - h)
        pieces = [jnp.zeros((other_off, TQ), F32)] if other_off else []
        pieces += [sel_bias_t[:, h * TQ:(h + 1) * TQ], jnp.zeros((LANES - other_off - n_slc, TQ), F32)]
        sel_bias = jnp.concatenate(pieces, axis=0).T.astype(BF16)
        for g in range(GQA):
            sl = slice(g * LANES, (g + 1) * LANES)
            q_sc[h, g * TQ:(g + 1) * TQ, :] = jnp.where(keep, qr_ref[:, sl], sel_bias)

    def span_operands(kv_ref, kt, n_tiles, h, other_lanes):
        k = kv_ref[pl.ds(kt * TQ, n_tiles * TQ), 0:LANES]
        v = kv_ref[pl.ds(kt * TQ, n_tiles * TQ), LANES:2 * LANES]
        own = _div_pow2(_lane_iota(k.shape), HEAD_DIM) == h
        return jnp.where(own, k, other_lanes(k.shape)), jnp.where(own, v, 1.0)

    def sel_span(kt, n_tiles, bias):
        for h in range(N_KV):
            def indicator(shape, h=h):
                first = HEAD_DIM * (1 - h) + kt * blocks_per_tile
                hit = _lane_iota(shape) == first + _div_pow2(_row_iota(shape), SLC_BLOCK)
                return jnp.where(hit, 1, 0).astype(BF16)
            flash_tile(0, h, *span_operands(ksel_ref, kt, n_tiles, h, indicator), bias)

    flash_init(0)

    def sel_pair(i, carry_):
        sel_span(2 * i, 2, None)
        return carry_
    lax.fori_loop(0, _div_pow2(jnp.maximum(qi - 1, 0), 2), sel_pair, 0)

    @pl.when((qi >= 2) & (_mod_pow2(qi, 2) == 0))
    def _():
        sel_span(qi - 2, 1, None)

    @pl.when(qi >= 1)
    def _():
        sel_span(qi - 1, 2, pair_bias)

    @pl.when(qi == 0)
    def _():
        sel_span(qi, 1, causal_bias)

    def win_span(kt, n_tiles, bias):
        for h in range(N_KV):
            flash_tile(1, h, *span_operands(kwin_ref, kt, n_tiles, h, lambda shape: jnp.zeros(shape, BF16)), bias)

    flash_init(1)
    far = WINDOW // TQ
    assert far == 2

    @pl.when(qi >= far)
    def _():
        win_span(qi - far, 1, far_bias)

    @pl.when(qi >= 1)
    def _():
        win_span(qi - 1, 2, pair_bias)

    @pl.when(qi == 0)
    def _():
        win_span(qi, 1, causal_bias)

    gates_x = _dot_split(gate_ref[...], eg_ref[...])
    for g in range(GQA):
        rows = slice(g * TQ, (g + 1) * TQ)
        sl = slice(g * LANES, (g + 1) * LANES)
        o = gates_x[:, (g * 3) * LANES:(g * 3 + 1) * LANES] * cmp_sc[:, sl]
        for br in range(2):
            a0 = acc_sc[br, 0, rows, :]
            a1 = acc_sc[br, 1, rows, :]
            num = jnp.where(lo_half, a0, a1)
            den = pltpu.roll(jnp.where(lo_half, a1, a0), HEAD_DIM, 1)
            o = o + gates_x[:, (g * 3 + 1 + br) * LANES:(g * 3 + 2 + br) * LANES] * (num / den)
        out_ref[:, sl] = (o * szb_ref[:, sl].astype(F32)).astype(out_ref.dtype)


def _gate_expansion():
    eg = np.zeros((LANES, GQA * 3 * LANES), np.float32)
    for h in range(N_KV):
        for g in range(GQA):
            for br in range(3):
                c0 = (g * 3 + br) * LANES + h * HEAD_DIM
                eg[h * GQA * 3 + g * 3 + br, c0:c0 + HEAD_DIM] = 1.0
    return jnp.asarray(eg, dtype=BF16)


def _attn_prompt_call(qc, qr, cmp_p, ov, ksel, kwin, gates, szb, n_batch, seq):
    nq = seq // TQ
    tile = lambda b, q: (b * nq + q, 0)
    whole = lambda b, q: (b, 0)
    return pl.pallas_call(
        _attn_prompt_kernel,
        grid=(n_batch, nq),
        in_specs=[
            pl.BlockSpec((TQ, D_B), tile),
            pl.BlockSpec((TQ, D_B), tile),
            pl.BlockSpec((1, LANES, 2 * LANES), lambda b, q: (b, 0, 0)),
            pl.BlockSpec((LANES, LANES), lambda b, q: (0, 0)),
            pl.BlockSpec((LANES, GQA * 3 * LANES), lambda b, q: (0, 0)),
            pl.BlockSpec((seq, 2 * LANES), whole),
            pl.BlockSpec((seq, 2 * LANES), whole),
            pl.BlockSpec((TQ, LANES), tile),
            pl.BlockSpec((TQ, D_B), tile),
        ],
        out_specs=pl.BlockSpec((TQ, D_B), tile),
        out_shape=jax.ShapeDtypeStruct((n_batch * seq, D_B), BF16),
        scratch_shapes=[
            pltpu.VMEM((N_KV, GQA * TQ, LANES), BF16),
            pltpu.VMEM((N_KV, GQA * TQ, LANES), F32),
            pltpu.VMEM((2, N_KV, GQA * TQ, LANES), F32),
            pltpu.VMEM((TQ, D_B), F32),
        ],
        compiler_params=pltpu.CompilerParams(dimension_semantics=("arbitrary", "arbitrary"),
                                             vmem_limit_bytes=VMEM_LIMIT),
        name="attn_prompt",
    )(qc, qr, cmp_p, ov, _gate_expansion(), ksel, kwin, gates, szb)


def _cmpwin_sample_kernel(qc_ref, qr_ref, cmp_ref, ov_ref, win_ref, kvw_ref, gate_ref,
                          ocw_ref, sel_ref, need_ref, *, past_len, n_pick, t_new):
    seqs = cmp_ref.shape[0]
    n_rows = N_KV * GQA * t_new
    lane = _lane_iota((t_new, LANES))
    n_cmp = cmp_ref.shape[1]
    row_t = _mod_pow2(_row_iota((n_rows, n_cmp)), t_new)
    slot = _lane_iota((n_rows, n_cmp))
    cmp_valid = (slot >= 1) & (slot * CMP_STRIDE + (CMP_STRIDE - 1) <= past_len + row_t)
    n_buf = win_ref.shape[1]
    dist = n_buf + _mod_pow2(_row_iota((n_rows, n_buf)), t_new) - _lane_iota((n_rows, n_buf))
    win_valid = (dist >= 0) & (dist < WINDOW)
    row_t1 = _mod_pow2(_row_iota((n_rows, 1)), t_new)

    def one_sequence(sq):
        rows_sq = slice(sq * t_new, (sq + 1) * t_new)
        gates = gate_ref[rows_sq, :]

        def q_rows(ref):
            return jnp.concatenate(
                [jnp.where(_div_pow2(lane, HEAD_DIM) == h, ref[rows_sq, g * LANES:(g + 1) * LANES], 0).astype(BF16)
                 for h in range(N_KV) for g in range(GQA)], axis=0)

        p = _softmax_rows(_dot_nt(q_rows(qc_ref), cmp_ref[sq, :, 0:LANES]), cmp_valid)
        o_cmp = _dot(p.astype(BF16), cmp_ref[sq, :, LANES:2 * LANES])
        p_sum = jnp.concatenate(
            [sum(p[(h * GQA + g) * t_new:(h * GQA + g + 1) * t_new] for g in range(GQA)) for h in range(N_KV)],
            axis=0)
        imp = _dot_split(p_sum, ov_ref[...])

        qr = q_rows(qr_ref)
        k_buf = win_ref[sq, :, 0:LANES].astype(BF16)
        v_buf = win_ref[sq, :, LANES:2 * LANES].astype(BF16)
        s_buf = jnp.where(win_valid, _dot_nt(qr, k_buf), NEG)
        k_new = kvw_ref[rows_sq, 0:LANES].astype(BF16).astype(F32)
        v_new = kvw_ref[rows_sq, LANES:2 * LANES].astype(BF16).astype(F32)
        qr32 = qr.astype(F32)
        s_new = [jnp.where(row_t1 >= i, jnp.sum(qr32 * k_new[i:i + 1, :], axis=-1, keepdims=True), NEG)
                 for i in range(t_new)]
        m = jnp.max(s_buf, axis=-1, keepdims=True)
        for s in s_new:
            m = jnp.maximum(m, s)
        e_buf = jnp.exp(s_buf - m)
        e_new = [jnp.exp(s - m) for s in s_new]
        denom = jnp.sum(e_buf, axis=-1, keepdims=True) + sum(e_new)
        o_win = _dot(e_buf.astype(BF16), v_buf)
        for i in range(t_new):
            o_win = o_win + e_new[i].astype(BF16).astype(F32) * v_new[i:i + 1, :]
        o_win = o_win / denom

        for g in range(GQA):
            parts = []
            for h in range(N_KV):
                rows = slice((h * GQA + g) * t_new, (h * GQA + g + 1) * t_new)
                base = h * GQA * 3 + g * 3
                parts.append(_gate_col(gates, base) * o_cmp[rows] + _gate_col(gates, base + 2) * o_win[rows])
            ocw_ref[rows_sq, g * LANES:(g + 1) * LANES] = jnp.where(lane < HEAD_DIM, parts[0], parts[1])
        return imp

    imp = jnp.concatenate([one_sequence(sq) for sq in range(seqs)], axis=0)
    blk = _lane_iota(imp.shape)
    tb = (past_len + t_new - 1) // SLC_BLOCK
    candidate = (blk >= 1) & (blk < tb - 1)
    score = jnp.where(candidate, imp, -1.0)
    out_lane = _lane_iota((imp.shape[0], LANES))
    picks = jnp.zeros((imp.shape[0], LANES), jnp.int32)
    blocks_per_page = PAGE_SIZE // SLC_BLOCK
    page_used = jnp.zeros((imp.shape[0], LANES), jnp.int32)
    for k in range(n_pick):
        best = jnp.max(score, axis=-1, keepdims=True)
        idx = jnp.min(jnp.where(score == best, blk, 1 << 20), axis=-1, keepdims=True)
        picks = jnp.where(out_lane == k, idx, picks)
        page_used = jnp.where(out_lane == _div_pow2(idx, blocks_per_page), 1, page_used)
        score = jnp.where(blk == idx, -2.0, score)
    sel_ref[...] = picks.reshape(sel_ref.shape)
    used = jnp.max(page_used.reshape(seqs * N_KV, t_new, LANES), axis=1)
    page = _lane_iota(used.shape)
    used = jnp.where((page == 0) | (page == (tb - 1) // blocks_per_page), 1, used)
    need_ref[...] = jnp.broadcast_to(used.reshape(seqs, N_KV, 1, LANES), need_ref.shape)


def _cmpwin_sample_call(qc, qr, cmp_s, ov, win, kvw, gates, t_new, past_len, n_pick):
    n_seq = cmp_s.shape[0]
    seqs = SAMPLE_SEQS_PER_STEP
    assert n_seq % seqs == 0
    tile = lambda b: (b, 0)
    per_seq = lambda b: (b, 0, 0)
    return pl.pallas_call(
        functools.partial(_cmpwin_sample_kernel, past_len=past_len, n_pick=n_pick, t_new=t_new),
        grid=(n_seq // seqs,),
        in_specs=[
            pl.BlockSpec((seqs * t_new, D_B), tile),
            pl.BlockSpec((seqs * t_new, D_B), tile),
            pl.BlockSpec((seqs, cmp_s.shape[1], 2 * LANES), per_seq),
            pl.BlockSpec(ov.shape, lambda b: (0, 0)),
            pl.BlockSpec((seqs, win.shape[1], 2 * LANES), per_seq),
            pl.BlockSpec((seqs * t_new, 2 * LANES), tile),
            pl.BlockSpec((seqs * t_new, LANES), tile),
        ],
        out_specs=[
            pl.BlockSpec((seqs * t_new, D_B), tile),
            pl.BlockSpec((seqs, N_KV * t_new, LANES), per_seq),
            pl.BlockSpec((seqs, N_KV, 8, LANES), lambda b: (b, 0, 0, 0)),
        ],
        out_shape=[
            jax.ShapeDtypeStruct((n_seq * t_new, D_B), F32),
            jax.ShapeDtypeStruct((n_seq, N_KV * t_new, LANES), jnp.int32),
            jax.ShapeDtypeStruct((n_seq, N_KV, 8, LANES), jnp.int32),
        ],
        compiler_params=pltpu.CompilerParams(dimension_semantics=("arbitrary",),
                                             vmem_limit_bytes=VMEM_LIMIT),
        name="cmpwin_sample",
    )(qc, qr, cmp_s, ov, win, kvw, gates)


def _sel_sample_kernel(sel_ref, need_ref, pt_ref, cache_ref, qr_ref, rows_ref, gate_ref, ocw_ref, szb_ref,
                       out_ref, buf, sem, *, n_pick, n_pages, tb):
    t_new = qr_ref.shape[0]
    n_own = t_new * n_pick
    n_steps = pl.num_programs(0)
    step = pl.program_id(0)
    blocks_per_page = PAGE_SIZE // SLC_BLOCK

    def for_all_blocks(seq, slot, fn):
        for h in range(N_KV):
            def body(p, carry_, h=h):
                @pl.when(need_ref[(seq * N_KV + h) * n_pages + p] != 0)
                def _():
                    fn(pltpu.make_async_copy(cache_ref.at[pt_ref[seq * n_pages + p], pl.ds(2, 2), h],
                                             buf.at[slot, h * n_pages + p], sem.at[slot]))
                return carry_
            lax.fori_loop(0, n_pages, body, 0, unroll=DMA_LOOP_UNROLL)

    @pl.when(step == 0)
    def _():
        for_all_blocks(0, 0, lambda cp: cp.start())

    @pl.when(step + 1 < n_steps)
    def _():
        for_all_blocks(step + 1, (step + 1) % 2, lambda cp: cp.start())

    slot = step % 2
    n_copies = lax.fori_loop(0, N_KV * n_pages, lambda i, acc: acc + need_ref[step * N_KV * n_pages + i], 0,
                             unroll=DMA_LOOP_UNROLL)
    for bit in range((N_KV * n_pages).bit_length()):
        @pl.when((n_copies & (1 << bit)) != 0)
        def _():
            pltpu.make_async_copy(cache_ref.at[pl.ds(0, 1 << bit), pl.ds(2, 2), 0],
                                  buf.at[slot, pl.ds(0, 1 << bit)], sem.at[slot]).wait()

    n_rows = GQA * t_new
    half = _div_pow2(_lane_iota((1, LANES)), SLC_BLOCK)
    row_t = _mod_pow2(_row_iota((n_rows, 1)), t_new)
    gates = gate_ref[...]

    def one_head(h):
        def head_half(x):
            return x[:, h * HEAD_DIM:(h + 1) * HEAD_DIM]

        qr = jnp.concatenate([head_half(qr_ref[:, g * LANES:(g + 1) * LANES]) for g in range(GQA)], axis=0)
        qrb = qr.astype(BF16)
        qr32 = qrb.astype(F32)

        def slabs(blocks, kind):
            return jnp.concatenate(
                [buf[slot, h * n_pages + _div_pow2(blk, blocks_per_page), kind].astype(BF16) for blk in blocks],
                axis=1)

        def own_blocks(t):
            return [sel_ref[(step * N_KV + h) * n_own + t * n_pick + k] for k in range(n_pick)]

        bias_sh = jnp.concatenate(
            [jnp.where(half == blk % blocks_per_page, 0.0, NEG) for blk in (0, tb - 1)], axis=1)
        s_sh = _dot(qrb, slabs((0, tb - 1), 0)) + bias_sh
        k_new = head_half(rows_ref[:, 2 * LANES:3 * LANES]).astype(BF16).astype(F32)
        v_new = head_half(rows_ref[:, 3 * LANES:4 * LANES]).astype(BF16).astype(F32)
        s_new = [jnp.where(row_t >= i, jnp.sum(qr32 * k_new[i:i + 1, :], axis=-1, keepdims=True), NEG)
                 for i in range(t_new)]
        own_keys = n_pick * PAGE_SIZE
        s_own = jnp.zeros((n_rows, own_keys), F32)
        for t in range(t_new):
            bias_t = jnp.concatenate(
                [jnp.where(half == _mod_pow2(blk, blocks_per_page), 0.0, NEG) for blk in own_blocks(t)], axis=1)
            s_t = _dot(qrb, slabs(own_blocks(t), 0)) + bias_t
            s_own = jnp.where(row_t == t, s_t, s_own)

        m = jnp.maximum(jnp.max(s_sh, axis=-1, keepdims=True), jnp.max(s_own, axis=-1, keepdims=True))
        for s in s_new:
            m = jnp.maximum(m, s)
        e_sh = jnp.exp(s_sh - m)
        e_own = jnp.exp(s_own - m)
        e_new = [jnp.exp(s - m) for s in s_new]
        denom = jnp.sum(e_sh, axis=-1, keepdims=True) + jnp.sum(e_own, axis=-1, keepdims=True) + sum(e_new)
        o = _dot_nt(e_sh.astype(BF16), slabs((0, tb - 1), 1))
        for t in range(t_new):
            o = o + _dot_nt(jnp.where(row_t == t, e_own, 0.0).astype(BF16), slabs(own_blocks(t), 1))
        for i in range(t_new):
            o = o + e_new[i].astype(BF16).astype(F32) * v_new[i:i + 1, :]
        o = o / denom
        return [_gate_col(gates, h * GQA * 3 + g * 3 + 1) * o[g * t_new:(g + 1) * t_new] for g in range(GQA)]

    gated = [one_head(h) for h in range(N_KV)]
    for g in range(GQA):
        sl = slice(g * LANES, (g + 1) * LANES)
        o_slc = jnp.concatenate([gated[h][g] for h in range(N_KV)], axis=1)
        out_ref[:, sl] = (ocw_ref[:, sl] + o_slc) * szb_ref[:, sl]


def _sel_sample_call(sel_flat, need_flat, pt_flat, cache_t, qr, rows, gates, ocw, szb, n_seq, t_new, n_pick, n_pages,
                     tb):
    tile = lambda b, sel, need, pt: (b, 0)
    grid_spec = pltpu.PrefetchScalarGridSpec(
        num_scalar_prefetch=3,
        grid=(n_seq,),
        in_specs=[
            pl.BlockSpec(memory_space=pl.ANY),
            pl.BlockSpec((t_new, D_B), tile),
            pl.BlockSpec((t_new, 4 * LANES), tile),
            pl.BlockSpec((t_new, LANES), tile),
            pl.BlockSpec((t_new, D_B), tile),
            pl.BlockSpec((t_new, D_B), tile),
        ],
        out_specs=pl.BlockSpec((t_new, D_B), tile),
        scratch_shapes=[
            pltpu.VMEM((2, N_KV * n_pages, 2, HEAD_DIM, PAGE_SIZE), F32),
            pltpu.SemaphoreType.DMA((2,)),
        ],
    )
    return pl.pallas_call(
        functools.partial(_sel_sample_kernel, n_pick=n_pick, n_pages=n_pages, tb=tb),
        grid_spec=grid_spec,
        out_shape=jax.ShapeDtypeStruct((n_seq * t_new, D_B), F32),
        compiler_params=pltpu.CompilerParams(dimension_semantics=("arbitrary",),
                                             vmem_limit_bytes=VMEM_LIMIT),
        name="sel_sample",
    )(sel_flat, need_flat, pt_flat, cache_t, qr, rows, gates, ocw, szb)


def _merge_kernel(x_ref, a_ref, b_ref, wa_ref, wb_ref, fg_ref, y_ref):
    delta = _dot(a_ref[...].astype(BF16), wa_ref[...]) + _dot(b_ref[...].astype(BF16), wb_ref[...])
    x = x_ref[...] + delta
    ms = jnp.mean(x * x, axis=-1, keepdims=True)
    y_ref[...] = (x * lax.rsqrt(ms + EPS)) * fg_ref[...]


def _merge_call(x2d, a, b, wa, wb, fg):
    n_rows = x2d.shape[0]
    tm = min(TM_MERGE, n_rows)
    row = lambda i: (i, 0)
    const2 = lambda i: (0, 0)
    return pl.pallas_call(
        _merge_kernel,
        grid=(n_rows // tm,),
        in_specs=[
            pl.BlockSpec((tm, D_MODEL), row),
            pl.BlockSpec((tm, D_A), row),
            pl.BlockSpec((tm, D_B), row),
            pl.BlockSpec((D_A, D_MODEL), const2),
            pl.BlockSpec((D_B, D_MODEL), const2),
            pl.BlockSpec((1, D_MODEL), const2),
        ],
        out_specs=pl.BlockSpec((tm, D_MODEL), row),
        out_shape=jax.ShapeDtypeStruct((n_rows, D_MODEL), F32),
        compiler_params=pltpu.CompilerParams(dimension_semantics=("arbitrary",),
                                             vmem_limit_bytes=VMEM_LIMIT),
        name="merge",
    )(x2d, a, b, wa, wb, fg)


def _head_pair_perm(w_cols):
    lead = w_cols.shape[:-1]
    return w_cols.reshape(*lead, N_KV, GQA, HEAD_DIM).swapaxes(-3, -2).reshape(*lead, D_B)


def _prep_w_in(w_in):
    cuts = np.cumsum([D_A, D_A, D_A, D_B, 6 * N_KV * HEAD_DIM, 3 * N_HEADS]).tolist()
    u, v, za, q, kv, g, zb = jnp.split(w_in, cuts, axis=-1)
    g_pad = jnp.pad(g, ((0, 0), (0, LANES - g.shape[1])))
    return jnp.concatenate([u, v, za, _head_pair_perm(q), kv, _head_pair_perm(zb), g_pad], axis=-1).astype(BF16)


def _prep_mix(w_s, b_s, chunk_len):
    reps = MIX_ROWS // chunk_len
    row = np.arange(MIX_ROWS)[:, None]
    col = np.arange(MIX_ROWS)[None, :]
    causal_same_chunk = (row // chunk_len == col // chunk_len) & (col <= row)
    spread = jnp.asarray(np.arange(chunk_len)[:, None] == np.arange(MIX_ROWS)[None, :] % chunk_len, dtype=w_s.dtype)
    wide = jnp.einsum("gts,sc->gtc", w_s[:, :chunk_len, :chunk_len], spread)
    wmix = jnp.where(causal_same_chunk, jnp.tile(wide, (1, reps, 1)), 0).astype(BF16)
    bias = jnp.repeat(b_s[:, :chunk_len].T, HEAD_DIM, axis=1)
    return wmix, jnp.tile(bias, (reps, 1))


def _rope_tables(pos):
    half = HEAD_DIM // 2
    inv = ROPE_THETA ** (-jnp.arange(half, dtype=F32) / half)
    ang = pos.astype(F32)[:, None] * inv
    cos = jnp.tile(jnp.cos(ang), (1, LANES // half))
    sin = jnp.tile(jnp.concatenate([-jnp.sin(ang), jnp.sin(ang)], axis=1), (1, LANES // HEAD_DIM))
    return cos, sin


def _prep_compress(cmp_pos, w_cmp1, b_cmp1, w_cmp2):
    ratio = CMP_BLOCK // CMP_STRIDE
    w1 = w_cmp1.reshape(2, ratio, CMP_STRIDE, HEAD_DIM, CMP_HIDDEN).transpose(0, 2, 3, 1, 4)
    w1 = w1.reshape(2, CMP_STRIDE, HEAD_DIM, ratio * CMP_HIDDEN)
    zeros = jnp.zeros_like(w1)
    wc = jnp.stack([jnp.concatenate([w1, zeros], axis=-1), jnp.concatenate([zeros, w1], axis=-1)], axis=2)
    wc = wc.reshape(2, CMP_STRIDE * LANES, N_KV * ratio * CMP_HIDDEN).astype(BF16)
    pe = cmp_pos.reshape(2, ratio, CMP_STRIDE, 1, HEAD_DIM)
    pe = jnp.broadcast_to(pe, (2, ratio, CMP_STRIDE, N_KV, HEAD_DIM)).reshape(2, ratio, CMP_STRIDE * LANES)
    pe = jnp.pad(pe, ((0, 0), (0, 16 - ratio), (0, 0))).astype(BF16)
    w2 = w_cmp2[:, None]
    zeros2 = jnp.zeros_like(w2)
    w2p = jnp.concatenate([jnp.concatenate([w2, zeros2], axis=-1), jnp.concatenate([zeros2, w2], axis=-1)],
                          axis=1).astype(BF16)
    return wc, pe, b_cmp1.reshape(2, 1, CMP_HIDDEN), w2p


def _overlap_matrix(n_cmp_slots, n_slc, n_lanes):
    start = (np.arange(n_cmp_slots)[:, None] - 1) * CMP_STRIDE
    j = np.arange(n_lanes)[None, :]
    ov = (start <= j * SLC_BLOCK + SLC_BLOCK - 1) & (start + CMP_BLOCK - 1 >= j * SLC_BLOCK)
    ov &= (np.arange(n_cmp_slots)[:, None] >= 1) & (j < n_slc)
    return jnp.asarray(ov, dtype=BF16)


def kernel(x_prompt, x_sample, cache_kv, state_win, page_table, norm_g, w_in, ln_g, ln_b, w_s, b_s,
           cmp_pos, w_cmp1, b_cmp1, w_cmp2, w_out, final_g):
    n_batch, seq, _ = x_prompt.shape
    n_seq, t_new, _ = x_sample.shape
    depth, n_phys = cache_kv.shape[:2]
    n_pages = page_table.shape[1]
    past_len = n_pages * PAGE_SIZE
    win_buf = state_win.shape[2]
    assert depth == 1 and seq % TQ == 0 and n_seq * t_new == MIX_ROWS and win_buf == WINDOW
    assert (past_len + t_new - 1) // SLC_BLOCK == past_len // SLC_BLOCK and past_len % CMP_STRIDE == 0

    w_all = _prep_w_in(w_in[0])
    ng = norm_g[0].reshape(1, D_MODEL)
    lng = ln_g[0].reshape(1, D_A)
    lnb = ln_b[0].reshape(1, D_A)
    fg = final_g.reshape(1, D_MODEL)
    wo = w_out[0]
    wo_a = wo[:D_A].astype(BF16)
    wo_b = _head_pair_perm(wo[D_A:].T).T.astype(BF16)
    wc, pe, b1, w2p = _prep_compress(cmp_pos[0], w_cmp1[0], b_cmp1[0], w_cmp2[0])

    xp = x_prompt.reshape(n_batch * seq, D_MODEL)
    wmix_p, bmix_p = _prep_mix(w_s[0], b_s[0], CHUNK)
    cos_p, sin_p = _rope_tables(jnp.arange(seq))
    ag_p, qc_p, qr_p, rows_p, kvw_pt, ksel_p, kwin_p, gate_p, szb_p, rows_pt = _proj_call(
        xp, ng, w_all, lng, lnb, wmix_p, bmix_p, cos_p, sin_p, BF16, seq)
    pages_p = seq // PAGE_SIZE
    ident = jnp.arange(n_batch * pages_p, dtype=jnp.int32).reshape(n_batch, pages_p)
    cmp_p = _compress_call(rows_p.reshape(n_batch * pages_p, PAGE_SIZE, 4 * LANES), ident, wc, pe, b1, w2p,
                           pages_p, False)
    ov_p = _overlap_matrix(seq // CMP_STRIDE, seq // SLC_BLOCK, LANES)
    bg_p = _attn_prompt_call(qc_p, qr_p, cmp_p, ov_p, ksel_p, kwin_p, gate_p, szb_p, n_batch, seq)
    y_p = _merge_call(xp, ag_p, bg_p, wo_a, wo_b, fg)

    xs = x_sample.reshape(n_seq * t_new, D_MODEL)
    wmix_s, bmix_s = _prep_mix(w_s[0], b_s[0], t_new)
    cos_s, sin_s = _rope_tables(jnp.tile(past_len + jnp.arange(t_new), n_seq))
    ag_s, qc_s, qr_s, rows_s, kvw_s, _, _, gate_s, szb_s, vln_s = _proj_call(
        xs, ng, w_all, lng, lnb, wmix_s, bmix_s, cos_s, sin_s, F32, None)
    cache_t = cache_kv[0].transpose(0, 2, 3, 4, 1)
    cmp_s = _compress_call(cache_t, page_table, wc, pe, b1, w2p, SAMPLE_PAGES_PER_STEP, True)
    n_slc_s = -(-(past_len + t_new) // SLC_BLOCK)
    tb = past_len // SLC_BLOCK
    n_pick = TOP_N - 3
    ov_s = _overlap_matrix(past_len // CMP_STRIDE, n_slc_s, -(-n_slc_s // LANES) * LANES)
    win_prev = state_win[0].reshape(n_seq, win_buf, 2 * LANES)
    ocw_s, sel_s, need_s = _cmpwin_sample_call(qc_s, qr_s, cmp_s, ov_s, win_prev, kvw_s, gate_s, t_new, past_len, n_pick)
    sel_flat = sel_s[:, :, :n_pick].reshape(-1)
    assert n_pages == LANES
    bg_s = _sel_sample_call(sel_flat, need_s[:, :, 0, :].reshape(-1), page_table.reshape(-1), cache_t, qr_s, rows_s, gate_s, ocw_s, szb_s,
                            n_seq, t_new, n_pick, n_pages, tb)
    y_s = _merge_call(xs, ag_s, bg_s, wo_a, wo_b, fg)

    new_kv_p = rows_pt.reshape(n_batch, 4, N_KV, HEAD_DIM, seq).transpose(0, 4, 1, 2, 3)
    new_win_p = kvw_pt[:, :, seq - win_buf:].reshape(n_batch, 2, N_KV, HEAD_DIM, win_buf).transpose(0, 4, 1, 2, 3)
    new_win_s = jnp.concatenate([state_win[0][:, t_new:], kvw_s.reshape(n_seq, t_new, 2, N_KV, HEAD_DIM)], axis=1)
    return (y_p.reshape(n_batch, seq, D_MODEL),
            y_s.reshape(n_seq, t_new, D_MODEL),
            new_kv_p[None],
            new_win_p[None],
            rows_s.reshape(1, n_seq, t_new, 4, N_KV, HEAD_DIM),
            new_win_s[None],
            vln_s.reshape(1, n_seq, t_new, D_A))
```

```python
import functools
import math

import jax
import jax.numpy as jnp
import numpy as np
from jax import lax
from jax.experimental import pallas as pl
from jax.experimental.pallas import tpu as pltpu

F32 = jnp.float32
BF16 = jnp.bfloat16

D_MODEL = 1024
HEAD_DIM = 64
D_A = 512
A_GROUPS = 8
CHUNK = 128
N_HEADS = 8
D_B = 512
N_KV = 2
GQA = 4
CMP_BLOCK = 32
CMP_STRIDE = 16
CMP_HIDDEN = 128
SLC_BLOCK = 64
TOP_N = 16
WINDOW = 512
ROPE_THETA = 10000.0
EPS = 1e-6
SCALE = HEAD_DIM ** -0.5
NEG = -1e30
FORCE = 1e9
PAGE_SIZE = 128

LANES = 128
VMEM_LIMIT = 56 * 1024 * 1024

C_U, C_V, C_ZA, C_Q, C_KV, C_ZB, C_G, C_END = 0, 512, 1024, 1536, 2048, 2816, 3328, 3456

MIX_ROWS = 256
TM = 512
TM_MERGE = 1024
DMA_LOOP_UNROLL = 16
SAMPLE_SEQS_PER_STEP = 16
SAMPLE_PAGES_PER_STEP = 64
TQ = 256
SEGS_PER_PAGE = PAGE_SIZE // CMP_STRIDE


def _lane_iota(shape):
    return lax.broadcasted_iota(jnp.int32, shape, len(shape) - 1)


def _row_iota(shape):
    return lax.broadcasted_iota(jnp.int32, shape, len(shape) - 2)


def _div_pow2(x, n):
    assert n & (n - 1) == 0
    return lax.shift_right_logical(x, int(math.log2(n))) if n > 1 else x


def _mod_pow2(x, n):
    assert n & (n - 1) == 0
    return x & (n - 1)


def _dot(a, b):
    return jnp.dot(a, b, preferred_element_type=F32)


def _dot_nt(a, b):
    return lax.dot_general(a, b, (((1,), (1,)), ((), ())), preferred_element_type=F32)


def _rope(x, cos, sin_signed):
    lo = _mod_pow2(_lane_iota(x.shape), HEAD_DIM) < (HEAD_DIM // 2)
    swapped = jnp.where(lo, pltpu.roll(x, LANES - HEAD_DIM // 2, 1), pltpu.roll(x, HEAD_DIM // 2, 1))
    return x * cos + swapped * sin_signed


def _proj_kernel(x_ref, ng_ref, w_ref, lng_ref, lnb_ref, wmix_ref, bmix_ref, cos_ref, sin_ref,
                 ag_ref, qc_ref, qr_ref, rows_ref, kvw_ref, ksel_ref, kwin_ref, gate_ref, szb_ref,
                 extra_ref, *, prompt):
    x = x_ref[...]
    ms = jnp.mean(x * x, axis=-1, keepdims=True)
    hb = ((x * lax.rsqrt(ms + EPS)) * ng_ref[...]).astype(BF16)
    cos = cos_ref[...]
    sin = sin_ref[...]
    pair = 2 * LANES

    def proj(c0, n):
        return _dot(hb, w_ref[:, c0:c0 + n])

    def halves(x2):
        return x2[:, 0:LANES], x2[:, LANES:pair]

    v = jax.nn.gelu(proj(C_V, D_A))
    mu = jnp.mean(v, axis=-1, keepdims=True)
    vc = v - mu
    var = jnp.mean(vc * vc, axis=-1, keepdims=True)
    vln = vc * lax.rsqrt(var + EPS) * lng_ref[...] + lnb_ref[...]
    if not prompt:
        extra_ref[...] = vln
    vb = vln.astype(BF16)
    first_head = _lane_iota((MIX_ROWS, LANES)) < HEAD_DIM
    for gpp in range(A_GROUPS // 4):
        u2 = halves(jax.nn.gelu(proj(C_U + gpp * pair, pair)))
        za2 = halves(proj(C_ZA + gpp * pair, pair))
        for k in range(2):
            gp = 2 * gpp + k
            sl = slice(gp * LANES, (gp + 1) * LANES)
            mixed = jnp.concatenate(
                [jnp.where(first_head, _dot(wmix_ref[2 * gp], vb[r0:r0 + MIX_ROWS, sl]),
                           _dot(wmix_ref[2 * gp + 1], vb[r0:r0 + MIX_ROWS, sl])) + bmix_ref[:, sl]
                 for r0 in range(0, x.shape[0], MIX_ROWS)], axis=0)
            ag_ref[:, sl] = (u2[k] * mixed * jax.nn.silu(za2[k])).astype(ag_ref.dtype)

    for gpp in range(GQA // 2):
        q2 = halves(proj(C_Q + gpp * pair, pair))
        for k in range(2):
            sl = slice((2 * gpp + k) * LANES, (2 * gpp + k + 1) * LANES)
            qc_ref[:, sl] = (q2[k] * SCALE).astype(qc_ref.dtype)
            qr_ref[:, sl] = (_rope(q2[k], cos, sin) * SCALE).astype(qr_ref.dtype)

    kc, vcm = halves(proj(C_KV, pair))
    ks, vs = halves(proj(C_KV + pair, pair))
    kw, vw = halves(proj(C_KV + 2 * pair, pair))
    ks = _rope(ks, cos, sin)
    kw = _rope(kw, cos, sin)
    for i, blk in enumerate((kc, vcm, ks, vs)):
        rows_ref[:, i * LANES:(i + 1) * LANES] = blk
        if prompt:
            extra_ref[0, i * LANES:(i + 1) * LANES, :] = blk.T
    for i, blk in enumerate((kw, vw)):
        if prompt:
            kvw_ref[0, i * LANES:(i + 1) * LANES, :] = blk.T
        else:
            kvw_ref[:, i * LANES:(i + 1) * LANES] = blk
    ksel_ref[:, 0:LANES] = ks.astype(BF16)
    ksel_ref[:, LANES:2 * LANES] = vs.astype(BF16)
    kwin_ref[:, 0:LANES] = kw.astype(BF16)
    kwin_ref[:, LANES:2 * LANES] = vw.astype(BF16)

    gate_ref[...] = jax.nn.sigmoid(proj(C_G, LANES))
    for gpp in range(GQA // 2):
        zb2 = halves(proj(C_ZB + gpp * pair, pair))
        for k in range(2):
            sl = slice((2 * gpp + k) * LANES, (2 * gpp + k + 1) * LANES)
            szb_ref[:, sl] = jax.nn.silu(zb2[k]).astype(szb_ref.dtype)


def _proj_call(x2d, ng, w, lng, lnb, wmix, bmix, cos_t, sin_t, act_dtype, seq_len):
    n_rows = x2d.shape[0]
    tm = min(TM, n_rows)
    n_tiles = n_rows // tm
    pos_tiles = cos_t.shape[0] // tm
    row = lambda i: (i, 0)
    const2 = lambda i: (0, 0)
    out_shapes = [
        jax.ShapeDtypeStruct((n_rows, D_A), act_dtype),
        jax.ShapeDtypeStruct((n_rows, D_B), act_dtype),
        jax.ShapeDtypeStruct((n_rows, D_B), act_dtype),
        jax.ShapeDtypeStruct((n_rows, 4 * LANES), F32),
        jax.ShapeDtypeStruct((n_rows, 2 * LANES), F32),
        jax.ShapeDtypeStruct((n_rows, 2 * LANES), BF16),
        jax.ShapeDtypeStruct((n_rows, 2 * LANES), BF16),
        jax.ShapeDtypeStruct((n_rows, LANES), F32),
        jax.ShapeDtypeStruct((n_rows, D_B), act_dtype),
        jax.ShapeDtypeStruct((n_rows, D_A), F32),
    ]
    out_specs = [pl.BlockSpec((tm, s.shape[1]), row) for s in out_shapes]
    if seq_len is not None:
        tiles_per_seq = seq_len // tm
        dim_major = lambda i: (i // tiles_per_seq, 0, i % tiles_per_seq)
        out_shapes[9] = jax.ShapeDtypeStruct((n_rows // seq_len, 4 * LANES, seq_len), F32)
        out_specs[9] = pl.BlockSpec((1, 4 * LANES, tm), dim_major)
        kept_tiles = WINDOW // tm
        assert WINDOW % tm == 0 and kept_tiles <= tiles_per_seq
        out_shapes[4] = jax.ShapeDtypeStruct((n_rows // seq_len, 2 * LANES, WINDOW), F32)
        out_specs[4] = pl.BlockSpec(
            (1, 2 * LANES, tm),
            lambda i: (i // tiles_per_seq, 0, jnp.maximum(i % tiles_per_seq - (tiles_per_seq - kept_tiles), 0)))
    return pl.pallas_call(
        functools.partial(_proj_kernel, prompt=seq_len is not None),
        grid=(n_tiles,),
        in_specs=[
            pl.BlockSpec((tm, D_MODEL), row),
            pl.BlockSpec((1, D_MODEL), const2),
            pl.BlockSpec((D_MODEL, C_END), const2),
            pl.BlockSpec((1, D_A), const2),
            pl.BlockSpec((1, D_A), const2),
            pl.BlockSpec((A_GROUPS, MIX_ROWS, MIX_ROWS), lambda i: (0, 0, 0)),
            pl.BlockSpec((MIX_ROWS, D_A), const2),
            pl.BlockSpec((tm, LANES), lambda i: (i % pos_tiles, 0)),
            pl.BlockSpec((tm, LANES), lambda i: (i % pos_tiles, 0)),
        ],
        out_specs=out_specs,
        out_shape=out_shapes,
        compiler_params=pltpu.CompilerParams(dimension_semantics=("arbitrary",),
                                             vmem_limit_bytes=VMEM_LIMIT),
        name="proj",
    )(x2d, ng, w, lng, lnb, wmix, bmix, cos_t, sin_t)


def _compress_kernel(pt_ref, src_ref, wc_ref, pe_ref, b1_ref, w2_ref, out_ref, buf, sem, carry,
                     *maybe_stage, pages_per_step):
    transposed_src = bool(maybe_stage)
    n_steps = pl.num_programs(0) * pl.num_programs(1)
    step = pl.program_id(0) * pl.num_programs(1) + pl.program_id(1)
    m = pages_per_step * SEGS_PER_PAGE

    def page_copies(step_idx, slot, i):
        phys = pt_ref[step_idx * pages_per_step + i]
        if transposed_src:
            return [pltpu.make_async_copy(src_ref.at[phys, pl.ds(0, 2)], maybe_stage[0].at[slot, i], sem.at[slot])]
        return [pltpu.make_async_copy(src_ref.at[phys, :, pl.ds(c * LANES, LANES)],
                                      buf.at[slot, c, pl.ds(i * PAGE_SIZE, PAGE_SIZE), :],
                                      sem.at[slot]) for c in range(2)]

    def start_step(step_idx, slot):
        def body(i, carry_):
            for cp in page_copies(step_idx, slot, i):
                cp.start()
            return carry_
        lax.fori_loop(0, pages_per_step, body, 0, unroll=DMA_LOOP_UNROLL)

    def wait_step(step_idx, slot):
        def body(i, carry_):
            for cp in page_copies(step_idx, slot, i):
                cp.wait()
            return carry_
        lax.fori_loop(0, pages_per_step, body, 0, unroll=DMA_LOOP_UNROLL)

    slot = step % 2

    first_of_seq = pl.program_id(1) == 0
    first_row = _row_iota((m, LANES)) == 0

    def compress_rows(rows_slot, seg_pitch):
        for c in range(2):
            pieces = [buf[rows_slot, c, pl.ds(j, m, stride=seg_pitch), :].astype(BF16)
                      for j in range(CMP_STRIDE)]
            part = _dot(jnp.concatenate(pieces, axis=1), wc_ref[c])
            pe_part = _dot(pe_ref[c], wc_ref[c])
            acc_out = None
            for h in range(N_KV):
                lo = slice(h * 2 * CMP_HIDDEN, h * 2 * CMP_HIDDEN + CMP_HIDDEN)
                hi = slice(h * 2 * CMP_HIDDEN + CMP_HIDDEN, (h + 1) * 2 * CMP_HIDDEN)
                bias = pe_part[0:1, lo] + pe_part[1:2, hi] + b1_ref[c]
                part0 = part[:, lo]
                prev_last = jnp.where(first_of_seq, 0.0, carry[c, h, 0:1, :])
                prev0 = jnp.where(first_row, prev_last, pltpu.roll(part0, 1, 0))
                carry[c, h, 0:1, :] = part0[m - 1:m, :]
                hid = jax.nn.gelu(prev0 + part[:, hi] + bias).astype(BF16)
                contrib = _dot(hid, w2_ref[c, h])
                acc_out = contrib if acc_out is None else acc_out + contrib
            out_ref[0, :, c * LANES:(c + 1) * LANES] = acc_out.astype(out_ref.dtype)

    if transposed_src:
        stage = maybe_stage[0]
        seg_pitch = CMP_STRIDE + 1

        def transpose_step(stage_slot):
            for i in range(pages_per_step):
                for c in range(2):
                    rows = stage[stage_slot, i, c].reshape(N_KV * HEAD_DIM, PAGE_SIZE).T
                    for s in range(SEGS_PER_PAGE):
                        r0 = (i * SEGS_PER_PAGE + s) * seg_pitch
                        buf[stage_slot, c, r0:r0 + CMP_STRIDE, :] = rows[s * CMP_STRIDE:(s + 1) * CMP_STRIDE]

        @pl.when(step == 0)
        def _():
            start_step(0, 0)
            start_step(1, 1)
            wait_step(0, 0)
            transpose_step(0)

        @pl.when(step + 1 < n_steps)
        def _():
            wait_step(step + 1, (step + 1) % 2)

        @pl.when(step + 2 < n_steps)
        def _():
            start_step(step + 2, slot)

        for parity in range(2):
            @pl.when(slot == parity)
            def _():
                transpose_step(1 - parity)
                compress_rows(parity, seg_pitch)
    else:
        @pl.when(step == 0)
        def _():
            start_step(0, 0)

        @pl.when(step + 1 < n_steps)
        def _():
            start_step(step + 1, (step + 1) % 2)

        wait_step(step, slot)
        compress_rows(slot, CMP_STRIDE)


def _compress_call(src, page_table, wc, pe, b1, w2p, pages_per_step, transposed_src):
    n_seq, n_pages = page_table.shape
    n_chunks = n_pages // pages_per_step
    m = pages_per_step * SEGS_PER_PAGE
    assert not transposed_src or n_seq * n_chunks >= 2
    seg_pitch = CMP_STRIDE + 1 if transposed_src else CMP_STRIDE
    scratch = [
        pltpu.VMEM((2, 2, m * seg_pitch, LANES), F32),
        pltpu.SemaphoreType.DMA((2,)),
        pltpu.VMEM((2, N_KV, 8, CMP_HIDDEN), F32),
    ]
    if transposed_src:
        scratch.append(pltpu.VMEM((2, pages_per_step, 2, N_KV, HEAD_DIM, PAGE_SIZE), F32))
    grid_spec = pltpu.PrefetchScalarGridSpec(
        num_scalar_prefetch=1,
        grid=(n_seq, n_chunks),
        in_specs=[
            pl.BlockSpec(memory_space=pl.ANY),
            pl.BlockSpec((2, CMP_STRIDE * LANES, 4 * CMP_HIDDEN), lambda b, k, pt: (0, 0, 0)),
            pl.BlockSpec((2, 16, CMP_STRIDE * LANES), lambda b, k, pt: (0, 0, 0)),
            pl.BlockSpec((2, 1, CMP_HIDDEN), lambda b, k, pt: (0, 0, 0)),
            pl.BlockSpec((2, N_KV, CMP_HIDDEN, LANES), lambda b, k, pt: (0, 0, 0, 0)),
        ],
        out_specs=pl.BlockSpec((1, m, 2 * LANES), lambda b, k, pt: (b, k, 0)),
        scratch_shapes=scratch,
    )
    return pl.pallas_call(
        functools.partial(_compress_kernel, pages_per_step=pages_per_step),
        grid_spec=grid_spec,
        out_shape=jax.ShapeDtypeStruct((n_seq, n_pages * SEGS_PER_PAGE, 2 * LANES), BF16),
        compiler_params=pltpu.CompilerParams(dimension_semantics=("arbitrary", "arbitrary"),
                                             vmem_limit_bytes=VMEM_LIMIT),
        name="compress",
    )(page_table.reshape(-1), src, wc, pe, b1, w2p)


def _softmax_rows(s, mask):
    s = jnp.where(mask, s, NEG)
    e = jnp.where(mask, jnp.exp(s - jnp.max(s, axis=-1, keepdims=True)), 0.0)
    return e / jnp.maximum(jnp.sum(e, axis=-1, keepdims=True), 1.0)


def _dot_split(p, w):
    hi = p.astype(BF16)
    lo = (p - hi.astype(F32)).astype(BF16)
    return _dot(hi, w) + _dot(lo, w)


def _gate_col(gates, idx):
    return gates[:, idx:idx + 1]


def _attn_prompt_kernel(qc_ref, qr_ref, cmp_ref, ov_ref, eg_ref, ksel_ref, kwin_ref, gate_ref, szb_ref,
                        out_ref, q_sc, m_sc, acc_sc, cmp_sc):
    qi = pl.program_id(1)
    lane = _lane_iota((TQ, LANES))
    lo_half = lane < HEAD_DIM
    r_minus_c = _row_iota((TQ, TQ)) - _lane_iota((TQ, TQ))
    causal_bias = jnp.where(r_minus_c >= 0, 0.0, NEG)
    far_bias = jnp.where(r_minus_c < 0, 0.0, NEG)
    pair_bias = jnp.concatenate([jnp.zeros((TQ, TQ), F32), causal_bias], axis=1)
    kcmp = cmp_ref[0, :, 0:LANES]
    vcmp = cmp_ref[0, :, LANES:2 * LANES]
    n_slc = ksel_ref.shape[0] // SLC_BLOCK
    blocks_per_tile = TQ // SLC_BLOCK

    def flash_tile(br, h, k_tile, v_aug, bias):
        width = k_tile.shape[0]
        s = _dot_nt(q_sc[h], k_tile)
        if bias is not None:
            s = (s.reshape(GQA, TQ, width) + bias[None]).reshape(GQA * TQ, width)
        m_prev = m_sc[h]
        m_new = jnp.maximum(m_prev, jnp.max(s, axis=-1, keepdims=True))
        alpha = jnp.exp(m_prev - m_new)
        p = jnp.exp(s - jnp.concatenate([m_new] * (width // LANES), axis=1))
        acc_sc[br, h] = alpha * acc_sc[br, h] + _dot(p.astype(BF16), v_aug)
        m_sc[h] = m_new

    def flash_init(br):
        m_sc[...] = jnp.full(m_sc.shape, NEG, F32)
        acc_sc[br] = jnp.zeros(acc_sc.shape[1:], F32)

    n_hg = N_KV * GQA
    q_cmp = jnp.concatenate([jnp.where(_div_pow2(lane, HEAD_DIM) == h, qc_ref[:, g * LANES:(g + 1) * LANES], 0)
                             for h in range(N_KV) for g in range(GQA)], axis=0)
    lane_all = _lane_iota((n_hg * TQ, LANES))
    tq_all = qi * TQ + _mod_pow2(_row_iota((n_hg * TQ, LANES)), TQ)
    cmp_valid = (lane_all >= 1) & (lane_all * CMP_STRIDE + (CMP_STRIDE - 1) <= tq_all)
    p_all = _softmax_rows(_dot_nt(q_cmp, kcmp), cmp_valid)
    o_all = _dot(p_all.astype(BF16), vcmp)
    for g in range(GQA):
        cmp_sc[:, g * LANES:(g + 1) * LANES] = jnp.where(
            lo_half, o_all[g * TQ:(g + 1) * TQ], o_all[(GQA + g) * TQ:(GQA + g + 1) * TQ])
    p4 = p_all.reshape(N_KV, GQA, TQ, LANES)
    p_sum = sum(p4[:, g] for g in range(GQA)).reshape(N_KV * TQ, LANES)
    imp = _dot_split(p_sum, ov_ref[...])
    lane_ht = _lane_iota((N_KV * TQ, LANES))
    tq_ht = qi * TQ + _mod_pow2(_row_iota((N_KV * TQ, LANES)), TQ)
    tb = _div_pow2(tq_ht, SLC_BLOCK)
    forced = (lane_ht == 0) | (lane_ht == tb) | (lane_ht == tb - 1)
    valid = lane_ht * SLC_BLOCK <= tq_ht
    score = jnp.where(forced, FORCE, jnp.where(valid, imp, NEG))
    score_t = score.T[0:n_slc, :]
    blk = _row_iota((n_slc, N_KV * TQ))
    rank = jnp.zeros((n_slc, N_KV * TQ), jnp.int32)
    for j in range(n_slc):
        other = score_t[j:j + 1, :]
        ahead = (other > score_t) | ((other == score_t) & (blk > j))
        rank = rank + ahead.astype(jnp.int32)
    sel_bias_t = jnp.where(rank < TOP_N, 0.0, NEG)

    for h in range(N_KV):
        keep = _div_pow2(lane, HEAD_DIM) == h
        other_off = HEAD_DIM * (1 - h)
        pieces = [jnp.zeros((other_off, TQ), F32)] if other_off else []
        pieces += [sel_bias_t[:, h * TQ:(h + 1) * TQ], jnp.zeros((LANES - other_off - n_slc, TQ), F32)]
        sel_bias = jnp.concatenate(pieces, axis=0).T.astype(BF16)
        for g in range(GQA):
            sl = slice(g * LANES, (g + 1) * LANES)
            q_sc[h, g * TQ:(g + 1) * TQ, :] = jnp.where(keep, qr_ref[:, sl], sel_bias)

    def span_operands(kv_ref, kt, n_tiles, h, other_lanes):
        k = kv_ref[pl.ds(kt * TQ, n_tiles * TQ), 0:LANES]
        v = kv_ref[pl.ds(kt * TQ, n_tiles * TQ), LANES:2 * LANES]
        own = _div_pow2(_lane_iota(k.shape), HEAD_DIM) == h
        return jnp.where(own, k, other_lanes(k.shape)), jnp.where(own, v, 1.0)

    def sel_span(kt, n_tiles, bias):
        for h in range(N_KV):
            def indicator(shape, h=h):
                first = HEAD_DIM * (1 - h) + kt * blocks_per_tile
                hit = _lane_iota(shape) == first + _div_pow2(_row_iota(shape), SLC_BLOCK)
                return jnp.where(hit, 1, 0).astype(BF16)
            flash_tile(0, h, *span_operands(ksel_ref, kt, n_tiles, h, indicator), bias)

    flash_init(0)

    def sel_pair(i, carry_):
        sel_span(2 * i, 2, None)
        return carry_
    lax.fori_loop(0, _div_pow2(jnp.maximum(qi - 1, 0), 2), sel_pair, 0)

    @pl.when((qi >= 2) & (_mod_pow2(qi, 2) == 0))
    def _():
        sel_span(qi - 2, 1, None)

    @pl.when(qi >= 1)
    def _():
        sel_span(qi - 1, 2, pair_bias)

    @pl.when(qi == 0)
    def _():
        sel_span(qi, 1, causal_bias)

    def win_span(kt, n_tiles, bias):
        for h in range(N_KV):
            flash_tile(1, h, *span_operands(kwin_ref, kt, n_tiles, h, lambda shape: jnp.zeros(shape, BF16)), bias)

    flash_init(1)
    far = WINDOW // TQ
    assert far == 2

    @pl.when(qi >= far)
    def _():
        win_span(qi - far, 1, far_bias)

    @pl.when(qi >= 1)
    def _():
        win_span(qi - 1, 2, pair_bias)

    @pl.when(qi == 0)
    def _():
        win_span(qi, 1, causal_bias)

    gates_x = _dot_split(gate_ref[...], eg_ref[...])
    for g in range(GQA):
        rows = slice(g * TQ, (g + 1) * TQ)
        sl = slice(g * LANES, (g + 1) * LANES)
        o = gates_x[:, (g * 3) * LANES:(g * 3 + 1) * LANES] * cmp_sc[:, sl]
        for br in range(2):
            a0 = acc_sc[br, 0, rows, :]
            a1 = acc_sc[br, 1, rows, :]
            num = jnp.where(lo_half, a0, a1)
            den = pltpu.roll(jnp.where(lo_half, a1, a0), HEAD_DIM, 1)
            o = o + gates_x[:, (g * 3 + 1 + br) * LANES:(g * 3 + 2 + br) * LANES] * (num / den)
        out_ref[:, sl] = (o * szb_ref[:, sl].astype(F32)).astype(out_ref.dtype)


def _gate_expansion():
    eg = np.zeros((LANES, GQA * 3 * LANES), np.float32)
    for h in range(N_KV):
        for g in range(GQA):
            for br in range(3):
                c0 = (g * 3 + br) * LANES + h * HEAD_DIM
                eg[h * GQA * 3 + g * 3 + br, c0:c0 + HEAD_DIM] = 1.0
    return jnp.asarray(eg, dtype=BF16)


def _attn_prompt_call(qc, qr, cmp_p, ov, ksel, kwin, gates, szb, n_batch, seq):
    nq = seq // TQ
    tile = lambda b, q: (b * nq + q, 0)
    whole = lambda b, q: (b, 0)
    return pl.pallas_call(
        _attn_prompt_kernel,
        grid=(n_batch, nq),
        in_specs=[
            pl.BlockSpec((TQ, D_B), tile),
            pl.BlockSpec((TQ, D_B), tile),
            pl.BlockSpec((1, LANES, 2 * LANES), lambda b, q: (b, 0, 0)),
            pl.BlockSpec((LANES, LANES), lambda b, q: (0, 0)),
            pl.BlockSpec((LANES, GQA * 3 * LANES), lambda b, q: (0, 0)),
            pl.BlockSpec((seq, 2 * LANES), whole),
            pl.BlockSpec((seq, 2 * LANES), whole),
            pl.BlockSpec((TQ, LANES), tile),
            pl.BlockSpec((TQ, D_B), tile),
        ],
        out_specs=pl.BlockSpec((TQ, D_B), tile),
        out_shape=jax.ShapeDtypeStruct((n_batch * seq, D_B), BF16),
        scratch_shapes=[
            pltpu.VMEM((N_KV, GQA * TQ, LANES), BF16),
            pltpu.VMEM((N_KV, GQA * TQ, LANES), F32),
            pltpu.VMEM((2, N_KV, GQA * TQ, LANES), F32),
            pltpu.VMEM((TQ, D_B), F32),
        ],
        compiler_params=pltpu.CompilerParams(dimension_semantics=("arbitrary", "arbitrary"),
                                             vmem_limit_bytes=VMEM_LIMIT),
        name="attn_prompt",
    )(qc, qr, cmp_p, ov, _gate_expansion(), ksel, kwin, gates, szb)


def _cmpwin_sample_kernel(qc_ref, qr_ref, cmp_ref, ov_ref, win_ref, kvw_ref, gate_ref,
                          ocw_ref, sel_ref, need_ref, *, past_len, n_pick, t_new):
    seqs = cmp_ref.shape[0]
    n_rows = N_KV * GQA * t_new
    lane = _lane_iota((t_new, LANES))
    n_cmp = cmp_ref.shape[1]
    row_t = _mod_pow2(_row_iota((n_rows, n_cmp)), t_new)
    slot = _lane_iota((n_rows, n_cmp))
    cmp_valid = (slot >= 1) & (slot * CMP_STRIDE + (CMP_STRIDE - 1) <= past_len + row_t)
    n_buf = win_ref.shape[1]
    dist = n_buf + _mod_pow2(_row_iota((n_rows, n_buf)), t_new) - _lane_iota((n_rows, n_buf))
    win_valid = (dist >= 0) & (dist < WINDOW)
    row_t1 = _mod_pow2(_row_iota((n_rows, 1)), t_new)

    def one_sequence(sq):
        rows_sq = slice(sq * t_new, (sq + 1) * t_new)
        gates = gate_ref[rows_sq, :]

        def q_rows(ref):
            return jnp.concatenate(
                [jnp.where(_div_pow2(lane, HEAD_DIM) == h, ref[rows_sq, g * LANES:(g + 1) * LANES], 0).astype(BF16)
                 for h in range(N_KV) for g in range(GQA)], axis=0)

        p = _softmax_rows(_dot_nt(q_rows(qc_ref), cmp_ref[sq, :, 0:LANES]), cmp_valid)
        o_cmp = _dot(p.astype(BF16), cmp_ref[sq, :, LANES:2 * LANES])
        p_sum = jnp.concatenate(
            [sum(p[(h * GQA + g) * t_new:(h * GQA + g + 1) * t_new] for g in range(GQA)) for h in range(N_KV)],
            axis=0)
        imp = _dot_split(p_sum, ov_ref[...])

        qr = q_rows(qr_ref)
        k_buf = win_ref[sq, :, 0:LANES].astype(BF16)
        v_buf = win_ref[sq, :, LANES:2 * LANES].astype(BF16)
        s_buf = jnp.where(win_valid, _dot_nt(qr, k_buf), NEG)
        k_new = kvw_ref[rows_sq, 0:LANES].astype(BF16).astype(F32)
        v_new = kvw_ref[rows_sq, LANES:2 * LANES].astype(BF16).astype(F32)
        qr32 = qr.astype(F32)
        s_new = [jnp.where(row_t1 >= i, jnp.sum(qr32 * k_new[i:i + 1, :], axis=-1, keepdims=True), NEG)
                 for i in range(t_new)]
        m = jnp.max(s_buf, axis=-1, keepdims=True)
        for s in s_new:
            m = jnp.maximum(m, s)
        e_buf = jnp.exp(s_buf - m)
        e_new = [jnp.exp(s - m) for s in s_new]
        denom = jnp.sum(e_buf, axis=-1, keepdims=True) + sum(e_new)
        o_win = _dot(e_buf.astype(BF16), v_buf)
        for i in range(t_new):
            o_win = o_win + e_new[i].astype(BF16).astype(F32) * v_new[i:i + 1, :]
        o_win = o_win / denom

        for g in range(GQA):
            parts = []
            for h in range(N_KV):
                rows = slice((h * GQA + g) * t_new, (h * GQA + g + 1) * t_new)
                base = h * GQA * 3 + g * 3
                parts.append(_gate_col(gates, base) * o_cmp[rows] + _gate_col(gates, base + 2) * o_win[rows])
            ocw_ref[rows_sq, g * LANES:(g + 1) * LANES] = jnp.where(lane < HEAD_DIM, parts[0], parts[1])
        return imp

    imp = jnp.concatenate([one_sequence(sq) for sq in range(seqs)], axis=0)
    blk = _lane_iota(imp.shape)
    tb = (past_len + t_new - 1) // SLC_BLOCK
    candidate = (blk >= 1) & (blk < tb - 1)
    score = jnp.where(candidate, imp, -1.0)
    out_lane = _lane_iota((imp.shape[0], LANES))
    picks = jnp.zeros((imp.shape[0], LANES), jnp.int32)
    blocks_per_page = PAGE_SIZE // SLC_BLOCK
    page_used = jnp.zeros((imp.shape[0], LANES), jnp.int32)
    for k in range(n_pick):
        best = jnp.max(score, axis=-1, keepdims=True)
        idx = jnp.min(jnp.where(score == best, blk, 1 << 20), axis=-1, keepdims=True)
        picks = jnp.where(out_lane == k, idx, picks)
        page_used = jnp.where(out_lane == _div_pow2(idx, blocks_per_page), 1, page_used)
        score = jnp.where(blk == idx, -2.0, score)
    sel_ref[...] = picks.reshape(sel_ref.shape)
    used = jnp.max(page_used.reshape(seqs * N_KV, t_new, LANES), axis=1)
    page = _lane_iota(used.shape)
    used = jnp.where((page == 0) | (page == (tb - 1) // blocks_per_page), 1, used)
    need_ref[...] = jnp.broadcast_to(used.reshape(seqs, N_KV, 1, LANES), need_ref.shape)


def _cmpwin_sample_call(qc, qr, cmp_s, ov, win, kvw, gates, t_new, past_len, n_pick):
    n_seq = cmp_s.shape[0]
    seqs = SAMPLE_SEQS_PER_STEP
    assert n_seq % seqs == 0
    tile = lambda b: (b, 0)
    per_seq = lambda b: (b, 0, 0)
    return pl.pallas_call(
        functools.partial(_cmpwin_sample_kernel, past_len=past_len, n_pick=n_pick, t_new=t_new),
        grid=(n_seq // seqs,),
        in_specs=[
            pl.BlockSpec((seqs * t_new, D_B), tile),
            pl.BlockSpec((seqs * t_new, D_B), tile),
            pl.BlockSpec((seqs, cmp_s.shape[1], 2 * LANES), per_seq),
            pl.BlockSpec(ov.shape, lambda b: (0, 0)),
            pl.BlockSpec((seqs, win.shape[1], 2 * LANES), per_seq),
            pl.BlockSpec((seqs * t_new, 2 * LANES), tile),
            pl.BlockSpec((seqs * t_new, LANES), tile),
        ],
        out_specs=[
            pl.BlockSpec((seqs * t_new, D_B), tile),
            pl.BlockSpec((seqs, N_KV * t_new, LANES), per_seq),
            pl.BlockSpec((seqs, N_KV, 8, LANES), lambda b: (b, 0, 0, 0)),
        ],
        out_shape=[
            jax.ShapeDtypeStruct((n_seq * t_new, D_B), F32),
            jax.ShapeDtypeStruct((n_seq, N_KV * t_new, LANES), jnp.int32),
            jax.ShapeDtypeStruct((n_seq, N_KV, 8, LANES), jnp.int32),
        ],
        compiler_params=pltpu.CompilerParams(dimension_semantics=("arbitrary",),
                                             vmem_limit_bytes=VMEM_LIMIT),
        name="cmpwin_sample",
    )(qc, qr, cmp_s, ov, win, kvw, gates)


def _sel_sample_kernel(sel_ref, need_ref, pt_ref, cache_ref, qr_ref, rows_ref, gate_ref, ocw_ref, szb_ref,
                       out_ref, buf, sem, *, n_pick, n_pages, tb):
    t_new = qr_ref.shape[0]
    n_own = t_new * n_pick
    n_steps = pl.num_programs(0)
    step = pl.program_id(0)
    blocks_per_page = PAGE_SIZE // SLC_BLOCK

    def for_all_blocks(seq, slot, fn):
        for h in range(N_KV):
            def body(p, carry_, h=h):
                @pl.when(need_ref[(seq * N_KV + h) * n_pages + p] != 0)
                def _():
                    fn(pltpu.make_async_copy(cache_ref.at[pt_ref[seq * n_pages + p], pl.ds(2, 2), h],
                                             buf.at[slot, h * n_pages + p], sem.at[slot]))
                return carry_
            lax.fori_loop(0, n_pages, body, 0, unroll=DMA_LOOP_UNROLL)

    @pl.when(step == 0)
    def _():
        for_all_blocks(0, 0, lambda cp: cp.start())

    @pl.when(step + 1 < n_steps)
    def _():
        for_all_blocks(step + 1, (step + 1) % 2, lambda cp: cp.start())

    slot = step % 2
    n_copies = lax.fori_loop(0, N_KV * n_pages, lambda i, acc: acc + need_ref[step * N_KV * n_pages + i], 0,
                             unroll=DMA_LOOP_UNROLL)
    for bit in range((N_KV * n_pages).bit_length()):
        @pl.when((n_copies & (1 << bit)) != 0)
        def _():
            pltpu.make_async_copy(cache_ref.at[pl.ds(0, 1 << bit), pl.ds(2, 2), 0],
                                  buf.at[slot, pl.ds(0, 1 << bit)], sem.at[slot]).wait()

    n_rows = GQA * t_new
    half = _div_pow2(_lane_iota((1, LANES)), SLC_BLOCK)
    row_t = _mod_pow2(_row_iota((n_rows, 1)), t_new)
    gates = gate_ref[...]

    def one_head(h):
        def head_half(x):
            return x[:, h * HEAD_DIM:(h + 1) * HEAD_DIM]

        qr = jnp.concatenate([head_half(qr_ref[:, g * LANES:(g + 1) * LANES]) for g in range(GQA)], axis=0)
        qrb = qr.astype(BF16)
        qr32 = qrb.astype(F32)

        def slabs(blocks, kind):
            return jnp.concatenate(
                [buf[slot, h * n_pages + _div_pow2(blk, blocks_per_page), kind].astype(BF16) for blk in blocks],
                axis=1)

        def own_blocks(t):
            return [sel_ref[(step * N_KV + h) * n_own + t * n_pick + k] for k in range(n_pick)]

        bias_sh = jnp.concatenate(
            [jnp.where(half == blk % blocks_per_page, 0.0, NEG) for blk in (0, tb - 1)], axis=1)
        s_sh = _dot(qrb, slabs((0, tb - 1), 0)) + bias_sh
        k_new = head_half(rows_ref[:, 2 * LANES:3 * LANES]).astype(BF16).astype(F32)
        v_new = head_half(rows_ref[:, 3 * LANES:4 * LANES]).astype(BF16).astype(F32)
        s_new = [jnp.where(row_t >= i, jnp.sum(qr32 * k_new[i:i + 1, :], axis=-1, keepdims=True), NEG)
                 for i in range(t_new)]
        own_keys = n_pick * PAGE_SIZE
        s_own = jnp.zeros((n_rows, own_keys), F32)
        for t in range(t_new):
            bias_t = jnp.concatenate(
                [jnp.where(half == _mod_pow2(blk, blocks_per_page), 0.0, NEG) for blk in own_blocks(t)], axis=1)
            s_t = _dot(qrb, slabs(own_blocks(t), 0)) + bias_t
            s_own = jnp.where(row_t == t, s_t, s_own)

        m = jnp.maximum(jnp.max(s_sh, axis=-1, keepdims=True), jnp.max(s_own, axis=-1, keepdims=True))
        for s in s_new:
            m = jnp.maximum(m, s)
        e_sh = jnp.exp(s_sh - m)
        e_own = jnp.exp(s_own - m)
        e_new = [jnp.exp(s - m) for s in s_new]
        denom = jnp.sum(e_sh, axis=-1, keepdims=True) + jnp.sum(e_own, axis=-1, keepdims=True) + sum(e_new)
        o = _dot_nt(e_sh.astype(BF16), slabs((0, tb - 1), 1))
        for t in range(t_new):
            o = o + _dot_nt(jnp.where(row_t == t, e_own, 0.0).astype(BF16), slabs(own_blocks(t), 1))
        for i in range(t_new):
            o = o + e_new[i].astype(BF16).astype(F32) * v_new[i:i + 1, :]
        o = o / denom
        return [_gate_col(gates, h * GQA * 3 + g * 3 + 1) * o[g * t_new:(g + 1) * t_new] for g in range(GQA)]

    gated = [one_head(h) for h in range(N_KV)]
    for g in range(GQA):
        sl = slice(g * LANES, (g + 1) * LANES)
        o_slc = jnp.concatenate([gated[h][g] for h in range(N_KV)], axis=1)
        out_ref[:, sl] = (ocw_ref[:, sl] + o_slc) * szb_ref[:, sl]


def _sel_sample_call(sel_flat, need_flat, pt_flat, cache_t, qr, rows, gates, ocw, szb, n_seq, t_new, n_pick, n_pages,
                     tb):
    tile = lambda b, sel, need, pt: (b, 0)
    grid_spec = pltpu.PrefetchScalarGridSpec(
        num_scalar_prefetch=3,
        grid=(n_seq,),
        in_specs=[
            pl.BlockSpec(memory_space=pl.ANY),
            pl.BlockSpec((t_new, D_B), tile),
            pl.BlockSpec((t_new, 4 * LANES), tile),
            pl.BlockSpec((t_new, LANES), tile),
            pl.BlockSpec((t_new, D_B), tile),
            pl.BlockSpec((t_new, D_B), tile),
        ],
        out_specs=pl.BlockSpec((t_new, D_B), tile),
        scratch_shapes=[
            pltpu.VMEM((2, N_KV * n_pages, 2, HEAD_DIM, PAGE_SIZE), F32),
            pltpu.SemaphoreType.DMA((2,)),
        ],
    )
    return pl.pallas_call(
        functools.partial(_sel_sample_kernel, n_pick=n_pick, n_pages=n_pages, tb=tb),
        grid_spec=grid_spec,
        out_shape=jax.ShapeDtypeStruct((n_seq * t_new, D_B), F32),
        compiler_params=pltpu.CompilerParams(dimension_semantics=("arbitrary",),
                                             vmem_limit_bytes=VMEM_LIMIT),
        name="sel_sample",
    )(sel_flat, need_flat, pt_flat, cache_t, qr, rows, gates, ocw, szb)


def _merge_kernel(x_ref, a_ref, b_ref, wa_ref, wb_ref, fg_ref, y_ref):
    delta = _dot(a_ref[...].astype(BF16), wa_ref[...]) + _dot(b_ref[...].astype(BF16), wb_ref[...])
    x = x_ref[...] + delta
    ms = jnp.mean(x * x, axis=-1, keepdims=True)
    y_ref[...] = (x * lax.rsqrt(ms + EPS)) * fg_ref[...]


def _merge_call(x2d, a, b, wa, wb, fg):
    n_rows = x2d.shape[0]
    tm = min(TM_MERGE, n_rows)
    row = lambda i: (i, 0)
    const2 = lambda i: (0, 0)
    return pl.pallas_call(
        _merge_kernel,
        grid=(n_rows // tm,),
        in_specs=[
            pl.BlockSpec((tm, D_MODEL), row),
            pl.BlockSpec((tm, D_A), row),
            pl.BlockSpec((tm, D_B), row),
            pl.BlockSpec((D_A, D_MODEL), const2),
            pl.BlockSpec((D_B, D_MODEL), const2),
            pl.BlockSpec((1, D_MODEL), const2),
        ],
        out_specs=pl.BlockSpec((tm, D_MODEL), row),
        out_shape=jax.ShapeDtypeStruct((n_rows, D_MODEL), F32),
        compiler_params=pltpu.CompilerParams(dimension_semantics=("arbitrary",),
                                             vmem_limit_bytes=VMEM_LIMIT),
        name="merge",
    )(x2d, a, b, wa, wb, fg)


def _head_pair_perm(w_cols):
    lead = w_cols.shape[:-1]
    return w_cols.reshape(*lead, N_KV, GQA, HEAD_DIM).swapaxes(-3, -2).reshape(*lead, D_B)


def _prep_w_in(w_in):
    cuts = np.cumsum([D_A, D_A, D_A, D_B, 6 * N_KV * HEAD_DIM, 3 * N_HEADS]).tolist()
    u, v, za, q, kv, g, zb = jnp.split(w_in, cuts, axis=-1)
    g_pad = jnp.pad(g, ((0, 0), (0, LANES - g.shape[1])))
    return jnp.concatenate([u, v, za, _head_pair_perm(q), kv, _head_pair_perm(zb), g_pad], axis=-1).astype(BF16)


def _prep_mix(w_s, b_s, chunk_len):
    reps = MIX_ROWS // chunk_len
    row = np.arange(MIX_ROWS)[:, None]
    col = np.arange(MIX_ROWS)[None, :]
    causal_same_chunk = (row // chunk_len == col // chunk_len) & (col <= row)
    spread = jnp.asarray(np.arange(chunk_len)[:, None] == np.arange(MIX_ROWS)[None, :] % chunk_len, dtype=w_s.dtype)
    wide = jnp.einsum("gts,sc->gtc", w_s[:, :chunk_len, :chunk_len], spread)
    wmix = jnp.where(causal_same_chunk, jnp.tile(wide, (1, reps, 1)), 0).astype(BF16)
    bias = jnp.repeat(b_s[:, :chunk_len].T, HEAD_DIM, axis=1)
    return wmix, jnp.tile(bias, (reps, 1))


def _rope_tables(pos):
    half = HEAD_DIM // 2
    inv = ROPE_THETA ** (-jnp.arange(half, dtype=F32) / half)
    ang = pos.astype(F32)[:, None] * inv
    cos = jnp.tile(jnp.cos(ang), (1, LANES // half))
    sin = jnp.tile(jnp.concatenate([-jnp.sin(ang), jnp.sin(ang)], axis=1), (1, LANES // HEAD_DIM))
    return cos, sin


def _prep_compress(cmp_pos, w_cmp1, b_cmp1, w_cmp2):
    ratio = CMP_BLOCK // CMP_STRIDE
    w1 = w_cmp1.reshape(2, ratio, CMP_STRIDE, HEAD_DIM, CMP_HIDDEN).transpose(0, 2, 3, 1, 4)
    w1 = w1.reshape(2, CMP_STRIDE, HEAD_DIM, ratio * CMP_HIDDEN)
    zeros = jnp.zeros_like(w1)
    wc = jnp.stack([jnp.concatenate([w1, zeros], axis=-1), jnp.concatenate([zeros, w1], axis=-1)], axis=2)
    wc = wc.reshape(2, CMP_STRIDE * LANES, N_KV * ratio * CMP_HIDDEN).astype(BF16)
    pe = cmp_pos.reshape(2, ratio, CMP_STRIDE, 1, HEAD_DIM)
    pe = jnp.broadcast_to(pe, (2, ratio, CMP_STRIDE, N_KV, HEAD_DIM)).reshape(2, ratio, CMP_STRIDE * LANES)
    pe = jnp.pad(pe, ((0, 0), (0, 16 - ratio), (0, 0))).astype(BF16)
    w2 = w_cmp2[:, None]
    zeros2 = jnp.zeros_like(w2)
    w2p = jnp.concatenate([jnp.concatenate([w2, zeros2], axis=-1), jnp.concatenate([zeros2, w2], axis=-1)],
                          axis=1).astype(BF16)
    return wc, pe, b_cmp1.reshape(2, 1, CMP_HIDDEN), w2p


def _overlap_matrix(n_cmp_slots, n_slc, n_lanes):
    start = (np.arange(n_cmp_slots)[:, None] - 1) * CMP_STRIDE
    j = np.arange(n_lanes)[None, :]
    ov = (start <= j * SLC_BLOCK + SLC_BLOCK - 1) & (start + CMP_BLOCK - 1 >= j * SLC_BLOCK)
    ov &= (np.arange(n_cmp_slots)[:, None] >= 1) & (j < n_slc)
    return jnp.asarray(ov, dtype=BF16)


def kernel(x_prompt, x_sample, cache_kv, state_win, page_table, norm_g, w_in, ln_g, ln_b, w_s, b_s,
           cmp_pos, w_cmp1, b_cmp1, w_cmp2, w_out, final_g):
    n_batch, seq, _ = x_prompt.shape
    n_seq, t_new, _ = x_sample.shape
    depth, n_phys = cache_kv.shape[:2]
    n_pages = page_table.shape[1]
    past_len = n_pages * PAGE_SIZE
    win_buf = state_win.shape[2]
    assert depth == 1 and seq % TQ == 0 and n_seq * t_new == MIX_ROWS and win_buf == WINDOW
    assert (past_len + t_new - 1) // SLC_BLOCK == past_len // SLC_BLOCK and past_len % CMP_STRIDE == 0

    w_all = _prep_w_in(w_in[0])
    ng = norm_g[0].reshape(1, D_MODEL)
    lng = ln_g[0].reshape(1, D_A)
    lnb = ln_b[0].reshape(1, D_A)
    fg = final_g.reshape(1, D_MODEL)
    wo = w_out[0]
    wo_a = wo[:D_A].astype(BF16)
    wo_b = _head_pair_perm(wo[D_A:].T).T.astype(BF16)
    wc, pe, b1, w2p = _prep_compress(cmp_pos[0], w_cmp1[0], b_cmp1[0], w_cmp2[0])

    xp = x_prompt.reshape(n_batch * seq, D_MODEL)
    wmix_p, bmix_p = _prep_mix(w_s[0], b_s[0], CHUNK)
    cos_p, sin_p = _rope_tables(jnp.arange(seq))
    ag_p, qc_p, qr_p, rows_p, kvw_pt, ksel_p, kwin_p, gate_p, szb_p, rows_pt = _proj_call(
        xp, ng, w_all, lng, lnb, wmix_p, bmix_p, cos_p, sin_p, BF16, seq)
    pages_p = seq // PAGE_SIZE
    ident = jnp.arange(n_batch * pages_p, dtype=jnp.int32).reshape(n_batch, pages_p)
    cmp_p = _compress_call(rows_p.reshape(n_batch * pages_p, PAGE_SIZE, 4 * LANES), ident, wc, pe, b1, w2p,
                           pages_p, False)
    ov_p = _overlap_matrix(seq // CMP_STRIDE, seq // SLC_BLOCK, LANES)
    bg_p = _attn_prompt_call(qc_p, qr_p, cmp_p, ov_p, ksel_p, kwin_p, gate_p, szb_p, n_batch, seq)
    y_p = _merge_call(xp, ag_p, bg_p, wo_a, wo_b, fg)

    xs = x_sample.reshape(n_seq * t_new, D_MODEL)
    wmix_s, bmix_s = _prep_mix(w_s[0], b_s[0], t_new)
    cos_s, sin_s = _rope_tables(jnp.tile(past_len + jnp.arange(t_new), n_seq))
    ag_s, qc_s, qr_s, rows_s, kvw_s, _, _, gate_s, szb_s, vln_s = _proj_call(
        xs, ng, w_all, lng, lnb, wmix_s, bmix_s, cos_s, sin_s, F32, None)
    cache_t = cache_kv[0].transpose(0, 2, 3, 4, 1)
    cmp_s = _compress_call(cache_t, page_table, wc, pe, b1, w2p, SAMPLE_PAGES_PER_STEP, True)
    n_slc_s = -(-(past_len + t_new) // SLC_BLOCK)
    tb = past_len // SLC_BLOCK
    n_pick = TOP_N - 3
    ov_s = _overlap_matrix(past_len // CMP_STRIDE, n_slc_s, -(-n_slc_s // LANES) * LANES)
    win_prev = state_win[0].reshape(n_seq, win_buf, 2 * LANES)
    ocw_s, sel_s, need_s = _cmpwin_sample_call(qc_s, qr_s, cmp_s, ov_s, win_prev, kvw_s, gate_s, t_new, past_len, n_pick)
    sel_flat = sel_s[:, :, :n_pick].reshape(-1)
    assert n_pages == LANES
    bg_s = _sel_sample_call(sel_flat, need_s[:, :, 0, :].reshape(-1), page_table.reshape(-1), cache_t, qr_s, rows_s, gate_s, ocw_s, szb_s,
                            n_seq, t_new, n_pick, n_pages, tb)
    y_s = _merge_call(xs, ag_s, bg_s, wo_a, wo_b, fg)

    new_kv_p = rows_pt.reshape(n_batch, 4, N_KV, HEAD_DIM, seq).transpose(0, 4, 1, 2, 3)
    new_win_p = kvw_pt.reshape(n_batch, 2, N_KV, HEAD_DIM, win_buf).transpose(0, 4, 1, 2, 3)
    new_win_s = jnp.concatenate([state_win[0][:, t_new:], kvw_s.reshape(n_seq, t_new, 2, N_KV, HEAD_DIM)], axis=1)
    return (y_p.reshape(n_batch, seq, D_MODEL),
            y_s.reshape(n_seq, t_new, D_MODEL),
            new_kv_p[None],
            new_win_p[None],
            rows_s.reshape(1, n_seq, t_new, 4, N_KV, HEAD_DIM),
            new_win_s[None],
            vln_s.reshape(1, n_seq, t_new, D_A))
```
